```python
import math
import jax
import jax.numpy as jnp
from jax import lax
import numpy as np

D_MODEL = 1024
BATCH = 8
SEQ = 4096
DEPTH = 2

HEAD_DIM = 64
NSA_HEADS = 8
NSA_KV_GROUPS = 2
CMP_BLOCK = 32
CMP_STRIDE = 16
CMP_HIDDEN = 256
SLC_BLOCK = 64
SLC_TOPK = 16
WINDOW = 512
FOX_HEADS = 4
DIFF_HEADS = 4
DIFF_QK_DIM = HEAD_DIM // 2
N_GROUPS = 4
EXPERTS_PER_GROUP = 8
N_EXPERTS = N_GROUPS * EXPERTS_PER_GROUP
TOPK_IN_GROUP = 2
D_EXPERT = 256
Q_BLOCK = 128
SLC_Q_BLOCK = 64
RMS_EPS = 1e-6
NEG_INF = -1e30
FORCE_SCORE = 1e4

NSA_Q = NSA_HEADS * HEAD_DIM
NSA_KV = NSA_KV_GROUPS * HEAD_DIM
NSA_GATE = NSA_HEADS * 3
FOX_W = FOX_HEADS * HEAD_DIM
DIFF_W = DIFF_HEADS * HEAD_DIM
MIX_WIDTH = NSA_Q + FOX_W + DIFF_W
IN_WIDTHS = (NSA_Q, NSA_KV, NSA_KV, NSA_KV, NSA_KV, NSA_KV, NSA_KV, NSA_GATE,
             FOX_W, FOX_W, FOX_W, FOX_HEADS, DIFF_W, DIFF_W, DIFF_W)
D_IN = sum(IN_WIDTHS)

kernel_name = "hybrid_nsa_fox_diff_hmoe"


def _rmsnorm(x, g):
    xf = x.astype(jnp.float32)
    y = xf * lax.rsqrt(jnp.mean(xf * xf, axis=-1, keepdims=True) + RMS_EPS)
    return (y * g.astype(jnp.float32)).astype(x.dtype)


def _masked_softmax(logits, mask):
    logits = jnp.where(mask, logits.astype(jnp.float32), NEG_INF)
    return jnp.where(mask, jax.nn.softmax(logits, axis=-1), 0.0)


def _alibi_slopes(n):
    return jnp.asarray(2.0 ** (-8.0 * np.arange(1, n + 1) / n), dtype=jnp.float32)


def _lambda_init(layer):
    return 0.8 - 0.6 * math.exp(-0.3 * layer)


def _nsa_mixer(q, kc, vc, ks, vs, kw, vw, gate_logits,
               cmp_k_pe, cmp_k_w1, cmp_k_w2, cmp_v_pe, cmp_v_w1, cmp_v_w2):
    B, T, _ = q.shape
    G, HG, HD = NSA_KV_GROUPS, NSA_HEADS // NSA_KV_GROUPS, HEAD_DIM
    scale = HD ** -0.5
    slopes = _alibi_slopes(NSA_HEADS).reshape(G, HG)[:, :, None, None]
    qh = q.reshape(B, T, G, HG, HD).transpose(0, 2, 3, 1, 4)

    def kvh(a):
        return a.reshape(B, T, G, HD).transpose(0, 2, 1, 3)

    kc, vc, ks, vs, kw, vw = (kvh(a) for a in (kc, vc, ks, vs, kw, vw))
    pos = jnp.arange(T)

    n_cmp = (T - CMP_BLOCK) // CMP_STRIDE + 1
    tok = np.arange(n_cmp)[:, None] * CMP_STRIDE + np.arange(CMP_BLOCK)[None, :]

    def compress(a, pe, w1, w2):
        blk = (a[:, :, tok, :] + pe).reshape(B, G, n_cmp, CMP_BLOCK * HD)
        return jax.nn.gelu(blk @ w1) @ w2

    k_cmp = compress(kc, cmp_k_pe, cmp_k_w1, cmp_k_w2)
    v_cmp = compress(vc, cmp_v_pe, cmp_v_w1, cmp_v_w2)
    cmp_end = jnp.arange(n_cmp) * CMP_STRIDE + CMP_BLOCK - 1
    dist_c = (pos[:, None] - cmp_end[None, :]).astype(jnp.float32)
    s_cmp = jnp.einsum('bghtd,bgcd->bghtc', qh, k_cmp) * scale - slopes * dist_c
    p_cmp = _masked_softmax(s_cmp, dist_c >= 0)
    o_cmp = jnp.einsum('bghtc,bgcd->bghtd', p_cmp.astype(v_cmp.dtype), v_cmp)

    n_sel = T // SLC_BLOCK
    top_n = min(SLC_TOPK, n_sel)
    c_start = np.arange(n_cmp) * CMP_STRIDE
    s_start = np.arange(n_sel) * SLC_BLOCK
    overlap = np.clip(np.minimum(c_start[:, None] + CMP_BLOCK, s_start[None, :] + SLC_BLOCK)
                      - np.maximum(c_start[:, None], s_start[None, :]), 0, None) / CMP_BLOCK
    overlap = jnp.asarray(overlap, dtype=jnp.float32)
    imp = jnp.einsum('bghtc,cj->bgtj', p_cmp, overlap)
    cur = pos // SLC_BLOCK
    j = jnp.arange(n_sel)
    valid = j[None, :] <= cur[:, None]
    forced = (j[None, :] == 0) | (j[None, :] == cur[:, None]) | (j[None, :] == cur[:, None] - 1)
    imp = jnp.where(valid, jnp.where(forced, FORCE_SCORE, imp), NEG_INF)
    _, sel_idx = lax.top_k(imp, top_n)

    ks_b = ks.reshape(B, G, n_sel, SLC_BLOCK, HD)
    vs_b = vs.reshape(B, G, n_sel, SLC_BLOCK, HD)
    nq = T // SLC_Q_BLOCK
    q_chunks = qh.reshape(B, G, HG, nq, SLC_Q_BLOCK, HD).transpose(3, 0, 1, 2, 4, 5)
    idx_chunks = sel_idx.reshape(B, G, nq, SLC_Q_BLOCK, top_n).transpose(2, 0, 1, 3, 4)
    pos_chunks = pos.reshape(nq, SLC_Q_BLOCK)
    b_ix = jnp.arange(B)[:, None, None, None]
    g_ix = jnp.arange(G)[None, :, None, None]
    n_keys = top_n * SLC_BLOCK

    def slc_block(args):
        qc, ic, pc = args
        k_sel = ks_b[b_ix, g_ix, ic].reshape(B, G, SLC_Q_BLOCK, n_keys, HD)
        v_sel = vs_b[b_ix, g_ix, ic].reshape(B, G, SLC_Q_BLOCK, n_keys, HD)
        kpos = (ic[..., None] * SLC_BLOCK + jnp.arange(SLC_BLOCK)).reshape(B, G, SLC_Q_BLOCK, n_keys)
        dist = (pc[:, None] - kpos).astype(jnp.float32)[:, :, None]
        s = jnp.einsum('bghqd,bgqkd->bghqk', qc, k_sel) * scale - slopes * dist
        p = _masked_softmax(s, dist >= 0)
        return jnp.einsum('bghqk,bgqkd->bghqd', p.astype(v_sel.dtype), v_sel)

    o_slc = lax.map(slc_block, (q_chunks, idx_chunks, pos_chunks))
    o_slc = o_slc.transpose(1, 2, 3, 0, 4, 5).reshape(B, G, HG, T, HD)

    nb = T // Q_BLOCK
    span = WINDOW + Q_BLOCK
    kw_p = jnp.pad(kw, ((0, 0), (0, 0), (WINDOW, 0), (0, 0)))
    vw_p = jnp.pad(vw, ((0, 0), (0, 0), (WINDOW, 0), (0, 0)))
    q_blocks = qh.reshape(B, G, HG, nb, Q_BLOCK, HD).transpose(3, 0, 1, 2, 4, 5)

    def win_block(args):
        qb, i = args
        start = i * Q_BLOCK
        kb = lax.dynamic_slice_in_dim(kw_p, start, span, axis=2)
        vb = lax.dynamic_slice_in_dim(vw_p, start, span, axis=2)
        qpos = start + jnp.arange(Q_BLOCK)
        kpos = start - WINDOW + jnp.arange(span)
        dist = (qpos[:, None] - kpos[None, :]).astype(jnp.float32)
        mask = (dist >= 0) & (dist < WINDOW) & (kpos[None, :] >= 0)
        s = jnp.einsum('bghqd,bgkd->bghqk', qb, kb) * scale - slopes * dist
        p = _masked_softmax(s, mask)
        return jnp.einsum('bghqk,bgkd->bghqd', p.astype(vb.dtype), vb)

    o_win = lax.map(win_block, (q_blocks, jnp.arange(nb)))
    o_win = o_win.transpose(1, 2, 3, 0, 4, 5).reshape(B, G, HG, T, HD)

    gates = jax.nn.sigmoid(gate_logits).reshape(B, T, G, HG, 3).transpose(0, 2, 3, 1, 4)
    o = gates[..., 0:1] * o_cmp + gates[..., 1:2] * o_slc + gates[..., 2:3] * o_win
    return o.transpose(0, 3, 1, 2, 4).reshape(B, T, NSA_Q)


def _fox_mixer(q, k, v, f_logit):
    B, T, _ = q.shape
    H, HD = FOX_HEADS, HEAD_DIM
    scale = HD ** -0.5
    qh = q.reshape(B, T, H, HD).transpose(0, 2, 1, 3)
    kh = k.reshape(B, T, H, HD).transpose(0, 2, 1, 3)
    vh = v.reshape(B, T, H, HD).transpose(0, 2, 1, 3)
    log_f = jax.nn.log_sigmoid(f_logit.astype(jnp.float32)).transpose(0, 2, 1)
    cum = jnp.cumsum(log_f, axis=-1)
    nb = T // Q_BLOCK
    q_blocks = qh.reshape(B, H, nb, Q_BLOCK, HD).transpose(2, 0, 1, 3, 4)
    c_blocks = cum.reshape(B, H, nb, Q_BLOCK).transpose(2, 0, 1, 3)
    pos_blocks = jnp.arange(T).reshape(nb, Q_BLOCK)
    kpos = jnp.arange(T)

    def blk(args):
        qb, cb, pb = args
        s = jnp.einsum('bhqd,bhkd->bhqk', qb, kh) * scale + (cb[..., None] - cum[:, :, None, :])
        p = _masked_softmax(s, pb[:, None] >= kpos[None, :])
        return jnp.einsum('bhqk,bhkd->bhqd', p.astype(vh.dtype), vh)

    o = lax.map(blk, (q_blocks, c_blocks, pos_blocks))
    return o.transpose(1, 0, 3, 2, 4).reshape(B, T, FOX_W)


def _diff_mixer(q, k, v, lam, subln, lambda_init):
    B, T, _ = q.shape
    H, d, HD = DIFF_HEADS, DIFF_QK_DIM, HEAD_DIM
    scale = d ** -0.5
    qh = q.reshape(B, T, H, 2, d).transpose(0, 2, 3, 1, 4)
    kh = k.reshape(B, T, H, 2, d).transpose(0, 2, 3, 1, 4)
    vh = v.reshape(B, T, H, HD).transpose(0, 2, 1, 3)
    lam_f = lam.astype(jnp.float32)
    lam_full = jnp.exp(jnp.sum(lam_f[0] * lam_f[1])) - jnp.exp(jnp.sum(lam_f[2] * lam_f[3])) + lambda_init
    slopes = _alibi_slopes(H)[:, None, None, None]
    nb = T // Q_BLOCK
    q_blocks = qh.reshape(B, H, 2, nb, Q_BLOCK, d).transpose(3, 0, 1, 2, 4, 5)
    pos_blocks = jnp.arange(T).reshape(nb, Q_BLOCK)
    kpos = jnp.arange(T)

    def blk(args):
        qb, pb = args
        dist = (pb[:, None] - kpos[None, :]).astype(jnp.float32)
        s = jnp.einsum('bhiqd,bhikd->bhiqk', qb, kh) * scale - slopes * dist
        p = _masked_softmax(s, dist >= 0)
        attn = p[:, :, 0] - lam_full * p[:, :, 1]
        return jnp.einsum('bhqk,bhkd->bhqd', attn.astype(vh.dtype), vh)

    o = lax.map(blk, (q_blocks, pos_blocks))
    o = o.transpose(1, 0, 3, 2, 4).reshape(B, T, H, HD)
    o = _rmsnorm(o, subln) * (1.0 - lambda_init)
    return o.reshape(B, T, DIFF_W)


def _hier_moe(h, w_grp, b_grp, w_exp, b_exp, w_e_gate, w_e_up, w_e_down):
    B, T, D = h.shape
    hf = h.reshape(B * T, D)
    n = hf.shape[0]
    rows = jnp.arange(n)
    grp_logits = (hf @ w_grp).astype(jnp.float32) + b_grp.astype(jnp.float32)
    grp_prob = jax.nn.softmax(grp_logits, axis=-1)
    grp = jnp.argmax(grp_logits, axis=-1)
    grp_gate = grp_prob[rows, grp][:, None]
    exp_logits = ((hf @ w_exp).astype(jnp.float32) + b_exp.astype(jnp.float32)).reshape(n, N_GROUPS, EXPERTS_PER_GROUP)
    in_grp = exp_logits[rows, grp]
    top_val, top_idx = lax.top_k(in_grp, TOPK_IN_GROUP)
    w_top = jax.nn.softmax(top_val, axis=-1) * grp_gate
    eid = grp[:, None] * EXPERTS_PER_GROUP + top_idx
    gate = jnp.sum(jax.nn.one_hot(eid, N_EXPERTS, dtype=jnp.float32) * w_top[..., None], axis=1)
    gate = gate.astype(hf.dtype)
    out = jnp.zeros_like(hf)
    for e in range(N_EXPERTS):
        a = hf @ w_e_gate[e]
        u = hf @ w_e_up[e]
        out = out + (jax.nn.silu(a) * u * gate[:, e:e + 1]) @ w_e_down[e]
    return out.reshape(B, T, D)


def setup_inputs(seed: int = 0) -> dict:
    key = jax.random.key(seed)
    ks = jax.random.split(key, 24)
    L, D = DEPTH, D_MODEL

    def nrm(k, shape, fan_in):
        return jax.random.normal(k, shape, jnp.float32) * fan_in ** -0.5

    def small(k, shape, s):
        return s * jax.random.normal(k, shape, jnp.float32)

    return {
        'x': jax.random.normal(ks[0], (BATCH, SEQ, D), jnp.float32),
        'norm_attn': 1.0 + small(ks[1], (L, D), 0.02),
        'w_in': nrm(ks[2], (L, D, D_IN), D),
        'b_gate': small(ks[3], (L, NSA_GATE), 0.02),
        'b_fgt': 2.0 + 2.0 * jax.random.uniform(ks[4], (L, FOX_HEADS), jnp.float32),
        'cmp_k_pe': small(ks[5], (L, CMP_BLOCK, HEAD_DIM), 0.1),
        'cmp_k_w1': nrm(ks[6], (L, CMP_BLOCK * HEAD_DIM, CMP_HIDDEN), CMP_BLOCK * HEAD_DIM),
        'cmp_k_w2': nrm(ks[7], (L, CMP_HIDDEN, HEAD_DIM), CMP_HIDDEN),
        'cmp_v_pe': small(ks[8], (L, CMP_BLOCK, HEAD_DIM), 0.1),
        'cmp_v_w1': nrm(ks[9], (L, CMP_BLOCK * HEAD_DIM, CMP_HIDDEN), CMP_BLOCK * HEAD_DIM),
        'cmp_v_w2': nrm(ks[10], (L, CMP_HIDDEN, HEAD_DIM), CMP_HIDDEN),
        'diff_lambda': small(ks[11], (L, 4, DIFF_QK_DIM), 0.1),
        'diff_subln': 1.0 + small(ks[12], (L, HEAD_DIM), 0.02),
        'w_out': nrm(ks[13], (L, MIX_WIDTH, D), MIX_WIDTH),
        'norm_ffn': 1.0 + small(ks[14], (L, D), 0.02),
        'w_grp': nrm(ks[15], (L, D, N_GROUPS), D),
        'b_grp': small(ks[16], (L, N_GROUPS), 0.01),
        'w_exp': nrm(ks[17], (L, D, N_EXPERTS), D),
        'b_exp': small(ks[18], (L, N_EXPERTS), 0.01),
        'w_e_gate': nrm(ks[19], (L, N_EXPERTS, D, D_EXPERT), D),
        'w_e_up': nrm(ks[20], (L, N_EXPERTS, D, D_EXPERT), D),
        'w_e_down': nrm(ks[21], (L, N_EXPERTS, D_EXPERT, D), D_EXPERT),
        'norm_final': 1.0 + small(ks[22], (D,), 0.02),
    }


def reference(x, norm_attn, w_in, b_gate, b_fgt, cmp_k_pe, cmp_k_w1, cmp_k_w2,
              cmp_v_pe, cmp_v_w1, cmp_v_w2, diff_lambda, diff_subln, w_out,
              norm_ffn, w_grp, b_grp, w_exp, b_exp, w_e_gate, w_e_up, w_e_down,
              norm_final):
    splits = [int(s) for s in np.cumsum(IN_WIDTHS)[:-1]]
    for l in range(DEPTH):
        h = _rmsnorm(x, norm_attn[l])
        proj = h @ w_in[l]
        (nsa_q, kc, vc, ks_, vs_, kw, vw, nsa_g,
         fq, fk, fv, ff, dq, dk, dv) = jnp.split(proj, splits, axis=-1)
        o_a = _nsa_mixer(nsa_q, kc, vc, ks_, vs_, kw, vw, nsa_g + b_gate[l],
                         cmp_k_pe[l], cmp_k_w1[l], cmp_k_w2[l],
                         cmp_v_pe[l], cmp_v_w1[l], cmp_v_w2[l])
        o_b = _fox_mixer(fq, fk, fv, ff + b_fgt[l])
        o_c = _diff_mixer(dq, dk, dv, diff_lambda[l], diff_subln[l], _lambda_init(l))
        x = x + jnp.concatenate([o_a, o_b, o_c], axis=-1) @ w_out[l]
        x = x + _hier_moe(_rmsnorm(x, norm_ffn[l]), w_grp[l], b_grp[l], w_exp[l], b_exp[l],
                          w_e_gate[l], w_e_up[l], w_e_down[l])
    return _rmsnorm(x, norm_final)
```

```python
import functools
import math

import numpy as np
import jax
import jax.numpy as jnp
from jax import lax
from jax.experimental import pallas as pl
from jax.experimental.pallas import tpu as pltpu

F32 = jnp.float32
BF16 = jnp.bfloat16

D_MODEL = 1024
HEAD_DIM = 64
NSA_HEADS = 8
NSA_KV_GROUPS = 2
CMP_BLOCK = 32
CMP_STRIDE = 16
CMP_HIDDEN = 256
SLC_BLOCK = 64
SLC_SHIFT = 6
SLC_TOPK = 16
WINDOW = 512
FOX_HEADS = 4
DIFF_HEADS = 4
DIFF_QK_DIM = HEAD_DIM // 2
N_GROUPS = 4
EXPERTS_PER_GROUP = 8
N_EXPERTS = N_GROUPS * EXPERTS_PER_GROUP
D_EXPERT = 256
RMS_EPS = 1e-6
FORCE_SCORE = 1e4
NEG_INF = -1e30
MASKED = -2e30

NSA_Q = NSA_HEADS * HEAD_DIM
NSA_KV = NSA_KV_GROUPS * HEAD_DIM
NSA_GATE = NSA_HEADS * 3
FOX_W = FOX_HEADS * HEAD_DIM
DIFF_W = DIFF_HEADS * HEAD_DIM
MIX_WIDTH = NSA_Q + FOX_W + DIFF_W
IN_WIDTHS = (NSA_Q, NSA_KV, NSA_KV, NSA_KV, NSA_KV, NSA_KV, NSA_KV, NSA_GATE,
             FOX_W, FOX_W, FOX_W, FOX_HEADS, DIFF_W, DIFF_W, DIFF_W)

LANES = 128
VMEM_LIMIT_BYTES = 56 * 1024 * 1024

MAIN_W = NSA_Q + 4 * NSA_KV + 3 * FOX_W + 3 * DIFF_W
PROJ_W = MAIN_W + 2 * NSA_KV
COL_KS, COL_VS, COL_KW, COL_VW = 4, 5, 6, 7
COL_FQ, COL_FK, COL_FV = 4, 5, 6
COL_DQ, COL_DK, COL_DV = 7, 8, 9
FGT_COL = NSA_GATE

TQ = 256
TK = 512


def _alibi(n):
    return [float(2.0 ** (-8.0 * (i + 1) / n)) for i in range(n)]


def _lambda_init(layer):
    return 0.8 - 0.6 * math.exp(-0.3 * layer)


def _cparams(*sem):
    return pltpu.CompilerParams(dimension_semantics=sem, vmem_limit_bytes=VMEM_LIMIT_BYTES)


def _dot(a, b):
    return jnp.dot(a, b, preferred_element_type=F32)


def _dot_nt(a, b):
    return lax.dot_general(a, b, (((1,), (1,)), ((), ())), preferred_element_type=F32)


def _split2(x):
    hi = x.astype(BF16)
    lo = (x - hi.astype(F32)).astype(BF16)
    return hi, lo


def _sigmoid(z):
    return 1.0 / (1.0 + jnp.exp(-z))


def _keep_lanes(sel, blk):
    return jnp.where(sel, blk.astype(F32), 0.0).astype(BF16)


def _inproj_kernel(x_ref, g_ref, wm_ref, wsh_ref, wsl_ref, bs_ref, main_ref, kc_ref, vc_ref, small_ref):
    x = x_ref[...]
    ms = jnp.mean(x * x, axis=-1, keepdims=True)
    y = x * lax.rsqrt(ms + RMS_EPS) * g_ref[...]
    hb, hl = _split2(y)
    full = _dot(hb, wm_ref[...]).astype(BF16)
    main_ref[...] = full[:, :MAIN_W]
    kc_ref[...] = full[:, MAIN_W:MAIN_W + NSA_KV]
    vc_ref[...] = full[:, MAIN_W + NSA_KV:]
    small_ref[...] = (_dot(hb, wsh_ref[...]) + _dot(hl, wsh_ref[...]) + _dot(hb, wsl_ref[...])) + bs_ref[...]


def _inproj(x2, g, wm, wsh, wsl, bs, tm=512):
    n = x2.shape[0]
    full = lambda i: (0, 0)
    return pl.pallas_call(
        _inproj_kernel,
        grid=(n // tm,),
        in_specs=[pl.BlockSpec((tm, D_MODEL), lambda i: (i, 0)),
                  pl.BlockSpec((1, D_MODEL), full),
                  pl.BlockSpec((D_MODEL, PROJ_W), full),
                  pl.BlockSpec((D_MODEL, LANES), full),
                  pl.BlockSpec((D_MODEL, LANES), full),
                  pl.BlockSpec((1, LANES), full)],
        out_specs=[pl.BlockSpec((tm, MAIN_W), lambda i: (i, 0)),
                   pl.BlockSpec((tm, NSA_KV), lambda i: (i, 0)),
                   pl.BlockSpec((tm, NSA_KV), lambda i: (i, 0)),
                   pl.BlockSpec((tm, LANES), lambda i: (i, 0))],
        out_shape=[jax.ShapeDtypeStruct((n, MAIN_W), BF16),
                   jax.ShapeDtypeStruct((n, NSA_KV), BF16),
                   jax.ShapeDtypeStruct((n, NSA_KV), BF16),
                   jax.ShapeDtypeStruct((n, LANES), F32)],
        compiler_params=_cparams("parallel"),
        name="inproj",
    )(x2, g, wm, wsh, wsl, bs)


def _compress_kernel(r_ref, pe_ref, w1a_ref, w1b_ref, w2_ref, o_ref):
    r = r_ref[0].astype(F32)
    pe = pe_ref[...]
    ra = (r + pe[0:1, :]).astype(BF16)
    rb = (r + pe[1:2, :]).astype(BF16)
    a = _dot(ra, w1a_ref[...])
    b = _dot(rb, w1b_ref[...])
    n_slab = a.shape[0]
    hid = a + pltpu.roll(b, n_slab - 1, 0)
    hid = jax.nn.gelu(hid)
    o_ref[0] = _dot(hid.astype(BF16), w2_ref[...]).astype(BF16)


def _compress(r, pe2, w1a, w1b, w2bd):
    b, n_slab, w = r.shape
    full = lambda i: (0, 0)
    return pl.pallas_call(
        _compress_kernel,
        grid=(b,),
        in_specs=[pl.BlockSpec((1, n_slab, w), lambda i: (i, 0, 0)),
                  pl.BlockSpec((2, w), full),
                  pl.BlockSpec((w, 2 * CMP_HIDDEN), full),
                  pl.BlockSpec((w, 2 * CMP_HIDDEN), full),
                  pl.BlockSpec((2 * CMP_HIDDEN, LANES), full)],
        out_specs=pl.BlockSpec((1, n_slab, LANES), lambda i: (i, 0, 0)),
        out_shape=jax.ShapeDtypeStruct((b, n_slab, LANES), BF16),
        compiler_params=_cparams("parallel"),
        name="compress",
    )(r, pe2, w1a, w1b, w2bd)


def _cumgate_kernel(z_ref, o_ref, carry_ref, *, tc):
    @pl.when(pl.program_id(1) == 0)
    def _():
        carry_ref[...] = jnp.zeros_like(carry_ref)

    z = z_ref[0]
    logf = -(jnp.maximum(-z, 0.0) + jnp.log(1.0 + jnp.exp(-jnp.abs(z))))
    hi = logf.astype(BF16)
    r1 = logf - hi.astype(F32)
    mid = r1.astype(BF16)
    lo = (r1 - mid.astype(F32)).astype(BF16)
    tri = (lax.broadcasted_iota(jnp.int32, (tc, tc), 0) >= lax.broadcasted_iota(jnp.int32, (tc, tc), 1))
    tri = jnp.where(tri, 1.0, 0.0).astype(BF16)
    c = (_dot(tri, hi) + _dot(tri, mid)) + _dot(tri, lo) + carry_ref[0:1, :]
    o_ref[0] = c
    carry_ref[...] = jnp.broadcast_to(c[tc - 1:tc, :], carry_ref.shape)


def _cumgate(small3, tc=256):
    b, t, w = small3.shape
    return pl.pallas_call(
        functools.partial(_cumgate_kernel, tc=tc),
        grid=(b, t // tc),
        in_specs=[pl.BlockSpec((1, tc, w), lambda i, j: (i, j, 0))],
        out_specs=pl.BlockSpec((1, tc, w), lambda i, j: (i, j, 0)),
        out_shape=jax.ShapeDtypeStruct((b, t, w), F32),
        scratch_shapes=[pltpu.VMEM((8, w), F32)],
        compiler_params=_cparams("parallel", "arbitrary"),
        name="cumgate",
    )(small3)


def _cmp_kernel(q_ref, kc_ref, vc_ref, sm_ref, ov_ref, o_ref, pen_ref, *, tq, n_cmp, n_sel, top_n):
    t0 = pl.program_id(1) * tq
    nck = kc_ref.shape[1]
    row = t0 + lax.broadcasted_iota(jnp.int32, (tq, nck), 0)
    col = lax.broadcasted_iota(jnp.int32, (tq, nck), 1)
    dist = row - (col * CMP_STRIDE + (CMP_BLOCK - 1))
    mask = (dist >= 0) & (col < n_cmp)
    distf = dist.astype(F32)
    lane = lax.broadcasted_iota(jnp.int32, (tq, LANES), 1)
    low = lane < HEAD_DIM
    kc = kc_ref[0]
    vc = vc_ref[0]
    sig = _sigmoid(sm_ref[0])
    slopes = _alibi(NSA_HEADS)
    psum = [None, None]
    hg = NSA_HEADS // NSA_KV_GROUPS
    for j in range(hg):
        qblk = q_ref[0, :, j * LANES:(j + 1) * LANES]
        outs = []
        for g in range(NSA_KV_GROUPS):
            h = j + hg * g
            qm = _keep_lanes(low if g == 0 else ~low, qblk)
            s = _dot_nt(qm, kc) - slopes[h] * distf
            s = jnp.where(mask, s, NEG_INF)
            m = jnp.max(s, axis=-1, keepdims=True)
            p = jnp.where(mask, jnp.exp(s - m), 0.0)
            l = jnp.sum(p, axis=-1, keepdims=True)
            p = p * jnp.where(l > 0.0, 1.0 / l, 0.0)
            psum[g] = p if psum[g] is None else psum[g] + p
            outs.append(_dot(p.astype(BF16), vc))
        gate = jnp.where(low, sig[:, j:j + 1], sig[:, j + hg:j + hg + 1])
        o_ref[0, :, j * LANES:(j + 1) * LANES] = jnp.where(low, outs[0], outs[1]) * gate

    rows_t = lax.broadcasted_iota(jnp.int32, (n_sel, tq), 0)
    cur_r = (t0 + lax.broadcasted_iota(jnp.int32, (tq, LANES), 0)) >> SLC_SHIFT
    pen_t = []
    for g in range(NSA_KV_GROUPS):
        ph, plo = _split2(psum[g])
        imp = _dot(ph, ov_ref[...]) + _dot(plo, ov_ref[...])
        valid = lane <= cur_r
        forced = (lane == 0) | (lane == cur_r) | (lane == cur_r - 1)
        imp = jnp.where(valid, jnp.where(forced, FORCE_SCORE, imp), NEG_INF)
        imp_t = imp.T[:n_sel, :]
        cnt = jnp.zeros((n_sel, tq), F32)
        for k in range(n_sel):
            rk = imp_t[k:k + 1, :]
            cnt = cnt + jnp.where(rows_t > k, jnp.where(rk >= imp_t, 1.0, 0.0), jnp.where(rk > imp_t, 1.0, 0.0))
        pen_t.append(jnp.where(cnt < float(top_n), 0.0, MASKED))
    pad = HEAD_DIM - n_sel
    parts = []
    for g in range(NSA_KV_GROUPS):
        parts.append(pen_t[g])
        if pad:
            parts.append(jnp.zeros((pad, tq), F32))
    pen_ref[0] = jnp.concatenate(parts, axis=0).T.astype(BF16)


def _cmp_attn(main3, kcmp, vcmp, small3, ov, n_cmp, n_sel, top_n, tq=TQ):
    b, t, _ = main3.shape
    nck = kcmp.shape[1]
    return pl.pallas_call(
        functools.partial(_cmp_kernel, tq=tq, n_cmp=n_cmp, n_sel=n_sel, top_n=top_n),
        grid=(b, t // tq),
        in_specs=[pl.BlockSpec((1, tq, NSA_Q), lambda i, j: (i, j, 0)),
                  pl.BlockSpec((1, nck, LANES), lambda i, j: (i, 0, 0)),
                  pl.BlockSpec((1, nck, LANES), lambda i, j: (i, 0, 0)),
                  pl.BlockSpec((1, tq, LANES), lambda i, j: (i, j, 0)),
                  pl.BlockSpec((nck, LANES), lambda i, j: (0, 0))],
        out_specs=[pl.BlockSpec((1, tq, NSA_Q), lambda i, j: (i, j, 0)),
                   pl.BlockSpec((1, tq, LANES), lambda i, j: (i, j, 0))],
        out_shape=[jax.ShapeDtypeStruct((b, t, NSA_Q), F32),
                   jax.ShapeDtypeStruct((b, t, LANES), BF16)],
        compiler_params=_cparams("parallel", "parallel"),
        name="cmp_attn",
    )(main3, kcmp, vcmp, small3, ov)


def _flash_init(m_ref, l_ref, acc_ref):
    m_ref[...] = jnp.full(m_ref.shape, NEG_INF, F32)
    l_ref[...] = jnp.zeros(l_ref.shape, F32)
    acc_ref[...] = jnp.zeros(acc_ref.shape, F32)


def _flash_update(s, v, r, tq, m_ref, l_ref, acc_ref):
    sl = slice(r * tq, (r + 1) * tq)
    m_old = m_ref[sl, :]
    m_new = jnp.maximum(m_old, jnp.max(s, axis=-1, keepdims=True))
    p = jnp.exp(s - m_new)
    alpha = jnp.exp(m_old - m_new)
    l_ref[sl, :] = alpha * l_ref[sl, :] + jnp.sum(p, axis=-1, keepdims=True)
    m_ref[sl, :] = m_new
    acc_ref[sl, :] = alpha * acc_ref[sl, :] + _dot(p.astype(BF16), v)


def _flash_out(r, tq, l_ref, acc_ref):
    sl = slice(r * tq, (r + 1) * tq)
    return acc_ref[sl, :] * (1.0 / l_ref[sl, :])


def _tile_dist(tq, tk, q0, k0):
    return (q0 - k0) + lax.broadcasted_iota(jnp.int32, (tq, tk), 0) - lax.broadcasted_iota(jnp.int32, (tq, tk), 1)


def _nsa_kernel(*refs, tq, tk, mode, nsteps):
    if mode == "slc":
        q_ref, k_ref, v_ref, pen_ref, sm_ref, add_ref, o_ref, qst_ref, m_ref, l_ref, acc_ref = refs
    else:
        q_ref, k_ref, v_ref, sm_ref, add_ref, o_ref, qst_ref, m_ref, l_ref, acc_ref = refs
    i = pl.program_id(1)
    j = pl.program_id(2)
    q0 = i * tq
    last = (q0 + tq - 1) // tk
    kt = j if mode == "slc" else last - (nsteps - 1) + j
    active = (kt <= last) if mode == "slc" else (kt >= 0)
    hg = NSA_HEADS // NSA_KV_GROUPS
    slopes = _alibi(NSA_HEADS)
    lane = lax.broadcasted_iota(jnp.int32, (tq, LANES), 1)
    low = lane < HEAD_DIM

    @pl.when(j == 0)
    def _():
        _flash_init(m_ref, l_ref, acc_ref)
        for h in range(NSA_HEADS):
            jb, g = h % hg, h // hg
            qblk = q_ref[0, :, jb * LANES:(jb + 1) * LANES]
            sel = low if g == 0 else ~low
            qst_ref[h * tq:(h + 1) * tq, 0:LANES] = _keep_lanes(sel, qblk)
            if mode == "slc":
                qst_ref[h * tq:(h + 1) * tq, LANES:2 * LANES] = _keep_lanes(sel, pen_ref[0])

    @pl.when(active)
    def _():
        k0 = kt * tk
        k = k_ref[0]
        if mode == "slc":
            blk = (k0 + lax.broadcasted_iota(jnp.int32, (tk, LANES), 0)) >> SLC_SHIFT
            ind = jnp.where(blk == (lax.broadcasted_iota(jnp.int32, (tk, LANES), 1) & (HEAD_DIM - 1)), 1.0, 0.0)
            k = jnp.concatenate([k, ind.astype(BF16)], axis=1)
        s_all = _dot_nt(qst_ref[...], k)
        dist = _tile_dist(tq, tk, q0, k0)
        distf = dist.astype(F32)
        mask = dist >= 0
        if mode == "win":
            mask = mask & (dist < WINDOW)
        v = v_ref[0]
        for h in range(NSA_HEADS):
            s = s_all[h * tq:(h + 1) * tq, :] - slopes[h] * distf
            s = jnp.where(mask, s, MASKED)
            _flash_update(s, v, h, tq, m_ref, l_ref, acc_ref)

    @pl.when(j == nsteps - 1)
    def _():
        sig = _sigmoid(sm_ref[0])
        br = 1 if mode == "slc" else 2
        for jb in range(hg):
            o = jnp.where(low, _flash_out(jb, tq, l_ref, acc_ref), _flash_out(jb + hg, tq, l_ref, acc_ref))
            c0 = br * NSA_HEADS + jb
            gate = jnp.where(low, sig[:, c0:c0 + 1], sig[:, c0 + hg:c0 + hg + 1])
            o_ref[0, :, jb * LANES:(jb + 1) * LANES] = add_ref[0, :, jb * LANES:(jb + 1) * LANES] + o * gate


def _nsa_branch(main3, pen, small3, addend, mode, tq=TQ, tk=TK):
    b, t, _ = main3.shape
    tk = min(tk, t)
    nq = t // tq
    last = lambda i: (i * tq + tq - 1) // tk
    if mode == "slc":
        nsteps = t // tk
        kt = lambda i, j: jnp.minimum(j, last(i))
        kcol, vcol = COL_KS, COL_VS
    else:
        first = lambda i: max(i * tq - (WINDOW - 1), 0) // tk
        nsteps = max((i * tq + tq - 1) // tk - first(i) + 1 for i in range(nq))
        kt = lambda i, j: jnp.maximum(last(i) - (nsteps - 1) + j, 0)
        kcol, vcol = COL_KW, COL_VW
    in_specs = [pl.BlockSpec((1, tq, NSA_Q), lambda bi, i, j: (bi, i, 0)),
                pl.BlockSpec((1, tk, LANES), lambda bi, i, j: (bi, kt(i, j), kcol)),
                pl.BlockSpec((1, tk, LANES), lambda bi, i, j: (bi, kt(i, j), vcol))]
    args = [main3, main3, main3]
    if mode == "slc":
        in_specs.append(pl.BlockSpec((1, tq, LANES), lambda bi, i, j: (bi, i, 0)))
        args.append(pen)
    in_specs += [pl.BlockSpec((1, tq, LANES), lambda bi, i, j: (bi, i, 0)),
                 pl.BlockSpec((1, tq, NSA_Q), lambda bi, i, j: (bi, i, 0))]
    args += [small3, addend]
    kdim = 2 * LANES if mode == "slc" else LANES
    rows = NSA_HEADS * tq
    return pl.pallas_call(
        functools.partial(_nsa_kernel, tq=tq, tk=tk, mode=mode, nsteps=nsteps),
        grid=(b, nq, nsteps),
        in_specs=in_specs,
        out_specs=pl.BlockSpec((1, tq, NSA_Q), lambda bi, i, j: (bi, i, 0)),
        out_shape=jax.ShapeDtypeStruct((b, t, NSA_Q), F32),
        scratch_shapes=[pltpu.VMEM((rows, kdim), BF16),
                        pltpu.VMEM((rows, 1), F32),
                        pltpu.VMEM((rows, 1), F32),
                        pltpu.VMEM((rows, LANES), F32)],
        compiler_params=_cparams("parallel", "parallel", "arbitrary"),
        name="nsa_" + mode,
    )(*args)


def _pair_kernel(*refs, tq, tk, mode, nsteps, lam_init):
    if mode == "fox":
        q_ref, k_ref, v_ref, nf_ref, o_ref, qst_ref, m_ref, l_ref, acc_ref = refs
        nv, width = 2, HEAD_DIM
    else:
        q_ref, k_ref, v_ref, lam_ref, sub_ref, o_ref, qst_ref, m_ref, l_ref, acc_ref = refs
        nv, width = 4, DIFF_QK_DIM
    npair = 2
    i = pl.program_id(1)
    j = pl.program_id(2)
    q0 = i * tq
    last = (q0 + tq - 1) // tk
    lane = lax.broadcasted_iota(jnp.int32, (tq, LANES), 1)
    low = lane < HEAD_DIM
    slopes = _alibi(DIFF_HEADS)

    @pl.when(j == 0)
    def _():
        _flash_init(m_ref, l_ref, acc_ref)
        for pb in range(npair):
            qblk = q_ref[0, :, pb * LANES:(pb + 1) * LANES]
            for r in range(nv):
                sel = (lane >= r * width) & (lane < (r + 1) * width)
                vh = pb * nv + r
                qst_ref[vh * tq:(vh + 1) * tq, :] = _keep_lanes(sel, qblk)

    @pl.when(j <= last)
    def _():
        k0 = j * tk
        dist = _tile_dist(tq, tk, q0, k0)
        mask = dist >= 0
        distf = dist.astype(F32)
        for pb in range(npair):
            k = k_ref[0, :, pb * LANES:(pb + 1) * LANES]
            v = v_ref[0, :, pb * LANES:(pb + 1) * LANES]
            s_all = _dot_nt(qst_ref[pb * nv * tq:(pb + 1) * nv * tq, :], k)
            for r in range(nv):
                vh = pb * nv + r
                s = s_all[r * tq:(r + 1) * tq, :]
                if mode == "fox":
                    s = s + nf_ref[0, vh:vh + 1, :]
                else:
                    s = s - slopes[vh // 2] * distf
                s = jnp.where(mask, s, MASKED)
                _flash_update(s, v, vh, tq, m_ref, l_ref, acc_ref)

    @pl.when(j == nsteps - 1)
    def _():
        if mode == "fox":
            for pb in range(npair):
                o_ref[0, :, pb * LANES:(pb + 1) * LANES] = jnp.where(
                    low, _flash_out(2 * pb, tq, l_ref, acc_ref), _flash_out(2 * pb + 1, tq, l_ref, acc_ref)
                ).astype(o_ref.dtype)
        else:
            lam = lam_ref[...]
            lam_full = (jnp.exp(jnp.sum(lam[0:1, :] * lam[1:2, :], axis=-1, keepdims=True))
                        - jnp.exp(jnp.sum(lam[2:3, :] * lam[3:4, :], axis=-1, keepdims=True)) + lam_init)
            for pb in range(npair):
                heads = []
                for hh in range(2):
                    vh = pb * nv + 2 * hh
                    heads.append(_flash_out(vh, tq, l_ref, acc_ref) - lam_full * _flash_out(vh + 1, tq, l_ref, acc_ref))
                o = jnp.where(low, heads[0], heads[1])
                sq = o * o
                ss_lo = jnp.sum(jnp.where(low, sq, 0.0), axis=-1, keepdims=True)
                ss_hi = jnp.sum(jnp.where(low, 0.0, sq), axis=-1, keepdims=True)
                ms = jnp.where(low, ss_lo, ss_hi) * (1.0 / HEAD_DIM)
                y = o * lax.rsqrt(ms + RMS_EPS) * sub_ref[...] * (1.0 - lam_init)
                o_ref[0, :, pb * LANES:(pb + 1) * LANES] = y.astype(o_ref.dtype)


def _pair_attn(main3, mode, extra, lam_init=0.0, tq=TQ, tk=TK):
    b, t, _ = main3.shape
    tk = min(tk, t)
    nq, nsteps = t // tq, t // tk
    last = lambda i: (i * tq + tq - 1) // tk
    kt = lambda i, j: jnp.minimum(j, last(i))
    w2 = 2 * LANES
    qc, kc, vc = (COL_FQ, COL_FK, COL_FV) if mode == "fox" else (COL_DQ, COL_DK, COL_DV)
    in_specs = [pl.BlockSpec((1, tq, w2), lambda bi, i, j: (bi, i, qc)),
                pl.BlockSpec((1, tk, w2), lambda bi, i, j: (bi, kt(i, j), kc)),
                pl.BlockSpec((1, tk, w2), lambda bi, i, j: (bi, kt(i, j), vc))]
    if mode == "fox":
        in_specs.append(pl.BlockSpec((1, 8, tk), lambda bi, i, j: (bi, 0, kt(i, j))))
        nvh = 4
    else:
        in_specs += [pl.BlockSpec((4, DIFF_QK_DIM), lambda bi, i, j: (0, 0)),
                     pl.BlockSpec((1, LANES), lambda bi, i, j: (0, 0))]
        nvh = 8
    rows = nvh * tq
    return pl.pallas_call(
        functools.partial(_pair_kernel, tq=tq, tk=tk, mode=mode, nsteps=nsteps, lam_init=lam_init),
        grid=(b, nq, nsteps),
        in_specs=in_specs,
        out_specs=pl.BlockSpec((1, tq, w2), lambda bi, i, j: (bi, i, 0)),
        out_shape=jax.ShapeDtypeStruct((b, t, w2), BF16),
        scratch_shapes=[pltpu.VMEM((rows, LANES), BF16),
                        pltpu.VMEM((rows, 1), F32),
                        pltpu.VMEM((rows, 1), F32),
                        pltpu.VMEM((rows, LANES), F32)],
        compiler_params=_cparams("parallel", "parallel", "arbitrary"),
        name="attn_" + mode,
    )(main3, main3, main3, *extra)


def _outproj_kernel(x_ref, oa_ref, ob_ref, oc_ref, wa_ref, wb_ref, wc_ref, o_ref):
    acc = _dot(oa_ref[...].astype(BF16), wa_ref[...])
    acc = acc + _dot(ob_ref[...], wb_ref[...])
    acc = acc + _dot(oc_ref[...], wc_ref[...])
    o_ref[...] = x_ref[...] + acc


def _outproj(x2, oa, ob, oc, wa, wb, wc, tm=512):
    n = x2.shape[0]
    row = lambda i: (i, 0)
    full = lambda i: (0, 0)
    return pl.pallas_call(
        _outproj_kernel,
        grid=(n // tm,),
        in_specs=[pl.BlockSpec((tm, D_MODEL), row),
                  pl.BlockSpec((tm, NSA_Q), row),
                  pl.BlockSpec((tm, FOX_W), row),
                  pl.BlockSpec((tm, DIFF_W), row),
                  pl.BlockSpec((NSA_Q, D_MODEL), full),
                  pl.BlockSpec((FOX_W, D_MODEL), full),
                  pl.BlockSpec((DIFF_W, D_MODEL), full)],
        out_specs=pl.BlockSpec((tm, D_MODEL), row),
        out_shape=jax.ShapeDtypeStruct((n, D_MODEL), F32),
        compiler_params=_cparams("parallel"),
        name="outproj",
    )(x2, oa, ob, oc, wa, wb, wc)


ROUTE_OFF = N_GROUPS


def _moe_kernel(x_ref, g_ref, wrh_ref, wrl_ref, br_ref, wg_ref, wu_ref, wd_ref, gf_ref, o_ref,
                h_ref, gate_ref, acc_ref, *, ec, nsteps, final_norm):
    e_step = pl.program_id(1)

    @pl.when(e_step == 0)
    def _():
        x = x_ref[...]
        ms = jnp.mean(x * x, axis=-1, keepdims=True)
        y = x * lax.rsqrt(ms + RMS_EPS) * g_ref[...]
        hb, hl = _split2(y)
        h_ref[...] = hb
        lg = (_dot(hb, wrh_ref[...]) + _dot(hl, wrh_ref[...]) + _dot(hb, wrl_ref[...])) + br_ref[...]
        lane = lax.broadcasted_iota(jnp.int32, lg.shape, 1).astype(F32)
        big = float(LANES)
        isg = lane < N_GROUPS
        gmax = jnp.max(jnp.where(isg, lg, -jnp.inf), axis=-1, keepdims=True)
        grp = jnp.min(jnp.where(isg & (lg == gmax), lane, big), axis=-1, keepdims=True)
        gprob = 1.0 / jnp.sum(jnp.where(isg, jnp.exp(lg - gmax), 0.0), axis=-1, keepdims=True)
        lo_lane = ROUTE_OFF + grp * EXPERTS_PER_GROUP
        ing = (lane >= lo_lane) & (lane < lo_lane + EXPERTS_PER_GROUP)
        v1 = jnp.max(jnp.where(ing, lg, -jnp.inf), axis=-1, keepdims=True)
        i1 = jnp.min(jnp.where(ing & (lg == v1), lane, big), axis=-1, keepdims=True)
        rest = ing & (lane != i1)
        v2 = jnp.max(jnp.where(rest, lg, -jnp.inf), axis=-1, keepdims=True)
        i2 = jnp.min(jnp.where(rest & (lg == v2), lane, big), axis=-1, keepdims=True)
        e2 = jnp.exp(v2 - v1)
        w1 = gprob / (1.0 + e2)
        w2 = gprob * e2 / (1.0 + e2)
        gate_ref[...] = jnp.where(lane == i1, w1, 0.0) + jnp.where(lane == i2, w2, 0.0)
        acc_ref[...] = jnp.zeros_like(acc_ref)

    h = h_ref[...]
    acc = acc_ref[...]
    for e in range(ec):
        a = _dot(h, wg_ref[0, e])
        u = _dot(h, wu_ref[0, e])
        act = a * _sigmoid(a) * u
        col = _gate_col(gate_ref, e_step, e, ec)
        acc = acc + _dot((act * col).astype(BF16), wd_ref[0, e])
    acc_ref[...] = acc

    @pl.when(e_step == nsteps - 1)
    def _():
        y = x_ref[...] + acc_ref[...]
        if final_norm:
            ms = jnp.mean(y * y, axis=-1, keepdims=True)
            y = y * lax.rsqrt(ms + RMS_EPS) * gf_ref[...]
        o_ref[...] = y


def _gate_col(gate_ref, e_step, e, ec):
    g = gate_ref[...]
    lane = lax.broadcasted_iota(jnp.int32, g.shape, 1)
    want = ROUTE_OFF + e_step * ec + e
    return jnp.sum(jnp.where(lane == want, g, 0.0), axis=-1, keepdims=True)


def _moe(x2, g, wrh, wrl, br, wg, wu, wd, gf, layer, final_norm, tm=1024, ec=4):
    n = x2.shape[0]
    tm = min(tm, n)
    nsteps = N_EXPERTS // ec
    row = lambda i, e: (i, 0)
    full = lambda i, e: (0, 0)
    wspec_in = pl.BlockSpec((1, ec, D_MODEL, D_EXPERT), lambda i, e: (layer, e, 0, 0))
    wspec_out = pl.BlockSpec((1, ec, D_EXPERT, D_MODEL), lambda i, e: (layer, e, 0, 0))
    return pl.pallas_call(
        functools.partial(_moe_kernel, ec=ec, nsteps=nsteps, final_norm=final_norm),
        grid=(n // tm, nsteps),
        in_specs=[pl.BlockSpec((tm, D_MODEL), row),
                  pl.BlockSpec((1, D_MODEL), full),
                  pl.BlockSpec((D_MODEL, LANES), full),
                  pl.BlockSpec((D_MODEL, LANES), full),
                  pl.BlockSpec((1, LANES), full),
                  wspec_in, wspec_in, wspec_out,
                  pl.BlockSpec((1, D_MODEL), full)],
        out_specs=pl.BlockSpec((tm, D_MODEL), row),
        out_shape=jax.ShapeDtypeStruct((n, D_MODEL), F32),
        scratch_shapes=[pltpu.VMEM((tm, D_MODEL), BF16),
                        pltpu.VMEM((tm, LANES), F32),
                        pltpu.VMEM((tm, D_MODEL), F32)],
        compiler_params=_cparams("parallel", "arbitrary"),
        name="moe",
    )(x2, g, wrh, wrl, br, wg, wu, wd, gf)


def _head_perm():
    hg = NSA_HEADS // NSA_KV_GROUPS
    order = []
    for j in range(hg):
        order += [j, j + hg]
    return np.concatenate([np.arange(h * HEAD_DIM, (h + 1) * HEAD_DIM) for h in order])


def _split_hi_lo(w):
    hi = w.astype(BF16)
    return hi, (w - hi.astype(F32)).astype(BF16)


def _layer_params(l, w_in, b_gate, b_fgt, cmp_pe, cmp_w1, cmp_w2, w_out):
    offs = np.concatenate([[0], np.cumsum(IN_WIDTHS)])
    seg = [w_in[l][:, offs[i]:offs[i + 1]] for i in range(len(IN_WIDTHS))]
    (wq, wkc, wvc, wks, wvs, wkw, wvw, wgt, wfq, wfk, wfv, wff, wdq, wdk, wdv) = seg
    perm = _head_perm()
    gate_perm = np.array([h * 3 + br for br in range(3) for h in range(NSA_HEADS)])
    wm = jnp.concatenate([wq[:, perm] * (HEAD_DIM ** -0.5), wks, wvs, wkw, wvw,
                          wfq * (HEAD_DIM ** -0.5), wfk, wfv,
                          wdq * (DIFF_QK_DIM ** -0.5), wdk, wdv, wkc, wvc], axis=1).astype(BF16)
    pad = LANES - NSA_GATE - FOX_HEADS
    ws = jnp.concatenate([wgt[:, gate_perm], wff, jnp.zeros((D_MODEL, pad), F32)], axis=1)
    wsh, wsl = _split_hi_lo(ws)
    bs = jnp.concatenate([b_gate[l][gate_perm], b_fgt[l], jnp.zeros((pad,), F32)])[None, :]

    eye = jnp.eye(NSA_KV_GROUPS, dtype=F32)
    half = CMP_BLOCK // 2
    cmp = []
    for pe, w1, w2 in zip(cmp_pe, cmp_w1, cmp_w2):
        w1r = w1[l].reshape(2, half, HEAD_DIM, CMP_HIDDEN)
        w1x = jnp.einsum('srdk,gh->srgdhk', w1r, eye).reshape(2, half * NSA_KV, NSA_KV_GROUPS * CMP_HIDDEN)
        w2bd = jnp.einsum('kd,gh->gkhd', w2[l], eye).reshape(NSA_KV_GROUPS * CMP_HIDDEN, NSA_KV)
        pe2 = jnp.broadcast_to(pe[l].reshape(2, half, 1, HEAD_DIM), (2, half, NSA_KV_GROUPS, HEAD_DIM))
        cmp.append((pe2.reshape(2, half * NSA_KV), w1x[0].astype(BF16), w1x[1].astype(BF16), w2bd.astype(BF16)))

    wo = w_out[l]
    wa = wo[:NSA_Q][perm].astype(BF16)
    wb = wo[NSA_Q:NSA_Q + FOX_W].astype(BF16)
    wc = wo[NSA_Q + FOX_W:].astype(BF16)
    return dict(wm=wm, wsh=wsh, wsl=wsl, bs=bs, cmp_k=cmp[0], cmp_v=cmp[1], wa=wa, wb=wb, wc=wc)


def _overlap_matrix(n_slab, n_cmp, n_sel):
    c_start = np.arange(n_cmp) * CMP_STRIDE
    s_start = np.arange(n_sel) * SLC_BLOCK
    ov = np.clip(np.minimum(c_start[:, None] + CMP_BLOCK, s_start[None, :] + SLC_BLOCK)
                 - np.maximum(c_start[:, None], s_start[None, :]), 0, None) / CMP_BLOCK
    full = np.zeros((n_slab, LANES), np.float32)
    full[:n_cmp, :n_sel] = ov
    return jnp.asarray(full, dtype=BF16)


def kernel(x, norm_attn, w_in, b_gate, b_fgt, cmp_k_pe, cmp_k_w1, cmp_k_w2, cmp_v_pe, cmp_v_w1, cmp_v_w2, diff_lambda, diff_subln, w_out, norm_ffn, w_grp, b_grp, w_exp, b_exp, w_e_gate, w_e_up, w_e_down, norm_final):
    b, t, d = x.shape
    depth = w_in.shape[0]
    n = b * t
    n_slab = t // CMP_STRIDE
    n_cmp = (t - CMP_BLOCK) // CMP_STRIDE + 1
    n_sel = t // SLC_BLOCK
    top_n = min(SLC_TOPK, n_sel)
    ov = _overlap_matrix(n_slab, n_cmp, n_sel)

    wg_all = w_e_gate.astype(BF16)
    wu_all = w_e_up.astype(BF16)
    wd_all = w_e_down.astype(BF16)
    gf = norm_final[None, :]

    x2 = x.reshape(n, d)
    for l in range(depth):
        p = _layer_params(l, w_in, b_gate, b_fgt, (cmp_k_pe, cmp_v_pe), (cmp_k_w1, cmp_v_w1),
                          (cmp_k_w2, cmp_v_w2), w_out)
        main, kc, vc, small = _inproj(x2, norm_attn[l][None, :], p["wm"], p["wsh"], p["wsl"], p["bs"])
        main3 = main.reshape(b, t, MAIN_W)
        small3 = small.reshape(b, t, LANES)
        kcmp = _compress(kc.reshape(b, n_slab, CMP_STRIDE * NSA_KV), *p["cmp_k"])
        vcmp = _compress(vc.reshape(b, n_slab, CMP_STRIDE * NSA_KV), *p["cmp_v"])
        o_nsa, pen = _cmp_attn(main3, kcmp, vcmp, small3, ov, n_cmp, n_sel, top_n)
        o_nsa = _nsa_branch(main3, pen, small3, o_nsa, "slc")
        o_nsa = _nsa_branch(main3, None, small3, o_nsa, "win")

        cum = _cumgate(small3)
        negf = -jnp.transpose(cum[:, :, FGT_COL:FGT_COL + FOX_HEADS], (0, 2, 1))
        negf = jnp.concatenate([negf, jnp.zeros((b, 8 - FOX_HEADS, t), F32)], axis=1)
        o_fox = _pair_attn(main3, "fox", (negf,))
        sub2 = jnp.concatenate([diff_subln[l], diff_subln[l]])[None, :]
        o_diff = _pair_attn(main3, "diff", (diff_lambda[l], sub2), lam_init=_lambda_init(l))

        x2 = _outproj(x2, o_nsa.reshape(n, NSA_Q), o_fox.reshape(n, FOX_W), o_diff.reshape(n, DIFF_W),
                      p["wa"], p["wb"], p["wc"])

        wr = jnp.concatenate([w_grp[l], w_exp[l], jnp.zeros((d, LANES - N_GROUPS - N_EXPERTS), F32)], axis=1)
        wrh, wrl = _split_hi_lo(wr)
        br = jnp.concatenate([b_grp[l], b_exp[l], jnp.zeros((LANES - N_GROUPS - N_EXPERTS,), F32)])[None, :]
        x2 = _moe(x2, norm_ffn[l][None, :], wrh, wrl, br, wg_all, wu_all, wd_all, gf, l,
                  final_norm=(l == depth - 1))
    return x2.reshape(b, t, d)
```

```python
import functools
import math

import ml_dtypes
import numpy as np
import jax
import jax.numpy as jnp
from jax import lax
from jax.experimental import pallas as pl
from jax.experimental.pallas import tpu as pltpu

F32 = jnp.float32
BF16 = jnp.bfloat16

D_MODEL = 1024
HEAD_DIM = 64
NSA_HEADS = 8
NSA_KV_GROUPS = 2
CMP_BLOCK = 32
CMP_STRIDE = 16
CMP_HIDDEN = 256
SLC_BLOCK = 64
SLC_SHIFT = 6
SLC_TOPK = 16
WINDOW = 512
FOX_HEADS = 4
DIFF_HEADS = 4
DIFF_QK_DIM = HEAD_DIM // 2
N_GROUPS = 4
EXPERTS_PER_GROUP = 8
N_EXPERTS = N_GROUPS * EXPERTS_PER_GROUP
D_EXPERT = 256
RMS_EPS = 1e-6
FORCE_SCORE = 1e4
NEG_INF = -1e30
MASKED = -2e30

NSA_Q = NSA_HEADS * HEAD_DIM
NSA_KV = NSA_KV_GROUPS * HEAD_DIM
NSA_GATE = NSA_HEADS * 3
FOX_W = FOX_HEADS * HEAD_DIM
DIFF_W = DIFF_HEADS * HEAD_DIM
MIX_WIDTH = NSA_Q + FOX_W + DIFF_W
IN_WIDTHS = (NSA_Q, NSA_KV, NSA_KV, NSA_KV, NSA_KV, NSA_KV, NSA_KV, NSA_GATE,
             FOX_W, FOX_W, FOX_W, FOX_HEADS, DIFF_W, DIFF_W, DIFF_W)

LANES = 128
VMEM_LIMIT_BYTES = 56 * 1024 * 1024

MAIN_W = NSA_Q + 4 * NSA_KV + 3 * FOX_W + 3 * DIFF_W
PROJ_W = MAIN_W + 2 * NSA_KV
COL_KS, COL_VS, COL_KW, COL_VW = 4, 5, 6, 7
COL_FQ, COL_FK, COL_FV = 4, 5, 6
COL_DQ, COL_DK, COL_DV = 7, 8, 9
FGT_LANES = (NSA_GATE, NSA_GATE + FOX_HEADS, NSA_GATE + 2 * FOX_HEADS)
LOG2E = 1.4426950408889634

TQ = 256
TK = 512


def _alibi(n):
    return [float(2.0 ** (-8.0 * (i + 1) / n)) for i in range(n)]


def _lambda_init(layer):
    return 0.8 - 0.6 * math.exp(-0.3 * layer)


def _cparams(*sem):
    return pltpu.CompilerParams(dimension_semantics=sem, vmem_limit_bytes=VMEM_LIMIT_BYTES)


def _dot(a, b):
    return jnp.dot(a, b, preferred_element_type=F32)


def _dot_nt(a, b):
    return lax.dot_general(a, b, (((1,), (1,)), ((), ())), preferred_element_type=F32)


def _split2(x):
    hi = x.astype(BF16)
    lo = (x - hi.astype(F32)).astype(BF16)
    return hi, lo


def _sigmoid(z):
    return 1.0 / (1.0 + jnp.exp(-z))


def _keep_lanes(sel, blk):
    return jnp.where(sel, blk.astype(F32), 0.0).astype(BF16)


def _inproj_kernel(x_ref, g_ref, wm_ref, wsh_ref, wsl_ref, bs_ref, main_ref, kc_ref, vc_ref, small_ref):
    x = x_ref[...]
    ms = jnp.mean(x * x, axis=-1, keepdims=True)
    y = x * lax.rsqrt(ms + RMS_EPS) * g_ref[...]
    hb, hl = _split2(y)
    full = _dot(hb, wm_ref[...]).astype(BF16)
    main_ref[...] = full[:, :MAIN_W]
    kc_ref[...] = full[:, MAIN_W:MAIN_W + NSA_KV]
    vc_ref[...] = full[:, MAIN_W + NSA_KV:]
    small_ref[...] = (_dot(hb, wsh_ref[...]) + _dot(hl, wsh_ref[...]) + _dot(hb, wsl_ref[...])) + bs_ref[...]


def _inproj(x2, g, wm, wsh, wsl, bs, tm=512):
    n = x2.shape[0]
    full = lambda i: (0, 0)
    return pl.pallas_call(
        _inproj_kernel,
        grid=(n // tm,),
        in_specs=[pl.BlockSpec((tm, D_MODEL), lambda i: (i, 0)),
                  pl.BlockSpec((1, D_MODEL), full),
                  pl.BlockSpec((D_MODEL, PROJ_W), full),
                  pl.BlockSpec((D_MODEL, LANES), full),
                  pl.BlockSpec((D_MODEL, LANES), full),
                  pl.BlockSpec((1, LANES), full)],
        out_specs=[pl.BlockSpec((tm, MAIN_W), lambda i: (i, 0)),
                   pl.BlockSpec((tm, NSA_KV), lambda i: (i, 0)),
                   pl.BlockSpec((tm, NSA_KV), lambda i: (i, 0)),
                   pl.BlockSpec((tm, LANES), lambda i: (i, 0))],
        out_shape=[jax.ShapeDtypeStruct((n, MAIN_W), BF16),
                   jax.ShapeDtypeStruct((n, NSA_KV), BF16),
                   jax.ShapeDtypeStruct((n, NSA_KV), BF16),
                   jax.ShapeDtypeStruct((n, LANES), F32)],
        compiler_params=_cparams("parallel"),
        name="inproj",
    )(x2, g, wm, wsh, wsl, bs)


def _compress_kernel(r_ref, pe_ref, w1a_ref, w1b_ref, w2_ref, o_ref):
    r = r_ref[0].astype(F32)
    pe = pe_ref[...]
    ra = (r + pe[0:1, :]).astype(BF16)
    rb = (r + pe[1:2, :]).astype(BF16)
    a = _dot(ra, w1a_ref[...])
    b = _dot(rb, w1b_ref[...])
    n_slab = a.shape[0]
    hid = a + pltpu.roll(b, n_slab - 1, 0)
    hid = jax.nn.gelu(hid)
    o_ref[0] = _dot(hid.astype(BF16), w2_ref[...]).astype(BF16)


def _compress(r, pe2, w1a, w1b, w2bd):
    b, n_slab, w = r.shape
    full = lambda i: (0, 0)
    return pl.pallas_call(
        _compress_kernel,
        grid=(b,),
        in_specs=[pl.BlockSpec((1, n_slab, w), lambda i: (i, 0, 0)),
                  pl.BlockSpec((2, w), full),
                  pl.BlockSpec((w, 2 * CMP_HIDDEN), full),
                  pl.BlockSpec((w, 2 * CMP_HIDDEN), full),
                  pl.BlockSpec((2 * CMP_HIDDEN, LANES), full)],
        out_specs=pl.BlockSpec((1, n_slab, LANES), lambda i: (i, 0, 0)),
        out_shape=jax.ShapeDtypeStruct((b, n_slab, LANES), BF16),
        compiler_params=_cparams("parallel"),
        name="compress",
    )(r, pe2, w1a, w1b, w2bd)


def _cumgate_kernel(z_ref, o_ref, carry_ref, *, tc):
    @pl.when(pl.program_id(1) == 0)
    def _():
        carry_ref[...] = jnp.zeros_like(carry_ref)

    z = z_ref[0]
    logf = -(jnp.maximum(-z, 0.0) + jnp.log(1.0 + jnp.exp(-jnp.abs(z))))
    hi = logf.astype(BF16)
    r1 = logf - hi.astype(F32)
    mid = r1.astype(BF16)
    lo = (r1 - mid.astype(F32)).astype(BF16)
    tri = (lax.broadcasted_iota(jnp.int32, (tc, tc), 0) >= lax.broadcasted_iota(jnp.int32, (tc, tc), 1))
    tri = jnp.where(tri, 1.0, 0.0).astype(BF16)
    c = (_dot(tri, hi) + _dot(tri, mid)) + _dot(tri, lo) + carry_ref[0:1, :]
    carry_ref[...] = jnp.broadcast_to(c[tc - 1:tc, :], carry_ref.shape)
    v = c * (-LOG2E)
    p0 = v.astype(BF16)
    r1 = v - p0.astype(F32)
    p1 = r1.astype(BF16)
    p2 = (r1 - p1.astype(F32)).astype(BF16)
    lane = lax.broadcasted_iota(jnp.int32, v.shape, 1)
    pieces = jnp.where(lane < FGT_LANES[1], p0.astype(F32), jnp.where(lane < FGT_LANES[2], p1.astype(F32), p2.astype(F32)))
    keep = (lane >= FGT_LANES[0]) & (lane < FGT_LANES[2] + FOX_HEADS)
    o_ref[0] = jnp.where(keep, pieces, 0.0).astype(BF16)


def _cumgate(small3, tc=256):
    b, t, w = small3.shape
    return pl.pallas_call(
        functools.partial(_cumgate_kernel, tc=tc),
        grid=(b, t // tc),
        in_specs=[pl.BlockSpec((1, tc, w), lambda i, j: (i, j, 0))],
        out_specs=pl.BlockSpec((1, tc, w), lambda i, j: (i, j, 0)),
        out_shape=jax.ShapeDtypeStruct((b, t, w), BF16),
        scratch_shapes=[pltpu.VMEM((8, w), F32)],
        compiler_params=_cparams("parallel", "arbitrary"),
        name="cumgate",
    )(small3)


def _cmp_kernel(q_ref, kc_ref, vc_ref, sm_ref, ov_ref, o_ref, pen_ref, *, tq, n_cmp, n_sel, top_n):
    t0 = pl.program_id(1) * tq
    nck = kc_ref.shape[1]
    row = t0 + lax.broadcasted_iota(jnp.int32, (tq, nck), 0)
    col = lax.broadcasted_iota(jnp.int32, (tq, nck), 1)
    dist = row - (col * CMP_STRIDE + (CMP_BLOCK - 1))
    mask = (dist >= 0) & (col < n_cmp)
    distf = dist.astype(F32)
    lane = lax.broadcasted_iota(jnp.int32, (tq, LANES), 1)
    low = lane < HEAD_DIM
    kc = kc_ref[0]
    vc = vc_ref[0]
    sig = _sigmoid(sm_ref[0])
    slopes = _alibi(NSA_HEADS)
    psum = [None, None]
    hg = NSA_HEADS // NSA_KV_GROUPS
    for j in range(hg):
        qblk = q_ref[0, :, j * LANES:(j + 1) * LANES]
        outs = []
        for g in range(NSA_KV_GROUPS):
            h = j + hg * g
            qm = _keep_lanes(low if g == 0 else ~low, qblk)
            s = _dot_nt(qm, kc) - (slopes[h] * LOG2E) * distf
            s = jnp.where(mask, s, NEG_INF)
            m = jnp.max(s, axis=-1, keepdims=True)
            p = jnp.where(mask, jnp.exp2(s - m), 0.0)
            l = jnp.sum(p, axis=-1, keepdims=True)
            p = p * jnp.where(l > 0.0, 1.0 / l, 0.0)
            psum[g] = p if psum[g] is None else psum[g] + p
            outs.append(_dot(p.astype(BF16), vc))
        gate = jnp.where(low, sig[:, j:j + 1], sig[:, j + hg:j + hg + 1])
        o_ref[0, :, j * LANES:(j + 1) * LANES] = jnp.where(low, outs[0], outs[1]) * gate

    rows_t = lax.broadcasted_iota(jnp.int32, (n_sel, tq), 0)
    cur_r = (t0 + lax.broadcasted_iota(jnp.int32, (tq, LANES), 0)) >> SLC_SHIFT
    pen_t = []
    for g in range(NSA_KV_GROUPS):
        ph, plo = _split2(psum[g])
        imp = _dot(ph, ov_ref[...]) + _dot(plo, ov_ref[...])
        valid = lane <= cur_r
        forced = (lane == 0) | (lane == cur_r) | (lane == cur_r - 1)
        imp = jnp.where(valid, jnp.where(forced, FORCE_SCORE, imp), NEG_INF)
        imp_t = imp.T[:n_sel, :]
        cnt = jnp.zeros((n_sel, tq), F32)
        for k in range(n_sel):
            rk = imp_t[k:k + 1, :]
            cnt = cnt + jnp.where(rows_t > k, jnp.where(rk >= imp_t, 1.0, 0.0), jnp.where(rk > imp_t, 1.0, 0.0))
        pen_t.append(jnp.where(cnt < float(top_n), 0.0, MASKED))
    for g in range(NSA_KV_GROUPS):
        full = jnp.concatenate([pen_t[g], jnp.zeros((LANES - n_sel, tq), F32)], axis=0)
        pen_ref[0, :, g * LANES:(g + 1) * LANES] = full.T.astype(BF16)


def _cmp_attn(main3, kcmp, vcmp, small3, ov, n_cmp, n_sel, top_n, tq=TQ):
    b, t, _ = main3.shape
    nck = kcmp.shape[1]
    return pl.pallas_call(
        functools.partial(_cmp_kernel, tq=tq, n_cmp=n_cmp, n_sel=n_sel, top_n=top_n),
        grid=(b, t // tq),
        in_specs=[pl.BlockSpec((1, tq, NSA_Q), lambda i, j: (i, j, 0)),
                  pl.BlockSpec((1, nck, LANES), lambda i, j: (i, 0, 0)),
                  pl.BlockSpec((1, nck, LANES), lambda i, j: (i, 0, 0)),
                  pl.BlockSpec((1, tq, LANES), lambda i, j: (i, j, 0)),
                  pl.BlockSpec((nck, LANES), lambda i, j: (0, 0))],
        out_specs=[pl.BlockSpec((1, tq, NSA_Q), lambda i, j: (i, j, 0)),
                   pl.BlockSpec((1, tq, NSA_KV_GROUPS * LANES), lambda i, j: (i, j, 0))],
        out_shape=[jax.ShapeDtypeStruct((b, t, NSA_Q), F32),
                   jax.ShapeDtypeStruct((b, t, NSA_KV_GROUPS * LANES), BF16)],
        compiler_params=_cparams("parallel", "parallel"),
        name="cmp_attn",
    )(main3, kcmp, vcmp, small3, ov)


def _bf16_terms(c, n=3):
    out, r = [], float(c)
    for _ in range(n):
        p = float(np.float32(r).astype(ml_dtypes.bfloat16))
        out.append(p)
        r -= p
    return out


def _alibi_q_aug(lane, a0, slope):
    aug = jnp.zeros(lane.shape, F32)
    for i, c in enumerate(_bf16_terms(slope * LOG2E)):
        aug = jnp.where(lane == a0 + 2 * i, float(SLC_BLOCK) * c, aug)
        aug = jnp.where(lane == a0 + 2 * i + 1, c, aug)
    return aug


def _alibi_k_aug(lane, rel, a0):
    hi = (rel >> SLC_SHIFT).astype(F32)
    lo = (rel & (SLC_BLOCK - 1)).astype(F32)
    inside = (lane >= a0) & (lane < a0 + 6)
    odd = ((lane - a0) & 1) == 1
    return jnp.where(inside, jnp.where(odd, lo, hi), 0.0)


ALIBI_LANE = {"slc": SLC_BLOCK, "win": 0, "diff": 0}
STACKS = {"slc": (1, 8), "win": (1, 8), "fox": (2, 2), "diff": (2, 4)}


def _flash_kernel(*refs, mode, tq, tk, nsteps, lam_init):
    if mode == "slc":
        q_ref, k_ref, v_ref, pen_ref, sm_ref, add_ref, o_ref, qst_ref, m_ref, l_ref, acc_ref = refs
    elif mode == "win":
        q_ref, k_ref, v_ref, sm_ref, add_ref, o_ref, qst_ref, m_ref, l_ref, acc_ref = refs
    elif mode == "fox":
        q_ref, k_ref, v_ref, fa_ref, o_ref, qst_ref, m_ref, l_ref, acc_ref = refs
    else:
        q_ref, k_ref, v_ref, lam_ref, sub_ref, o_ref, qst_ref, m_ref, l_ref, acc_ref = refs
    nstack, nv = STACKS[mode]
    nsa = mode in ("slc", "win")
    hg = NSA_HEADS // NSA_KV_GROUPS
    i = pl.program_id(1)
    j = pl.program_id(2)
    q0 = i * tq
    last = (q0 + tq - 1) // tk
    if mode == "win":
        kt = last - (nsteps - 1) + j
        active = kt >= 0
    else:
        kt = j
        active = j <= last
    k0 = kt * tk

    @pl.when(j == 0)
    def _():
        m_ref[...] = jnp.full(m_ref.shape, NEG_INF, F32)
        l_ref[...] = jnp.zeros(l_ref.shape, F32)
        acc_ref[...] = jnp.zeros(acc_ref.shape, F32)
        lane = lax.broadcasted_iota(jnp.int32, (tq, LANES), 1)
        for vh in range(nstack * nv):
            if nsa:
                jb, g = vh % hg, vh // hg
                lo_lane, width = g * HEAD_DIM, HEAD_DIM
                aug = _alibi_q_aug(lane, ALIBI_LANE[mode], _alibi(NSA_HEADS)[vh])
                if mode == "slc":
                    aug = aug + pen_ref[0, :, g * LANES:(g + 1) * LANES].astype(F32)
            elif mode == "fox":
                jb, r = vh // nv, vh % nv
                lo_lane, width = r * HEAD_DIM, HEAD_DIM
                hit = (lane == FGT_LANES[0] + vh) | (lane == FGT_LANES[1] + vh) | (lane == FGT_LANES[2] + vh)
                aug = jnp.where(hit, 1.0, 0.0)
            else:
                jb, r = vh // nv, vh % nv
                lo_lane, width = r * DIFF_QK_DIM, DIFF_QK_DIM
                aug = _alibi_q_aug(lane, ALIBI_LANE[mode], _alibi(DIFF_HEADS)[vh // 2])
            sel = (lane >= lo_lane) & (lane < lo_lane + width)
            qst_ref[vh * tq:(vh + 1) * tq, 0:LANES] = _keep_lanes(sel, q_ref[0, :, jb * LANES:(jb + 1) * LANES])
            qst_ref[vh * tq:(vh + 1) * tq, LANES:2 * LANES] = aug.astype(BF16)

    def step(masked):
        lane_k = lax.broadcasted_iota(jnp.int32, (tk, LANES), 1)
        row_k = lax.broadcasted_iota(jnp.int32, (tk, LANES), 0)
        if mode == "fox":
            k_aug = fa_ref[0]
        else:
            ka = _alibi_k_aug(lane_k, (k0 - q0) + row_k, ALIBI_LANE[mode])
            if mode == "slc":
                ka = ka + jnp.where(((k0 + row_k) >> SLC_SHIFT) == lane_k, 1.0, 0.0)
            k_aug = ka.astype(BF16)
        if masked:
            dist = (q0 - k0) + lax.broadcasted_iota(jnp.int32, (tk, tq), 1) - lax.broadcasted_iota(jnp.int32, (tk, tq), 0)
            ok = dist >= 0
            if mode == "win":
                ok = ok & (dist < WINDOW)
            bias = jnp.tile(jnp.where(ok, 0.0, MASKED), (1, nv))
        for st in range(nstack):
            cols = slice(st * nv * tq, (st + 1) * nv * tq)
            kblk = k_ref[0] if nsa else k_ref[0, :, st * LANES:(st + 1) * LANES]
            vblk = v_ref[0] if nsa else v_ref[0, :, st * LANES:(st + 1) * LANES]
            s = _dot_nt(jnp.concatenate([kblk, k_aug], axis=1), qst_ref[cols, :])
            if masked:
                s = s + bias
            m_old = m_ref[:, cols]
            m_new = jnp.maximum(m_old, jnp.max(s, axis=0, keepdims=True))
            p = jnp.exp2(s - m_new)
            alpha = jnp.exp2(m_old - m_new)
            l_ref[:, cols] = alpha * l_ref[:, cols] + jnp.sum(p, axis=0, keepdims=True)
            m_ref[:, cols] = m_new
            v_t = vblk.astype(F32).T.astype(BF16)
            acc_ref[:, cols] = alpha * acc_ref[:, cols] + _dot(v_t, p.astype(BF16))

    if mode == "win":
        pl.when(active)(lambda: step(True))
    else:
        diag = k0 + tk - 1 > q0
        pl.when(active & diag)(lambda: step(True))
        pl.when(active & jnp.logical_not(diag))(lambda: step(False))

    @pl.when(j == nsteps - 1)
    def _():
        inv = 1.0 / l_ref[...]
        low_r = lax.broadcasted_iota(jnp.int32, (LANES, tq), 0) < HEAD_DIM

        def out_t(vh):
            return acc_ref[:, vh * tq:(vh + 1) * tq] * inv[:, vh * tq:(vh + 1) * tq]

        low = lax.broadcasted_iota(jnp.int32, (tq, LANES), 1) < HEAD_DIM
        if nsa:
            sig = _sigmoid(sm_ref[0])
            br = 1 if mode == "slc" else 2
            for jb in range(hg):
                o = jnp.where(low_r, out_t(jb), out_t(jb + hg)).T
                c0 = br * NSA_HEADS + jb
                gate = jnp.where(low, sig[:, c0:c0 + 1], sig[:, c0 + hg:c0 + hg + 1])
                o_ref[0, :, jb * LANES:(jb + 1) * LANES] = add_ref[0, :, jb * LANES:(jb + 1) * LANES] + o * gate
        elif mode == "fox":
            for pb in range(nstack):
                o = jnp.where(low_r, out_t(2 * pb), out_t(2 * pb + 1)).T
                o_ref[0, :, pb * LANES:(pb + 1) * LANES] = o.astype(o_ref.dtype)
        else:
            lam = lam_ref[...]
            lam_full = (jnp.exp(jnp.sum(lam[0:1, :] * lam[1:2, :], axis=-1, keepdims=True))
                        - jnp.exp(jnp.sum(lam[2:3, :] * lam[3:4, :], axis=-1, keepdims=True)) + lam_init)
            for pb in range(nstack):
                heads = [out_t(pb * nv + 2 * hh) - lam_full * out_t(pb * nv + 2 * hh + 1) for hh in range(2)]
                o = jnp.where(low_r, heads[0], heads[1]).T
                sq = o * o
                ss_lo = jnp.sum(jnp.where(low, sq, 0.0), axis=-1, keepdims=True)
                ss_hi = jnp.sum(jnp.where(low, 0.0, sq), axis=-1, keepdims=True)
                ms = jnp.where(low, ss_lo, ss_hi) * (1.0 / HEAD_DIM)
                y = o * lax.rsqrt(ms + RMS_EPS) * sub_ref[...] * (1.0 - lam_init)
                o_ref[0, :, pb * LANES:(pb + 1) * LANES] = y.astype(o_ref.dtype)


def _flash(main3, mode, extra, out_dtype, lam_init=0.0, tq=TQ, tk=TK):
    b, t, _ = main3.shape
    tk = min(tk, t)
    nq = t // tq
    nstack, nv = STACKS[mode]
    last = lambda i: (i * tq + tq - 1) // tk
    if mode == "win":
        first = lambda i: max(i * tq - (WINDOW - 1), 0) // tk
        nsteps = max((i * tq + tq - 1) // tk - first(i) + 1 for i in range(nq))
        kt = lambda i, j: jnp.maximum(last(i) - (nsteps - 1) + j, 0)
    else:
        nsteps = t // tk
        kt = lambda i, j: jnp.minimum(j, last(i))
    qtile = lambda w, c: pl.BlockSpec((1, tq, w), lambda bi, i, j: (bi, i, c))
    ktile = lambda w, c: pl.BlockSpec((1, tk, w), lambda bi, i, j: (bi, kt(i, j), c))
    if mode in ("slc", "win"):
        kcol, vcol = (COL_KS, COL_VS) if mode == "slc" else (COL_KW, COL_VW)
        in_specs = [qtile(NSA_Q, 0), ktile(LANES, kcol), ktile(LANES, vcol)]
        if mode == "slc":
            in_specs.append(qtile(NSA_KV_GROUPS * LANES, 0))
        in_specs += [qtile(LANES, 0), qtile(NSA_Q, 0)]
        out_w = NSA_Q
    else:
        qc, kc, vc = (COL_FQ, COL_FK, COL_FV) if mode == "fox" else (COL_DQ, COL_DK, COL_DV)
        in_specs = [qtile(2 * LANES, qc), ktile(2 * LANES, kc), ktile(2 * LANES, vc)]
        if mode == "fox":
            in_specs.append(ktile(LANES, 0))
        else:
            in_specs += [pl.BlockSpec((4, DIFF_QK_DIM), lambda bi, i, j: (0, 0)),
                         pl.BlockSpec((1, LANES), lambda bi, i, j: (0, 0))]
        out_w = 2 * LANES
    cols = nstack * nv * tq
    return pl.pallas_call(
        functools.partial(_flash_kernel, mode=mode, tq=tq, tk=tk, nsteps=nsteps, lam_init=lam_init),
        grid=(b, nq, nsteps),
        in_specs=in_specs,
        out_specs=pl.BlockSpec((1, tq, out_w), lambda bi, i, j: (bi, i, 0)),
        out_shape=jax.ShapeDtypeStruct((b, t, out_w), out_dtype),
        scratch_shapes=[pltpu.VMEM((cols, 2 * LANES), BF16),
                        pltpu.VMEM((1, cols), F32),
                        pltpu.VMEM((1, cols), F32),
                        pltpu.VMEM((LANES, cols), F32)],
        compiler_params=_cparams("parallel", "parallel", "arbitrary"),
        name="flash_" + mode,
    )(main3, main3, main3, *extra)


def _outproj_kernel(x_ref, oa_ref, ob_ref, oc_ref, wa_ref, wb_ref, wc_ref, o_ref):
    acc = _dot(oa_ref[...].astype(BF16), wa_ref[...])
    acc = acc + _dot(ob_ref[...], wb_ref[...])
    acc = acc + _dot(oc_ref[...], wc_ref[...])
    o_ref[...] = x_ref[...] + acc


def _outproj(x2, oa, ob, oc, wa, wb, wc, tm=512):
    n = x2.shape[0]
    row = lambda i: (i, 0)
    full = lambda i: (0, 0)
    return pl.pallas_call(
        _outproj_kernel,
        grid=(n // tm,),
        in_specs=[pl.BlockSpec((tm, D_MODEL), row),
                  pl.BlockSpec((tm, NSA_Q), row),
                  pl.BlockSpec((tm, FOX_W), row),
                  pl.BlockSpec((tm, DIFF_W), row),
                  pl.BlockSpec((NSA_Q, D_MODEL), full),
                  pl.BlockSpec((FOX_W, D_MODEL), full),
                  pl.BlockSpec((DIFF_W, D_MODEL), full)],
        out_specs=pl.BlockSpec((tm, D_MODEL), row),
        out_shape=jax.ShapeDtypeStruct((n, D_MODEL), F32),
        compiler_params=_cparams("parallel"),
        name="outproj",
    )(x2, oa, ob, oc, wa, wb, wc)


ROUTE_OFF = N_GROUPS


def _moe_kernel(x_ref, g_ref, wrh_ref, wrl_ref, br_ref, wg_ref, wu_ref, wd_ref, gf_ref, o_ref,
                h_ref, gate_ref, acc_ref, *, ec, nsteps, final_norm):
    e_step = pl.program_id(1)

    @pl.when(e_step == 0)
    def _():
        x = x_ref[...]
        ms = jnp.mean(x * x, axis=-1, keepdims=True)
        y = x * lax.rsqrt(ms + RMS_EPS) * g_ref[...]
        hb, hl = _split2(y)
        h_ref[...] = hb
        lg = (_dot(hb, wrh_ref[...]) + _dot(hl, wrh_ref[...]) + _dot(hb, wrl_ref[...])) + br_ref[...]
        lane = lax.broadcasted_iota(jnp.int32, lg.shape, 1).astype(F32)
        big = float(LANES)
        isg = lane < N_GROUPS
        gmax = jnp.max(jnp.where(isg, lg, -jnp.inf), axis=-1, keepdims=True)
        grp = jnp.min(jnp.where(isg & (lg == gmax), lane, big), axis=-1, keepdims=True)
        gprob = 1.0 / jnp.sum(jnp.where(isg, jnp.exp(lg - gmax), 0.0), axis=-1, keepdims=True)
        lo_lane = ROUTE_OFF + grp * EXPERTS_PER_GROUP
        ing = (lane >= lo_lane) & (lane < lo_lane + EXPERTS_PER_GROUP)
        v1 = jnp.max(jnp.where(ing, lg, -jnp.inf), axis=-1, keepdims=True)
        i1 = jnp.min(jnp.where(ing & (lg == v1), lane, big), axis=-1, keepdims=True)
        rest = ing & (lane != i1)
        v2 = jnp.max(jnp.where(rest, lg, -jnp.inf), axis=-1, keepdims=True)
        i2 = jnp.min(jnp.where(rest & (lg == v2), lane, big), axis=-1, keepdims=True)
        e2 = jnp.exp(v2 - v1)
        w1 = gprob / (1.0 + e2)
        w2 = gprob * e2 / (1.0 + e2)
        gate_ref[...] = jnp.where(lane == i1, w1, 0.0) + jnp.where(lane == i2, w2, 0.0)
        acc_ref[...] = jnp.zeros_like(acc_ref)

    h = h_ref[...]
    acc = acc_ref[...]
    for e in range(ec):
        a = _dot(h, wg_ref[0, e])
        u = _dot(h, wu_ref[0, e])
        act = a * _sigmoid(a) * u
        col = _gate_col(gate_ref, e_step, e, ec)
        acc = acc + _dot((act * col).astype(BF16), wd_ref[0, e])
    acc_ref[...] = acc

    @pl.when(e_step == nsteps - 1)
    def _():
        y = x_ref[...] + acc_ref[...]
        if final_norm:
            ms = jnp.mean(y * y, axis=-1, keepdims=True)
            y = y * lax.rsqrt(ms + RMS_EPS) * gf_ref[...]
        o_ref[...] = y


def _gate_col(gate_ref, e_step, e, ec):
    g = gate_ref[...]
    lane = lax.broadcasted_iota(jnp.int32, g.shape, 1)
    want = ROUTE_OFF + e_step * ec + e
    return jnp.sum(jnp.where(lane == want, g, 0.0), axis=-1, keepdims=True)


def _moe(x2, g, wrh, wrl, br, wg, wu, wd, gf, layer, final_norm, tm=1024, ec=4):
    n = x2.shape[0]
    tm = min(tm, n)
    nsteps = N_EXPERTS // ec
    row = lambda i, e: (i, 0)
    full = lambda i, e: (0, 0)
    wspec_in = pl.BlockSpec((1, ec, D_MODEL, D_EXPERT), lambda i, e: (layer, e, 0, 0))
    wspec_out = pl.BlockSpec((1, ec, D_EXPERT, D_MODEL), lambda i, e: (layer, e, 0, 0))
    return pl.pallas_call(
        functools.partial(_moe_kernel, ec=ec, nsteps=nsteps, final_norm=final_norm),
        grid=(n // tm, nsteps),
        in_specs=[pl.BlockSpec((tm, D_MODEL), row),
                  pl.BlockSpec((1, D_MODEL), full),
                  pl.BlockSpec((D_MODEL, LANES), full),
                  pl.BlockSpec((D_MODEL, LANES), full),
                  pl.BlockSpec((1, LANES), full),
                  wspec_in, wspec_in, wspec_out,
                  pl.BlockSpec((1, D_MODEL), full)],
        out_specs=pl.BlockSpec((tm, D_MODEL), row),
        out_shape=jax.ShapeDtypeStruct((n, D_MODEL), F32),
        scratch_shapes=[pltpu.VMEM((tm, D_MODEL), BF16),
                        pltpu.VMEM((tm, LANES), F32),
                        pltpu.VMEM((tm, D_MODEL), F32)],
        compiler_params=_cparams("parallel", "arbitrary"),
        name="moe",
    )(x2, g, wrh, wrl, br, wg, wu, wd, gf)


def _head_perm():
    hg = NSA_HEADS // NSA_KV_GROUPS
    order = []
    for j in range(hg):
        order += [j, j + hg]
    return np.concatenate([np.arange(h * HEAD_DIM, (h + 1) * HEAD_DIM) for h in order])


def _split_hi_lo(w):
    hi = w.astype(BF16)
    return hi, (w - hi.astype(F32)).astype(BF16)


def _layer_params(l, w_in, b_gate, b_fgt, cmp_pe, cmp_w1, cmp_w2, w_out):
    offs = np.concatenate([[0], np.cumsum(IN_WIDTHS)])
    seg = [w_in[l][:, offs[i]:offs[i + 1]] for i in range(len(IN_WIDTHS))]
    (wq, wkc, wvc, wks, wvs, wkw, wvw, wgt, wfq, wfk, wfv, wff, wdq, wdk, wdv) = seg
    perm = _head_perm()
    gate_perm = np.array([h * 3 + br for br in range(3) for h in range(NSA_HEADS)])
    wm = jnp.concatenate([wq[:, perm] * (HEAD_DIM ** -0.5 * LOG2E), wks, wvs, wkw, wvw,
                          wfq * (HEAD_DIM ** -0.5 * LOG2E), wfk, wfv,
                          wdq * (DIFF_QK_DIM ** -0.5 * LOG2E), wdk, wdv, wkc, wvc], axis=1).astype(BF16)
    pad = LANES - NSA_GATE - 3 * FOX_HEADS
    ws = jnp.concatenate([wgt[:, gate_perm], wff, wff, wff, jnp.zeros((D_MODEL, pad), F32)], axis=1)
    wsh, wsl = _split_hi_lo(ws)
    bs = jnp.concatenate([b_gate[l][gate_perm], b_fgt[l], b_fgt[l], b_fgt[l], jnp.zeros((pad,), F32)])[None, :]

    eye = jnp.eye(NSA_KV_GROUPS, dtype=F32)
    half = CMP_BLOCK // 2
    cmp = []
    for pe, w1, w2 in zip(cmp_pe, cmp_w1, cmp_w2):
        w1r = w1[l].reshape(2, half, HEAD_DIM, CMP_HIDDEN)
        w1x = jnp.einsum('srdk,gh->srgdhk', w1r, eye).reshape(2, half * NSA_KV, NSA_KV_GROUPS * CMP_HIDDEN)
        w2bd = jnp.einsum('kd,gh->gkhd', w2[l], eye).reshape(NSA_KV_GROUPS * CMP_HIDDEN, NSA_KV)
        pe2 = jnp.broadcast_to(pe[l].reshape(2, half, 1, HEAD_DIM), (2, half, NSA_KV_GROUPS, HEAD_DIM))
        cmp.append((pe2.reshape(2, half * NSA_KV), w1x[0].astype(BF16), w1x[1].astype(BF16), w2bd.astype(BF16)))

    wo = w_out[l]
    wa = wo[:NSA_Q][perm].astype(BF16)
    wb = wo[NSA_Q:NSA_Q + FOX_W].astype(BF16)
    wc = wo[NSA_Q + FOX_W:].astype(BF16)
    return dict(wm=wm, wsh=wsh, wsl=wsl, bs=bs, cmp_k=cmp[0], cmp_v=cmp[1], wa=wa, wb=wb, wc=wc)


def _overlap_matrix(n_slab, n_cmp, n_sel):
    c_start = np.arange(n_cmp) * CMP_STRIDE
    s_start = np.arange(n_sel) * SLC_BLOCK
    ov = np.clip(np.minimum(c_start[:, None] + CMP_BLOCK, s_start[None, :] + SLC_BLOCK)
                 - np.maximum(c_start[:, None], s_start[None, :]), 0, None) / CMP_BLOCK
    full = np.zeros((n_slab, LANES), np.float32)
    full[:n_cmp, :n_sel] = ov
    return jnp.asarray(full, dtype=BF16)


def kernel(x, norm_attn, w_in, b_gate, b_fgt, cmp_k_pe, cmp_k_w1, cmp_k_w2, cmp_v_pe, cmp_v_w1, cmp_v_w2, diff_lambda, diff_subln, w_out, norm_ffn, w_grp, b_grp, w_exp, b_exp, w_e_gate, w_e_up, w_e_down, norm_final):
    b, t, d = x.shape
    depth = w_in.shape[0]
    n = b * t
    n_slab = t // CMP_STRIDE
    n_cmp = (t - CMP_BLOCK) // CMP_STRIDE + 1
    n_sel = t // SLC_BLOCK
    top_n = min(SLC_TOPK, n_sel)
    ov = _overlap_matrix(n_slab, n_cmp, n_sel)

    wg_all = w_e_gate.astype(BF16)
    wu_all = w_e_up.astype(BF16)
    wd_all = w_e_down.astype(BF16)
    gf = norm_final[None, :]

    x2 = x.reshape(n, d)
    for l in range(depth):
        p = _layer_params(l, w_in, b_gate, b_fgt, (cmp_k_pe, cmp_v_pe), (cmp_k_w1, cmp_v_w1),
                          (cmp_k_w2, cmp_v_w2), w_out)
        main, kc, vc, small = _inproj(x2, norm_attn[l][None, :], p["wm"], p["wsh"], p["wsl"], p["bs"])
        main3 = main.reshape(b, t, MAIN_W)
        small3 = small.reshape(b, t, LANES)
        kcmp = _compress(kc.reshape(b, n_slab, CMP_STRIDE * NSA_KV), *p["cmp_k"])
        vcmp = _compress(vc.reshape(b, n_slab, CMP_STRIDE * NSA_KV), *p["cmp_v"])
        o_nsa, pen = _cmp_attn(main3, kcmp, vcmp, small3, ov, n_cmp, n_sel, top_n)
        o_nsa = _flash(main3, "slc", (pen, small3, o_nsa), F32)
        o_nsa = _flash(main3, "win", (small3, o_nsa), F32)
        o_fox = _flash(main3, "fox", (_cumgate(small3),), BF16)
        sub2 = jnp.concatenate([diff_subln[l], diff_subln[l]])[None, :]
        o_diff = _flash(main3, "diff", (diff_lambda[l], sub2), BF16, lam_init=_lambda_init(l))

        x2 = _outproj(x2, o_nsa.reshape(n, NSA_Q), o_fox.reshape(n, FOX_W), o_diff.reshape(n, DIFF_W),
                      p["wa"], p["wb"], p["wc"])

        wr = jnp.concatenate([w_grp[l], w_exp[l], jnp.zeros((d, LANES - N_GROUPS - N_EXPERTS), F32)], axis=1)
        wrh, wrl = _split_hi_lo(wr)
        br = jnp.concatenate([b_grp[l], b_exp[l], jnp.zeros((LANES - N_GROUPS - N_EXPERTS,), F32)])[None, :]
        x2 = _moe(x2, norm_ffn[l][None, :], wrh, wrl, br, wg_all, wu_all, wd_all, gf, l,
                  final_norm=(l == depth - 1))
    return x2.reshape(b, t, d)
```

```python
import functools
import math

import ml_dtypes
import numpy as np
import jax
import jax.numpy as jnp
from jax import lax
from jax.experimental import pallas as pl
from jax.experimental.pallas import tpu as pltpu

F32 = jnp.float32
BF16 = jnp.bfloat16

D_MODEL = 1024
HEAD_DIM = 64
NSA_HEADS = 8
NSA_KV_GROUPS = 2
CMP_BLOCK = 32
CMP_STRIDE = 16
CMP_HIDDEN = 256
SLC_BLOCK = 64
SLC_SHIFT = 6
SLC_TOPK = 16
WINDOW = 512
FOX_HEADS = 4
DIFF_HEADS = 4
DIFF_QK_DIM = HEAD_DIM // 2
N_GROUPS = 4
EXPERTS_PER_GROUP = 8
N_EXPERTS = N_GROUPS * EXPERTS_PER_GROUP
D_EXPERT = 256
RMS_EPS = 1e-6
FORCE_SCORE = 1e4
NEG_INF = -1e30
MASKED = -2e30

NSA_Q = NSA_HEADS * HEAD_DIM
NSA_KV = NSA_KV_GROUPS * HEAD_DIM
NSA_GATE = NSA_HEADS * 3
FOX_W = FOX_HEADS * HEAD_DIM
DIFF_W = DIFF_HEADS * HEAD_DIM
MIX_WIDTH = NSA_Q + FOX_W + DIFF_W
IN_WIDTHS = (NSA_Q, NSA_KV, NSA_KV, NSA_KV, NSA_KV, NSA_KV, NSA_KV, NSA_GATE,
             FOX_W, FOX_W, FOX_W, FOX_HEADS, DIFF_W, DIFF_W, DIFF_W)

LANES = 128
VMEM_LIMIT_BYTES = 56 * 1024 * 1024

MAIN_W = NSA_Q + 4 * NSA_KV + 3 * FOX_W + 3 * DIFF_W
PROJ_W = MAIN_W + 2 * NSA_KV
COL_KS, COL_VS, COL_KW, COL_VW = 4, 5, 6, 7
COL_FQ, COL_FK, COL_FV = 4, 5, 6
COL_DQ, COL_DK, COL_DV = 7, 8, 9
FGT_LANES = (NSA_GATE, NSA_GATE + FOX_HEADS, NSA_GATE + 2 * FOX_HEADS)
LOG2E = 1.4426950408889634

TQ = 256
TK = 512
KSTRIP = 256
HEADS_PER_DOT = 2
ONES_ROWS = 16

def _alibi(n):
    return [float(2.0 ** (-8.0 * (i + 1) / n)) for i in range(n)]


def _lambda_init(layer):
    return 0.8 - 0.6 * math.exp(-0.3 * layer)


def _cparams(*sem):
    return pltpu.CompilerParams(dimension_semantics=sem, vmem_limit_bytes=VMEM_LIMIT_BYTES)


def _dot(a, b):
    return jnp.dot(a, b, preferred_element_type=F32)


def _dot_nt(a, b):
    return lax.dot_general(a, b, (((1,), (1,)), ((), ())), preferred_element_type=F32)


def _split2(x):
    hi = x.astype(BF16)
    lo = (x - hi.astype(F32)).astype(BF16)
    return hi, lo


def _sigmoid(z):
    return 1.0 / (1.0 + jnp.exp(-z))


def _keep_lanes(sel, blk):
    return jnp.where(sel, blk.astype(F32), 0.0).astype(BF16)


def _inproj_kernel(x_ref, g_ref, wm_ref, wsh_ref, wsl_ref, bs_ref, main_ref, kc_ref, vc_ref, small_ref):
    x = x_ref[...]
    ms = jnp.mean(x * x, axis=-1, keepdims=True)
    y = x * lax.rsqrt(ms + RMS_EPS) * g_ref[...]
    hb, hl = _split2(y)
    full = _dot(hb, wm_ref[...]).astype(BF16)
    main_ref[...] = full[:, :MAIN_W]
    kc_ref[...] = full[:, MAIN_W:MAIN_W + NSA_KV]
    vc_ref[...] = full[:, MAIN_W + NSA_KV:]
    small_ref[...] = (_dot(hb, wsh_ref[...]) + _dot(hl, wsh_ref[...]) + _dot(hb, wsl_ref[...])) + bs_ref[...]


def _inproj(x2, g, wm, wsh, wsl, bs, tm=512):
    n = x2.shape[0]
    full = lambda i: (0, 0)
    return pl.pallas_call(
        _inproj_kernel,
        grid=(n // tm,),
        in_specs=[pl.BlockSpec((tm, D_MODEL), lambda i: (i, 0)),
                  pl.BlockSpec((1, D_MODEL), full),
                  pl.BlockSpec((D_MODEL, PROJ_W), full),
                  pl.BlockSpec((D_MODEL, LANES), full),
                  pl.BlockSpec((D_MODEL, LANES), full),
                  pl.BlockSpec((1, LANES), full)],
        out_specs=[pl.BlockSpec((tm, MAIN_W), lambda i: (i, 0)),
                   pl.BlockSpec((tm, NSA_KV), lambda i: (i, 0)),
                   pl.BlockSpec((tm, NSA_KV), lambda i: (i, 0)),
                   pl.BlockSpec((tm, LANES), lambda i: (i, 0))],
        out_shape=[jax.ShapeDtypeStruct((n, MAIN_W), BF16),
                   jax.ShapeDtypeStruct((n, NSA_KV), BF16),
                   jax.ShapeDtypeStruct((n, NSA_KV), BF16),
                   jax.ShapeDtypeStruct((n, LANES), F32)],
        compiler_params=_cparams("parallel"),
        name="inproj",
    )(x2, g, wm, wsh, wsl, bs)


def _compress_kernel(r_ref, pe_ref, w1a_ref, w1b_ref, w2_ref, o_ref):
    r = r_ref[0].astype(F32)
    pe = pe_ref[...]
    ra = (r + pe[0:1, :]).astype(BF16)
    rb = (r + pe[1:2, :]).astype(BF16)
    a = _dot(ra, w1a_ref[...])
    b = _dot(rb, w1b_ref[...])
    n_slab = a.shape[0]
    hid = a + pltpu.roll(b, n_slab - 1, 0)
    hid = jax.nn.gelu(hid)
    o_ref[0] = _dot(hid.astype(BF16), w2_ref[...]).astype(BF16)


def _compress(r, pe2, w1a, w1b, w2bd):
    b, n_slab, w = r.shape
    full = lambda i: (0, 0)
    return pl.pallas_call(
        _compress_kernel,
        grid=(b,),
        in_specs=[pl.BlockSpec((1, n_slab, w), lambda i: (i, 0, 0)),
                  pl.BlockSpec((2, w), full),
                  pl.BlockSpec((w, 2 * CMP_HIDDEN), full),
                  pl.BlockSpec((w, 2 * CMP_HIDDEN), full),
                  pl.BlockSpec((2 * CMP_HIDDEN, LANES), full)],
        out_specs=pl.BlockSpec((1, n_slab, LANES), lambda i: (i, 0, 0)),
        out_shape=jax.ShapeDtypeStruct((b, n_slab, LANES), BF16),
        compiler_params=_cparams("parallel"),
        name="compress",
    )(r, pe2, w1a, w1b, w2bd)


def _cumgate_kernel(z_ref, o_ref, carry_ref, *, tc):
    @pl.when(pl.program_id(1) == 0)
    def _():
        carry_ref[...] = jnp.zeros_like(carry_ref)

    z = z_ref[0]
    logf = -(jnp.maximum(-z, 0.0) + jnp.log(1.0 + jnp.exp(-jnp.abs(z))))
    hi = logf.astype(BF16)
    r1 = logf - hi.astype(F32)
    mid = r1.astype(BF16)
    lo = (r1 - mid.astype(F32)).astype(BF16)
    tri = (lax.broadcasted_iota(jnp.int32, (tc, tc), 0) >= lax.broadcasted_iota(jnp.int32, (tc, tc), 1))
    tri = jnp.where(tri, 1.0, 0.0).astype(BF16)
    c = (_dot(tri, hi) + _dot(tri, mid)) + _dot(tri, lo) + carry_ref[0:1, :]
    carry_ref[...] = jnp.broadcast_to(c[tc - 1:tc, :], carry_ref.shape)
    v = c * (-LOG2E)
    p0 = v.astype(BF16)
    r1 = v - p0.astype(F32)
    p1 = r1.astype(BF16)
    p2 = (r1 - p1.astype(F32)).astype(BF16)
    lane = lax.broadcasted_iota(jnp.int32, v.shape, 1)
    pieces = jnp.where(lane < FGT_LANES[1], p0.astype(F32), jnp.where(lane < FGT_LANES[2], p1.astype(F32), p2.astype(F32)))
    keep = (lane >= FGT_LANES[0]) & (lane < FGT_LANES[2] + FOX_HEADS)
    o_ref[0] = jnp.where(keep, pieces, 0.0).astype(BF16)


def _cumgate(small3, tc=256):
    b, t, w = small3.shape
    return pl.pallas_call(
        functools.partial(_cumgate_kernel, tc=tc),
        grid=(b, t // tc),
        in_specs=[pl.BlockSpec((1, tc, w), lambda i, j: (i, j, 0))],
        out_specs=pl.BlockSpec((1, tc, w), lambda i, j: (i, j, 0)),
        out_shape=jax.ShapeDtypeStruct((b, t, w), BF16),
        scratch_shapes=[pltpu.VMEM((8, w), F32)],
        compiler_params=_cparams("parallel", "arbitrary"),
        name="cumgate",
    )(small3)


def _cmp_kernel(q_ref, kc_ref, vc_ref, sm_ref, ov_ref, o_ref, pen_ref, *, tq, n_cmp, n_sel, top_n):
    t0 = pl.program_id(1) * tq
    nck = kc_ref.shape[1]
    row = t0 + lax.broadcasted_iota(jnp.int32, (tq, nck), 0)
    col = lax.broadcasted_iota(jnp.int32, (tq, nck), 1)
    dist = row - (col * CMP_STRIDE + (CMP_BLOCK - 1))
    mask = (dist >= 0) & (col < n_cmp)
    distf = dist.astype(F32)
    lane = lax.broadcasted_iota(jnp.int32, (tq, LANES), 1)
    low = lane < HEAD_DIM
    kc = kc_ref[0]
    vc = vc_ref[0]
    sig = _sigmoid(sm_ref[0])
    slopes = _alibi(NSA_HEADS)
    psum = [None, None]
    hg = NSA_HEADS // NSA_KV_GROUPS
    for j in range(hg):
        qblk = q_ref[0, :, j * LANES:(j + 1) * LANES]
        outs = []
        for g in range(NSA_KV_GROUPS):
            h = j + hg * g
            qm = _keep_lanes(low if g == 0 else ~low, qblk)
            s = _dot_nt(qm, kc) - (slopes[h] * LOG2E) * distf
            s = jnp.where(mask, s, NEG_INF)
            m = jnp.max(s, axis=-1, keepdims=True)
            p = jnp.where(mask, jnp.exp2(s - m), 0.0)
            l = jnp.sum(p, axis=-1, keepdims=True)
            p = p * jnp.where(l > 0.0, 1.0 / l, 0.0)
            psum[g] = p if psum[g] is None else psum[g] + p
            outs.append(_dot(p.astype(BF16), vc))
        gate = jnp.where(low, sig[:, j:j + 1], sig[:, j + hg:j + hg + 1])
        o_ref[0, :, j * LANES:(j + 1) * LANES] = jnp.where(low, outs[0], outs[1]) * gate

    rows_t = lax.broadcasted_iota(jnp.int32, (n_sel, tq), 0)
    cur_r = (t0 + lax.broadcasted_iota(jnp.int32, (tq, LANES), 0)) >> SLC_SHIFT
    pen_t = []
    for g in range(NSA_KV_GROUPS):
        ph, plo = _split2(psum[g])
        imp = _dot(ph, ov_ref[...]) + _dot(plo, ov_ref[...])
        valid = lane <= cur_r
        forced = (lane == 0) | (lane == cur_r) | (lane == cur_r - 1)
        imp = jnp.where(valid, jnp.where(forced, FORCE_SCORE, imp), NEG_INF)
        imp_t = imp.T[:n_sel, :]
        cnt = jnp.zeros((n_sel, tq), F32)
        for k in range(n_sel):
            rk = imp_t[k:k + 1, :]
            cnt = cnt + jnp.where(rows_t > k, jnp.where(rk >= imp_t, 1.0, 0.0), jnp.where(rk > imp_t, 1.0, 0.0))
        pen_t.append(jnp.where(cnt < float(top_n), 0.0, MASKED))
    for g in range(NSA_KV_GROUPS):
        full = jnp.concatenate([pen_t[g], jnp.zeros((LANES - n_sel, tq), F32)], axis=0)
        pen_ref[0, :, g * LANES:(g + 1) * LANES] = full.T.astype(BF16)


def _cmp_attn(main3, kcmp, vcmp, small3, ov, n_cmp, n_sel, top_n, tq=TQ):
    b, t, _ = main3.shape
    nck = kcmp.shape[1]
    return pl.pallas_call(
        functools.partial(_cmp_kernel, tq=tq, n_cmp=n_cmp, n_sel=n_sel, top_n=top_n),
        grid=(b, t // tq),
        in_specs=[pl.BlockSpec((1, tq, NSA_Q), lambda i, j: (i, j, 0)),
                  pl.BlockSpec((1, nck, LANES), lambda i, j: (i, 0, 0)),
                  pl.BlockSpec((1, nck, LANES), lambda i, j: (i, 0, 0)),
                  pl.BlockSpec((1, tq, LANES), lambda i, j: (i, j, 0)),
                  pl.BlockSpec((nck, LANES), lambda i, j: (0, 0))],
        out_specs=[pl.BlockSpec((1, tq, NSA_Q), lambda i, j: (i, j, 0)),
                   pl.BlockSpec((1, tq, NSA_KV_GROUPS * LANES), lambda i, j: (i, j, 0))],
        out_shape=[jax.ShapeDtypeStruct((b, t, NSA_Q), F32),
                   jax.ShapeDtypeStruct((b, t, NSA_KV_GROUPS * LANES), BF16)],
        compiler_params=_cparams("parallel", "parallel"),
        name="cmp_attn",
    )(main3, kcmp, vcmp, small3, ov)


def _bf16_terms(c, n=3):
    out, r = [], float(c)
    for _ in range(n):
        p = float(np.float32(r).astype(ml_dtypes.bfloat16))
        out.append(p)
        r -= p
    return out


def _alibi_q_aug(lane, a0, slope):
    aug = jnp.zeros(lane.shape, F32)
    for i, c in enumerate(_bf16_terms(slope * LOG2E)):
        aug = jnp.where(lane == a0 + 2 * i, float(SLC_BLOCK) * c, aug)
        aug = jnp.where(lane == a0 + 2 * i + 1, c, aug)
    return aug


def _alibi_k_aug(lane, rel, a0):
    hi = (rel >> SLC_SHIFT).astype(F32)
    lo = (rel & (SLC_BLOCK - 1)).astype(F32)
    inside = (lane >= a0) & (lane < a0 + 6)
    odd = ((lane - a0) & 1) == 1
    return jnp.where(inside, jnp.where(odd, lo, hi), 0.0)


ALIBI_LANE = {"slc": SLC_BLOCK, "win": 0, "diff": 0}
STACKS = {"slc": (1, 8), "win": (1, 8), "fox": (2, 2), "diff": (2, 4)}


def _flash_kernel(*refs, mode, tq, tk, nsteps, lam_init):
    if mode == "slc":
        q_ref, k_ref, v_ref, pen_ref, sm_ref, add_ref, o_ref, qst_ref, m_ref, acc_ref = refs
    elif mode == "win":
        q_ref, k_ref, v_ref, sm_ref, add_ref, o_ref, qst_ref, m_ref, acc_ref = refs
    elif mode == "fox":
        q_ref, k_ref, v_ref, fa_ref, o_ref, qst_ref, m_ref, acc_ref = refs
    else:
        q_ref, k_ref, v_ref, lam_ref, sub_ref, o_ref, qst_ref, m_ref, acc_ref = refs
    nstack, nv = STACKS[mode]
    nsa = mode in ("slc", "win")
    hg = NSA_HEADS // NSA_KV_GROUPS
    i = pl.program_id(1)
    j = pl.program_id(2)
    q0 = i * tq
    last = (q0 + tq - 1) // tk
    if mode == "win":
        kt = last - (nsteps - 1) + j
        active = kt >= 0
    else:
        kt = j
        active = j <= last
    k0 = kt * tk

    @pl.when(j == 0)
    def _():
        m_ref[...] = jnp.full(m_ref.shape, NEG_INF, F32)
        acc_ref[...] = jnp.zeros(acc_ref.shape, F32)
        lane = lax.broadcasted_iota(jnp.int32, (tq, LANES), 1)
        for vh in range(nstack * nv):
            if nsa:
                jb, g = vh % hg, vh // hg
                lo_lane, width = g * HEAD_DIM, HEAD_DIM
                aug = _alibi_q_aug(lane, ALIBI_LANE[mode], _alibi(NSA_HEADS)[vh])
                if mode == "slc":
                    aug = aug + pen_ref[0, :, g * LANES:(g + 1) * LANES].astype(F32)
            elif mode == "fox":
                jb, r = vh // nv, vh % nv
                lo_lane, width = r * HEAD_DIM, HEAD_DIM
                hit = (lane == FGT_LANES[0] + vh) | (lane == FGT_LANES[1] + vh) | (lane == FGT_LANES[2] + vh)
                aug = jnp.where(hit, 1.0, 0.0)
            else:
                jb, r = vh // nv, vh % nv
                lo_lane, width = r * DIFF_QK_DIM, DIFF_QK_DIM
                aug = _alibi_q_aug(lane, ALIBI_LANE[mode], _alibi(DIFF_HEADS)[vh // 2])
            sel = (lane >= lo_lane) & (lane < lo_lane + width)
            qst_ref[vh * tq:(vh + 1) * tq, 0:LANES] = _keep_lanes(sel, q_ref[0, :, jb * LANES:(jb + 1) * LANES])
            qst_ref[vh * tq:(vh + 1) * tq, LANES:2 * LANES] = aug.astype(BF16)

    def step(masked):
        lane_k = lax.broadcasted_iota(jnp.int32, (tk, LANES), 1)
        row_k = lax.broadcasted_iota(jnp.int32, (tk, LANES), 0)
        if mode == "fox":
            k_aug = fa_ref[0]
        else:
            ka = _alibi_k_aug(lane_k, (k0 - q0) + row_k, ALIBI_LANE[mode])
            if mode == "slc":
                ka = ka + jnp.where(((k0 + row_k) >> SLC_SHIFT) == lane_k, 1.0, 0.0)
            k_aug = ka.astype(BF16)
        if masked:
            dist = (q0 - k0) + lax.broadcasted_iota(jnp.int32, (tk, tq), 1) - lax.broadcasted_iota(jnp.int32, (tk, tq), 0)
            ok = dist >= 0
            if mode == "win":
                ok = ok & (dist < WINDOW)
            bias = jnp.where(ok, 0.0, MASKED)
        keys, vals_t = [], []
        for st in range(nstack):
            kblk = k_ref[0] if nsa else k_ref[0, :, st * LANES:(st + 1) * LANES]
            vblk = v_ref[0] if nsa else v_ref[0, :, st * LANES:(st + 1) * LANES]
            keys.append(jnp.concatenate([kblk, k_aug], axis=1))
            vals_t.append(jnp.concatenate([vblk.astype(F32).T.astype(BF16), jnp.ones((ONES_ROWS, tk), BF16)], axis=0))

        nstrip = tk // KSTRIP
        wq = HEADS_PER_DOT * tq
        nvh = nstack * nv // HEADS_PER_DOT
        if masked:
            bias = jnp.tile(bias, (1, HEADS_PER_DOT))

        def scores(vh, r):
            st = vh * HEADS_PER_DOT // nv
            s = _dot_nt(keys[st][r * KSTRIP:(r + 1) * KSTRIP, :], qst_ref[vh * wq:(vh + 1) * wq, :])
            if masked:
                s = s + bias[r * KSTRIP:(r + 1) * KSTRIP, :]
            return s

        def colmax(a, s):
            mx = jnp.max(s, axis=0, keepdims=True)
            return mx if a is None else jnp.maximum(a, mx)

        cur, cur_max = [], None
        for r in range(nstrip):
            cur.append(scores(0, r))
            cur_max = colmax(cur_max, cur[r])
        for vh in range(nvh):
            cols = slice(vh * wq, (vh + 1) * wq)
            m_old = m_ref[:, cols]
            m_new = jnp.maximum(m_old, cur_max)
            alpha = jnp.exp2(m_old - m_new)
            nxt, nxt_max, pv = [], None, None
            for r in range(nstrip):
                if vh + 1 < nvh:
                    nxt.append(scores(vh + 1, r))
                    nxt_max = colmax(nxt_max, nxt[r])
                p = jnp.exp2((cur[r] - m_new).astype(BF16))
                part = _dot(vals_t[vh * HEADS_PER_DOT // nv][:, r * KSTRIP:(r + 1) * KSTRIP], p)
                pv = part if pv is None else pv + part
            m_ref[:, cols] = m_new
            acc_ref[:, cols] = alpha * acc_ref[:, cols] + pv
            cur, cur_max = nxt, nxt_max

    if mode == "win":
        pl.when(active)(lambda: step(True))
    else:
        diag = k0 + tk - 1 > q0
        pl.when(active & diag)(lambda: step(True))
        pl.when(active & jnp.logical_not(diag))(lambda: step(False))

    @pl.when(j == nsteps - 1)
    def _():
        inv = 1.0 / acc_ref[LANES:LANES + 1, :]
        low_r = lax.broadcasted_iota(jnp.int32, (LANES, tq), 0) < HEAD_DIM

        def out_t(vh):
            return acc_ref[0:LANES, vh * tq:(vh + 1) * tq] * inv[:, vh * tq:(vh + 1) * tq]

        low = lax.broadcasted_iota(jnp.int32, (tq, LANES), 1) < HEAD_DIM
        if nsa:
            sig = _sigmoid(sm_ref[0])
            br = 1 if mode == "slc" else 2
            for jb in range(hg):
                o = jnp.where(low_r, out_t(jb), out_t(jb + hg)).T
                c0 = br * NSA_HEADS + jb
                gate = jnp.where(low, sig[:, c0:c0 + 1], sig[:, c0 + hg:c0 + hg + 1])
                o_ref[0, :, jb * LANES:(jb + 1) * LANES] = add_ref[0, :, jb * LANES:(jb + 1) * LANES] + o * gate
        elif mode == "fox":
            for pb in range(nstack):
                o = jnp.where(low_r, out_t(2 * pb), out_t(2 * pb + 1)).T
                o_ref[0, :, pb * LANES:(pb + 1) * LANES] = o.astype(o_ref.dtype)
        else:
            lam = lam_ref[...]
            lam_full = (jnp.exp(jnp.sum(lam[0:1, :] * lam[1:2, :], axis=-1, keepdims=True))
                        - jnp.exp(jnp.sum(lam[2:3, :] * lam[3:4, :], axis=-1, keepdims=True)) + lam_init)
            for pb in range(nstack):
                heads = [out_t(pb * nv + 2 * hh) - lam_full * out_t(pb * nv + 2 * hh + 1) for hh in range(2)]
                o = jnp.where(low_r, heads[0], heads[1]).T
                sq = o * o
                ss_lo = jnp.sum(jnp.where(low, sq, 0.0), axis=-1, keepdims=True)
                ss_hi = jnp.sum(jnp.where(low, 0.0, sq), axis=-1, keepdims=True)
                ms = jnp.where(low, ss_lo, ss_hi) * (1.0 / HEAD_DIM)
                y = o * lax.rsqrt(ms + RMS_EPS) * sub_ref[...] * (1.0 - lam_init)
                o_ref[0, :, pb * LANES:(pb + 1) * LANES] = y.astype(o_ref.dtype)


def _flash(main3, mode, extra, out_dtype, lam_init=0.0, tq=TQ, tk=TK):
    b, t, _ = main3.shape
    tk = min(tk, t)
    nq = t // tq
    nstack, nv = STACKS[mode]
    last = lambda i: (i * tq + tq - 1) // tk
    if mode == "win":
        first = lambda i: max(i * tq - (WINDOW - 1), 0) // tk
        nsteps = max((i * tq + tq - 1) // tk - first(i) + 1 for i in range(nq))
        kt = lambda i, j: jnp.maximum(last(i) - (nsteps - 1) + j, 0)
    else:
        nsteps = t // tk
        kt = lambda i, j: jnp.minimum(j, last(i))
    qtile = lambda w, c: pl.BlockSpec((1, tq, w), lambda bi, i, j: (bi, i, c))
    ktile = lambda w, c: pl.BlockSpec((1, tk, w), lambda bi, i, j: (bi, kt(i, j), c))
    if mode in ("slc", "win"):
        kcol, vcol = (COL_KS, COL_VS) if mode == "slc" else (COL_KW, COL_VW)
        in_specs = [qtile(NSA_Q, 0), ktile(LANES, kcol), ktile(LANES, vcol)]
        if mode == "slc":
            in_specs.append(qtile(NSA_KV_GROUPS * LANES, 0))
        in_specs += [qtile(LANES, 0), qtile(NSA_Q, 0)]
        out_w = NSA_Q
    else:
        qc, kc, vc = (COL_FQ, COL_FK, COL_FV) if mode == "fox" else (COL_DQ, COL_DK, COL_DV)
        in_specs = [qtile(2 * LANES, qc), ktile(2 * LANES, kc), ktile(2 * LANES, vc)]
        if mode == "fox":
            in_specs.append(ktile(LANES, 0))
        else:
            in_specs += [pl.BlockSpec((4, DIFF_QK_DIM), lambda bi, i, j: (0, 0)),
                         pl.BlockSpec((1, LANES), lambda bi, i, j: (0, 0))]
        out_w = 2 * LANES
    cols = nstack * nv * tq
    return pl.pallas_call(
        functools.partial(_flash_kernel, mode=mode, tq=tq, tk=tk, nsteps=nsteps, lam_init=lam_init),
        grid=(b, nq, nsteps),
        in_specs=in_specs,
        out_specs=pl.BlockSpec((1, tq, out_w), lambda bi, i, j: (bi, i, 0)),
        out_shape=jax.ShapeDtypeStruct((b, t, out_w), out_dtype),
        scratch_shapes=[pltpu.VMEM((cols, 2 * LANES), BF16),
                        pltpu.VMEM((1, cols), F32),
                        pltpu.VMEM((LANES + ONES_ROWS, cols), F32)],
        compiler_params=_cparams("parallel", "parallel", "arbitrary"),
        name="flash_" + mode,
    )(main3, main3, main3, *extra)


def _outproj_kernel(x_ref, oa_ref, ob_ref, oc_ref, wa_ref, wb_ref, wc_ref, o_ref):
    acc = _dot(oa_ref[...].astype(BF16), wa_ref[...])
    acc = acc + _dot(ob_ref[...], wb_ref[...])
    acc = acc + _dot(oc_ref[...], wc_ref[...])
    o_ref[...] = x_ref[...] + acc


def _outproj(x2, oa, ob, oc, wa, wb, wc, tm=512):
    n = x2.shape[0]
    row = lambda i: (i, 0)
    full = lambda i: (0, 0)
    return pl.pallas_call(
        _outproj_kernel,
        grid=(n // tm,),
        in_specs=[pl.BlockSpec((tm, D_MODEL), row),
                  pl.BlockSpec((tm, NSA_Q), row),
                  pl.BlockSpec((tm, FOX_W), row),
                  pl.BlockSpec((tm, DIFF_W), row),
                  pl.BlockSpec((NSA_Q, D_MODEL), full),
                  pl.BlockSpec((FOX_W, D_MODEL), full),
                  pl.BlockSpec((DIFF_W, D_MODEL), full)],
        out_specs=pl.BlockSpec((tm, D_MODEL), row),
        out_shape=jax.ShapeDtypeStruct((n, D_MODEL), F32),
        compiler_params=_cparams("parallel"),
        name="outproj",
    )(x2, oa, ob, oc, wa, wb, wc)


ROUTE_OFF = N_GROUPS


def _moe_kernel(x_ref, g_ref, wrh_ref, wrl_ref, br_ref, wg_ref, wu_ref, wd_ref, gf_ref, o_ref,
                h_ref, gate_ref, acc_ref, *, ec, nsteps, final_norm):
    e_step = pl.program_id(1)

    @pl.when(e_step == 0)
    def _():
        x = x_ref[...]
        ms = jnp.mean(x * x, axis=-1, keepdims=True)
        y = x * lax.rsqrt(ms + RMS_EPS) * g_ref[...]
        hb, hl = _split2(y)
        h_ref[...] = hb
        lg = (_dot(hb, wrh_ref[...]) + _dot(hl, wrh_ref[...]) + _dot(hb, wrl_ref[...])) + br_ref[...]
        lane = lax.broadcasted_iota(jnp.int32, lg.shape, 1).astype(F32)
        big = float(LANES)
        isg = lane < N_GROUPS
        gmax = jnp.max(jnp.where(isg, lg, -jnp.inf), axis=-1, keepdims=True)
        grp = jnp.min(jnp.where(isg & (lg == gmax), lane, big), axis=-1, keepdims=True)
        gprob = 1.0 / jnp.sum(jnp.where(isg, jnp.exp(lg - gmax), 0.0), axis=-1, keepdims=True)
        lo_lane = ROUTE_OFF + grp * EXPERTS_PER_GROUP
        ing = (lane >= lo_lane) & (lane < lo_lane + EXPERTS_PER_GROUP)
        v1 = jnp.max(jnp.where(ing, lg, -jnp.inf), axis=-1, keepdims=True)
        i1 = jnp.min(jnp.where(ing & (lg == v1), lane, big), axis=-1, keepdims=True)
        rest = ing & (lane != i1)
        v2 = jnp.max(jnp.where(rest, lg, -jnp.inf), axis=-1, keepdims=True)
        i2 = jnp.min(jnp.where(rest & (lg == v2), lane, big), axis=-1, keepdims=True)
        e2 = jnp.exp(v2 - v1)
        w1 = gprob / (1.0 + e2)
        w2 = gprob * e2 / (1.0 + e2)
        gate_ref[...] = jnp.where(lane == i1, w1, 0.0) + jnp.where(lane == i2, w2, 0.0)
        acc_ref[...] = jnp.zeros_like(acc_ref)

    h = h_ref[...]
    acc = acc_ref[...]
    for e in range(ec):
        a = _dot(h, wg_ref[0, e])
        u = _dot(h, wu_ref[0, e])
        act = a * _sigmoid(a) * u
        col = _gate_col(gate_ref, e_step, e, ec)
        acc = acc + _dot((act * col).astype(BF16), wd_ref[0, e])
    acc_ref[...] = acc

    @pl.when(e_step == nsteps - 1)
    def _():
        y = x_ref[...] + acc_ref[...]
        if final_norm:
            ms = jnp.mean(y * y, axis=-1, keepdims=True)
            y = y * lax.rsqrt(ms + RMS_EPS) * gf_ref[...]
        o_ref[...] = y


def _gate_col(gate_ref, e_step, e, ec):
    g = gate_ref[...]
    lane = lax.broadcasted_iota(jnp.int32, g.shape, 1)
    want = ROUTE_OFF + e_step * ec + e
    return jnp.sum(jnp.where(lane == want, g, 0.0), axis=-1, keepdims=True)


def _moe(x2, g, wrh, wrl, br, wg, wu, wd, gf, layer, final_norm, tm=1024, ec=4):
    n = x2.shape[0]
    tm = min(tm, n)
    nsteps = N_EXPERTS // ec
    row = lambda i, e: (i, 0)
    full = lambda i, e: (0, 0)
    wspec_in = pl.BlockSpec((1, ec, D_MODEL, D_EXPERT), lambda i, e: (layer, e, 0, 0))
    wspec_out = pl.BlockSpec((1, ec, D_EXPERT, D_MODEL), lambda i, e: (layer, e, 0, 0))
    return pl.pallas_call(
        functools.partial(_moe_kernel, ec=ec, nsteps=nsteps, final_norm=final_norm),
        grid=(n // tm, nsteps),
        in_specs=[pl.BlockSpec((tm, D_MODEL), row),
                  pl.BlockSpec((1, D_MODEL), full),
                  pl.BlockSpec((D_MODEL, LANES), full),
                  pl.BlockSpec((D_MODEL, LANES), full),
                  pl.BlockSpec((1, LANES), full),
                  wspec_in, wspec_in, wspec_out,
                  pl.BlockSpec((1, D_MODEL), full)],
        out_specs=pl.BlockSpec((tm, D_MODEL), row),
        out_shape=jax.ShapeDtypeStruct((n, D_MODEL), F32),
        scratch_shapes=[pltpu.VMEM((tm, D_MODEL), BF16),
                        pltpu.VMEM((tm, LANES), F32),
                        pltpu.VMEM((tm, D_MODEL), F32)],
        compiler_params=_cparams("parallel", "arbitrary"),
        name="moe",
    )(x2, g, wrh, wrl, br, wg, wu, wd, gf)


def _head_perm():
    hg = NSA_HEADS // NSA_KV_GROUPS
    order = []
    for j in range(hg):
        order += [j, j + hg]
    return np.concatenate([np.arange(h * HEAD_DIM, (h + 1) * HEAD_DIM) for h in order])


def _split_hi_lo(w):
    hi = w.astype(BF16)
    return hi, (w - hi.astype(F32)).astype(BF16)


def _layer_params(l, w_in, b_gate, b_fgt, cmp_pe, cmp_w1, cmp_w2, w_out):
    offs = np.concatenate([[0], np.cumsum(IN_WIDTHS)])
    seg = [w_in[l][:, offs[i]:offs[i + 1]] for i in range(len(IN_WIDTHS))]
    (wq, wkc, wvc, wks, wvs, wkw, wvw, wgt, wfq, wfk, wfv, wff, wdq, wdk, wdv) = seg
    perm = _head_perm()
    gate_perm = np.array([h * 3 + br for br in range(3) for h in range(NSA_HEADS)])
    wm = jnp.concatenate([wq[:, perm] * (HEAD_DIM ** -0.5 * LOG2E), wks, wvs, wkw, wvw,
                          wfq * (HEAD_DIM ** -0.5 * LOG2E), wfk, wfv,
                          wdq * (DIFF_QK_DIM ** -0.5 * LOG2E), wdk, wdv, wkc, wvc], axis=1).astype(BF16)
    pad = LANES - NSA_GATE - 3 * FOX_HEADS
    ws = jnp.concatenate([wgt[:, gate_perm], wff, wff, wff, jnp.zeros((D_MODEL, pad), F32)], axis=1)
    wsh, wsl = _split_hi_lo(ws)
    bs = jnp.concatenate([b_gate[l][gate_perm], b_fgt[l], b_fgt[l], b_fgt[l], jnp.zeros((pad,), F32)])[None, :]

    eye = jnp.eye(NSA_KV_GROUPS, dtype=F32)
    half = CMP_BLOCK // 2
    cmp = []
    for pe, w1, w2 in zip(cmp_pe, cmp_w1, cmp_w2):
        w1r = w1[l].reshape(2, half, HEAD_DIM, CMP_HIDDEN)
        w1x = jnp.einsum('srdk,gh->srgdhk', w1r, eye).reshape(2, half * NSA_KV, NSA_KV_GROUPS * CMP_HIDDEN)
        w2bd = jnp.einsum('kd,gh->gkhd', w2[l], eye).reshape(NSA_KV_GROUPS * CMP_HIDDEN, NSA_KV)
        pe2 = jnp.broadcast_to(pe[l].reshape(2, half, 1, HEAD_DIM), (2, half, NSA_KV_GROUPS, HEAD_DIM))
        cmp.append((pe2.reshape(2, half * NSA_KV), w1x[0].astype(BF16), w1x[1].astype(BF16), w2bd.astype(BF16)))

    wo = w_out[l]
    wa = wo[:NSA_Q][perm].astype(BF16)
    wb = wo[NSA_Q:NSA_Q + FOX_W].astype(BF16)
    wc = wo[NSA_Q + FOX_W:].astype(BF16)
    return dict(wm=wm, wsh=wsh, wsl=wsl, bs=bs, cmp_k=cmp[0], cmp_v=cmp[1], wa=wa, wb=wb, wc=wc)


def _overlap_matrix(n_slab, n_cmp, n_sel):
    c_start = np.arange(n_cmp) * CMP_STRIDE
    s_start = np.arange(n_sel) * SLC_BLOCK
    ov = np.clip(np.minimum(c_start[:, None] + CMP_BLOCK, s_start[None, :] + SLC_BLOCK)
                 - np.maximum(c_start[:, None], s_start[None, :]), 0, None) / CMP_BLOCK
    full = np.zeros((n_slab, LANES), np.float32)
    full[:n_cmp, :n_sel] = ov
    return jnp.asarray(full, dtype=BF16)


def kernel(x, norm_attn, w_in, b_gate, b_fgt, cmp_k_pe, cmp_k_w1, cmp_k_w2, cmp_v_pe, cmp_v_w1, cmp_v_w2, diff_lambda, diff_subln, w_out, norm_ffn, w_grp, b_grp, w_exp, b_exp, w_e_gate, w_e_up, w_e_down, norm_final):
    b, t, d = x.shape
    depth = w_in.shape[0]
    n = b * t
    n_slab = t // CMP_STRIDE
    n_cmp = (t - CMP_BLOCK) // CMP_STRIDE + 1
    n_sel = t // SLC_BLOCK
    top_n = min(SLC_TOPK, n_sel)
    ov = _overlap_matrix(n_slab, n_cmp, n_sel)

    wg_all = w_e_gate.astype(BF16)
    wu_all = w_e_up.astype(BF16)
    wd_all = w_e_down.astype(BF16)
    gf = norm_final[None, :]

    x2 = x.reshape(n, d)
    for l in range(depth):
        p = _layer_params(l, w_in, b_gate, b_fgt, (cmp_k_pe, cmp_v_pe), (cmp_k_w1, cmp_v_w1),
                          (cmp_k_w2, cmp_v_w2), w_out)
        main, kc, vc, small = _inproj(x2, norm_attn[l][None, :], p["wm"], p["wsh"], p["wsl"], p["bs"])
        main3 = main.reshape(b, t, MAIN_W)
        small3 = small.reshape(b, t, LANES)
        kcmp = _compress(kc.reshape(b, n_slab, CMP_STRIDE * NSA_KV), *p["cmp_k"])
        vcmp = _compress(vc.reshape(b, n_slab, CMP_STRIDE * NSA_KV), *p["cmp_v"])
        o_nsa, pen = _cmp_attn(main3, kcmp, vcmp, small3, ov, n_cmp, n_sel, top_n)
        o_nsa = _flash(main3, "slc", (pen, small3, o_nsa), F32)
        o_nsa = _flash(main3, "win", (small3, o_nsa), F32)
        o_fox = _flash(main3, "fox", (_cumgate(small3),), BF16)
        sub2 = jnp.concatenate([diff_subln[l], diff_subln[l]])[None, :]
        o_diff = _flash(main3, "diff", (diff_lambda[l], sub2), BF16, lam_init=_lambda_init(l))

        x2 = _outproj(x2, o_nsa.reshape(n, NSA_Q), o_fox.reshape(n, FOX_W), o_diff.reshape(n, DIFF_W),
                      p["wa"], p["wb"], p["wc"])

        wr = jnp.concatenate([w_grp[l], w_exp[l], jnp.zeros((d, LANES - N_GROUPS - N_EXPERTS), F32)], axis=1)
        wrh, wrl = _split_hi_lo(wr)
        br = jnp.concatenate([b_grp[l], b_exp[l], jnp.zeros((LANES - N_GROUPS - N_EXPERTS,), F32)])[None, :]
        x2 = _moe(x2, norm_ffn[l][None, :], wrh, wrl, br, wg_all, wu_all, wd_all, gf, l,
                  final_norm=(l == depth - 1))
    return x2.reshape(b, t, d)
```

```python
import functools
import math

import ml_dtypes
import numpy as np
import jax
import jax.numpy as jnp
from jax import lax
from jax.experimental import pallas as pl
from jax.experimental.pallas import tpu as pltpu

F32 = jnp.float32
BF16 = jnp.bfloat16

D_MODEL = 1024
HEAD_DIM = 64
NSA_HEADS = 8
NSA_KV_GROUPS = 2
CMP_BLOCK = 32
CMP_STRIDE = 16
CMP_HIDDEN = 256
SLC_BLOCK = 64
SLC_SHIFT = 6
SLC_TOPK = 16
WINDOW = 512
FOX_HEADS = 4
DIFF_HEADS = 4
DIFF_QK_DIM = HEAD_DIM // 2
N_GROUPS = 4
EXPERTS_PER_GROUP = 8
N_EXPERTS = N_GROUPS * EXPERTS_PER_GROUP
D_EXPERT = 256
RMS_EPS = 1e-6
FORCE_SCORE = 1e4
NEG_INF = -1e30
MASKED = -2e30

NSA_Q = NSA_HEADS * HEAD_DIM
NSA_KV = NSA_KV_GROUPS * HEAD_DIM
NSA_GATE = NSA_HEADS * 3
FOX_W = FOX_HEADS * HEAD_DIM
DIFF_W = DIFF_HEADS * HEAD_DIM
MIX_WIDTH = NSA_Q + FOX_W + DIFF_W
IN_WIDTHS = (NSA_Q, NSA_KV, NSA_KV, NSA_KV, NSA_KV, NSA_KV, NSA_KV, NSA_GATE,
             FOX_W, FOX_W, FOX_W, FOX_HEADS, DIFF_W, DIFF_W, DIFF_W)

LANES = 128
VMEM_LIMIT_BYTES = 56 * 1024 * 1024

MAIN_W = NSA_Q + 4 * NSA_KV + 3 * FOX_W + 3 * DIFF_W
PROJ_W = MAIN_W + 2 * NSA_KV
COL_KS, COL_VS, COL_KW, COL_VW = 4, 5, 6, 7
COL_FQ, COL_FK, COL_FV = 4, 5, 6
COL_DQ, COL_DK, COL_DV = 7, 8, 9
FGT_LANES = (NSA_GATE, NSA_GATE + FOX_HEADS, NSA_GATE + 2 * FOX_HEADS)
LOG2E = 1.4426950408889634

TQ = 512
TK = 512
KSTRIP = 256
DOT_LANES = 512
ONES_ROWS = 16

def _alibi(n):
    return [float(2.0 ** (-8.0 * (i + 1) / n)) for i in range(n)]


def _lambda_init(layer):
    return 0.8 - 0.6 * math.exp(-0.3 * layer)


def _cparams(*sem):
    return pltpu.CompilerParams(dimension_semantics=sem, vmem_limit_bytes=VMEM_LIMIT_BYTES)


def _dot(a, b):
    return jnp.dot(a, b, preferred_element_type=F32)


def _dot_nt(a, b):
    return lax.dot_general(a, b, (((1,), (1,)), ((), ())), preferred_element_type=F32)


def _split2(x):
    hi = x.astype(BF16)
    lo = (x - hi.astype(F32)).astype(BF16)
    return hi, lo


def _sigmoid(z):
    return 1.0 / (1.0 + jnp.exp(-z))


def _keep_lanes(sel, blk):
    return jnp.where(sel, blk.astype(F32), 0.0).astype(BF16)


def _inproj_kernel(x_ref, g_ref, wm_ref, wsh_ref, wsl_ref, bs_ref, main_ref, kc_ref, vc_ref, small_ref):
    x = x_ref[...]
    ms = jnp.mean(x * x, axis=-1, keepdims=True)
    y = x * lax.rsqrt(ms + RMS_EPS) * g_ref[...]
    hb, hl = _split2(y)
    full = _dot(hb, wm_ref[...]).astype(BF16)
    main_ref[...] = full[:, :MAIN_W]
    kc_ref[...] = full[:, MAIN_W:MAIN_W + NSA_KV]
    vc_ref[...] = full[:, MAIN_W + NSA_KV:]
    small_ref[...] = (_dot(hb, wsh_ref[...]) + _dot(hl, wsh_ref[...]) + _dot(hb, wsl_ref[...])) + bs_ref[...]


def _inproj(x2, g, wm, wsh, wsl, bs, tm=512):
    n = x2.shape[0]
    full = lambda i: (0, 0)
    return pl.pallas_call(
        _inproj_kernel,
        grid=(n // tm,),
        in_specs=[pl.BlockSpec((tm, D_MODEL), lambda i: (i, 0)),
                  pl.BlockSpec((1, D_MODEL), full),
                  pl.BlockSpec((D_MODEL, PROJ_W), full),
                  pl.BlockSpec((D_MODEL, LANES), full),
                  pl.BlockSpec((D_MODEL, LANES), full),
                  pl.BlockSpec((1, LANES), full)],
        out_specs=[pl.BlockSpec((tm, MAIN_W), lambda i: (i, 0)),
                   pl.BlockSpec((tm, NSA_KV), lambda i: (i, 0)),
                   pl.BlockSpec((tm, NSA_KV), lambda i: (i, 0)),
                   pl.BlockSpec((tm, LANES), lambda i: (i, 0))],
        out_shape=[jax.ShapeDtypeStruct((n, MAIN_W), BF16),
                   jax.ShapeDtypeStruct((n, NSA_KV), BF16),
                   jax.ShapeDtypeStruct((n, NSA_KV), BF16),
                   jax.ShapeDtypeStruct((n, LANES), F32)],
        compiler_params=_cparams("parallel"),
        name="inproj",
    )(x2, g, wm, wsh, wsl, bs)


def _compress_kernel(r_ref, pe_ref, w1a_ref, w1b_ref, w2_ref, o_ref):
    r = r_ref[0].astype(F32)
    pe = pe_ref[...]
    ra = (r + pe[0:1, :]).astype(BF16)
    rb = (r + pe[1:2, :]).astype(BF16)
    a = _dot(ra, w1a_ref[...])
    b = _dot(rb, w1b_ref[...])
    n_slab = a.shape[0]
    hid = a + pltpu.roll(b, n_slab - 1, 0)
    hid = jax.nn.gelu(hid)
    o_ref[0] = _dot(hid.astype(BF16), w2_ref[...]).astype(BF16)


def _compress(r, pe2, w1a, w1b, w2bd):
    b, n_slab, w = r.shape
    full = lambda i: (0, 0)
    return pl.pallas_call(
        _compress_kernel,
        grid=(b,),
        in_specs=[pl.BlockSpec((1, n_slab, w), lambda i: (i, 0, 0)),
                  pl.BlockSpec((2, w), full),
                  pl.BlockSpec((w, 2 * CMP_HIDDEN), full),
                  pl.BlockSpec((w, 2 * CMP_HIDDEN), full),
                  pl.BlockSpec((2 * CMP_HIDDEN, LANES), full)],
        out_specs=pl.BlockSpec((1, n_slab, LANES), lambda i: (i, 0, 0)),
        out_shape=jax.ShapeDtypeStruct((b, n_slab, LANES), BF16),
        compiler_params=_cparams("parallel"),
        name="compress",
    )(r, pe2, w1a, w1b, w2bd)


def _cumgate_kernel(z_ref, o_ref, carry_ref, *, tc):
    @pl.when(pl.program_id(1) == 0)
    def _():
        carry_ref[...] = jnp.zeros_like(carry_ref)

    z = z_ref[0]
    logf = -(jnp.maximum(-z, 0.0) + jnp.log(1.0 + jnp.exp(-jnp.abs(z))))
    hi = logf.astype(BF16)
    r1 = logf - hi.astype(F32)
    mid = r1.astype(BF16)
    lo = (r1 - mid.astype(F32)).astype(BF16)
    tri = (lax.broadcasted_iota(jnp.int32, (tc, tc), 0) >= lax.broadcasted_iota(jnp.int32, (tc, tc), 1))
    tri = jnp.where(tri, 1.0, 0.0).astype(BF16)
    c = (_dot(tri, hi) + _dot(tri, mid)) + _dot(tri, lo) + carry_ref[0:1, :]
    carry_ref[...] = jnp.broadcast_to(c[tc - 1:tc, :], carry_ref.shape)
    v = c * (-LOG2E)
    p0 = v.astype(BF16)
    r1 = v - p0.astype(F32)
    p1 = r1.astype(BF16)
    p2 = (r1 - p1.astype(F32)).astype(BF16)
    lane = lax.broadcasted_iota(jnp.int32, v.shape, 1)
    pieces = jnp.where(lane < FGT_LANES[1], p0.astype(F32), jnp.where(lane < FGT_LANES[2], p1.astype(F32), p2.astype(F32)))
    keep = (lane >= FGT_LANES[0]) & (lane < FGT_LANES[2] + FOX_HEADS)
    o_ref[0] = jnp.where(keep, pieces, 0.0).astype(BF16)


def _cumgate(small3, tc=256):
    b, t, w = small3.shape
    return pl.pallas_call(
        functools.partial(_cumgate_kernel, tc=tc),
        grid=(b, t // tc),
        in_specs=[pl.BlockSpec((1, tc, w), lambda i, j: (i, j, 0))],
        out_specs=pl.BlockSpec((1, tc, w), lambda i, j: (i, j, 0)),
        out_shape=jax.ShapeDtypeStruct((b, t, w), BF16),
        scratch_shapes=[pltpu.VMEM((8, w), F32)],
        compiler_params=_cparams("parallel", "arbitrary"),
        name="cumgate",
    )(small3)


def _cmp_kernel(q_ref, kc_ref, vc_ref, sm_ref, ov_ref, o_ref, pen_ref, *, tq, n_cmp, n_sel, top_n):
    t0 = pl.program_id(1) * tq
    nck = kc_ref.shape[1]
    row = t0 + lax.broadcasted_iota(jnp.int32, (tq, nck), 0)
    col = lax.broadcasted_iota(jnp.int32, (tq, nck), 1)
    dist = row - (col * CMP_STRIDE + (CMP_BLOCK - 1))
    mask = (dist >= 0) & (col < n_cmp)
    distf = dist.astype(F32)
    lane = lax.broadcasted_iota(jnp.int32, (tq, LANES), 1)
    low = lane < HEAD_DIM
    kc = kc_ref[0]
    vc = vc_ref[0]
    sig = _sigmoid(sm_ref[0])
    slopes = _alibi(NSA_HEADS)
    psum = [None, None]
    hg = NSA_HEADS // NSA_KV_GROUPS
    for j in range(hg):
        qblk = q_ref[0, :, j * LANES:(j + 1) * LANES]
        outs = []
        for g in range(NSA_KV_GROUPS):
            h = j + hg * g
            qm = _keep_lanes(low if g == 0 else ~low, qblk)
            s = _dot_nt(qm, kc) - (slopes[h] * LOG2E) * distf
            s = jnp.where(mask, s, NEG_INF)
            m = jnp.max(s, axis=-1, keepdims=True)
            p = jnp.where(mask, jnp.exp2(s - m), 0.0)
            l = jnp.sum(p, axis=-1, keepdims=True)
            p = p * jnp.where(l > 0.0, 1.0 / l, 0.0)
            psum[g] = p if psum[g] is None else psum[g] + p
            outs.append(_dot(p.astype(BF16), vc))
        gate = jnp.where(low, sig[:, j:j + 1], sig[:, j + hg:j + hg + 1])
        o_ref[0, :, j * LANES:(j + 1) * LANES] = jnp.where(low, outs[0], outs[1]) * gate

    rows_t = lax.broadcasted_iota(jnp.int32, (n_sel, tq), 0)
    cur_r = (t0 + lax.broadcasted_iota(jnp.int32, (tq, LANES), 0)) >> SLC_SHIFT
    pen_t = []
    for g in range(NSA_KV_GROUPS):
        ph, plo = _split2(psum[g])
        imp = _dot(ph, ov_ref[...]) + _dot(plo, ov_ref[...])
        valid = lane <= cur_r
        forced = (lane == 0) | (lane == cur_r) | (lane == cur_r - 1)
        imp = jnp.where(valid, jnp.where(forced, FORCE_SCORE, imp), NEG_INF)
        imp_t = imp.T[:n_sel, :]
        cnt = jnp.zeros((n_sel, tq), F32)
        for k in range(n_sel):
            rk = imp_t[k:k + 1, :]
            cnt = cnt + jnp.where(rows_t > k, jnp.where(rk >= imp_t, 1.0, 0.0), jnp.where(rk > imp_t, 1.0, 0.0))
        pen_t.append(jnp.where(cnt < float(top_n), 0.0, MASKED))
    for g in range(NSA_KV_GROUPS):
        full = jnp.concatenate([pen_t[g], jnp.zeros((LANES - n_sel, tq), F32)], axis=0)
        pen_ref[0, :, g * LANES:(g + 1) * LANES] = full.T.astype(BF16)


def _cmp_attn(main3, kcmp, vcmp, small3, ov, n_cmp, n_sel, top_n, tq=256):
    b, t, _ = main3.shape
    nck = kcmp.shape[1]
    return pl.pallas_call(
        functools.partial(_cmp_kernel, tq=tq, n_cmp=n_cmp, n_sel=n_sel, top_n=top_n),
        grid=(b, t // tq),
        in_specs=[pl.BlockSpec((1, tq, NSA_Q), lambda i, j: (i, j, 0)),
                  pl.BlockSpec((1, nck, LANES), lambda i, j: (i, 0, 0)),
                  pl.BlockSpec((1, nck, LANES), lambda i, j: (i, 0, 0)),
                  pl.BlockSpec((1, tq, LANES), lambda i, j: (i, j, 0)),
                  pl.BlockSpec((nck, LANES), lambda i, j: (0, 0))],
        out_specs=[pl.BlockSpec((1, tq, NSA_Q), lambda i, j: (i, j, 0)),
                   pl.BlockSpec((1, tq, NSA_KV_GROUPS * LANES), lambda i, j: (i, j, 0))],
        out_shape=[jax.ShapeDtypeStruct((b, t, NSA_Q), F32),
                   jax.ShapeDtypeStruct((b, t, NSA_KV_GROUPS * LANES), BF16)],
        compiler_params=_cparams("parallel", "parallel"),
        name="cmp_attn",
    )(main3, kcmp, vcmp, small3, ov)


def _bf16_terms(c, n=3):
    out, r = [], float(c)
    for _ in range(n):
        p = float(np.float32(r).astype(ml_dtypes.bfloat16))
        out.append(p)
        r -= p
    return out


def _alibi_q_aug(lane, a0, slope):
    aug = jnp.zeros(lane.shape, F32)
    for i, c in enumerate(_bf16_terms(slope * LOG2E)):
        aug = jnp.where(lane == a0 + 2 * i, float(SLC_BLOCK) * c, aug)
        aug = jnp.where(lane == a0 + 2 * i + 1, c, aug)
    return aug


def _alibi_k_aug(lane, rel, a0):
    hi = (rel >> SLC_SHIFT).astype(F32)
    lo = (rel & (SLC_BLOCK - 1)).astype(F32)
    inside = (lane >= a0) & (lane < a0 + 6)
    odd = ((lane - a0) & 1) == 1
    return jnp.where(inside, jnp.where(odd, lo, hi), 0.0)


ALIBI_LANE = {"slc": SLC_BLOCK, "win": 0, "diff": 0}
STACKS = {"slc": (1, 8), "win": (1, 8), "fox": (2, 2), "diff": (2, 4)}


def _flash_kernel(*refs, mode, tq, tk, lam_init):
    sched_ref, refs = refs[0], refs[1:]
    if mode == "slc":
        q_ref, k_ref, v_ref, pen_ref, sm_ref, add_ref, o_ref, qst_ref, m_ref, acc_ref = refs
    elif mode == "win":
        q_ref, k_ref, v_ref, sm_ref, add_ref, o_ref, qst_ref, m_ref, acc_ref = refs
    elif mode == "fox":
        q_ref, k_ref, v_ref, fa_ref, o_ref, qst_ref, m_ref, acc_ref = refs
    else:
        q_ref, k_ref, v_ref, lam_ref, sub_ref, o_ref, qst_ref, m_ref, acc_ref = refs
    nstack, nv = STACKS[mode]
    nsa = mode in ("slc", "win")
    hg = NSA_HEADS // NSA_KV_GROUPS
    step_id = pl.program_id(1)
    q0 = sched_ref[SCHED_Q, step_id] * tq
    k0 = sched_ref[SCHED_K, step_id] * tk

    @pl.when(sched_ref[SCHED_FIRST, step_id] == 1)
    def _():
        m_ref[...] = jnp.full(m_ref.shape, NEG_INF, F32)
        acc_ref[...] = jnp.zeros(acc_ref.shape, F32)
        lane = lax.broadcasted_iota(jnp.int32, (tq, LANES), 1)
        for vh in range(nstack * nv):
            if nsa:
                jb, g = vh % hg, vh // hg
                lo_lane, width = g * HEAD_DIM, HEAD_DIM
                aug = _alibi_q_aug(lane, ALIBI_LANE[mode], _alibi(NSA_HEADS)[vh])
                if mode == "slc":
                    aug = aug + pen_ref[0, :, g * LANES:(g + 1) * LANES].astype(F32)
            elif mode == "fox":
                jb, r = vh // nv, vh % nv
                lo_lane, width = r * HEAD_DIM, HEAD_DIM
                hit = (lane == FGT_LANES[0] + vh) | (lane == FGT_LANES[1] + vh) | (lane == FGT_LANES[2] + vh)
                aug = jnp.where(hit, 1.0, 0.0)
            else:
                jb, r = vh // nv, vh % nv
                lo_lane, width = r * DIFF_QK_DIM, DIFF_QK_DIM
                aug = _alibi_q_aug(lane, ALIBI_LANE[mode], _alibi(DIFF_HEADS)[vh // 2])
            sel = (lane >= lo_lane) & (lane < lo_lane + width)
            qst_ref[vh * tq:(vh + 1) * tq, 0:LANES] = _keep_lanes(sel, q_ref[0, :, jb * LANES:(jb + 1) * LANES])
            qst_ref[vh * tq:(vh + 1) * tq, LANES:2 * LANES] = aug.astype(BF16)

    def step(masked):
        lane_k = lax.broadcasted_iota(jnp.int32, (tk, LANES), 1)
        row_k = lax.broadcasted_iota(jnp.int32, (tk, LANES), 0)
        if mode == "fox":
            k_aug = fa_ref[0]
        else:
            ka = _alibi_k_aug(lane_k, (k0 - q0) + row_k, ALIBI_LANE[mode])
            if mode == "slc":
                ka = ka + jnp.where(((k0 + row_k) >> SLC_SHIFT) == lane_k, 1.0, 0.0)
            k_aug = ka.astype(BF16)
        if masked:
            dist = (q0 - k0) + lax.broadcasted_iota(jnp.int32, (tk, tq), 1) - lax.broadcasted_iota(jnp.int32, (tk, tq), 0)
            ok = dist >= 0
            if mode == "win":
                ok = ok & (dist < WINDOW)
            bias = jnp.where(ok, 0.0, MASKED)
        keys, vals_t = [], []
        for st in range(nstack):
            kblk = k_ref[0] if nsa else k_ref[0, :, st * LANES:(st + 1) * LANES]
            vblk = v_ref[0] if nsa else v_ref[0, :, st * LANES:(st + 1) * LANES]
            keys.append(jnp.concatenate([kblk, k_aug], axis=1))
            vals_t.append(jnp.concatenate([vblk.astype(F32).T.astype(BF16), jnp.ones((ONES_ROWS, tk), BF16)], axis=0))

        nstrip = tk // KSTRIP
        hpd = max(1, DOT_LANES // tq)
        wq = hpd * tq
        nvh = nstack * nv // hpd
        if masked:
            bias = jnp.tile(bias, (1, hpd))

        def scores(vh, r):
            st = vh * hpd // nv
            s = _dot_nt(keys[st][r * KSTRIP:(r + 1) * KSTRIP, :], qst_ref[vh * wq:(vh + 1) * wq, :])
            if masked:
                s = s + bias[r * KSTRIP:(r + 1) * KSTRIP, :]
            return s

        def colmax(a, s):
            mx = jnp.max(s, axis=0, keepdims=True)
            return mx if a is None else jnp.maximum(a, mx)

        cur, cur_max = [], None
        for r in range(nstrip):
            cur.append(scores(0, r))
            cur_max = colmax(cur_max, cur[r])
        for vh in range(nvh):
            cols = slice(vh * wq, (vh + 1) * wq)
            m_old = m_ref[:, cols]
            m_new = jnp.maximum(m_old, cur_max)
            alpha = jnp.exp2(m_old - m_new)
            nxt, nxt_max, pv = [], None, None
            for r in range(nstrip):
                if vh + 1 < nvh:
                    nxt.append(scores(vh + 1, r))
                    nxt_max = colmax(nxt_max, nxt[r])
                p = jnp.exp2((cur[r] - m_new).astype(BF16))
                part = _dot(vals_t[vh * hpd // nv][:, r * KSTRIP:(r + 1) * KSTRIP], p)
                pv = part if pv is None else pv + part
            m_ref[:, cols] = m_new
            acc_ref[:, cols] = alpha * acc_ref[:, cols] + pv
            cur, cur_max = nxt, nxt_max

    pl.when(sched_ref[SCHED_MASKED, step_id] == 1)(lambda: step(True))
    if mode != "win":
        pl.when(sched_ref[SCHED_MASKED, step_id] == 0)(lambda: step(False))

    @pl.when(sched_ref[SCHED_LAST, step_id] == 1)
    def _():
        inv = 1.0 / acc_ref[LANES:LANES + 1, :]
        low_r = lax.broadcasted_iota(jnp.int32, (LANES, tq), 0) < HEAD_DIM

        def out_t(vh):
            return acc_ref[0:LANES, vh * tq:(vh + 1) * tq] * inv[:, vh * tq:(vh + 1) * tq]

        low = lax.broadcasted_iota(jnp.int32, (tq, LANES), 1) < HEAD_DIM
        if nsa:
            sig = _sigmoid(sm_ref[0])
            br = 1 if mode == "slc" else 2
            for jb in range(hg):
                o = jnp.where(low_r, out_t(jb), out_t(jb + hg)).T
                c0 = br * NSA_HEADS + jb
                gate = jnp.where(low, sig[:, c0:c0 + 1], sig[:, c0 + hg:c0 + hg + 1])
                o_ref[0, :, jb * LANES:(jb + 1) * LANES] = add_ref[0, :, jb * LANES:(jb + 1) * LANES] + o * gate
        elif mode == "fox":
            for pb in range(nstack):
                o = jnp.where(low_r, out_t(2 * pb), out_t(2 * pb + 1)).T
                o_ref[0, :, pb * LANES:(pb + 1) * LANES] = o.astype(o_ref.dtype)
        else:
            lam = lam_ref[...]
            lam_full = (jnp.exp(jnp.sum(lam[0:1, :] * lam[1:2, :], axis=-1, keepdims=True))
                        - jnp.exp(jnp.sum(lam[2:3, :] * lam[3:4, :], axis=-1, keepdims=True)) + lam_init)
            for pb in range(nstack):
                heads = [out_t(pb * nv + 2 * hh) - lam_full * out_t(pb * nv + 2 * hh + 1) for hh in range(2)]
                o = jnp.where(low_r, heads[0], heads[1]).T
                sq = o * o
                ss_lo = jnp.sum(jnp.where(low, sq, 0.0), axis=-1, keepdims=True)
                ss_hi = jnp.sum(jnp.where(low, 0.0, sq), axis=-1, keepdims=True)
                ms = jnp.where(low, ss_lo, ss_hi) * (1.0 / HEAD_DIM)
                y = o * lax.rsqrt(ms + RMS_EPS) * sub_ref[...] * (1.0 - lam_init)
                o_ref[0, :, pb * LANES:(pb + 1) * LANES] = y.astype(o_ref.dtype)


def _flash(main3, mode, extra, out_dtype, lam_init=0.0, tq=TQ, tk=TK):
    b, t, _ = main3.shape
    tk = min(tk, t)
    nstack, nv = STACKS[mode]
    sched = _flash_schedule(t, tq, tk, mode == "win")
    qtile = lambda w, c: pl.BlockSpec((1, tq, w), lambda bi, s, sch: (bi, sch[SCHED_Q, s], c))
    ktile = lambda w, c: pl.BlockSpec((1, tk, w), lambda bi, s, sch: (bi, sch[SCHED_K, s], c))
    if mode in ("slc", "win"):
        kcol, vcol = (COL_KS, COL_VS) if mode == "slc" else (COL_KW, COL_VW)
        in_specs = [qtile(NSA_Q, 0), ktile(LANES, kcol), ktile(LANES, vcol)]
        if mode == "slc":
            in_specs.append(qtile(NSA_KV_GROUPS * LANES, 0))
        in_specs += [qtile(LANES, 0), qtile(NSA_Q, 0)]
        out_w = NSA_Q
    else:
        qc, kc, vc = (COL_FQ, COL_FK, COL_FV) if mode == "fox" else (COL_DQ, COL_DK, COL_DV)
        in_specs = [qtile(2 * LANES, qc), ktile(2 * LANES, kc), ktile(2 * LANES, vc)]
        if mode == "fox":
            in_specs.append(ktile(LANES, 0))
        else:
            in_specs += [pl.BlockSpec((4, DIFF_QK_DIM), lambda bi, s, sch: (0, 0)),
                         pl.BlockSpec((1, LANES), lambda bi, s, sch: (0, 0))]
        out_w = 2 * LANES
    cols = nstack * nv * tq
    return pl.pallas_call(
        functools.partial(_flash_kernel, mode=mode, tq=tq, tk=tk, lam_init=lam_init),
        grid_spec=pltpu.PrefetchScalarGridSpec(
            num_scalar_prefetch=1,
            grid=(b, sched.shape[1]),
            in_specs=in_specs,
            out_specs=qtile(out_w, 0),
            scratch_shapes=[pltpu.VMEM((cols, 2 * LANES), BF16),
                            pltpu.VMEM((1, cols), F32),
                            pltpu.VMEM((LANES + ONES_ROWS, cols), F32)]),
        out_shape=jax.ShapeDtypeStruct((b, t, out_w), out_dtype),
        compiler_params=_cparams("parallel", "arbitrary"),
        name="flash_" + mode,
    )(jnp.asarray(sched), main3, main3, main3, *extra)


SCHED_Q, SCHED_K, SCHED_FIRST, SCHED_LAST, SCHED_MASKED = range(5)


def _flash_schedule(t, tq, tk, window):
    rows = []
    for i in range(t // tq):
        q_lo, q_hi = i * tq, i * tq + tq - 1
        k_first = max(q_lo - (WINDOW - 1), 0) // tk if window else 0
        tiles = list(range(k_first, q_hi // tk + 1))
        for n, kt in enumerate(tiles):
            fully_visible = kt * tk + tk - 1 <= q_lo and not window
            rows.append((i, kt, int(n == 0), int(n == len(tiles) - 1), int(not fully_visible)))
    return np.asarray(rows, np.int32).T


def _outproj_kernel(x_ref, oa_ref, ob_ref, oc_ref, wa_ref, wb_ref, wc_ref, o_ref):
    acc = _dot(oa_ref[...].astype(BF16), wa_ref[...])
    acc = acc + _dot(ob_ref[...], wb_ref[...])
    acc = acc + _dot(oc_ref[...], wc_ref[...])
    o_ref[...] = x_ref[...] + acc


def _outproj(x2, oa, ob, oc, wa, wb, wc, tm=512):
    n = x2.shape[0]
    row = lambda i: (i, 0)
    full = lambda i: (0, 0)
    return pl.pallas_call(
        _outproj_kernel,
        grid=(n // tm,),
        in_specs=[pl.BlockSpec((tm, D_MODEL), row),
                  pl.BlockSpec((tm, NSA_Q), row),
                  pl.BlockSpec((tm, FOX_W), row),
                  pl.BlockSpec((tm, DIFF_W), row),
                  pl.BlockSpec((NSA_Q, D_MODEL), full),
                  pl.BlockSpec((FOX_W, D_MODEL), full),
                  pl.BlockSpec((DIFF_W, D_MODEL), full)],
        out_specs=pl.BlockSpec((tm, D_MODEL), row),
        out_shape=jax.ShapeDtypeStruct((n, D_MODEL), F32),
        compiler_params=_cparams("parallel"),
        name="outproj",
    )(x2, oa, ob, oc, wa, wb, wc)


ROUTE_OFF = N_GROUPS


def _moe_kernel(x_ref, g_ref, wrh_ref, wrl_ref, br_ref, wg_ref, wu_ref, wd_ref, gf_ref, o_ref,
                h_ref, gate_ref, acc_ref, *, ec, nsteps, final_norm):
    e_step = pl.program_id(1)

    @pl.when(e_step == 0)
    def _():
        x = x_ref[...]
        ms = jnp.mean(x * x, axis=-1, keepdims=True)
        y = x * lax.rsqrt(ms + RMS_EPS) * g_ref[...]
        hb, hl = _split2(y)
        h_ref[...] = hb
        lg = (_dot(hb, wrh_ref[...]) + _dot(hl, wrh_ref[...]) + _dot(hb, wrl_ref[...])) + br_ref[...]
        lane = lax.broadcasted_iota(jnp.int32, lg.shape, 1).astype(F32)
        big = float(LANES)
        isg = lane < N_GROUPS
        gmax = jnp.max(jnp.where(isg, lg, -jnp.inf), axis=-1, keepdims=True)
        grp = jnp.min(jnp.where(isg & (lg == gmax), lane, big), axis=-1, keepdims=True)
        gprob = 1.0 / jnp.sum(jnp.where(isg, jnp.exp(lg - gmax), 0.0), axis=-1, keepdims=True)
        lo_lane = ROUTE_OFF + grp * EXPERTS_PER_GROUP
        ing = (lane >= lo_lane) & (lane < lo_lane + EXPERTS_PER_GROUP)
        v1 = jnp.max(jnp.where(ing, lg, -jnp.inf), axis=-1, keepdims=True)
        i1 = jnp.min(jnp.where(ing & (lg == v1), lane, big), axis=-1, keepdims=True)
        rest = ing & (lane != i1)
        v2 = jnp.max(jnp.where(rest, lg, -jnp.inf), axis=-1, keepdims=True)
        i2 = jnp.min(jnp.where(rest & (lg == v2), lane, big), axis=-1, keepdims=True)
        e2 = jnp.exp(v2 - v1)
        w1 = gprob / (1.0 + e2)
        w2 = gprob * e2 / (1.0 + e2)
        gate_ref[...] = jnp.where(lane == i1, w1, 0.0) + jnp.where(lane == i2, w2, 0.0)
        acc_ref[...] = jnp.zeros_like(acc_ref)

    h = h_ref[...]
    acc = acc_ref[...]
    for e in range(ec):
        a = _dot(h, wg_ref[0, e])
        u = _dot(h, wu_ref[0, e])
        act = a * _sigmoid(a) * u
        col = _gate_col(gate_ref, e_step, e, ec)
        acc = acc + _dot((act * col).astype(BF16), wd_ref[0, e])
    acc_ref[...] = acc

    @pl.when(e_step == nsteps - 1)
    def _():
        y = x_ref[...] + acc_ref[...]
        if final_norm:
            ms = jnp.mean(y * y, axis=-1, keepdims=True)
            y = y * lax.rsqrt(ms + RMS_EPS) * gf_ref[...]
        o_ref[...] = y


def _gate_col(gate_ref, e_step, e, ec):
    g = gate_ref[...]
    lane = lax.broadcasted_iota(jnp.int32, g.shape, 1)
    want = ROUTE_OFF + e_step * ec + e
    return jnp.sum(jnp.where(lane == want, g, 0.0), axis=-1, keepdims=True)


def _moe(x2, g, wrh, wrl, br, wg, wu, wd, gf, layer, final_norm, tm=1024, ec=4):
    n = x2.shape[0]
    tm = min(tm, n)
    nsteps = N_EXPERTS // ec
    row = lambda i, e: (i, 0)
    full = lambda i, e: (0, 0)
    wspec_in = pl.BlockSpec((1, ec, D_MODEL, D_EXPERT), lambda i, e: (layer, e, 0, 0))
    wspec_out = pl.BlockSpec((1, ec, D_EXPERT, D_MODEL), lambda i, e: (layer, e, 0, 0))
    return pl.pallas_call(
        functools.partial(_moe_kernel, ec=ec, nsteps=nsteps, final_norm=final_norm),
        grid=(n // tm, nsteps),
        in_specs=[pl.BlockSpec((tm, D_MODEL), row),
                  pl.BlockSpec((1, D_MODEL), full),
                  pl.BlockSpec((D_MODEL, LANES), full),
                  pl.BlockSpec((D_MODEL, LANES), full),
                  pl.BlockSpec((1, LANES), full),
                  wspec_in, wspec_in, wspec_out,
                  pl.BlockSpec((1, D_MODEL), full)],
        out_specs=pl.BlockSpec((tm, D_MODEL), row),
        out_shape=jax.ShapeDtypeStruct((n, D_MODEL), F32),
        scratch_shapes=[pltpu.VMEM((tm, D_MODEL), BF16),
                        pltpu.VMEM((tm, LANES), F32),
                        pltpu.VMEM((tm, D_MODEL), F32)],
        compiler_params=_cparams("parallel", "arbitrary"),
        name="moe",
    )(x2, g, wrh, wrl, br, wg, wu, wd, gf)


def _head_perm():
    hg = NSA_HEADS // NSA_KV_GROUPS
    order = []
    for j in range(hg):
        order += [j, j + hg]
    return np.concatenate([np.arange(h * HEAD_DIM, (h + 1) * HEAD_DIM) for h in order])


def _split_hi_lo(w):
    hi = w.astype(BF16)
    return hi, (w - hi.astype(F32)).astype(BF16)


def _layer_params(l, w_in, b_gate, b_fgt, cmp_pe, cmp_w1, cmp_w2, w_out):
    offs = np.concatenate([[0], np.cumsum(IN_WIDTHS)])
    seg = [w_in[l][:, offs[i]:offs[i + 1]] for i in range(len(IN_WIDTHS))]
    (wq, wkc, wvc, wks, wvs, wkw, wvw, wgt, wfq, wfk, wfv, wff, wdq, wdk, wdv) = seg
    perm = _head_perm()
    gate_perm = np.array([h * 3 + br for br in range(3) for h in range(NSA_HEADS)])
    wm = jnp.concatenate([wq[:, perm] * (HEAD_DIM ** -0.5 * LOG2E), wks, wvs, wkw, wvw,
                          wfq * (HEAD_DIM ** -0.5 * LOG2E), wfk, wfv,
                          wdq * (DIFF_QK_DIM ** -0.5 * LOG2E), wdk, wdv, wkc, wvc], axis=1).astype(BF16)
    pad = LANES - NSA_GATE - 3 * FOX_HEADS
    ws = jnp.concatenate([wgt[:, gate_perm], wff, wff, wff, jnp.zeros((D_MODEL, pad), F32)], axis=1)
    wsh, wsl = _split_hi_lo(ws)
    bs = jnp.concatenate([b_gate[l][gate_perm], b_fgt[l], b_fgt[l], b_fgt[l], jnp.zeros((pad,), F32)])[None, :]

    eye = jnp.eye(NSA_KV_GROUPS, dtype=F32)
    half = CMP_BLOCK // 2
    cmp = []
    for pe, w1, w2 in zip(cmp_pe, cmp_w1, cmp_w2):
        w1r = w1[l].reshape(2, half, HEAD_DIM, CMP_HIDDEN)
        w1x = jnp.einsum('srdk,gh->srgdhk', w1r, eye).reshape(2, half * NSA_KV, NSA_KV_GROUPS * CMP_HIDDEN)
        w2bd = jnp.einsum('kd,gh->gkhd', w2[l], eye).reshape(NSA_KV_GROUPS * CMP_HIDDEN, NSA_KV)
        pe2 = jnp.broadcast_to(pe[l].reshape(2, half, 1, HEAD_DIM), (2, half, NSA_KV_GROUPS, HEAD_DIM))
        cmp.append((pe2.reshape(2, half * NSA_KV), w1x[0].astype(BF16), w1x[1].astype(BF16), w2bd.astype(BF16)))

    wo = w_out[l]
    wa = wo[:NSA_Q][perm].astype(BF16)
    wb = wo[NSA_Q:NSA_Q + FOX_W].astype(BF16)
    wc = wo[NSA_Q + FOX_W:].astype(BF16)
    return dict(wm=wm, wsh=wsh, wsl=wsl, bs=bs, cmp_k=cmp[0], cmp_v=cmp[1], wa=wa, wb=wb, wc=wc)


def _overlap_matrix(n_slab, n_cmp, n_sel):
    c_start = np.arange(n_cmp) * CMP_STRIDE
    s_start = np.arange(n_sel) * SLC_BLOCK
    ov = np.clip(np.minimum(c_start[:, None] + CMP_BLOCK, s_start[None, :] + SLC_BLOCK)
                 - np.maximum(c_start[:, None], s_start[None, :]), 0, None) / CMP_BLOCK
    full = np.zeros((n_slab, LANES), np.float32)
    full[:n_cmp, :n_sel] = ov
    return jnp.asarray(full, dtype=BF16)


def kernel(x, norm_attn, w_in, b_gate, b_fgt, cmp_k_pe, cmp_k_w1, cmp_k_w2, cmp_v_pe, cmp_v_w1, cmp_v_w2, diff_lambda, diff_subln, w_out, norm_ffn, w_grp, b_grp, w_exp, b_exp, w_e_gate, w_e_up, w_e_down, norm_final):
    b, t, d = x.shape
    depth = w_in.shape[0]
    n = b * t
    n_slab = t // CMP_STRIDE
    n_cmp = (t - CMP_BLOCK) // CMP_STRIDE + 1
    n_sel = t // SLC_BLOCK
    top_n = min(SLC_TOPK, n_sel)
    ov = _overlap_matrix(n_slab, n_cmp, n_sel)

    wg_all = w_e_gate.astype(BF16)
    wu_all = w_e_up.astype(BF16)
    wd_all = w_e_down.astype(BF16)
    gf = norm_final[None, :]

    x2 = x.reshape(n, d)
    for l in range(depth):
        p = _layer_params(l, w_in, b_gate, b_fgt, (cmp_k_pe, cmp_v_pe), (cmp_k_w1, cmp_v_w1),
                          (cmp_k_w2, cmp_v_w2), w_out)
        main, kc, vc, small = _inproj(x2, norm_attn[l][None, :], p["wm"], p["wsh"], p["wsl"], p["bs"])
        main3 = main.reshape(b, t, MAIN_W)
        small3 = small.reshape(b, t, LANES)
        kcmp = _compress(kc.reshape(b, n_slab, CMP_STRIDE * NSA_KV), *p["cmp_k"])
        vcmp = _compress(vc.reshape(b, n_slab, CMP_STRIDE * NSA_KV), *p["cmp_v"])
        o_nsa, pen = _cmp_attn(main3, kcmp, vcmp, small3, ov, n_cmp, n_sel, top_n)
        o_nsa = _flash(main3, "slc", (pen, small3, o_nsa), F32)
        o_nsa = _flash(main3, "win", (small3, o_nsa), F32)
        o_fox = _flash(main3, "fox", (_cumgate(small3),), BF16)
        sub2 = jnp.concatenate([diff_subln[l], diff_subln[l]])[None, :]
        o_diff = _flash(main3, "diff", (diff_lambda[l], sub2), BF16, lam_init=_lambda_init(l))

        x2 = _outproj(x2, o_nsa.reshape(n, NSA_Q), o_fox.reshape(n, FOX_W), o_diff.reshape(n, DIFF_W),
                      p["wa"], p["wb"], p["wc"])

        wr = jnp.concatenate([w_grp[l], w_exp[l], jnp.zeros((d, LANES - N_GROUPS - N_EXPERTS), F32)], axis=1)
        wrh, wrl = _split_hi_lo(wr)
        br = jnp.concatenate([b_grp[l], b_exp[l], jnp.zeros((LANES - N_GROUPS - N_EXPERTS,), F32)])[None, :]
        x2 = _moe(x2, norm_ffn[l][None, :], wrh, wrl, br, wg_all, wu_all, wd_all, gf, l,
                  final_norm=(l == depth - 1))
    return x2.reshape(b, t, d)
```

```python
import functools
import math

import ml_dtypes
import numpy as np
import jax
import jax.numpy as jnp
from jax import lax
from jax.experimental import pallas as pl
from jax.experimental.pallas import tpu as pltpu

F32 = jnp.float32
BF16 = jnp.bfloat16

D_MODEL = 1024
HEAD_DIM = 64
NSA_HEADS = 8
NSA_KV_GROUPS = 2
CMP_BLOCK = 32
CMP_STRIDE = 16
CMP_HIDDEN = 256
SLC_BLOCK = 64
SLC_SHIFT = 6
SLC_TOPK = 16
WINDOW = 512
FOX_HEADS = 4
DIFF_HEADS = 4
DIFF_QK_DIM = HEAD_DIM // 2
N_GROUPS = 4
EXPERTS_PER_GROUP = 8
N_EXPERTS = N_GROUPS * EXPERTS_PER_GROUP
D_EXPERT = 256
RMS_EPS = 1e-6
FORCE_SCORE = 1e4
NEG_INF = -1e30
MASKED = -2e30

NSA_Q = NSA_HEADS * HEAD_DIM
NSA_KV = NSA_KV_GROUPS * HEAD_DIM
NSA_GATE = NSA_HEADS * 3
FOX_W = FOX_HEADS * HEAD_DIM
DIFF_W = DIFF_HEADS * HEAD_DIM
MIX_WIDTH = NSA_Q + FOX_W + DIFF_W
IN_WIDTHS = (NSA_Q, NSA_KV, NSA_KV, NSA_KV, NSA_KV, NSA_KV, NSA_KV, NSA_GATE,
             FOX_W, FOX_W, FOX_W, FOX_HEADS, DIFF_W, DIFF_W, DIFF_W)

LANES = 128
VMEM_LIMIT_BYTES = 56 * 1024 * 1024

MAIN_W = NSA_Q + 4 * NSA_KV + 3 * FOX_W + 3 * DIFF_W
PROJ_W = MAIN_W + 2 * NSA_KV
COL_KS, COL_VS, COL_KW, COL_VW = 4, 5, 6, 7
COL_FQ, COL_FK, COL_FV = 4, 5, 6
COL_DQ, COL_DK, COL_DV = 7, 8, 9
FGT_LANES = (NSA_GATE, NSA_GATE + FOX_HEADS, NSA_GATE + 2 * FOX_HEADS)
LOG2E = 1.4426950408889634

TQ = 512
TK = 512
KSTRIP = 256
DOT_LANES = 512
ONES_ROWS = 16

def _alibi(n):
    return [float(2.0 ** (-8.0 * (i + 1) / n)) for i in range(n)]


def _lambda_init(layer):
    return 0.8 - 0.6 * math.exp(-0.3 * layer)


def _cparams(*sem):
    return pltpu.CompilerParams(dimension_semantics=sem, vmem_limit_bytes=VMEM_LIMIT_BYTES)


def _dot(a, b):
    return jnp.dot(a, b, preferred_element_type=F32)


def _dot_nt(a, b):
    return lax.dot_general(a, b, (((1,), (1,)), ((), ())), preferred_element_type=F32)


def _split2(x):
    hi = x.astype(BF16)
    lo = (x - hi.astype(F32)).astype(BF16)
    return hi, lo


def _sigmoid(z):
    return 1.0 / (1.0 + jnp.exp(-z))


def _keep_lanes(sel, blk):
    return jnp.where(sel, blk.astype(F32), 0.0).astype(BF16)


def _inproj_kernel(x_ref, g_ref, wm_ref, wsh_ref, wsl_ref, bs_ref, main_ref, kc_ref, vc_ref, small_ref):
    x = x_ref[...]
    ms = jnp.mean(x * x, axis=-1, keepdims=True)
    y = x * lax.rsqrt(ms + RMS_EPS) * g_ref[...]
    hb, hl = _split2(y)
    full = _dot(hb, wm_ref[...]).astype(BF16)
    main_ref[...] = full[:, :MAIN_W]
    kc_ref[...] = full[:, MAIN_W:MAIN_W + NSA_KV]
    vc_ref[...] = full[:, MAIN_W + NSA_KV:]
    small_ref[...] = (_dot(hb, wsh_ref[...]) + _dot(hl, wsh_ref[...]) + _dot(hb, wsl_ref[...])) + bs_ref[...]


def _inproj(x2, g, wm, wsh, wsl, bs, tm=512):
    n = x2.shape[0]
    full = lambda i: (0, 0)
    return pl.pallas_call(
        _inproj_kernel,
        grid=(n // tm,),
        in_specs=[pl.BlockSpec((tm, D_MODEL), lambda i: (i, 0)),
                  pl.BlockSpec((1, D_MODEL), full),
                  pl.BlockSpec((D_MODEL, PROJ_W), full),
                  pl.BlockSpec((D_MODEL, LANES), full),
                  pl.BlockSpec((D_MODEL, LANES), full),
                  pl.BlockSpec((1, LANES), full)],
        out_specs=[pl.BlockSpec((tm, MAIN_W), lambda i: (i, 0)),
                   pl.BlockSpec((tm, NSA_KV), lambda i: (i, 0)),
                   pl.BlockSpec((tm, NSA_KV), lambda i: (i, 0)),
                   pl.BlockSpec((tm, LANES), lambda i: (i, 0))],
        out_shape=[jax.ShapeDtypeStruct((n, MAIN_W), BF16),
                   jax.ShapeDtypeStruct((n, NSA_KV), BF16),
                   jax.ShapeDtypeStruct((n, NSA_KV), BF16),
                   jax.ShapeDtypeStruct((n, LANES), F32)],
        compiler_params=_cparams("parallel"),
        name="inproj",
    )(x2, g, wm, wsh, wsl, bs)


def _compress_kernel(r_ref, pe_ref, w1a_ref, w1b_ref, w2_ref, o_ref):
    r = r_ref[0].astype(F32)
    pe = pe_ref[...]
    ra = (r + pe[0:1, :]).astype(BF16)
    rb = (r + pe[1:2, :]).astype(BF16)
    a = _dot(ra, w1a_ref[...])
    b = _dot(rb, w1b_ref[...])
    n_slab = a.shape[0]
    hid = a + pltpu.roll(b, n_slab - 1, 0)
    hid = jax.nn.gelu(hid)
    o_ref[0] = _dot(hid.astype(BF16), w2_ref[...]).astype(BF16)


def _compress(r, pe2, w1a, w1b, w2bd):
    b, n_slab, w = r.shape
    full = lambda i: (0, 0)
    return pl.pallas_call(
        _compress_kernel,
        grid=(b,),
        in_specs=[pl.BlockSpec((1, n_slab, w), lambda i: (i, 0, 0)),
                  pl.BlockSpec((2, w), full),
                  pl.BlockSpec((w, 2 * CMP_HIDDEN), full),
                  pl.BlockSpec((w, 2 * CMP_HIDDEN), full),
                  pl.BlockSpec((2 * CMP_HIDDEN, LANES), full)],
        out_specs=pl.BlockSpec((1, n_slab, LANES), lambda i: (i, 0, 0)),
        out_shape=jax.ShapeDtypeStruct((b, n_slab, LANES), BF16),
        compiler_params=_cparams("parallel"),
        name="compress",
    )(r, pe2, w1a, w1b, w2bd)


def _cumgate_kernel(z_ref, o_ref, carry_ref, *, tc):
    @pl.when(pl.program_id(1) == 0)
    def _():
        carry_ref[...] = jnp.zeros_like(carry_ref)

    z = z_ref[0]
    logf = -(jnp.maximum(-z, 0.0) + jnp.log(1.0 + jnp.exp(-jnp.abs(z))))
    hi = logf.astype(BF16)
    r1 = logf - hi.astype(F32)
    mid = r1.astype(BF16)
    lo = (r1 - mid.astype(F32)).astype(BF16)
    tri = (lax.broadcasted_iota(jnp.int32, (tc, tc), 0) >= lax.broadcasted_iota(jnp.int32, (tc, tc), 1))
    tri = jnp.where(tri, 1.0, 0.0).astype(BF16)
    c = (_dot(tri, hi) + _dot(tri, mid)) + _dot(tri, lo) + carry_ref[0:1, :]
    carry_ref[...] = jnp.broadcast_to(c[tc - 1:tc, :], carry_ref.shape)
    v = c * (-LOG2E)
    p0 = v.astype(BF16)
    r1 = v - p0.astype(F32)
    p1 = r1.astype(BF16)
    p2 = (r1 - p1.astype(F32)).astype(BF16)
    lane = lax.broadcasted_iota(jnp.int32, v.shape, 1)
    pieces = jnp.where(lane < FGT_LANES[1], p0.astype(F32), jnp.where(lane < FGT_LANES[2], p1.astype(F32), p2.astype(F32)))
    keep = (lane >= FGT_LANES[0]) & (lane < FGT_LANES[2] + FOX_HEADS)
    o_ref[0] = jnp.where(keep, pieces, 0.0).astype(BF16)


def _cumgate(small3, tc=256):
    b, t, w = small3.shape
    return pl.pallas_call(
        functools.partial(_cumgate_kernel, tc=tc),
        grid=(b, t // tc),
        in_specs=[pl.BlockSpec((1, tc, w), lambda i, j: (i, j, 0))],
        out_specs=pl.BlockSpec((1, tc, w), lambda i, j: (i, j, 0)),
        out_shape=jax.ShapeDtypeStruct((b, t, w), BF16),
        scratch_shapes=[pltpu.VMEM((8, w), F32)],
        compiler_params=_cparams("parallel", "arbitrary"),
        name="cumgate",
    )(small3)


def _cmp_kernel(q_ref, kc_ref, vc_ref, sm_ref, ov_ref, o_ref, pen_ref, *, tq, n_cmp, n_sel, top_n):
    t0 = pl.program_id(1) * tq
    nck = kc_ref.shape[1]
    row = t0 + lax.broadcasted_iota(jnp.int32, (tq, nck), 0)
    col = lax.broadcasted_iota(jnp.int32, (tq, nck), 1)
    dist = row - (col * CMP_STRIDE + (CMP_BLOCK - 1))
    mask = (dist >= 0) & (col < n_cmp)
    distf = dist.astype(F32)
    lane = lax.broadcasted_iota(jnp.int32, (tq, LANES), 1)
    low = lane < HEAD_DIM
    kc = kc_ref[0]
    vc = vc_ref[0]
    sig = _sigmoid(sm_ref[0])
    slopes = _alibi(NSA_HEADS)
    psum = [None, None]
    hg = NSA_HEADS // NSA_KV_GROUPS
    for j in range(hg):
        qblk = q_ref[0, :, j * LANES:(j + 1) * LANES]
        outs = []
        for g in range(NSA_KV_GROUPS):
            h = j + hg * g
            qm = _keep_lanes(low if g == 0 else ~low, qblk)
            s = _dot_nt(qm, kc) - (slopes[h] * LOG2E) * distf
            s = jnp.where(mask, s, NEG_INF)
            m = jnp.max(s, axis=-1, keepdims=True)
            p = jnp.where(mask, jnp.exp2(s - m), 0.0)
            l = jnp.sum(p, axis=-1, keepdims=True)
            p = p * jnp.where(l > 0.0, 1.0 / l, 0.0)
            psum[g] = p if psum[g] is None else psum[g] + p
            outs.append(_dot(p.astype(BF16), vc))
        gate = jnp.where(low, sig[:, j:j + 1], sig[:, j + hg:j + hg + 1])
        o_ref[0, :, j * LANES:(j + 1) * LANES] = jnp.where(low, outs[0], outs[1]) * gate

    rows_t = lax.broadcasted_iota(jnp.int32, (n_sel, tq), 0)
    cur_r = (t0 + lax.broadcasted_iota(jnp.int32, (tq, LANES), 0)) >> SLC_SHIFT
    pen_t = []
    for g in range(NSA_KV_GROUPS):
        ph, plo = _split2(psum[g])
        imp = _dot(ph, ov_ref[...]) + _dot(plo, ov_ref[...])
        valid = lane <= cur_r
        forced = (lane == 0) | (lane == cur_r) | (lane == cur_r - 1)
        imp = jnp.where(valid, jnp.where(forced, FORCE_SCORE, imp), NEG_INF)
        imp_t = imp.T[:n_sel, :]
        cnt = jnp.zeros((n_sel, tq), F32)
        for k in range(n_sel):
            rk = imp_t[k:k + 1, :]
            cnt = cnt + jnp.where(rows_t > k, jnp.where(rk >= imp_t, 1.0, 0.0), jnp.where(rk > imp_t, 1.0, 0.0))
        pen_t.append(jnp.where(cnt < float(top_n), 0.0, MASKED))
    for g in range(NSA_KV_GROUPS):
        full = jnp.concatenate([pen_t[g], jnp.zeros((LANES - n_sel, tq), F32)], axis=0)
        pen_ref[0, :, g * LANES:(g + 1) * LANES] = full.T.astype(BF16)


def _cmp_attn(main3, kcmp, vcmp, small3, ov, n_cmp, n_sel, top_n, tq=256):
    b, t, _ = main3.shape
    nck = kcmp.shape[1]
    return pl.pallas_call(
        functools.partial(_cmp_kernel, tq=tq, n_cmp=n_cmp, n_sel=n_sel, top_n=top_n),
        grid=(b, t // tq),
        in_specs=[pl.BlockSpec((1, tq, NSA_Q), lambda i, j: (i, j, 0)),
                  pl.BlockSpec((1, nck, LANES), lambda i, j: (i, 0, 0)),
                  pl.BlockSpec((1, nck, LANES), lambda i, j: (i, 0, 0)),
                  pl.BlockSpec((1, tq, LANES), lambda i, j: (i, j, 0)),
                  pl.BlockSpec((nck, LANES), lambda i, j: (0, 0))],
        out_specs=[pl.BlockSpec((1, tq, NSA_Q), lambda i, j: (i, j, 0)),
                   pl.BlockSpec((1, tq, NSA_KV_GROUPS * LANES), lambda i, j: (i, j, 0))],
        out_shape=[jax.ShapeDtypeStruct((b, t, NSA_Q), F32),
                   jax.ShapeDtypeStruct((b, t, NSA_KV_GROUPS * LANES), BF16)],
        compiler_params=_cparams("parallel", "parallel"),
        name="cmp_attn",
    )(main3, kcmp, vcmp, small3, ov)


def _bf16_terms(c, n=3):
    out, r = [], float(c)
    for _ in range(n):
        p = float(np.float32(r).astype(ml_dtypes.bfloat16))
        out.append(p)
        r -= p
    return out


def _alibi_q_aug(lane, a0, slope):
    aug = jnp.zeros(lane.shape, F32)
    for i, c in enumerate(_bf16_terms(slope * LOG2E)):
        aug = jnp.where(lane == a0 + 2 * i, float(SLC_BLOCK) * c, aug)
        aug = jnp.where(lane == a0 + 2 * i + 1, c, aug)
    return aug


def _alibi_k_aug(lane, rel, a0):
    hi = (rel >> SLC_SHIFT).astype(F32)
    lo = (rel & (SLC_BLOCK - 1)).astype(F32)
    inside = (lane >= a0) & (lane < a0 + 6)
    odd = ((lane - a0) & 1) == 1
    return jnp.where(inside, jnp.where(odd, lo, hi), 0.0)


ALIBI_LANE = {"slc": SLC_BLOCK, "win": 0, "diff": 0}
STACKS = {"slc": (1, 8), "win": (1, 8), "fox": (2, 2), "diff": (2, 4)}


def _flash_kernel(*refs, mode, tq, tk, lam_init):
    sched_ref, refs = refs[0], refs[1:]
    if mode == "slc":
        q_ref, k_ref, v_ref, pen_ref, sm_ref, add_ref, o_ref, qst_ref, m_ref, acc_ref = refs
    elif mode == "win":
        q_ref, k_ref, v_ref, sm_ref, add_ref, o_ref, qst_ref, m_ref, acc_ref = refs
    elif mode == "fox":
        q_ref, k_ref, v_ref, fa_ref, o_ref, qst_ref, m_ref, acc_ref = refs
    else:
        q_ref, k_ref, v_ref, lam_ref, sub_ref, o_ref, qst_ref, m_ref, acc_ref = refs
    nstack, nv = STACKS[mode]
    nsa = mode in ("slc", "win")
    hg = NSA_HEADS // NSA_KV_GROUPS
    step_id = pl.program_id(1)
    q0 = sched_ref[SCHED_Q, step_id] * tq
    k0 = sched_ref[SCHED_K, step_id] * tk

    @pl.when(sched_ref[SCHED_FIRST, step_id] == 1)
    def _():
        m_ref[...] = jnp.full(m_ref.shape, NEG_INF, F32)
        acc_ref[...] = jnp.zeros(acc_ref.shape, F32)
        lane = lax.broadcasted_iota(jnp.int32, (tq, LANES), 1)
        for vh in range(nstack * nv):
            if nsa:
                jb, g = vh % hg, vh // hg
                lo_lane, width = g * HEAD_DIM, HEAD_DIM
                aug = _alibi_q_aug(lane, ALIBI_LANE[mode], _alibi(NSA_HEADS)[vh])
                if mode == "slc":
                    aug = aug + pen_ref[0, :, g * LANES:(g + 1) * LANES].astype(F32)
            elif mode == "fox":
                jb, r = vh // nv, vh % nv
                lo_lane, width = r * HEAD_DIM, HEAD_DIM
                hit = (lane == FGT_LANES[0] + vh) | (lane == FGT_LANES[1] + vh) | (lane == FGT_LANES[2] + vh)
                aug = jnp.where(hit, 1.0, 0.0)
            else:
                jb, r = vh // nv, vh % nv
                lo_lane, width = r * DIFF_QK_DIM, DIFF_QK_DIM
                aug = _alibi_q_aug(lane, ALIBI_LANE[mode], _alibi(DIFF_HEADS)[vh // 2])
            sel = (lane >= lo_lane) & (lane < lo_lane + width)
            qst_ref[vh * tq:(vh + 1) * tq, 0:LANES] = _keep_lanes(sel, q_ref[0, :, jb * LANES:(jb + 1) * LANES])
            qst_ref[vh * tq:(vh + 1) * tq, LANES:2 * LANES] = aug.astype(BF16)

    def step(masked):
        lane_k = lax.broadcasted_iota(jnp.int32, (tk, LANES), 1)
        row_k = lax.broadcasted_iota(jnp.int32, (tk, LANES), 0)
        if mode == "fox":
            k_aug = fa_ref[0]
        else:
            ka = _alibi_k_aug(lane_k, (k0 - q0) + row_k, ALIBI_LANE[mode])
            if mode == "slc":
                ka = ka + jnp.where(((k0 + row_k) >> SLC_SHIFT) == lane_k, 1.0, 0.0)
            k_aug = ka.astype(BF16)
        if masked:
            dist = (q0 - k0) + lax.broadcasted_iota(jnp.int32, (tk, tq), 1) - lax.broadcasted_iota(jnp.int32, (tk, tq), 0)
            ok = dist >= 0
            if mode == "win":
                ok = ok & (dist < WINDOW)
            bias = jnp.where(ok, 0.0, MASKED)
        keys, vals_t = [], []
        for st in range(nstack):
            kblk = k_ref[0] if nsa else k_ref[0, :, st * LANES:(st + 1) * LANES]
            vblk = v_ref[0] if nsa else v_ref[0, :, st * LANES:(st + 1) * LANES]
            keys.append(jnp.concatenate([kblk, k_aug], axis=1))
            vals_t.append(jnp.concatenate([vblk.astype(F32).T.astype(BF16), jnp.ones((ONES_ROWS, tk), BF16)], axis=0))

        nstrip = tk // KSTRIP
        hpd = max(1, DOT_LANES // tq)
        wq = hpd * tq
        nvh = nstack * nv // hpd
        if masked:
            bias = jnp.tile(bias, (1, hpd))

        def scores(vh, r):
            st = vh * hpd // nv
            s = _dot_nt(keys[st][r * KSTRIP:(r + 1) * KSTRIP, :], qst_ref[vh * wq:(vh + 1) * wq, :])
            if masked:
                s = s + bias[r * KSTRIP:(r + 1) * KSTRIP, :]
            return s

        def colmax(a, s):
            mx = jnp.max(s, axis=0, keepdims=True)
            return mx if a is None else jnp.maximum(a, mx)

        cur, cur_max = [], None
        for r in range(nstrip):
            cur.append(scores(0, r))
            cur_max = colmax(cur_max, cur[r])
        for vh in range(nvh):
            cols = slice(vh * wq, (vh + 1) * wq)
            m_old = m_ref[:, cols]
            m_new = jnp.maximum(m_old, cur_max)
            alpha = jnp.exp2(m_old - m_new)
            nxt, nxt_max, pv = [], None, None
            for r in range(nstrip):
                if vh + 1 < nvh:
                    nxt.append(scores(vh + 1, r))
                    nxt_max = colmax(nxt_max, nxt[r])
                p = jnp.exp2((cur[r] - m_new).astype(BF16))
                part = _dot(vals_t[vh * hpd // nv][:, r * KSTRIP:(r + 1) * KSTRIP], p)
                pv = part if pv is None else pv + part
            m_ref[:, cols] = m_new
            acc_ref[:, cols] = alpha * acc_ref[:, cols] + pv
            cur, cur_max = nxt, nxt_max

    pl.when(sched_ref[SCHED_MASKED, step_id] == 1)(lambda: step(True))
    if mode != "win":
        pl.when(sched_ref[SCHED_MASKED, step_id] == 0)(lambda: step(False))

    @pl.when(sched_ref[SCHED_LAST, step_id] == 1)
    def _():
        inv = 1.0 / acc_ref[LANES:LANES + 1, :]
        low_r = lax.broadcasted_iota(jnp.int32, (LANES, tq), 0) < HEAD_DIM

        def out_t(vh):
            return acc_ref[0:LANES, vh * tq:(vh + 1) * tq] * inv[:, vh * tq:(vh + 1) * tq]

        low = lax.broadcasted_iota(jnp.int32, (tq, LANES), 1) < HEAD_DIM
        if nsa:
            sig = _sigmoid(sm_ref[0])
            br = 1 if mode == "slc" else 2
            for jb in range(hg):
                o = jnp.where(low_r, out_t(jb), out_t(jb + hg)).T
                c0 = br * NSA_HEADS + jb
                gate = jnp.where(low, sig[:, c0:c0 + 1], sig[:, c0 + hg:c0 + hg + 1])
                o_ref[0, :, jb * LANES:(jb + 1) * LANES] = add_ref[0, :, jb * LANES:(jb + 1) * LANES] + o * gate
        elif mode == "fox":
            for pb in range(nstack):
                o = jnp.where(low_r, out_t(2 * pb), out_t(2 * pb + 1)).T
                o_ref[0, :, pb * LANES:(pb + 1) * LANES] = o.astype(o_ref.dtype)
        else:
            lam = lam_ref[...]
            lam_full = (jnp.exp(jnp.sum(lam[0:1, :] * lam[1:2, :], axis=-1, keepdims=True))
                        - jnp.exp(jnp.sum(lam[2:3, :] * lam[3:4, :], axis=-1, keepdims=True)) + lam_init)
            for pb in range(nstack):
                heads = [out_t(pb * nv + 2 * hh) - lam_full * out_t(pb * nv + 2 * hh + 1) for hh in range(2)]
                o = jnp.where(low_r, heads[0], heads[1]).T
                sq = o * o
                ss_lo = jnp.sum(jnp.where(low, sq, 0.0), axis=-1, keepdims=True)
                ss_hi = jnp.sum(jnp.where(low, 0.0, sq), axis=-1, keepdims=True)
                ms = jnp.where(low, ss_lo, ss_hi) * (1.0 / HEAD_DIM)
                y = o * lax.rsqrt(ms + RMS_EPS) * sub_ref[...] * (1.0 - lam_init)
                o_ref[0, :, pb * LANES:(pb + 1) * LANES] = y.astype(o_ref.dtype)


def _flash(main3, mode, extra, out_dtype, lam_init=0.0, tq=TQ, tk=TK):
    b, t, _ = main3.shape
    tk = min(tk, t)
    nstack, nv = STACKS[mode]
    sched = _flash_schedule(t, tq, tk, mode == "win")
    qtile = lambda w, c: pl.BlockSpec((1, tq, w), lambda bi, s, sch: (bi, sch[SCHED_Q, s], c))
    ktile = lambda w, c: pl.BlockSpec((1, tk, w), lambda bi, s, sch: (bi, sch[SCHED_K, s], c))
    if mode in ("slc", "win"):
        kcol, vcol = (COL_KS, COL_VS) if mode == "slc" else (COL_KW, COL_VW)
        in_specs = [qtile(NSA_Q, 0), ktile(LANES, kcol), ktile(LANES, vcol)]
        if mode == "slc":
            in_specs.append(qtile(NSA_KV_GROUPS * LANES, 0))
        in_specs += [qtile(LANES, 0), qtile(NSA_Q, 0)]
        out_w = NSA_Q
    else:
        qc, kc, vc = (COL_FQ, COL_FK, COL_FV) if mode == "fox" else (COL_DQ, COL_DK, COL_DV)
        in_specs = [qtile(2 * LANES, qc), ktile(2 * LANES, kc), ktile(2 * LANES, vc)]
        if mode == "fox":
            in_specs.append(ktile(LANES, 0))
        else:
            in_specs += [pl.BlockSpec((4, DIFF_QK_DIM), lambda bi, s, sch: (0, 0)),
                         pl.BlockSpec((1, LANES), lambda bi, s, sch: (0, 0))]
        out_w = 2 * LANES
    cols = nstack * nv * tq
    return pl.pallas_call(
        functools.partial(_flash_kernel, mode=mode, tq=tq, tk=tk, lam_init=lam_init),
        grid_spec=pltpu.PrefetchScalarGridSpec(
            num_scalar_prefetch=1,
            grid=(b, sched.shape[1]),
            in_specs=in_specs,
            out_specs=qtile(out_w, 0),
            scratch_shapes=[pltpu.VMEM((cols, 2 * LANES), BF16),
                            pltpu.VMEM((1, cols), F32),
                            pltpu.VMEM((LANES + ONES_ROWS, cols), F32)]),
        out_shape=jax.ShapeDtypeStruct((b, t, out_w), out_dtype),
        compiler_params=_cparams("parallel", "arbitrary"),
        name="flash_" + mode,
    )(jnp.asarray(sched), main3, main3, main3, *extra)


SCHED_Q, SCHED_K, SCHED_FIRST, SCHED_LAST, SCHED_MASKED = range(5)


def _flash_schedule(t, tq, tk, window):
    rows = []
    for i in range(t // tq):
        q_lo, q_hi = i * tq, i * tq + tq - 1
        k_first = max(q_lo - (WINDOW - 1), 0) // tk if window else 0
        tiles = list(range(k_first, q_hi // tk + 1))
        for n, kt in enumerate(tiles):
            fully_visible = kt * tk + tk - 1 <= q_lo and not window
            rows.append((i, kt, int(n == 0), int(n == len(tiles) - 1), int(not fully_visible)))
    return np.asarray(rows, np.int32).T


def _outproj_kernel(x_ref, oa_ref, ob_ref, oc_ref, wa_ref, wb_ref, wc_ref, o_ref):
    acc = _dot(oa_ref[...].astype(BF16), wa_ref[...])
    acc = acc + _dot(ob_ref[...], wb_ref[...])
    acc = acc + _dot(oc_ref[...], wc_ref[...])
    o_ref[...] = x_ref[...] + acc


def _outproj(x2, oa, ob, oc, wa, wb, wc, tm=512):
    n = x2.shape[0]
    row = lambda i: (i, 0)
    full = lambda i: (0, 0)
    return pl.pallas_call(
        _outproj_kernel,
        grid=(n // tm,),
        in_specs=[pl.BlockSpec((tm, D_MODEL), row),
                  pl.BlockSpec((tm, NSA_Q), row),
                  pl.BlockSpec((tm, FOX_W), row),
                  pl.BlockSpec((tm, DIFF_W), row),
                  pl.BlockSpec((NSA_Q, D_MODEL), full),
                  pl.BlockSpec((FOX_W, D_MODEL), full),
                  pl.BlockSpec((DIFF_W, D_MODEL), full)],
        out_specs=pl.BlockSpec((tm, D_MODEL), row),
        out_shape=jax.ShapeDtypeStruct((n, D_MODEL), F32),
        compiler_params=_cparams("parallel"),
        name="outproj",
    )(x2, oa, ob, oc, wa, wb, wc)


ROUTE_OFF = N_GROUPS


MOE_CHUNK = 256
MOE_PREFIX_BLOCK = 256


def _moe_kernel(x_ref, g_ref, wrh_ref, wrl_ref, br_ref, wg_ref, wu_ref, wd_ref, gf_ref, o_ref,
                h_ref, gsel_ref, col_ref, row_ref, cnt_ref, *, final_norm):
    grp_id = pl.program_id(1)
    tm = x_ref.shape[0]
    ch = MOE_CHUNK

    @pl.when(grp_id == 0)
    def _():
        x = x_ref[...]
        ms = jnp.mean(x * x, axis=-1, keepdims=True)
        y = x * lax.rsqrt(ms + RMS_EPS) * g_ref[...]
        hb, hl = _split2(y)
        h_ref[...] = hb
        lg = (_dot(hb, wrh_ref[...]) + _dot(hl, wrh_ref[...]) + _dot(hb, wrl_ref[...])) + br_ref[...]
        lane = lax.broadcasted_iota(jnp.int32, lg.shape, 1).astype(F32)
        big = float(LANES)
        isg = lane < N_GROUPS
        gmax = jnp.max(jnp.where(isg, lg, -jnp.inf), axis=-1, keepdims=True)
        grp = jnp.min(jnp.where(isg & (lg == gmax), lane, big), axis=-1, keepdims=True)
        gprob = 1.0 / jnp.sum(jnp.where(isg, jnp.exp(lg - gmax), 0.0), axis=-1, keepdims=True)
        lo_lane = ROUTE_OFF + grp * EXPERTS_PER_GROUP
        ing = (lane >= lo_lane) & (lane < lo_lane + EXPERTS_PER_GROUP)
        v1 = jnp.max(jnp.where(ing, lg, -jnp.inf), axis=-1, keepdims=True)
        i1 = jnp.min(jnp.where(ing & (lg == v1), lane, big), axis=-1, keepdims=True)
        rest = ing & (lane != i1)
        v2 = jnp.max(jnp.where(rest, lg, -jnp.inf), axis=-1, keepdims=True)
        i2 = jnp.min(jnp.where(rest & (lg == v2), lane, big), axis=-1, keepdims=True)
        e2 = jnp.exp(v2 - v1)
        w1 = gprob / (1.0 + e2)
        w2 = gprob * e2 / (1.0 + e2)
        gate = jnp.where(lane == i1, w1, 0.0) + jnp.where(lane == i2, w2, 0.0)
        gsel = jnp.zeros_like(gate)
        for gg in range(N_GROUPS):
            shifted = pltpu.roll(gate, LANES - (ROUTE_OFF + gg * EXPERTS_PER_GROUP), 1)
            gsel = jnp.where(grp == float(gg), shifted, gsel)
        gsel = jnp.where(lane < EXPERTS_PER_GROUP, gsel, 0.0)
        ghi, glo = _split2(gsel)
        gsel_ref[:, 0:LANES] = ghi
        gsel_ref[:, LANES:2 * LANES] = glo
        onehot = jnp.where(lane == grp, 1.0, 0.0)
        pb = MOE_PREFIX_BLOCK
        tri = jnp.where(lax.broadcasted_iota(jnp.int32, (pb, pb), 0) > lax.broadcasted_iota(jnp.int32, (pb, pb), 1),
                        1.0, 0.0).astype(BF16)
        carry = jnp.zeros((1, LANES), F32)
        lane_b = lax.broadcasted_iota(jnp.int32, (pb, LANES), 1)
        for blk in range(tm // pb):
            oh_b = onehot[blk * pb:(blk + 1) * pb, :]
            prefix = _dot(tri, oh_b.astype(BF16)) + carry
            pos = jnp.sum(oh_b * prefix, axis=-1, keepdims=True)
            col_ref[blk * pb:(blk + 1) * pb, :] = jnp.where(lane_b == 1, pos, 0.0)
            carry = carry + jnp.sum(oh_b, axis=0, keepdims=True)
        for gg in range(N_GROUPS):
            cnt_ref[gg] = carry[0, gg].astype(jnp.int32)
        colinfo = col_ref[...] + jnp.where(lane == 0.0, grp, 0.0)
        col_ref[...] = colinfo
        row_ref[...] = colinfo.T[0:8, :]
        o_ref[...] = x

    grp_f = grp_id.astype(F32)
    key_col = jnp.where(col_ref[:, 0:1] == grp_f, col_ref[:, 1:2], -1.0)
    key_row = jnp.where(row_ref[0:1, :] == grp_f, row_ref[1:2, :], -1.0)

    def chunk(c, carry):
        base = (c * ch).astype(F32)
        sub_iota = lax.broadcasted_iota(jnp.int32, (ch, tm), 0).astype(F32)
        gather = jnp.where(key_row - base == sub_iota, 1.0, 0.0).astype(BF16)
        xg = _dot(gather, h_ref[...]).astype(BF16)
        gparts = _dot(gather, gsel_ref[...])
        gates = gparts[:, 0:LANES] + gparts[:, LANES:2 * LANES]
        y = None
        for e in range(EXPERTS_PER_GROUP):
            a = _dot(xg, wg_ref[0, e])
            u = _dot(xg, wu_ref[0, e])
            act = a * _sigmoid(a) * u * gates[:, e:e + 1]
            part = _dot(act.astype(BF16), wd_ref[0, e])
            y = part if y is None else y + part
        lane_iota = lax.broadcasted_iota(jnp.int32, (tm, ch), 1).astype(F32)
        scatter = jnp.where(key_col - base == lane_iota, 1.0, 0.0).astype(BF16)
        o_ref[...] = o_ref[...] + _dot(scatter, y.astype(BF16))
        return carry

    lax.fori_loop(0, (cnt_ref[grp_id] + ch - 1) // ch, chunk, 0)

    @pl.when(grp_id == N_GROUPS - 1)
    def _():
        if final_norm:
            y = o_ref[...]
            ms = jnp.mean(y * y, axis=-1, keepdims=True)
            o_ref[...] = y * lax.rsqrt(ms + RMS_EPS) * gf_ref[...]


def _moe(x2, g, wrh, wrl, br, wg, wu, wd, gf, layer, final_norm, tm=1024):
    n = x2.shape[0]
    tm = min(tm, n)
    row = lambda i, e: (i, 0)
    full = lambda i, e: (0, 0)
    epg = EXPERTS_PER_GROUP
    wspec_in = pl.BlockSpec((1, epg, D_MODEL, D_EXPERT), lambda i, e: (layer, e, 0, 0))
    wspec_out = pl.BlockSpec((1, epg, D_EXPERT, D_MODEL), lambda i, e: (layer, e, 0, 0))
    return pl.pallas_call(
        functools.partial(_moe_kernel, final_norm=final_norm),
        grid=(n // tm, N_GROUPS),
        in_specs=[pl.BlockSpec((tm, D_MODEL), row, pipeline_mode=pl.Buffered(1)),
                  pl.BlockSpec((1, D_MODEL), full),
                  pl.BlockSpec((D_MODEL, LANES), full),
                  pl.BlockSpec((D_MODEL, LANES), full),
                  pl.BlockSpec((1, LANES), full),
                  wspec_in, wspec_in, wspec_out,
                  pl.BlockSpec((1, D_MODEL), full)],
        out_specs=pl.BlockSpec((tm, D_MODEL), row),
        out_shape=jax.ShapeDtypeStruct((n, D_MODEL), F32),
        scratch_shapes=[pltpu.VMEM((tm, D_MODEL), BF16),
                        pltpu.VMEM((tm, 2 * LANES), BF16),
                        pltpu.VMEM((tm, LANES), F32),
                        pltpu.VMEM((8, tm), F32),
                        pltpu.SMEM((N_GROUPS,), jnp.int32)],
        compiler_params=_cparams("parallel", "arbitrary"),
        name="moe",
    )(x2, g, wrh, wrl, br, wg, wu, wd, gf)


def _head_perm():
    hg = NSA_HEADS // NSA_KV_GROUPS
    order = []
    for j in range(hg):
        order += [j, j + hg]
    return np.concatenate([np.arange(h * HEAD_DIM, (h + 1) * HEAD_DIM) for h in order])


def _split_hi_lo(w):
    hi = w.astype(BF16)
    return hi, (w - hi.astype(F32)).astype(BF16)


def _layer_params(l, w_in, b_gate, b_fgt, cmp_pe, cmp_w1, cmp_w2, w_out):
    offs = np.concatenate([[0], np.cumsum(IN_WIDTHS)])
    seg = [w_in[l][:, offs[i]:offs[i + 1]] for i in range(len(IN_WIDTHS))]
    (wq, wkc, wvc, wks, wvs, wkw, wvw, wgt, wfq, wfk, wfv, wff, wdq, wdk, wdv) = seg
    perm = _head_perm()
    gate_perm = np.array([h * 3 + br for br in range(3) for h in range(NSA_HEADS)])
    wm = jnp.concatenate([wq[:, perm] * (HEAD_DIM ** -0.5 * LOG2E), wks, wvs, wkw, wvw,
                          wfq * (HEAD_DIM ** -0.5 * LOG2E), wfk, wfv,
                          wdq * (DIFF_QK_DIM ** -0.5 * LOG2E), wdk, wdv, wkc, wvc], axis=1).astype(BF16)
    pad = LANES - NSA_GATE - 3 * FOX_HEADS
    ws = jnp.concatenate([wgt[:, gate_perm], wff, wff, wff, jnp.zeros((D_MODEL, pad), F32)], axis=1)
    wsh, wsl = _split_hi_lo(ws)
    bs = jnp.concatenate([b_gate[l][gate_perm], b_fgt[l], b_fgt[l], b_fgt[l], jnp.zeros((pad,), F32)])[None, :]

    eye = jnp.eye(NSA_KV_GROUPS, dtype=F32)
    half = CMP_BLOCK // 2
    cmp = []
    for pe, w1, w2 in zip(cmp_pe, cmp_w1, cmp_w2):
        w1r = w1[l].reshape(2, half, HEAD_DIM, CMP_HIDDEN)
        w1x = jnp.einsum('srdk,gh->srgdhk', w1r, eye).reshape(2, half * NSA_KV, NSA_KV_GROUPS * CMP_HIDDEN)
        w2bd = jnp.einsum('kd,gh->gkhd', w2[l], eye).reshape(NSA_KV_GROUPS * CMP_HIDDEN, NSA_KV)
        pe2 = jnp.broadcast_to(pe[l].reshape(2, half, 1, HEAD_DIM), (2, half, NSA_KV_GROUPS, HEAD_DIM))
        cmp.append((pe2.reshape(2, half * NSA_KV), w1x[0].astype(BF16), w1x[1].astype(BF16), w2bd.astype(BF16)))

    wo = w_out[l]
    wa = wo[:NSA_Q][perm].astype(BF16)
    wb = wo[NSA_Q:NSA_Q + FOX_W].astype(BF16)
    wc = wo[NSA_Q + FOX_W:].astype(BF16)
    return dict(wm=wm, wsh=wsh, wsl=wsl, bs=bs, cmp_k=cmp[0], cmp_v=cmp[1], wa=wa, wb=wb, wc=wc)


def _overlap_matrix(n_slab, n_cmp, n_sel):
    c_start = np.arange(n_cmp) * CMP_STRIDE
    s_start = np.arange(n_sel) * SLC_BLOCK
    ov = np.clip(np.minimum(c_start[:, None] + CMP_BLOCK, s_start[None, :] + SLC_BLOCK)
                 - np.maximum(c_start[:, None], s_start[None, :]), 0, None) / CMP_BLOCK
    full = np.zeros((n_slab, LANES), np.float32)
    full[:n_cmp, :n_sel] = ov
    return jnp.asarray(full, dtype=BF16)


def kernel(x, norm_attn, w_in, b_gate, b_fgt, cmp_k_pe, cmp_k_w1, cmp_k_w2, cmp_v_pe, cmp_v_w1, cmp_v_w2, diff_lambda, diff_subln, w_out, norm_ffn, w_grp, b_grp, w_exp, b_exp, w_e_gate, w_e_up, w_e_down, norm_final):
    b, t, d = x.shape
    depth = w_in.shape[0]
    n = b * t
    n_slab = t // CMP_STRIDE
    n_cmp = (t - CMP_BLOCK) // CMP_STRIDE + 1
    n_sel = t // SLC_BLOCK
    top_n = min(SLC_TOPK, n_sel)
    ov = _overlap_matrix(n_slab, n_cmp, n_sel)

    wg_all = w_e_gate.astype(BF16)
    wu_all = w_e_up.astype(BF16)
    wd_all = w_e_down.astype(BF16)
    gf = norm_final[None, :]

    x2 = x.reshape(n, d)
    for l in range(depth):
        p = _layer_params(l, w_in, b_gate, b_fgt, (cmp_k_pe, cmp_v_pe), (cmp_k_w1, cmp_v_w1),
                          (cmp_k_w2, cmp_v_w2), w_out)
        main, kc, vc, small = _inproj(x2, norm_attn[l][None, :], p["wm"], p["wsh"], p["wsl"], p["bs"])
        main3 = main.reshape(b, t, MAIN_W)
        small3 = small.reshape(b, t, LANES)
        kcmp = _compress(kc.reshape(b, n_slab, CMP_STRIDE * NSA_KV), *p["cmp_k"])
        vcmp = _compress(vc.reshape(b, n_slab, CMP_STRIDE * NSA_KV), *p["cmp_v"])
        o_nsa, pen = _cmp_attn(main3, kcmp, vcmp, small3, ov, n_cmp, n_sel, top_n)
        o_nsa = _flash(main3, "slc", (pen, small3, o_nsa), F32)
        o_nsa = _flash(main3, "win", (small3, o_nsa), F32)
        o_fox = _flash(main3, "fox", (_cumgate(small3),), BF16)
        sub2 = jnp.concatenate([diff_subln[l], diff_subln[l]])[None, :]
        o_diff = _flash(main3, "diff", (diff_lambda[l], sub2), BF16, lam_init=_lambda_init(l))

        x2 = _outproj(x2, o_nsa.reshape(n, NSA_Q), o_fox.reshape(n, FOX_W), o_diff.reshape(n, DIFF_W),
                      p["wa"], p["wb"], p["wc"])

        wr = jnp.concatenate([w_grp[l], w_exp[l], jnp.zeros((d, LANES - N_GROUPS - N_EXPERTS), F32)], axis=1)
        wrh, wrl = _split_hi_lo(wr)
        br = jnp.concatenate([b_grp[l], b_exp[l], jnp.zeros((LANES - N_GROUPS - N_EXPERTS,), F32)])[None, :]
        x2 = _moe(x2, norm_ffn[l][None, :], wrh, wrl, br, wg_all, wu_all, wd_all, gf, l,
                  final_norm=(l == depth - 1))
    return x2.reshape(b, t, d)
```

```python
import functools
import math

import ml_dtypes
import numpy as np
import jax
import jax.numpy as jnp
from jax import lax
from jax.experimental import pallas as pl
from jax.experimental.pallas import tpu as pltpu

F32 = jnp.float32
BF16 = jnp.bfloat16

D_MODEL = 1024
HEAD_DIM = 64
NSA_HEADS = 8
NSA_KV_GROUPS = 2
CMP_BLOCK = 32
CMP_STRIDE = 16
CMP_HIDDEN = 256
SLC_BLOCK = 64
SLC_SHIFT = 6
SLC_TOPK = 16
WINDOW = 512
FOX_HEADS = 4
DIFF_HEADS = 4
DIFF_QK_DIM = HEAD_DIM // 2
N_GROUPS = 4
EXPERTS_PER_GROUP = 8
N_EXPERTS = N_GROUPS * EXPERTS_PER_GROUP
D_EXPERT = 256
RMS_EPS = 1e-6
FORCE_SCORE = 1e4
NEG_INF = -1e30
MASKED = -2e30

NSA_Q = NSA_HEADS * HEAD_DIM
NSA_KV = NSA_KV_GROUPS * HEAD_DIM
NSA_GATE = NSA_HEADS * 3
FOX_W = FOX_HEADS * HEAD_DIM
DIFF_W = DIFF_HEADS * HEAD_DIM
MIX_WIDTH = NSA_Q + FOX_W + DIFF_W
IN_WIDTHS = (NSA_Q, NSA_KV, NSA_KV, NSA_KV, NSA_KV, NSA_KV, NSA_KV, NSA_GATE,
             FOX_W, FOX_W, FOX_W, FOX_HEADS, DIFF_W, DIFF_W, DIFF_W)

LANES = 128
SUBLANES = 8
VMEM_LIMIT_BYTES = 56 * 1024 * 1024

MAIN_W = NSA_Q + 4 * NSA_KV + 3 * FOX_W + 3 * DIFF_W
PROJ_W = MAIN_W + 2 * NSA_KV
COL_KS, COL_VS, COL_KW, COL_VW = 4, 5, 6, 7
COL_FQ, COL_FK, COL_FV = 4, 5, 6
COL_DQ, COL_DK, COL_DV = 7, 8, 9
FGT_LANES = (NSA_GATE, NSA_GATE + FOX_HEADS, NSA_GATE + 2 * FOX_HEADS)
LOG2E = 1.4426950408889634

TQ = 512
TK = 512
KSTRIP = 256
ONES_ROWS = 16

def _alibi(n):
    return [float(2.0 ** (-8.0 * (i + 1) / n)) for i in range(n)]


def _lambda_init(layer):
    return 0.8 - 0.6 * math.exp(-0.3 * layer)


def _cparams(*sem):
    return pltpu.CompilerParams(dimension_semantics=sem, vmem_limit_bytes=VMEM_LIMIT_BYTES)


def _dot(a, b):
    return jnp.dot(a, b, preferred_element_type=F32)


def _dot_nt(a, b):
    return lax.dot_general(a, b, (((1,), (1,)), ((), ())), preferred_element_type=F32)


def _split2(x):
    hi = x.astype(BF16)
    lo = (x - hi.astype(F32)).astype(BF16)
    return hi, lo


def _sigmoid(z):
    return 1.0 / (1.0 + jnp.exp(-z))


def _keep_lanes(sel, blk):
    return jnp.where(sel, blk.astype(F32), 0.0).astype(BF16)


def _inproj_kernel(x_ref, g_ref, wm_ref, wsh_ref, wsl_ref, bs_ref, main_ref, kc_ref, vc_ref, small_ref):
    x = x_ref[...]
    ms = jnp.mean(x * x, axis=-1, keepdims=True)
    y = x * lax.rsqrt(ms + RMS_EPS) * g_ref[...]
    hb, hl = _split2(y)
    full = _dot(hb, wm_ref[...]).astype(BF16)
    main_ref[...] = full[:, :MAIN_W]
    kc_ref[...] = full[:, MAIN_W:MAIN_W + NSA_KV]
    vc_ref[...] = full[:, MAIN_W + NSA_KV:]
    small_ref[...] = (_dot(hb, wsh_ref[...]) + _dot(hl, wsh_ref[...]) + _dot(hb, wsl_ref[...])) + bs_ref[...]


def _inproj(x2, g, wm, wsh, wsl, bs, tm=512):
    n = x2.shape[0]
    full = lambda i: (0, 0)
    return pl.pallas_call(
        _inproj_kernel,
        grid=(n // tm,),
        in_specs=[pl.BlockSpec((tm, D_MODEL), lambda i: (i, 0)),
                  pl.BlockSpec((1, D_MODEL), full),
                  pl.BlockSpec((D_MODEL, PROJ_W), full),
                  pl.BlockSpec((D_MODEL, LANES), full),
                  pl.BlockSpec((D_MODEL, LANES), full),
                  pl.BlockSpec((1, LANES), full)],
        out_specs=[pl.BlockSpec((tm, MAIN_W), lambda i: (i, 0)),
                   pl.BlockSpec((tm, NSA_KV), lambda i: (i, 0)),
                   pl.BlockSpec((tm, NSA_KV), lambda i: (i, 0)),
                   pl.BlockSpec((tm, LANES), lambda i: (i, 0))],
        out_shape=[jax.ShapeDtypeStruct((n, MAIN_W), BF16),
                   jax.ShapeDtypeStruct((n, NSA_KV), BF16),
                   jax.ShapeDtypeStruct((n, NSA_KV), BF16),
                   jax.ShapeDtypeStruct((n, LANES), F32)],
        compiler_params=_cparams("parallel"),
        name="inproj",
    )(x2, g, wm, wsh, wsl, bs)


def _compress_kernel(r_ref, pe_ref, w1a_ref, w1b_ref, w2_ref, o_ref):
    r = r_ref[0].astype(F32)
    pe = pe_ref[...]
    ra = (r + pe[0:1, :]).astype(BF16)
    rb = (r + pe[1:2, :]).astype(BF16)
    a = _dot(ra, w1a_ref[...])
    b = _dot(rb, w1b_ref[...])
    n_slab = a.shape[0]
    hid = a + pltpu.roll(b, n_slab - 1, 0)
    hid = jax.nn.gelu(hid)
    o_ref[0] = _dot(hid.astype(BF16), w2_ref[...]).astype(BF16)


def _compress(r, pe2, w1a, w1b, w2bd):
    b, n_slab, w = r.shape
    full = lambda i: (0, 0)
    return pl.pallas_call(
        _compress_kernel,
        grid=(b,),
        in_specs=[pl.BlockSpec((1, n_slab, w), lambda i: (i, 0, 0)),
                  pl.BlockSpec((2, w), full),
                  pl.BlockSpec((w, 2 * CMP_HIDDEN), full),
                  pl.BlockSpec((w, 2 * CMP_HIDDEN), full),
                  pl.BlockSpec((2 * CMP_HIDDEN, LANES), full)],
        out_specs=pl.BlockSpec((1, n_slab, LANES), lambda i: (i, 0, 0)),
        out_shape=jax.ShapeDtypeStruct((b, n_slab, LANES), BF16),
        compiler_params=_cparams("parallel"),
        name="compress",
    )(r, pe2, w1a, w1b, w2bd)


def _cumgate_kernel(z_ref, o_ref, carry_ref, *, tc):
    @pl.when(pl.program_id(1) == 0)
    def _():
        carry_ref[...] = jnp.zeros_like(carry_ref)

    z = z_ref[0]
    logf = -(jnp.maximum(-z, 0.0) + jnp.log(1.0 + jnp.exp(-jnp.abs(z))))
    hi = logf.astype(BF16)
    r1 = logf - hi.astype(F32)
    mid = r1.astype(BF16)
    lo = (r1 - mid.astype(F32)).astype(BF16)
    tri = (lax.broadcasted_iota(jnp.int32, (tc, tc), 0) >= lax.broadcasted_iota(jnp.int32, (tc, tc), 1))
    tri = jnp.where(tri, 1.0, 0.0).astype(BF16)
    c = (_dot(tri, hi) + _dot(tri, mid)) + _dot(tri, lo) + carry_ref[0:1, :]
    carry_ref[...] = jnp.broadcast_to(c[tc - 1:tc, :], carry_ref.shape)
    v = c * (-LOG2E)
    p0 = v.astype(BF16)
    r1 = v - p0.astype(F32)
    p1 = r1.astype(BF16)
    p2 = (r1 - p1.astype(F32)).astype(BF16)
    lane = lax.broadcasted_iota(jnp.int32, v.shape, 1)
    pieces = jnp.where(lane < FGT_LANES[1], p0.astype(F32), jnp.where(lane < FGT_LANES[2], p1.astype(F32), p2.astype(F32)))
    keep = (lane >= FGT_LANES[0]) & (lane < FGT_LANES[2] + FOX_HEADS)
    o_ref[0] = jnp.where(keep, pieces, 0.0).astype(BF16)


def _cumgate(small3, tc=512):
    b, t, w = small3.shape
    return pl.pallas_call(
        functools.partial(_cumgate_kernel, tc=tc),
        grid=(b, t // tc),
        in_specs=[pl.BlockSpec((1, tc, w), lambda i, j: (i, j, 0))],
        out_specs=pl.BlockSpec((1, tc, w), lambda i, j: (i, j, 0)),
        out_shape=jax.ShapeDtypeStruct((b, t, w), BF16),
        scratch_shapes=[pltpu.VMEM((8, w), F32)],
        compiler_params=_cparams("parallel", "arbitrary"),
        name="cumgate",
    )(small3)


def _cmp_kernel(q_ref, kc_ref, vc_ref, sm_ref, ov_ref, o_ref, pen_ref, *, tq, n_cmp, n_sel, top_n):
    t0 = pl.program_id(1) * tq
    nck = kc_ref.shape[1]
    row = t0 + lax.broadcasted_iota(jnp.int32, (tq, nck), 0)
    col = lax.broadcasted_iota(jnp.int32, (tq, nck), 1)
    dist = row - (col * CMP_STRIDE + (CMP_BLOCK - 1))
    mask = (dist >= 0) & (col < n_cmp)
    distf = dist.astype(F32)
    lane = lax.broadcasted_iota(jnp.int32, (tq, LANES), 1)
    low = lane < HEAD_DIM
    kc = kc_ref[0]
    vc = vc_ref[0]
    sig = _sigmoid(sm_ref[0])
    slopes = _alibi(NSA_HEADS)
    psum = [None, None]
    hg = NSA_HEADS // NSA_KV_GROUPS
    for j in range(hg):
        qblk = q_ref[0, :, j * LANES:(j + 1) * LANES]
        outs = []
        for g in range(NSA_KV_GROUPS):
            h = j + hg * g
            qm = _keep_lanes(low if g == 0 else ~low, qblk)
            s = _dot_nt(qm, kc) - (slopes[h] * LOG2E) * distf
            s = jnp.where(mask, s, NEG_INF)
            m = jnp.max(s, axis=-1, keepdims=True)
            p = jnp.where(mask, jnp.exp2(s - m), 0.0)
            l = jnp.sum(p, axis=-1, keepdims=True)
            p = p * jnp.where(l > 0.0, 1.0 / l, 0.0)
            psum[g] = p if psum[g] is None else psum[g] + p
            outs.append(_dot(p.astype(BF16), vc))
        gate = jnp.where(low, sig[:, j:j + 1], sig[:, j + hg:j + hg + 1])
        o_ref[0, :, j * LANES:(j + 1) * LANES] = jnp.where(low, outs[0], outs[1]) * gate

    sub8 = lax.broadcasted_iota(jnp.int32, (SUBLANES, tq), 0)
    cur_r = (t0 + lax.broadcasted_iota(jnp.int32, (tq, LANES), 0)) >> SLC_SHIFT
    pen_t = []
    for g in range(NSA_KV_GROUPS):
        ph, plo = _split2(psum[g])
        imp = _dot(ph, ov_ref[...]) + _dot(plo, ov_ref[...])
        valid = lane <= cur_r
        forced = (lane == 0) | (lane == cur_r) | (lane == cur_r - 1)
        imp = jnp.where(valid, jnp.where(forced, FORCE_SCORE, imp), NEG_INF)
        imp_t = imp.T[:n_sel, :]
        ngrp = n_sel // SUBLANES
        rows = [imp_t[i * SUBLANES:(i + 1) * SUBLANES, :] for i in range(ngrp)]
        cnt = [jnp.zeros((SUBLANES, tq), F32) for _ in range(ngrp)]
        for k in range(n_sel):
            rk = imp_t[k:k + 1, :]
            for i in range(ngrp):
                if i * SUBLANES > k:
                    one = jnp.where(rk >= rows[i], 1.0, 0.0)
                elif (i + 1) * SUBLANES <= k:
                    one = jnp.where(rk > rows[i], 1.0, 0.0)
                else:
                    one = jnp.where(sub8 > k - i * SUBLANES,
                                    jnp.where(rk >= rows[i], 1.0, 0.0), jnp.where(rk > rows[i], 1.0, 0.0))
                cnt[i] = cnt[i] + one
        pen_t.append(jnp.where(jnp.concatenate(cnt, axis=0) < float(top_n), 0.0, MASKED))
    for g in range(NSA_KV_GROUPS):
        full = jnp.concatenate([pen_t[g], jnp.zeros((LANES - n_sel, tq), F32)], axis=0)
        pen_ref[0, :, g * LANES:(g + 1) * LANES] = full.T.astype(BF16)


def _cmp_attn(main3, kcmp, vcmp, small3, ov, n_cmp, n_sel, top_n, tq=256):
    b, t, _ = main3.shape
    nck = kcmp.shape[1]
    return pl.pallas_call(
        functools.partial(_cmp_kernel, tq=tq, n_cmp=n_cmp, n_sel=n_sel, top_n=top_n),
        grid=(b, t // tq),
        in_specs=[pl.BlockSpec((1, tq, NSA_Q), lambda i, j: (i, j, 0)),
                  pl.BlockSpec((1, nck, LANES), lambda i, j: (i, 0, 0)),
                  pl.BlockSpec((1, nck, LANES), lambda i, j: (i, 0, 0)),
                  pl.BlockSpec((1, tq, LANES), lambda i, j: (i, j, 0)),
                  pl.BlockSpec((nck, LANES), lambda i, j: (0, 0))],
        out_specs=[pl.BlockSpec((1, tq, NSA_Q), lambda i, j: (i, j, 0)),
                   pl.BlockSpec((1, tq, NSA_KV_GROUPS * LANES), lambda i, j: (i, j, 0))],
        out_shape=[jax.ShapeDtypeStruct((b, t, NSA_Q), F32),
                   jax.ShapeDtypeStruct((b, t, NSA_KV_GROUPS * LANES), BF16)],
        compiler_params=_cparams("parallel", "parallel"),
        name="cmp_attn",
    )(main3, kcmp, vcmp, small3, ov)


def _bf16_terms(c, n=3):
    out, r = [], float(c)
    for _ in range(n):
        p = float(np.float32(r).astype(ml_dtypes.bfloat16))
        out.append(p)
        r -= p
    return out


def _alibi_q_aug(lane, a0, slope):
    aug = jnp.zeros(lane.shape, F32)
    for i, c in enumerate(_bf16_terms(slope * LOG2E)):
        aug = jnp.where(lane == a0 + 2 * i, float(SLC_BLOCK) * c, aug)
        aug = jnp.where(lane == a0 + 2 * i + 1, c, aug)
    return aug


def _alibi_k_aug(lane, rel, a0):
    hi = (rel >> SLC_SHIFT).astype(F32)
    lo = (rel & (SLC_BLOCK - 1)).astype(F32)
    inside = (lane >= a0) & (lane < a0 + 6)
    odd = ((lane - a0) & 1) == 1
    return jnp.where(inside, jnp.where(odd, lo, hi), 0.0)


ALIBI_LANE = {"slc": SLC_BLOCK, "win": 0, "diff": 0}
STACKS = {"slc": (1, 8), "win": (1, 8), "fox": (2, 2), "diff": (2, 4)}


def _v_half(mode, vh):
    nv = STACKS[mode][1]
    if mode in ("slc", "win"):
        return vh // (NSA_HEADS // NSA_KV_GROUPS)
    return (vh % nv) // (nv // 2)


def _flash_kernel(*refs, mode, tq, tk, lam_init):
    sched_ref, refs = refs[0], refs[1:]
    if mode == "slc":
        q_ref, k_ref, v_ref, pen_ref, sm_ref, add_ref, o_ref, qst_ref, m_ref, acc_ref = refs
    elif mode == "win":
        q_ref, k_ref, v_ref, sm_ref, add_ref, o_ref, qst_ref, m_ref, acc_ref = refs
    elif mode == "fox":
        q_ref, k_ref, v_ref, fa_ref, o_ref, qst_ref, m_ref, acc_ref = refs
    else:
        q_ref, k_ref, v_ref, lam_ref, sub_ref, o_ref, qst_ref, m_ref, acc_ref = refs
    nstack, nv = STACKS[mode]
    nsa = mode in ("slc", "win")
    hg = NSA_HEADS // NSA_KV_GROUPS
    step_id = pl.program_id(1)
    q0 = sched_ref[SCHED_Q, step_id] * tq
    k0 = sched_ref[SCHED_K, step_id] * tk

    @pl.when(sched_ref[SCHED_FIRST, step_id] == 1)
    def _():
        m_ref[...] = jnp.full(m_ref.shape, NEG_INF, F32)
        acc_ref[...] = jnp.zeros(acc_ref.shape, F32)
        lane = lax.broadcasted_iota(jnp.int32, (tq, LANES), 1)
        for vh in range(nstack * nv):
            if nsa:
                jb, g = vh % hg, vh // hg
                lo_lane, width = g * HEAD_DIM, HEAD_DIM
                aug = _alibi_q_aug(lane, ALIBI_LANE[mode], _alibi(NSA_HEADS)[vh])
                if mode == "slc":
                    aug = aug + pen_ref[0, :, g * LANES:(g + 1) * LANES].astype(F32)
            elif mode == "fox":
                jb, r = vh // nv, vh % nv
                lo_lane, width = r * HEAD_DIM, HEAD_DIM
                hit = (lane == FGT_LANES[0] + vh) | (lane == FGT_LANES[1] + vh) | (lane == FGT_LANES[2] + vh)
                aug = jnp.where(hit, 1.0, 0.0)
            else:
                jb, r = vh // nv, vh % nv
                lo_lane, width = r * DIFF_QK_DIM, DIFF_QK_DIM
                aug = _alibi_q_aug(lane, ALIBI_LANE[mode], _alibi(DIFF_HEADS)[vh // 2])
            sel = (lane >= lo_lane) & (lane < lo_lane + width)
            qst_ref[vh * tq:(vh + 1) * tq, 0:LANES] = _keep_lanes(sel, q_ref[0, :, jb * LANES:(jb + 1) * LANES])
            qst_ref[vh * tq:(vh + 1) * tq, LANES:2 * LANES] = aug.astype(BF16)

    def step(masked):
        lane_k = lax.broadcasted_iota(jnp.int32, (tk, LANES), 1)
        row_k = lax.broadcasted_iota(jnp.int32, (tk, LANES), 0)
        if mode == "fox":
            k_aug = fa_ref[0]
        else:
            ka = _alibi_k_aug(lane_k, (k0 - q0) + row_k, ALIBI_LANE[mode])
            if mode == "slc":
                ka = ka + jnp.where(((k0 + row_k) >> SLC_SHIFT) == lane_k, 1.0, 0.0)
            k_aug = ka.astype(BF16)
        if masked:
            dist = (q0 - k0) + lax.broadcasted_iota(jnp.int32, (tk, tq), 1) - lax.broadcasted_iota(jnp.int32, (tk, tq), 0)
            ok = dist >= 0
            if mode == "win":
                ok = ok & (dist < WINDOW)
            bias = jnp.where(ok, 0.0, MASKED)
        keys, vals_t = [], []
        for st in range(nstack):
            kblk = k_ref[0] if nsa else k_ref[0, :, st * LANES:(st + 1) * LANES]
            vblk = v_ref[0] if nsa else v_ref[0, :, st * LANES:(st + 1) * LANES]
            keys.append(jnp.concatenate([kblk, k_aug], axis=1))
            v_t = vblk.astype(F32).T.astype(BF16)
            ones = jnp.ones((ONES_ROWS, tk), BF16)
            vals_t.append([jnp.concatenate([v_t[hf * HEAD_DIM:(hf + 1) * HEAD_DIM, :], ones], axis=0)
                           for hf in range(LANES // HEAD_DIM)])

        nstrip = tk // KSTRIP
        nvh = nstack * nv

        def scores(vh):
            s = _dot_nt(keys[vh // nv], qst_ref[vh * tq:(vh + 1) * tq, :])
            if masked:
                s = s + bias
            return s, jnp.max(s, axis=0, keepdims=True)

        cur, cur_max = scores(0)
        for vh in range(nvh):
            cols = slice(vh * tq, (vh + 1) * tq)
            v_half = vals_t[vh // nv][_v_half(mode, vh)]
            m_old = m_ref[:, cols]
            m_new = jnp.maximum(m_old, cur_max)
            alpha = jnp.exp2(m_old - m_new)
            if vh + 1 < nvh:
                nxt, nxt_max = scores(vh + 1)
            pv = None
            for r in range(nstrip):
                p = jnp.exp2((cur[r * KSTRIP:(r + 1) * KSTRIP, :] - m_new).astype(BF16))
                part = _dot(v_half[:, r * KSTRIP:(r + 1) * KSTRIP], p)
                pv = part if pv is None else pv + part
            m_ref[:, cols] = m_new
            acc_ref[:, cols] = alpha * acc_ref[:, cols] + pv
            if vh + 1 < nvh:
                cur, cur_max = nxt, nxt_max

    pl.when(sched_ref[SCHED_MASKED, step_id] == 1)(lambda: step(True))
    if mode != "win":
        pl.when(sched_ref[SCHED_MASKED, step_id] == 0)(lambda: step(False))

    @pl.when(sched_ref[SCHED_LAST, step_id] == 1)
    def _():
        inv = 1.0 / acc_ref[HEAD_DIM:HEAD_DIM + 1, :]

        def out_t(vh):
            return acc_ref[0:HEAD_DIM, vh * tq:(vh + 1) * tq] * inv[:, vh * tq:(vh + 1) * tq]

        def pair(a, b):
            return jnp.concatenate([a, b], axis=0).T

        low = lax.broadcasted_iota(jnp.int32, (tq, LANES), 1) < HEAD_DIM
        if nsa:
            sig = _sigmoid(sm_ref[0])
            br = 1 if mode == "slc" else 2
            for jb in range(hg):
                o = pair(out_t(jb), out_t(jb + hg))
                c0 = br * NSA_HEADS + jb
                gate = jnp.where(low, sig[:, c0:c0 + 1], sig[:, c0 + hg:c0 + hg + 1])
                o_ref[0, :, jb * LANES:(jb + 1) * LANES] = add_ref[0, :, jb * LANES:(jb + 1) * LANES] + o * gate
        elif mode == "fox":
            for pb in range(nstack):
                o = pair(out_t(2 * pb), out_t(2 * pb + 1))
                o_ref[0, :, pb * LANES:(pb + 1) * LANES] = o.astype(o_ref.dtype)
        else:
            lam = lam_ref[...]
            lam_full = (jnp.exp(jnp.sum(lam[0:1, :] * lam[1:2, :], axis=-1, keepdims=True))
                        - jnp.exp(jnp.sum(lam[2:3, :] * lam[3:4, :], axis=-1, keepdims=True)) + lam_init)
            for pb in range(nstack):
                heads = [out_t(pb * nv + 2 * hh) - lam_full * out_t(pb * nv + 2 * hh + 1) for hh in range(2)]
                o = pair(heads[0], heads[1])
                sq = o * o
                ss_lo = jnp.sum(jnp.where(low, sq, 0.0), axis=-1, keepdims=True)
                ss_hi = jnp.sum(jnp.where(low, 0.0, sq), axis=-1, keepdims=True)
                ms = jnp.where(low, ss_lo, ss_hi) * (1.0 / HEAD_DIM)
                y = o * lax.rsqrt(ms + RMS_EPS) * sub_ref[...] * (1.0 - lam_init)
                o_ref[0, :, pb * LANES:(pb + 1) * LANES] = y.astype(o_ref.dtype)


def _flash(main3, mode, extra, out_dtype, lam_init=0.0, tq=TQ, tk=TK):
    b, t, _ = main3.shape
    tk = min(tk, t)
    nstack, nv = STACKS[mode]
    sched = _flash_schedule(t, tq, tk, mode == "win")
    qtile = lambda w, c: pl.BlockSpec((1, tq, w), lambda bi, s, sch: (bi, sch[SCHED_Q, s], c))
    ktile = lambda w, c: pl.BlockSpec((1, tk, w), lambda bi, s, sch: (bi, sch[SCHED_K, s], c))
    if mode in ("slc", "win"):
        kcol, vcol = (COL_KS, COL_VS) if mode == "slc" else (COL_KW, COL_VW)
        in_specs = [qtile(NSA_Q, 0), ktile(LANES, kcol), ktile(LANES, vcol)]
        if mode == "slc":
            in_specs.append(qtile(NSA_KV_GROUPS * LANES, 0))
        in_specs += [qtile(LANES, 0), qtile(NSA_Q, 0)]
        out_w = NSA_Q
    else:
        qc, kc, vc = (COL_FQ, COL_FK, COL_FV) if mode == "fox" else (COL_DQ, COL_DK, COL_DV)
        in_specs = [qtile(2 * LANES, qc), ktile(2 * LANES, kc), ktile(2 * LANES, vc)]
        if mode == "fox":
            in_specs.append(ktile(LANES, 0))
        else:
            in_specs += [pl.BlockSpec((4, DIFF_QK_DIM), lambda bi, s, sch: (0, 0)),
                         pl.BlockSpec((1, LANES), lambda bi, s, sch: (0, 0))]
        out_w = 2 * LANES
    cols = nstack * nv * tq
    return pl.pallas_call(
        functools.partial(_flash_kernel, mode=mode, tq=tq, tk=tk, lam_init=lam_init),
        grid_spec=pltpu.PrefetchScalarGridSpec(
            num_scalar_prefetch=1,
            grid=(b, sched.shape[1]),
            in_specs=in_specs,
            out_specs=qtile(out_w, 0),
            scratch_shapes=[pltpu.VMEM((cols, 2 * LANES), BF16),
                            pltpu.VMEM((1, cols), F32),
                            pltpu.VMEM((HEAD_DIM + ONES_ROWS, cols), F32)]),
        out_shape=jax.ShapeDtypeStruct((b, t, out_w), out_dtype),
        compiler_params=_cparams("parallel", "arbitrary"),
        name="flash_" + mode,
    )(jnp.asarray(sched), main3, main3, main3, *extra)


SCHED_Q, SCHED_K, SCHED_FIRST, SCHED_LAST, SCHED_MASKED = range(5)


def _flash_schedule(t, tq, tk, window):
    rows = []
    for i in range(t // tq):
        q_lo, q_hi = i * tq, i * tq + tq - 1
        k_first = max(q_lo - (WINDOW - 1), 0) // tk if window else 0
        tiles = list(range(k_first, q_hi // tk + 1))
        for n, kt in enumerate(tiles):
            fully_visible = kt * tk + tk - 1 <= q_lo and not window
            rows.append((i, kt, int(n == 0), int(n == len(tiles) - 1), int(not fully_visible)))
    return np.asarray(rows, np.int32).T


def _outproj_kernel(x_ref, oa_ref, ob_ref, oc_ref, wa_ref, wb_ref, wc_ref, o_ref):
    acc = _dot(oa_ref[...].astype(BF16), wa_ref[...])
    acc = acc + _dot(ob_ref[...], wb_ref[...])
    acc = acc + _dot(oc_ref[...], wc_ref[...])
    o_ref[...] = x_ref[...] + acc


def _outproj(x2, oa, ob, oc, wa, wb, wc, tm=512):
    n = x2.shape[0]
    row = lambda i: (i, 0)
    full = lambda i: (0, 0)
    return pl.pallas_call(
        _outproj_kernel,
        grid=(n // tm,),
        in_specs=[pl.BlockSpec((tm, D_MODEL), row),
                  pl.BlockSpec((tm, NSA_Q), row),
                  pl.BlockSpec((tm, FOX_W), row),
                  pl.BlockSpec((tm, DIFF_W), row),
                  pl.BlockSpec((NSA_Q, D_MODEL), full),
                  pl.BlockSpec((FOX_W, D_MODEL), full),
                  pl.BlockSpec((DIFF_W, D_MODEL), full)],
        out_specs=pl.BlockSpec((tm, D_MODEL), row),
        out_shape=jax.ShapeDtypeStruct((n, D_MODEL), F32),
        compiler_params=_cparams("parallel"),
        name="outproj",
    )(x2, oa, ob, oc, wa, wb, wc)


ROUTE_OFF = N_GROUPS


MOE_CHUNK = 304
MOE_PREFIX_BLOCK = 256


def _moe_kernel(x_ref, g_ref, wrh_ref, wrl_ref, br_ref, wg_ref, wu_ref, wd_ref, gf_ref, o_ref,
                h_ref, gsel_ref, col_ref, row_ref, cnt_ref, *, final_norm):
    grp_id = pl.program_id(1)
    tm = x_ref.shape[0]
    ch = MOE_CHUNK

    @pl.when(grp_id == 0)
    def _():
        x = x_ref[...]
        ms = jnp.mean(x * x, axis=-1, keepdims=True)
        y = x * lax.rsqrt(ms + RMS_EPS) * g_ref[...]
        hb, hl = _split2(y)
        h_ref[...] = hb
        lg = (_dot(hb, wrh_ref[...]) + _dot(hl, wrh_ref[...]) + _dot(hb, wrl_ref[...])) + br_ref[...]
        lane = lax.broadcasted_iota(jnp.int32, lg.shape, 1).astype(F32)
        big = float(LANES)
        isg = lane < N_GROUPS
        gmax = jnp.max(jnp.where(isg, lg, -jnp.inf), axis=-1, keepdims=True)
        grp = jnp.min(jnp.where(isg & (lg == gmax), lane, big), axis=-1, keepdims=True)
        gprob = 1.0 / jnp.sum(jnp.where(isg, jnp.exp(lg - gmax), 0.0), axis=-1, keepdims=True)
        lo_lane = ROUTE_OFF + grp * EXPERTS_PER_GROUP
        ing = (lane >= lo_lane) & (lane < lo_lane + EXPERTS_PER_GROUP)
        v1 = jnp.max(jnp.where(ing, lg, -jnp.inf), axis=-1, keepdims=True)
        i1 = jnp.min(jnp.where(ing & (lg == v1), lane, big), axis=-1, keepdims=True)
        rest = ing & (lane != i1)
        v2 = jnp.max(jnp.where(rest, lg, -jnp.inf), axis=-1, keepdims=True)
        i2 = jnp.min(jnp.where(rest & (lg == v2), lane, big), axis=-1, keepdims=True)
        e2 = jnp.exp(v2 - v1)
        w1 = gprob / (1.0 + e2)
        w2 = gprob * e2 / (1.0 + e2)
        gate = jnp.where(lane == i1, w1, 0.0) + jnp.where(lane == i2, w2, 0.0)
        gsel = jnp.zeros_like(gate)
        for gg in range(N_GROUPS):
            shifted = pltpu.roll(gate, LANES - (ROUTE_OFF + gg * EXPERTS_PER_GROUP), 1)
            gsel = jnp.where(grp == float(gg), shifted, gsel)
        gsel = jnp.where(lane < EXPERTS_PER_GROUP, gsel, 0.0)
        ghi, glo = _split2(gsel)
        gsel_ref[:, 0:LANES] = ghi
        gsel_ref[:, LANES:2 * LANES] = glo
        onehot = jnp.where(lane == grp, 1.0, 0.0)
        pb = MOE_PREFIX_BLOCK
        tri = jnp.where(lax.broadcasted_iota(jnp.int32, (pb, pb), 0) > lax.broadcasted_iota(jnp.int32, (pb, pb), 1),
                        1.0, 0.0).astype(BF16)
        carry = jnp.zeros((1, LANES), F32)
        lane_b = lax.broadcasted_iota(jnp.int32, (pb, LANES), 1)
        for blk in range(tm // pb):
            oh_b = onehot[blk * pb:(blk + 1) * pb, :]
            prefix = _dot(tri, oh_b.astype(BF16)) + carry
            pos = jnp.sum(oh_b * prefix, axis=-1, keepdims=True)
            col_ref[blk * pb:(blk + 1) * pb, :] = jnp.where(lane_b == 1, pos, 0.0)
            carry = carry + jnp.sum(oh_b, axis=0, keepdims=True)
        for gg in range(N_GROUPS):
            cnt_ref[gg] = carry[0, gg].astype(jnp.int32)
        colinfo = col_ref[...] + jnp.where(lane == 0.0, grp, 0.0)
        col_ref[...] = colinfo
        row_ref[...] = colinfo.T[0:8, :]
        o_ref[...] = x

    grp_f = grp_id.astype(F32)
    key_col = jnp.where(col_ref[:, 0:1] == grp_f, col_ref[:, 1:2], -1.0)
    key_row = jnp.where(row_ref[0:1, :] == grp_f, row_ref[1:2, :], -1.0)

    def chunk(c, carry):
        base = (c * ch).astype(F32)
        sub_iota = lax.broadcasted_iota(jnp.int32, (ch, tm), 0).astype(F32)
        gather = jnp.where(key_row - base == sub_iota, 1.0, 0.0).astype(BF16)
        xg = _dot(gather, h_ref[...]).astype(BF16)
        gparts = _dot(gather, gsel_ref[...])
        gates = gparts[:, 0:LANES] + gparts[:, LANES:2 * LANES]
        y = None
        for e in range(EXPERTS_PER_GROUP):
            a = _dot(xg, wg_ref[0, e])
            u = _dot(xg, wu_ref[0, e])
            act = a * _sigmoid(a) * u * gates[:, e:e + 1]
            part = _dot(act.astype(BF16), wd_ref[0, e])
            y = part if y is None else y + part
        chp = -(-ch // LANES) * LANES
        lane_iota = lax.broadcasted_iota(jnp.int32, (tm, chp), 1).astype(F32)
        scatter = jnp.where(key_col - base == lane_iota, 1.0, 0.0).astype(BF16)
        yb = y.astype(BF16)
        if chp > ch:
            yb = jnp.concatenate([yb, jnp.zeros((chp - ch, D_MODEL), BF16)], axis=0)
        o_ref[...] = o_ref[...] + _dot(scatter, yb)
        return carry

    lax.fori_loop(0, (cnt_ref[grp_id] + ch - 1) // ch, chunk, 0)

    @pl.when(grp_id == N_GROUPS - 1)
    def _():
        if final_norm:
            y = o_ref[...]
            ms = jnp.mean(y * y, axis=-1, keepdims=True)
            o_ref[...] = y * lax.rsqrt(ms + RMS_EPS) * gf_ref[...]


def _moe(x2, g, wrh, wrl, br, wg, wu, wd, gf, layer, final_norm, tm=1024):
    n = x2.shape[0]
    tm = min(tm, n)
    row = lambda i, e: (i, 0)
    full = lambda i, e: (0, 0)
    epg = EXPERTS_PER_GROUP
    wspec_in = pl.BlockSpec((1, epg, D_MODEL, D_EXPERT), lambda i, e: (layer, e, 0, 0))
    wspec_out = pl.BlockSpec((1, epg, D_EXPERT, D_MODEL), lambda i, e: (layer, e, 0, 0))
    return pl.pallas_call(
        functools.partial(_moe_kernel, final_norm=final_norm),
        grid=(n // tm, N_GROUPS),
        in_specs=[pl.BlockSpec((tm, D_MODEL), row, pipeline_mode=pl.Buffered(1)),
                  pl.BlockSpec((1, D_MODEL), full),
                  pl.BlockSpec((D_MODEL, LANES), full),
                  pl.BlockSpec((D_MODEL, LANES), full),
                  pl.BlockSpec((1, LANES), full),
                  wspec_in, wspec_in, wspec_out,
                  pl.BlockSpec((1, D_MODEL), full)],
        out_specs=pl.BlockSpec((tm, D_MODEL), row),
        out_shape=jax.ShapeDtypeStruct((n, D_MODEL), F32),
        scratch_shapes=[pltpu.VMEM((tm, D_MODEL), BF16),
                        pltpu.VMEM((tm, 2 * LANES), BF16),
                        pltpu.VMEM((tm, LANES), F32),
                        pltpu.VMEM((8, tm), F32),
                        pltpu.SMEM((N_GROUPS,), jnp.int32)],
        compiler_params=_cparams("parallel", "arbitrary"),
        name="moe",
    )(x2, g, wrh, wrl, br, wg, wu, wd, gf)


def _head_perm():
    hg = NSA_HEADS // NSA_KV_GROUPS
    order = []
    for j in range(hg):
        order += [j, j + hg]
    return np.concatenate([np.arange(h * HEAD_DIM, (h + 1) * HEAD_DIM) for h in order])


def _split_hi_lo(w):
    hi = w.astype(BF16)
    return hi, (w - hi.astype(F32)).astype(BF16)


def _layer_params(l, w_in, b_gate, b_fgt, cmp_pe, cmp_w1, cmp_w2, w_out):
    offs = np.concatenate([[0], np.cumsum(IN_WIDTHS)])
    seg = [w_in[l][:, offs[i]:offs[i + 1]] for i in range(len(IN_WIDTHS))]
    (wq, wkc, wvc, wks, wvs, wkw, wvw, wgt, wfq, wfk, wfv, wff, wdq, wdk, wdv) = seg
    perm = _head_perm()
    gate_perm = np.array([h * 3 + br for br in range(3) for h in range(NSA_HEADS)])
    wm = jnp.concatenate([wq[:, perm] * (HEAD_DIM ** -0.5 * LOG2E), wks, wvs, wkw, wvw,
                          wfq * (HEAD_DIM ** -0.5 * LOG2E), wfk, wfv,
                          wdq * (DIFF_QK_DIM ** -0.5 * LOG2E), wdk, wdv, wkc, wvc], axis=1).astype(BF16)
    pad = LANES - NSA_GATE - 3 * FOX_HEADS
    ws = jnp.concatenate([wgt[:, gate_perm], wff, wff, wff, jnp.zeros((D_MODEL, pad), F32)], axis=1)
    wsh, wsl = _split_hi_lo(ws)
    bs = jnp.concatenate([b_gate[l][gate_perm], b_fgt[l], b_fgt[l], b_fgt[l], jnp.zeros((pad,), F32)])[None, :]

    eye = jnp.eye(NSA_KV_GROUPS, dtype=F32)
    half = CMP_BLOCK // 2
    cmp = []
    for pe, w1, w2 in zip(cmp_pe, cmp_w1, cmp_w2):
        w1r = w1[l].reshape(2, half, HEAD_DIM, CMP_HIDDEN)
        w1x = jnp.einsum('srdk,gh->srgdhk', w1r, eye).reshape(2, half * NSA_KV, NSA_KV_GROUPS * CMP_HIDDEN)
        w2bd = jnp.einsum('kd,gh->gkhd', w2[l], eye).reshape(NSA_KV_GROUPS * CMP_HIDDEN, NSA_KV)
        pe2 = jnp.broadcast_to(pe[l].reshape(2, half, 1, HEAD_DIM), (2, half, NSA_KV_GROUPS, HEAD_DIM))
        cmp.append((pe2.reshape(2, half * NSA_KV), w1x[0].astype(BF16), w1x[1].astype(BF16), w2bd.astype(BF16)))

    wo = w_out[l]
    wa = wo[:NSA_Q][perm].astype(BF16)
    wb = wo[NSA_Q:NSA_Q + FOX_W].astype(BF16)
    wc = wo[NSA_Q + FOX_W:].astype(BF16)
    return dict(wm=wm, wsh=wsh, wsl=wsl, bs=bs, cmp_k=cmp[0], cmp_v=cmp[1], wa=wa, wb=wb, wc=wc)


def _overlap_matrix(n_slab, n_cmp, n_sel):
    c_start = np.arange(n_cmp) * CMP_STRIDE
    s_start = np.arange(n_sel) * SLC_BLOCK
    ov = np.clip(np.minimum(c_start[:, None] + CMP_BLOCK, s_start[None, :] + SLC_BLOCK)
                 - np.maximum(c_start[:, None], s_start[None, :]), 0, None) / CMP_BLOCK
    full = np.zeros((n_slab, LANES), np.float32)
    full[:n_cmp, :n_sel] = ov
    return jnp.asarray(full, dtype=BF16)


def kernel(x, norm_attn, w_in, b_gate, b_fgt, cmp_k_pe, cmp_k_w1, cmp_k_w2, cmp_v_pe, cmp_v_w1, cmp_v_w2, diff_lambda, diff_subln, w_out, norm_ffn, w_grp, b_grp, w_exp, b_exp, w_e_gate, w_e_up, w_e_down, norm_final):
    b, t, d = x.shape
    depth = w_in.shape[0]
    n = b * t
    n_slab = t // CMP_STRIDE
    n_cmp = (t - CMP_BLOCK) // CMP_STRIDE + 1
    n_sel = t // SLC_BLOCK
    top_n = min(SLC_TOPK, n_sel)
    ov = _overlap_matrix(n_slab, n_cmp, n_sel)

    wg_all = w_e_gate.astype(BF16)
    wu_all = w_e_up.astype(BF16)
    wd_all = w_e_down.astype(BF16)
    gf = norm_final[None, :]

    x2 = x.reshape(n, d)
    for l in range(depth):
        p = _layer_params(l, w_in, b_gate, b_fgt, (cmp_k_pe, cmp_v_pe), (cmp_k_w1, cmp_v_w1),
                          (cmp_k_w2, cmp_v_w2), w_out)
        main, kc, vc, small = _inproj(x2, norm_attn[l][None, :], p["wm"], p["wsh"], p["wsl"], p["bs"])
        main3 = main.reshape(b, t, MAIN_W)
        small3 = small.reshape(b, t, LANES)
        kcmp = _compress(kc.reshape(b, n_slab, CMP_STRIDE * NSA_KV), *p["cmp_k"])
        vcmp = _compress(vc.reshape(b, n_slab, CMP_STRIDE * NSA_KV), *p["cmp_v"])
        o_nsa, pen = _cmp_attn(main3, kcmp, vcmp, small3, ov, n_cmp, n_sel, top_n)
        o_nsa = _flash(main3, "slc", (pen, small3, o_nsa), F32)
        o_nsa = _flash(main3, "win", (small3, o_nsa), F32)
        o_fox = _flash(main3, "fox", (_cumgate(small3),), BF16)
        sub2 = jnp.concatenate([diff_subln[l], diff_subln[l]])[None, :]
        o_diff = _flash(main3, "diff", (diff_lambda[l], sub2), BF16, lam_init=_lambda_init(l))

        x2 = _outproj(x2, o_nsa.reshape(n, NSA_Q), o_fox.reshape(n, FOX_W), o_diff.reshape(n, DIFF_W),
                      p["wa"], p["wb"], p["wc"])

        wr = jnp.concatenate([w_grp[l], w_exp[l], jnp.zeros((d, LANES - N_GROUPS - N_EXPERTS), F32)], axis=1)
        wrh, wrl = _split_hi_lo(wr)
        br = jnp.concatenate([b_grp[l], b_exp[l], jnp.zeros((LANES - N_GROUPS - N_EXPERTS,), F32)])[None, :]
        x2 = _moe(x2, norm_ffn[l][None, :], wrh, wrl, br, wg_all, wu_all, wd_all, gf, l,
                  final_norm=(l == depth - 1))
    return x2.reshape(b, t, d)
```

```python
import functools
import math

import ml_dtypes
import numpy as np
import jax
import jax.numpy as jnp
from jax import lax
from jax.experimental import pallas as pl
from jax.experimental.pallas import tpu as pltpu

F32 = jnp.float32
BF16 = jnp.bfloat16

D_MODEL = 1024
HEAD_DIM = 64
NSA_HEADS = 8
NSA_KV_GROUPS = 2
CMP_BLOCK = 32
CMP_STRIDE = 16
CMP_HIDDEN = 256
SLC_BLOCK = 64
SLC_SHIFT = 6
SLC_TOPK = 16
WINDOW = 512
FOX_HEADS = 4
DIFF_HEADS = 4
DIFF_QK_DIM = HEAD_DIM // 2
N_GROUPS = 4
EXPERTS_PER_GROUP = 8
N_EXPERTS = N_GROUPS * EXPERTS_PER_GROUP
D_EXPERT = 256
RMS_EPS = 1e-6
FORCE_SCORE = 1e4
NEG_INF = -1e30
MASKED = -2e30

NSA_Q = NSA_HEADS * HEAD_DIM
NSA_KV = NSA_KV_GROUPS * HEAD_DIM
NSA_GATE = NSA_HEADS * 3
FOX_W = FOX_HEADS * HEAD_DIM
DIFF_W = DIFF_HEADS * HEAD_DIM
MIX_WIDTH = NSA_Q + FOX_W + DIFF_W
IN_WIDTHS = (NSA_Q, NSA_KV, NSA_KV, NSA_KV, NSA_KV, NSA_KV, NSA_KV, NSA_GATE,
             FOX_W, FOX_W, FOX_W, FOX_HEADS, DIFF_W, DIFF_W, DIFF_W)

LANES = 128
SUBLANES = 8
VMEM_LIMIT_BYTES = 56 * 1024 * 1024

MAIN_W = NSA_Q + 4 * NSA_KV + 3 * FOX_W + 3 * DIFF_W
PROJ_W = MAIN_W + 2 * NSA_KV
COL_KS, COL_VS, COL_KW, COL_VW = 4, 5, 6, 7
COL_FQ, COL_FK, COL_FV = 4, 5, 6
COL_DQ, COL_DK, COL_DV = 7, 8, 9
FGT_LANES = (NSA_GATE, NSA_GATE + FOX_HEADS, NSA_GATE + 2 * FOX_HEADS)
LOG2E = 1.4426950408889634

TQ = 512
TK = 512
KSTRIP = 256
ONES_ROWS = 16

def _alibi(n):
    return [float(2.0 ** (-8.0 * (i + 1) / n)) for i in range(n)]


def _lambda_init(layer):
    return 0.8 - 0.6 * math.exp(-0.3 * layer)


def _cparams(*sem):
    return pltpu.CompilerParams(dimension_semantics=sem, vmem_limit_bytes=VMEM_LIMIT_BYTES)


def _dot(a, b):
    return jnp.dot(a, b, preferred_element_type=F32)


def _dot_nt(a, b):
    return lax.dot_general(a, b, (((1,), (1,)), ((), ())), preferred_element_type=F32)


def _split2(x):
    hi = x.astype(BF16)
    lo = (x - hi.astype(F32)).astype(BF16)
    return hi, lo


def _sigmoid(z):
    return 1.0 / (1.0 + jnp.exp(-z))


def _keep_lanes(sel, blk):
    return jnp.where(sel, blk.astype(F32), 0.0).astype(BF16)


def _inproj_kernel(x_ref, g_ref, wm_ref, wsh_ref, wsl_ref, bs_ref, main_ref, kc_ref, vc_ref, small_ref):
    x = x_ref[...]
    ms = jnp.mean(x * x, axis=-1, keepdims=True)
    y = x * lax.rsqrt(ms + RMS_EPS) * g_ref[...]
    hb, hl = _split2(y)
    full = _dot(hb, wm_ref[...]).astype(BF16)
    main_ref[...] = full[:, :MAIN_W]
    kc_ref[...] = full[:, MAIN_W:MAIN_W + NSA_KV]
    vc_ref[...] = full[:, MAIN_W + NSA_KV:]
    small_ref[...] = (_dot(hb, wsh_ref[...]) + _dot(hl, wsh_ref[...]) + _dot(hb, wsl_ref[...])) + bs_ref[...]


def _inproj(x2, g, wm, wsh, wsl, bs, tm=512):
    n = x2.shape[0]
    full = lambda i: (0, 0)
    return pl.pallas_call(
        _inproj_kernel,
        grid=(n // tm,),
        in_specs=[pl.BlockSpec((tm, D_MODEL), lambda i: (i, 0)),
                  pl.BlockSpec((1, D_MODEL), full),
                  pl.BlockSpec((D_MODEL, PROJ_W), full),
                  pl.BlockSpec((D_MODEL, LANES), full),
                  pl.BlockSpec((D_MODEL, LANES), full),
                  pl.BlockSpec((1, LANES), full)],
        out_specs=[pl.BlockSpec((tm, MAIN_W), lambda i: (i, 0)),
                   pl.BlockSpec((tm, NSA_KV), lambda i: (i, 0)),
                   pl.BlockSpec((tm, NSA_KV), lambda i: (i, 0)),
                   pl.BlockSpec((tm, LANES), lambda i: (i, 0))],
        out_shape=[jax.ShapeDtypeStruct((n, MAIN_W), BF16),
                   jax.ShapeDtypeStruct((n, NSA_KV), BF16),
                   jax.ShapeDtypeStruct((n, NSA_KV), BF16),
                   jax.ShapeDtypeStruct((n, LANES), F32)],
        compiler_params=_cparams("parallel"),
        name="inproj",
    )(x2, g, wm, wsh, wsl, bs)


def _compress_kernel(r_ref, pe_ref, w1a_ref, w1b_ref, w2_ref, o_ref):
    r = r_ref[0].astype(F32)
    pe = pe_ref[...]
    ra = (r + pe[0:1, :]).astype(BF16)
    rb = (r + pe[1:2, :]).astype(BF16)
    a = _dot(ra, w1a_ref[...])
    b = _dot(rb, w1b_ref[...])
    n_slab = a.shape[0]
    hid = a + pltpu.roll(b, n_slab - 1, 0)
    hid = jax.nn.gelu(hid)
    o_ref[0] = _dot(hid.astype(BF16), w2_ref[...]).astype(BF16)


def _compress(r, pe2, w1a, w1b, w2bd):
    b, n_slab, w = r.shape
    full = lambda i: (0, 0)
    return pl.pallas_call(
        _compress_kernel,
        grid=(b,),
        in_specs=[pl.BlockSpec((1, n_slab, w), lambda i: (i, 0, 0)),
                  pl.BlockSpec((2, w), full),
                  pl.BlockSpec((w, 2 * CMP_HIDDEN), full),
                  pl.BlockSpec((w, 2 * CMP_HIDDEN), full),
                  pl.BlockSpec((2 * CMP_HIDDEN, LANES), full)],
        out_specs=pl.BlockSpec((1, n_slab, LANES), lambda i: (i, 0, 0)),
        out_shape=jax.ShapeDtypeStruct((b, n_slab, LANES), BF16),
        compiler_params=_cparams("parallel"),
        name="compress",
    )(r, pe2, w1a, w1b, w2bd)


def _cumgate_kernel(z_ref, o_ref, carry_ref, *, tc):
    @pl.when(pl.program_id(1) == 0)
    def _():
        carry_ref[...] = jnp.zeros_like(carry_ref)

    z = z_ref[0]
    logf = -(jnp.maximum(-z, 0.0) + jnp.log(1.0 + jnp.exp(-jnp.abs(z))))
    hi = logf.astype(BF16)
    r1 = logf - hi.astype(F32)
    mid = r1.astype(BF16)
    lo = (r1 - mid.astype(F32)).astype(BF16)
    tri = (lax.broadcasted_iota(jnp.int32, (tc, tc), 0) >= lax.broadcasted_iota(jnp.int32, (tc, tc), 1))
    tri = jnp.where(tri, 1.0, 0.0).astype(BF16)
    c = (_dot(tri, hi) + _dot(tri, mid)) + _dot(tri, lo) + carry_ref[0:1, :]
    carry_ref[...] = jnp.broadcast_to(c[tc - 1:tc, :], carry_ref.shape)
    v = c * (-LOG2E)
    p0 = v.astype(BF16)
    r1 = v - p0.astype(F32)
    p1 = r1.astype(BF16)
    p2 = (r1 - p1.astype(F32)).astype(BF16)
    lane = lax.broadcasted_iota(jnp.int32, v.shape, 1)
    pieces = jnp.where(lane < FGT_LANES[1], p0.astype(F32), jnp.where(lane < FGT_LANES[2], p1.astype(F32), p2.astype(F32)))
    keep = (lane >= FGT_LANES[0]) & (lane < FGT_LANES[2] + FOX_HEADS)
    o_ref[0] = jnp.where(keep, pieces, 0.0).astype(BF16)


def _cumgate(small3, tc=512):
    b, t, w = small3.shape
    return pl.pallas_call(
        functools.partial(_cumgate_kernel, tc=tc),
        grid=(b, t // tc),
        in_specs=[pl.BlockSpec((1, tc, w), lambda i, j: (i, j, 0))],
        out_specs=pl.BlockSpec((1, tc, w), lambda i, j: (i, j, 0)),
        out_shape=jax.ShapeDtypeStruct((b, t, w), BF16),
        scratch_shapes=[pltpu.VMEM((8, w), F32)],
        compiler_params=_cparams("parallel", "arbitrary"),
        name="cumgate",
    )(small3)


def _cmp_kernel(q_ref, kc_ref, vc_ref, sm_ref, ov_ref, o_ref, pen_ref, used_ref, *, tq, n_cmp, n_sel, top_n):
    t0 = pl.program_id(1) * tq
    nck = kc_ref.shape[1]
    row = t0 + lax.broadcasted_iota(jnp.int32, (tq, nck), 0)
    col = lax.broadcasted_iota(jnp.int32, (tq, nck), 1)
    dist = row - (col * CMP_STRIDE + (CMP_BLOCK - 1))
    mask = (dist >= 0) & (col < n_cmp)
    distf = dist.astype(F32)
    lane = lax.broadcasted_iota(jnp.int32, (tq, LANES), 1)
    low = lane < HEAD_DIM
    kc = kc_ref[0]
    vc = vc_ref[0]
    sig = _sigmoid(sm_ref[0])
    slopes = _alibi(NSA_HEADS)
    psum = [None, None]
    hg = NSA_HEADS // NSA_KV_GROUPS
    for j in range(hg):
        qblk = q_ref[0, :, j * LANES:(j + 1) * LANES]
        outs = []
        for g in range(NSA_KV_GROUPS):
            h = j + hg * g
            qm = _keep_lanes(low if g == 0 else ~low, qblk)
            s = _dot_nt(qm, kc) - (slopes[h] * LOG2E) * distf
            s = jnp.where(mask, s, NEG_INF)
            m = jnp.max(s, axis=-1, keepdims=True)
            p = jnp.where(mask, jnp.exp2(s - m), 0.0)
            l = jnp.sum(p, axis=-1, keepdims=True)
            p = p * jnp.where(l > 0.0, 1.0 / l, 0.0)
            psum[g] = p if psum[g] is None else psum[g] + p
            outs.append(_dot(p.astype(BF16), vc))
        gate = jnp.where(low, sig[:, j:j + 1], sig[:, j + hg:j + hg + 1])
        o_ref[0, :, j * LANES:(j + 1) * LANES] = jnp.where(low, outs[0], outs[1]) * gate

    sub8 = lax.broadcasted_iota(jnp.int32, (SUBLANES, tq), 0)
    cur_r = (t0 + lax.broadcasted_iota(jnp.int32, (tq, LANES), 0)) >> SLC_SHIFT
    pen_t = []
    for g in range(NSA_KV_GROUPS):
        ph, plo = _split2(psum[g])
        imp = _dot(ph, ov_ref[...]) + _dot(plo, ov_ref[...])
        valid = lane <= cur_r
        forced = (lane == 0) | (lane == cur_r) | (lane == cur_r - 1)
        imp = jnp.where(valid, jnp.where(forced, FORCE_SCORE, imp), NEG_INF)
        imp_t = imp.T[:n_sel, :]
        ngrp = n_sel // SUBLANES
        rows = [imp_t[i * SUBLANES:(i + 1) * SUBLANES, :] for i in range(ngrp)]
        cnt = [jnp.zeros((SUBLANES, tq), F32) for _ in range(ngrp)]
        for k in range(n_sel):
            rk = imp_t[k:k + 1, :]
            for i in range(ngrp):
                if i * SUBLANES > k:
                    one = jnp.where(rk >= rows[i], 1.0, 0.0)
                elif (i + 1) * SUBLANES <= k:
                    one = jnp.where(rk > rows[i], 1.0, 0.0)
                else:
                    one = jnp.where(sub8 > k - i * SUBLANES,
                                    jnp.where(rk >= rows[i], 1.0, 0.0), jnp.where(rk > rows[i], 1.0, 0.0))
                cnt[i] = cnt[i] + one
        pen_t.append(jnp.where(jnp.concatenate(cnt, axis=0) < float(top_n), 0.0, MASKED))
    for g in range(NSA_KV_GROUPS):
        full = jnp.concatenate([pen_t[g], jnp.zeros((LANES - n_sel, tq), F32)], axis=0)
        pen_ref[0, :, g * LANES:(g + 1) * LANES] = full.T.astype(BF16)
    picked = jnp.where(jnp.maximum(pen_t[0], pen_t[1]) == 0.0, 1.0, 0.0)
    bpt = TK // SLC_BLOCK
    lane1 = lax.broadcasted_iota(jnp.int32, (1, LANES), 1)
    used = jnp.zeros((1, LANES), F32)
    for kt in range(n_sel // bpt):
        hit = jnp.max(picked[kt * bpt:(kt + 1) * bpt, :], axis=(0, 1), keepdims=True)
        used = jnp.where(lane1 == kt, hit, used)
    used_ref[0, 0] = used


def _cmp_attn(main3, kcmp, vcmp, small3, ov, n_cmp, n_sel, top_n, tq=256):
    b, t, _ = main3.shape
    nck = kcmp.shape[1]
    return pl.pallas_call(
        functools.partial(_cmp_kernel, tq=tq, n_cmp=n_cmp, n_sel=n_sel, top_n=top_n),
        grid=(b, t // tq),
        in_specs=[pl.BlockSpec((1, tq, NSA_Q), lambda i, j: (i, j, 0)),
                  pl.BlockSpec((1, nck, LANES), lambda i, j: (i, 0, 0)),
                  pl.BlockSpec((1, nck, LANES), lambda i, j: (i, 0, 0)),
                  pl.BlockSpec((1, tq, LANES), lambda i, j: (i, j, 0)),
                  pl.BlockSpec((nck, LANES), lambda i, j: (0, 0))],
        out_specs=[pl.BlockSpec((1, tq, NSA_Q), lambda i, j: (i, j, 0)),
                   pl.BlockSpec((1, tq, NSA_KV_GROUPS * LANES), lambda i, j: (i, j, 0)),
                   pl.BlockSpec((1, 1, 1, LANES), lambda i, j: (i, j, 0, 0))],
        out_shape=[jax.ShapeDtypeStruct((b, t, NSA_Q), F32),
                   jax.ShapeDtypeStruct((b, t, NSA_KV_GROUPS * LANES), BF16),
                   jax.ShapeDtypeStruct((b, t // tq, 1, LANES), F32)],
        compiler_params=_cparams("parallel", "parallel"),
        name="cmp_attn",
    )(main3, kcmp, vcmp, small3, ov)


def _bf16_terms(c, n=3):
    out, r = [], float(c)
    for _ in range(n):
        p = float(np.float32(r).astype(ml_dtypes.bfloat16))
        out.append(p)
        r -= p
    return out


def _alibi_q_aug(lane, a0, slope):
    aug = jnp.zeros(lane.shape, F32)
    for i, c in enumerate(_bf16_terms(slope * LOG2E)):
        aug = jnp.where(lane == a0 + 2 * i, float(SLC_BLOCK) * c, aug)
        aug = jnp.where(lane == a0 + 2 * i + 1, c, aug)
    return aug


def _alibi_k_aug(lane, rel, a0):
    hi = (rel >> SLC_SHIFT).astype(F32)
    lo = (rel & (SLC_BLOCK - 1)).astype(F32)
    inside = (lane >= a0) & (lane < a0 + 6)
    odd = ((lane - a0) & 1) == 1
    return jnp.where(inside, jnp.where(odd, lo, hi), 0.0)


ALIBI_LANE = {"slc": SLC_BLOCK, "win": 0, "diff": 0}
STACKS = {"slc": (1, 8), "win": (1, 8), "fox": (2, 2), "diff": (2, 4)}


def _v_half(mode, vh):
    nv = STACKS[mode][1]
    if mode in ("slc", "win"):
        return vh // (NSA_HEADS // NSA_KV_GROUPS)
    return (vh % nv) // (nv // 2)


def _flash_kernel(*refs, mode, tq, tk, lam_init, ntiles):
    sched_ref, refs = refs[0], refs[1:]
    if mode == "slc":
        use_ref, q_ref, k_ref, v_ref, pen_ref, sm_ref, add_ref, o_ref, qst_ref, m_ref, acc_ref = refs
    elif mode == "win":
        q_ref, k_ref, v_ref, sm_ref, add_ref, o_ref, qst_ref, m_ref, acc_ref = refs
    elif mode == "fox":
        q_ref, k_ref, v_ref, fa_ref, o_ref, qst_ref, m_ref, acc_ref = refs
    else:
        q_ref, k_ref, v_ref, lam_ref, sub_ref, o_ref, qst_ref, m_ref, acc_ref = refs
    nstack, nv = STACKS[mode]
    nsa = mode in ("slc", "win")
    hg = NSA_HEADS // NSA_KV_GROUPS
    step_id = pl.program_id(1)
    q0 = sched_ref[SCHED_Q, step_id] * tq
    k0 = sched_ref[SCHED_K, step_id] * tk

    @pl.when(sched_ref[SCHED_FIRST, step_id] == 1)
    def _():
        m_ref[...] = jnp.full(m_ref.shape, NEG_INF, F32)
        acc_ref[...] = jnp.zeros(acc_ref.shape, F32)
        lane = lax.broadcasted_iota(jnp.int32, (tq, LANES), 1)
        for vh in range(nstack * nv):
            if nsa:
                jb, g = vh % hg, vh // hg
                lo_lane, width = g * HEAD_DIM, HEAD_DIM
                aug = _alibi_q_aug(lane, ALIBI_LANE[mode], _alibi(NSA_HEADS)[vh])
                if mode == "slc":
                    aug = aug + pen_ref[0, :, g * LANES:(g + 1) * LANES].astype(F32)
            elif mode == "fox":
                jb, r = vh // nv, vh % nv
                lo_lane, width = r * HEAD_DIM, HEAD_DIM
                hit = (lane == FGT_LANES[0] + vh) | (lane == FGT_LANES[1] + vh) | (lane == FGT_LANES[2] + vh)
                aug = jnp.where(hit, 1.0, 0.0)
            else:
                jb, r = vh // nv, vh % nv
                lo_lane, width = r * DIFF_QK_DIM, DIFF_QK_DIM
                aug = _alibi_q_aug(lane, ALIBI_LANE[mode], _alibi(DIFF_HEADS)[vh // 2])
            sel = (lane >= lo_lane) & (lane < lo_lane + width)
            qst_ref[vh * tq:(vh + 1) * tq, 0:LANES] = _keep_lanes(sel, q_ref[0, :, jb * LANES:(jb + 1) * LANES])
            qst_ref[vh * tq:(vh + 1) * tq, LANES:2 * LANES] = aug.astype(BF16)

    def step(masked):
        lane_k = lax.broadcasted_iota(jnp.int32, (tk, LANES), 1)
        row_k = lax.broadcasted_iota(jnp.int32, (tk, LANES), 0)
        if mode == "fox":
            k_aug = fa_ref[0]
        else:
            ka = _alibi_k_aug(lane_k, (k0 - q0) + row_k, ALIBI_LANE[mode])
            if mode == "slc":
                ka = ka + jnp.where(((k0 + row_k) >> SLC_SHIFT) == lane_k, 1.0, 0.0)
            k_aug = ka.astype(BF16)
        if masked:
            dist = (q0 - k0) + lax.broadcasted_iota(jnp.int32, (tk, tq), 1) - lax.broadcasted_iota(jnp.int32, (tk, tq), 0)
            ok = dist >= 0
            if mode == "win":
                ok = ok & (dist < WINDOW)
            bias = jnp.where(ok, 0.0, MASKED)
        keys, vals_t = [], []
        for st in range(nstack):
            kblk = k_ref[0] if nsa else k_ref[0, :, st * LANES:(st + 1) * LANES]
            vblk = v_ref[0] if nsa else v_ref[0, :, st * LANES:(st + 1) * LANES]
            keys.append(jnp.concatenate([kblk, k_aug], axis=1))
            v_t = vblk.astype(F32).T.astype(BF16)
            ones = jnp.ones((ONES_ROWS, tk), BF16)
            vals_t.append([jnp.concatenate([v_t[hf * HEAD_DIM:(hf + 1) * HEAD_DIM, :], ones], axis=0)
                           for hf in range(LANES // HEAD_DIM)])

        nstrip = tk // KSTRIP
        nvh = nstack * nv

        def scores(vh):
            s = _dot_nt(keys[vh // nv], qst_ref[vh * tq:(vh + 1) * tq, :])
            if masked:
                s = s + bias
            return s, jnp.max(s, axis=0, keepdims=True)

        cur, cur_max = scores(0)
        for vh in range(nvh):
            cols = slice(vh * tq, (vh + 1) * tq)
            v_half = vals_t[vh // nv][_v_half(mode, vh)]
            m_old = m_ref[:, cols]
            m_new = jnp.maximum(m_old, cur_max)
            alpha = jnp.exp2(m_old - m_new)
            if vh + 1 < nvh:
                nxt, nxt_max = scores(vh + 1)
            pv = None
            for r in range(nstrip):
                p = jnp.exp2((cur[r * KSTRIP:(r + 1) * KSTRIP, :] - m_new).astype(BF16))
                part = _dot(v_half[:, r * KSTRIP:(r + 1) * KSTRIP], p)
                pv = part if pv is None else pv + part
            m_ref[:, cols] = m_new
            acc_ref[:, cols] = alpha * acc_ref[:, cols] + pv
            if vh + 1 < nvh:
                cur, cur_max = nxt, nxt_max

    is_masked = sched_ref[SCHED_MASKED, step_id] == 1
    if mode == "win":
        pl.when(is_masked)(lambda: step(True))
    elif mode == "slc":
        tile_id = (pl.program_id(0) * ntiles[0] + sched_ref[SCHED_Q, step_id]) * ntiles[1] + sched_ref[SCHED_K, step_id]
        needed = use_ref[tile_id] == 1
        pl.when(is_masked & needed)(lambda: step(True))
        pl.when(jnp.logical_not(is_masked) & needed)(lambda: step(False))
    else:
        pl.when(is_masked)(lambda: step(True))
        pl.when(jnp.logical_not(is_masked))(lambda: step(False))

    @pl.when(sched_ref[SCHED_LAST, step_id] == 1)
    def _():
        inv = 1.0 / acc_ref[HEAD_DIM:HEAD_DIM + 1, :]

        def out_t(vh):
            return acc_ref[0:HEAD_DIM, vh * tq:(vh + 1) * tq] * inv[:, vh * tq:(vh + 1) * tq]

        def pair(a, b):
            return jnp.concatenate([a, b], axis=0).T

        low = lax.broadcasted_iota(jnp.int32, (tq, LANES), 1) < HEAD_DIM
        if nsa:
            sig = _sigmoid(sm_ref[0])
            br = 1 if mode == "slc" else 2
            for jb in range(hg):
                o = pair(out_t(jb), out_t(jb + hg))
                c0 = br * NSA_HEADS + jb
                gate = jnp.where(low, sig[:, c0:c0 + 1], sig[:, c0 + hg:c0 + hg + 1])
                o_ref[0, :, jb * LANES:(jb + 1) * LANES] = add_ref[0, :, jb * LANES:(jb + 1) * LANES] + o * gate
        elif mode == "fox":
            for pb in range(nstack):
                o = pair(out_t(2 * pb), out_t(2 * pb + 1))
                o_ref[0, :, pb * LANES:(pb + 1) * LANES] = o.astype(o_ref.dtype)
        else:
            lam = lam_ref[...]
            lam_full = (jnp.exp(jnp.sum(lam[0:1, :] * lam[1:2, :], axis=-1, keepdims=True))
                        - jnp.exp(jnp.sum(lam[2:3, :] * lam[3:4, :], axis=-1, keepdims=True)) + lam_init)
            for pb in range(nstack):
                heads = [out_t(pb * nv + 2 * hh) - lam_full * out_t(pb * nv + 2 * hh + 1) for hh in range(2)]
                o = pair(heads[0], heads[1])
                sq = o * o
                ss_lo = jnp.sum(jnp.where(low, sq, 0.0), axis=-1, keepdims=True)
                ss_hi = jnp.sum(jnp.where(low, 0.0, sq), axis=-1, keepdims=True)
                ms = jnp.where(low, ss_lo, ss_hi) * (1.0 / HEAD_DIM)
                y = o * lax.rsqrt(ms + RMS_EPS) * sub_ref[...] * (1.0 - lam_init)
                o_ref[0, :, pb * LANES:(pb + 1) * LANES] = y.astype(o_ref.dtype)


def _flash(main3, mode, extra, out_dtype, lam_init=0.0, tile_used=None, tq=TQ, tk=TK):
    b, t, _ = main3.shape
    tk = min(tk, t)
    nstack, nv = STACKS[mode]
    sched = _flash_schedule(t, tq, tk, mode == "win")
    prefetch = [jnp.asarray(sched)]
    if mode == "slc":
        prefetch.append(tile_used.reshape(-1))
    qtile = lambda w, c: pl.BlockSpec((1, tq, w), lambda bi, s, sch, *_: (bi, sch[SCHED_Q, s], c))
    ktile = lambda w, c: pl.BlockSpec((1, tk, w), lambda bi, s, sch, *_: (bi, sch[SCHED_K, s], c))
    if mode in ("slc", "win"):
        kcol, vcol = (COL_KS, COL_VS) if mode == "slc" else (COL_KW, COL_VW)
        in_specs = [qtile(NSA_Q, 0), ktile(LANES, kcol), ktile(LANES, vcol)]
        if mode == "slc":
            in_specs.append(qtile(NSA_KV_GROUPS * LANES, 0))
        in_specs += [qtile(LANES, 0), qtile(NSA_Q, 0)]
        out_w = NSA_Q
    else:
        qc, kc, vc = (COL_FQ, COL_FK, COL_FV) if mode == "fox" else (COL_DQ, COL_DK, COL_DV)
        in_specs = [qtile(2 * LANES, qc), ktile(2 * LANES, kc), ktile(2 * LANES, vc)]
        if mode == "fox":
            in_specs.append(ktile(LANES, 0))
        else:
            in_specs += [pl.BlockSpec((4, DIFF_QK_DIM), lambda bi, s, sch, *_: (0, 0)),
                         pl.BlockSpec((1, LANES), lambda bi, s, sch, *_: (0, 0))]
        out_w = 2 * LANES
    cols = nstack * nv * tq
    return pl.pallas_call(
        functools.partial(_flash_kernel, mode=mode, tq=tq, tk=tk, lam_init=lam_init, ntiles=(t // tq, t // tk)),
        grid_spec=pltpu.PrefetchScalarGridSpec(
            num_scalar_prefetch=len(prefetch),
            grid=(b, sched.shape[1]),
            in_specs=in_specs,
            out_specs=qtile(out_w, 0),
            scratch_shapes=[pltpu.VMEM((cols, 2 * LANES), BF16),
                            pltpu.VMEM((1, cols), F32),
                            pltpu.VMEM((HEAD_DIM + ONES_ROWS, cols), F32)]),
        out_shape=jax.ShapeDtypeStruct((b, t, out_w), out_dtype),
        compiler_params=_cparams("parallel", "arbitrary"),
        name="flash_" + mode,
    )(*prefetch, main3, main3, main3, *extra)


SCHED_Q, SCHED_K, SCHED_FIRST, SCHED_LAST, SCHED_MASKED = range(5)


def _flash_schedule(t, tq, tk, window):
    rows = []
    for i in range(t // tq):
        q_lo, q_hi = i * tq, i * tq + tq - 1
        k_first = max(q_lo - (WINDOW - 1), 0) // tk if window else 0
        tiles = list(range(k_first, q_hi // tk + 1))
        for n, kt in enumerate(tiles):
            fully_visible = kt * tk + tk - 1 <= q_lo and not window
            rows.append((i, kt, int(n == 0), int(n == len(tiles) - 1), int(not fully_visible)))
    return np.asarray(rows, np.int32).T


def _outproj_kernel(x_ref, oa_ref, ob_ref, oc_ref, wa_ref, wb_ref, wc_ref, o_ref):
    acc = _dot(oa_ref[...].astype(BF16), wa_ref[...])
    acc = acc + _dot(ob_ref[...], wb_ref[...])
    acc = acc + _dot(oc_ref[...], wc_ref[...])
    o_ref[...] = x_ref[...] + acc


def _outproj(x2, oa, ob, oc, wa, wb, wc, tm=512):
    n = x2.shape[0]
    row = lambda i: (i, 0)
    full = lambda i: (0, 0)
    return pl.pallas_call(
        _outproj_kernel,
        grid=(n // tm,),
        in_specs=[pl.BlockSpec((tm, D_MODEL), row),
                  pl.BlockSpec((tm, NSA_Q), row),
                  pl.BlockSpec((tm, FOX_W), row),
                  pl.BlockSpec((tm, DIFF_W), row),
                  pl.BlockSpec((NSA_Q, D_MODEL), full),
                  pl.BlockSpec((FOX_W, D_MODEL), full),
                  pl.BlockSpec((DIFF_W, D_MODEL), full)],
        out_specs=pl.BlockSpec((tm, D_MODEL), row),
        out_shape=jax.ShapeDtypeStruct((n, D_MODEL), F32),
        compiler_params=_cparams("parallel"),
        name="outproj",
    )(x2, oa, ob, oc, wa, wb, wc)


ROUTE_OFF = N_GROUPS


MOE_CHUNK = 304
MOE_PREFIX_BLOCK = 256


def _moe_kernel(x_ref, g_ref, wrh_ref, wrl_ref, br_ref, wg_ref, wu_ref, wd_ref, gf_ref, o_ref,
                h_ref, gsel_ref, col_ref, row_ref, cnt_ref, *, final_norm):
    grp_id = pl.program_id(1)
    tm = x_ref.shape[0]
    ch = MOE_CHUNK

    @pl.when(grp_id == 0)
    def _():
        x = x_ref[...]
        ms = jnp.mean(x * x, axis=-1, keepdims=True)
        y = x * lax.rsqrt(ms + RMS_EPS) * g_ref[...]
        hb, hl = _split2(y)
        h_ref[...] = hb
        lg = (_dot(hb, wrh_ref[...]) + _dot(hl, wrh_ref[...]) + _dot(hb, wrl_ref[...])) + br_ref[...]
        lane = lax.broadcasted_iota(jnp.int32, lg.shape, 1).astype(F32)
        big = float(LANES)
        isg = lane < N_GROUPS
        gmax = jnp.max(jnp.where(isg, lg, -jnp.inf), axis=-1, keepdims=True)
        grp = jnp.min(jnp.where(isg & (lg == gmax), lane, big), axis=-1, keepdims=True)
        gprob = 1.0 / jnp.sum(jnp.where(isg, jnp.exp(lg - gmax), 0.0), axis=-1, keepdims=True)
        lo_lane = ROUTE_OFF + grp * EXPERTS_PER_GROUP
        ing = (lane >= lo_lane) & (lane < lo_lane + EXPERTS_PER_GROUP)
        v1 = jnp.max(jnp.where(ing, lg, -jnp.inf), axis=-1, keepdims=True)
        i1 = jnp.min(jnp.where(ing & (lg == v1), lane, big), axis=-1, keepdims=True)
        rest = ing & (lane != i1)
        v2 = jnp.max(jnp.where(rest, lg, -jnp.inf), axis=-1, keepdims=True)
        i2 = jnp.min(jnp.where(rest & (lg == v2), lane, big), axis=-1, keepdims=True)
        e2 = jnp.exp(v2 - v1)
        w1 = gprob / (1.0 + e2)
        w2 = gprob * e2 / (1.0 + e2)
        gate = jnp.where(lane == i1, w1, 0.0) + jnp.where(lane == i2, w2, 0.0)
        gsel = jnp.zeros_like(gate)
        for gg in range(N_GROUPS):
            shifted = pltpu.roll(gate, LANES - (ROUTE_OFF + gg * EXPERTS_PER_GROUP), 1)
            gsel = jnp.where(grp == float(gg), shifted, gsel)
        gsel = jnp.where(lane < EXPERTS_PER_GROUP, gsel, 0.0)
        ghi, glo = _split2(gsel)
        gsel_ref[:, 0:LANES] = ghi
        gsel_ref[:, LANES:2 * LANES] = glo
        onehot = jnp.where(lane == grp, 1.0, 0.0)
        pb = MOE_PREFIX_BLOCK
        tri = jnp.where(lax.broadcasted_iota(jnp.int32, (pb, pb), 0) > lax.broadcasted_iota(jnp.int32, (pb, pb), 1),
                        1.0, 0.0).astype(BF16)
        carry = jnp.zeros((1, LANES), F32)
        lane_b = lax.broadcasted_iota(jnp.int32, (pb, LANES), 1)
        for blk in range(tm // pb):
            oh_b = onehot[blk * pb:(blk + 1) * pb, :]
            prefix = _dot(tri, oh_b.astype(BF16)) + carry
            pos = jnp.sum(oh_b * prefix, axis=-1, keepdims=True)
            col_ref[blk * pb:(blk + 1) * pb, :] = jnp.where(lane_b == 1, pos, 0.0)
            carry = carry + jnp.sum(oh_b, axis=0, keepdims=True)
        for gg in range(N_GROUPS):
            cnt_ref[gg] = carry[0, gg].astype(jnp.int32)
        colinfo = col_ref[...] + jnp.where(lane == 0.0, grp, 0.0)
        col_ref[...] = colinfo
        row_ref[...] = colinfo.T[0:8, :]
        o_ref[...] = x

    grp_f = grp_id.astype(F32)
    key_col = jnp.where(col_ref[:, 0:1] == grp_f, col_ref[:, 1:2], -1.0)
    key_row = jnp.where(row_ref[0:1, :] == grp_f, row_ref[1:2, :], -1.0)

    def chunk(c, carry):
        base = (c * ch).astype(F32)
        sub_iota = lax.broadcasted_iota(jnp.int32, (ch, tm), 0).astype(F32)
        gather = jnp.where(key_row - base == sub_iota, 1.0, 0.0).astype(BF16)
        xg = _dot(gather, h_ref[...]).astype(BF16)
        gparts = _dot(gather, gsel_ref[...])
        gates = gparts[:, 0:LANES] + gparts[:, LANES:2 * LANES]
        y = None
        for e in range(EXPERTS_PER_GROUP):
            a = _dot(xg, wg_ref[0, e])
            u = _dot(xg, wu_ref[0, e])
            act = a * _sigmoid(a) * u * gates[:, e:e + 1]
            part = _dot(act.astype(BF16), wd_ref[0, e])
            y = part if y is None else y + part
        chp = -(-ch // LANES) * LANES
        lane_iota = lax.broadcasted_iota(jnp.int32, (tm, chp), 1).astype(F32)
        scatter = jnp.where(key_col - base == lane_iota, 1.0, 0.0).astype(BF16)
        yb = y.astype(BF16)
        if chp > ch:
            yb = jnp.concatenate([yb, jnp.zeros((chp - ch, D_MODEL), BF16)], axis=0)
        o_ref[...] = o_ref[...] + _dot(scatter, yb)
        return carry

    lax.fori_loop(0, (cnt_ref[grp_id] + ch - 1) // ch, chunk, 0)

    @pl.when(grp_id == N_GROUPS - 1)
    def _():
        if final_norm:
            y = o_ref[...]
            ms = jnp.mean(y * y, axis=-1, keepdims=True)
            o_ref[...] = y * lax.rsqrt(ms + RMS_EPS) * gf_ref[...]


def _moe(x2, g, wrh, wrl, br, wg, wu, wd, gf, layer, final_norm, tm=1024):
    n = x2.shape[0]
    tm = min(tm, n)
    row = lambda i, e: (i, 0)
    full = lambda i, e: (0, 0)
    epg = EXPERTS_PER_GROUP
    wspec_in = pl.BlockSpec((1, epg, D_MODEL, D_EXPERT), lambda i, e: (layer, e, 0, 0))
    wspec_out = pl.BlockSpec((1, epg, D_EXPERT, D_MODEL), lambda i, e: (layer, e, 0, 0))
    return pl.pallas_call(
        functools.partial(_moe_kernel, final_norm=final_norm),
        grid=(n // tm, N_GROUPS),
        in_specs=[pl.BlockSpec((tm, D_MODEL), row, pipeline_mode=pl.Buffered(1)),
                  pl.BlockSpec((1, D_MODEL), full),
                  pl.BlockSpec((D_MODEL, LANES), full),
                  pl.BlockSpec((D_MODEL, LANES), full),
                  pl.BlockSpec((1, LANES), full),
                  wspec_in, wspec_in, wspec_out,
                  pl.BlockSpec((1, D_MODEL), full)],
        out_specs=pl.BlockSpec((tm, D_MODEL), row),
        out_shape=jax.ShapeDtypeStruct((n, D_MODEL), F32),
        scratch_shapes=[pltpu.VMEM((tm, D_MODEL), BF16),
                        pltpu.VMEM((tm, 2 * LANES), BF16),
                        pltpu.VMEM((tm, LANES), F32),
                        pltpu.VMEM((8, tm), F32),
                        pltpu.SMEM((N_GROUPS,), jnp.int32)],
        compiler_params=_cparams("parallel", "arbitrary"),
        name="moe",
    )(x2, g, wrh, wrl, br, wg, wu, wd, gf)


def _head_perm():
    hg = NSA_HEADS // NSA_KV_GROUPS
    order = []
    for j in range(hg):
        order += [j, j + hg]
    return np.concatenate([np.arange(h * HEAD_DIM, (h + 1) * HEAD_DIM) for h in order])


def _split_hi_lo(w):
    hi = w.astype(BF16)
    return hi, (w - hi.astype(F32)).astype(BF16)


def _layer_params(l, w_in, b_gate, b_fgt, cmp_pe, cmp_w1, cmp_w2, w_out):
    offs = np.concatenate([[0], np.cumsum(IN_WIDTHS)])
    seg = [w_in[l][:, offs[i]:offs[i + 1]] for i in range(len(IN_WIDTHS))]
    (wq, wkc, wvc, wks, wvs, wkw, wvw, wgt, wfq, wfk, wfv, wff, wdq, wdk, wdv) = seg
    perm = _head_perm()
    gate_perm = np.array([h * 3 + br for br in range(3) for h in range(NSA_HEADS)])
    wm = jnp.concatenate([wq[:, perm] * (HEAD_DIM ** -0.5 * LOG2E), wks, wvs, wkw, wvw,
                          wfq * (HEAD_DIM ** -0.5 * LOG2E), wfk, wfv,
                          wdq * (DIFF_QK_DIM ** -0.5 * LOG2E), wdk, wdv, wkc, wvc], axis=1).astype(BF16)
    pad = LANES - NSA_GATE - 3 * FOX_HEADS
    ws = jnp.concatenate([wgt[:, gate_perm], wff, wff, wff, jnp.zeros((D_MODEL, pad), F32)], axis=1)
    wsh, wsl = _split_hi_lo(ws)
    bs = jnp.concatenate([b_gate[l][gate_perm], b_fgt[l], b_fgt[l], b_fgt[l], jnp.zeros((pad,), F32)])[None, :]

    eye = jnp.eye(NSA_KV_GROUPS, dtype=F32)
    half = CMP_BLOCK // 2
    cmp = []
    for pe, w1, w2 in zip(cmp_pe, cmp_w1, cmp_w2):
        w1r = w1[l].reshape(2, half, HEAD_DIM, CMP_HIDDEN)
        w1x = jnp.einsum('srdk,gh->srgdhk', w1r, eye).reshape(2, half * NSA_KV, NSA_KV_GROUPS * CMP_HIDDEN)
        w2bd = jnp.einsum('kd,gh->gkhd', w2[l], eye).reshape(NSA_KV_GROUPS * CMP_HIDDEN, NSA_KV)
        pe2 = jnp.broadcast_to(pe[l].reshape(2, half, 1, HEAD_DIM), (2, half, NSA_KV_GROUPS, HEAD_DIM))
        cmp.append((pe2.reshape(2, half * NSA_KV), w1x[0].astype(BF16), w1x[1].astype(BF16), w2bd.astype(BF16)))

    wo = w_out[l]
    wa = wo[:NSA_Q][perm].astype(BF16)
    wb = wo[NSA_Q:NSA_Q + FOX_W].astype(BF16)
    wc = wo[NSA_Q + FOX_W:].astype(BF16)
    return dict(wm=wm, wsh=wsh, wsl=wsl, bs=bs, cmp_k=cmp[0], cmp_v=cmp[1], wa=wa, wb=wb, wc=wc)


def _overlap_matrix(n_slab, n_cmp, n_sel):
    c_start = np.arange(n_cmp) * CMP_STRIDE
    s_start = np.arange(n_sel) * SLC_BLOCK
    ov = np.clip(np.minimum(c_start[:, None] + CMP_BLOCK, s_start[None, :] + SLC_BLOCK)
                 - np.maximum(c_start[:, None], s_start[None, :]), 0, None) / CMP_BLOCK
    full = np.zeros((n_slab, LANES), np.float32)
    full[:n_cmp, :n_sel] = ov
    return jnp.asarray(full, dtype=BF16)


def kernel(x, norm_attn, w_in, b_gate, b_fgt, cmp_k_pe, cmp_k_w1, cmp_k_w2, cmp_v_pe, cmp_v_w1, cmp_v_w2, diff_lambda, diff_subln, w_out, norm_ffn, w_grp, b_grp, w_exp, b_exp, w_e_gate, w_e_up, w_e_down, norm_final):
    b, t, d = x.shape
    depth = w_in.shape[0]
    n = b * t
    n_slab = t // CMP_STRIDE
    n_cmp = (t - CMP_BLOCK) // CMP_STRIDE + 1
    n_sel = t // SLC_BLOCK
    top_n = min(SLC_TOPK, n_sel)
    ov = _overlap_matrix(n_slab, n_cmp, n_sel)

    wg_all = w_e_gate.astype(BF16)
    wu_all = w_e_up.astype(BF16)
    wd_all = w_e_down.astype(BF16)
    gf = norm_final[None, :]

    x2 = x.reshape(n, d)
    for l in range(depth):
        p = _layer_params(l, w_in, b_gate, b_fgt, (cmp_k_pe, cmp_v_pe), (cmp_k_w1, cmp_v_w1),
                          (cmp_k_w2, cmp_v_w2), w_out)
        main, kc, vc, small = _inproj(x2, norm_attn[l][None, :], p["wm"], p["wsh"], p["wsl"], p["bs"])
        main3 = main.reshape(b, t, MAIN_W)
        small3 = small.reshape(b, t, LANES)
        kcmp = _compress(kc.reshape(b, n_slab, CMP_STRIDE * NSA_KV), *p["cmp_k"])
        vcmp = _compress(vc.reshape(b, n_slab, CMP_STRIDE * NSA_KV), *p["cmp_v"])
        o_nsa, pen, used = _cmp_attn(main3, kcmp, vcmp, small3, ov, n_cmp, n_sel, top_n)
        nkt = t // min(TK, t)
        tile_used = used.reshape(b, t // TQ, -1, LANES).max(axis=2)[:, :, :nkt].astype(jnp.int32)
        o_nsa = _flash(main3, "slc", (pen, small3, o_nsa), F32, tile_used=tile_used)
        o_nsa = _flash(main3, "win", (small3, o_nsa), F32)
        o_fox = _flash(main3, "fox", (_cumgate(small3),), BF16)
        sub2 = jnp.concatenate([diff_subln[l], diff_subln[l]])[None, :]
        o_diff = _flash(main3, "diff", (diff_lambda[l], sub2), BF16, lam_init=_lambda_init(l))

        x2 = _outproj(x2, o_nsa.reshape(n, NSA_Q), o_fox.reshape(n, FOX_W), o_diff.reshape(n, DIFF_W),
                      p["wa"], p["wb"], p["wc"])

        wr = jnp.concatenate([w_grp[l], w_exp[l], jnp.zeros((d, LANES - N_GROUPS - N_EXPERTS), F32)], axis=1)
        wrh, wrl = _split_hi_lo(wr)
        br = jnp.concatenate([b_grp[l], b_exp[l], jnp.zeros((LANES - N_GROUPS - N_EXPERTS,), F32)])[None, :]
        x2 = _moe(x2, norm_ffn[l][None, :], wrh, wrl, br, wg_all, wu_all, wd_all, gf, l,
                  final_norm=(l == depth - 1))
    return x2.reshape(b, t, d)
```

```python
import functools
import math

import ml_dtypes
import numpy as np
import jax
import jax.numpy as jnp
from jax import lax
from jax.experimental import pallas as pl
from jax.experimental.pallas import tpu as pltpu

F32 = jnp.float32
BF16 = jnp.bfloat16

D_MODEL = 1024
HEAD_DIM = 64
NSA_HEADS = 8
NSA_KV_GROUPS = 2
CMP_BLOCK = 32
CMP_STRIDE = 16
CMP_HIDDEN = 256
SLC_BLOCK = 64
SLC_SHIFT = 6
SLC_TOPK = 16
WINDOW = 512
FOX_HEADS = 4
DIFF_HEADS = 4
DIFF_QK_DIM = HEAD_DIM // 2
N_GROUPS = 4
EXPERTS_PER_GROUP = 8
N_EXPERTS = N_GROUPS * EXPERTS_PER_GROUP
D_EXPERT = 256
RMS_EPS = 1e-6
FORCE_SCORE = 1e4
NEG_INF = -1e30
MASKED = -2e30

NSA_Q = NSA_HEADS * HEAD_DIM
NSA_KV = NSA_KV_GROUPS * HEAD_DIM
NSA_GATE = NSA_HEADS * 3
FOX_W = FOX_HEADS * HEAD_DIM
DIFF_W = DIFF_HEADS * HEAD_DIM
MIX_WIDTH = NSA_Q + FOX_W + DIFF_W
IN_WIDTHS = (NSA_Q, NSA_KV, NSA_KV, NSA_KV, NSA_KV, NSA_KV, NSA_KV, NSA_GATE,
             FOX_W, FOX_W, FOX_W, FOX_HEADS, DIFF_W, DIFF_W, DIFF_W)

LANES = 128
SUBLANES = 8
VMEM_LIMIT_BYTES = 56 * 1024 * 1024

MAIN_W = NSA_Q + 4 * NSA_KV + 3 * FOX_W + 3 * DIFF_W
PROJ_W = MAIN_W + 2 * NSA_KV
COL_KS, COL_VS, COL_KW, COL_VW = 4, 5, 6, 7
COL_FQ, COL_FK, COL_FV = 4, 5, 6
COL_DQ, COL_DK, COL_DV = 7, 8, 9
FGT_LANES = (NSA_GATE, NSA_GATE + FOX_HEADS, NSA_GATE + 2 * FOX_HEADS)
LOG2E = 1.4426950408889634

TQ = 512
TK = 512
KSTRIP = 256
ONES_ROWS = 16

def _alibi(n):
    return [float(2.0 ** (-8.0 * (i + 1) / n)) for i in range(n)]


def _lambda_init(layer):
    return 0.8 - 0.6 * math.exp(-0.3 * layer)


def _cparams(*sem):
    return pltpu.CompilerParams(dimension_semantics=sem, vmem_limit_bytes=VMEM_LIMIT_BYTES)


def _dot(a, b):
    return jnp.dot(a, b, preferred_element_type=F32)


def _dot_nt(a, b):
    return lax.dot_general(a, b, (((1,), (1,)), ((), ())), preferred_element_type=F32)


def _split2(x):
    hi = x.astype(BF16)
    lo = (x - hi.astype(F32)).astype(BF16)
    return hi, lo


def _sigmoid(z):
    return 1.0 / (1.0 + jnp.exp(-z))


def _keep_lanes(sel, blk):
    return jnp.where(sel, blk.astype(F32), 0.0).astype(BF16)


def _inproj_kernel(x_ref, g_ref, wm_ref, wsh_ref, wsl_ref, bs_ref, main_ref, kc_ref, vc_ref, small_ref):
    x = x_ref[...]
    ms = jnp.mean(x * x, axis=-1, keepdims=True)
    y = x * lax.rsqrt(ms + RMS_EPS) * g_ref[...]
    hb, hl = _split2(y)
    full = _dot(hb, wm_ref[...]).astype(BF16)
    main_ref[...] = full[:, :MAIN_W]
    kc_ref[...] = full[:, MAIN_W:MAIN_W + NSA_KV]
    vc_ref[...] = full[:, MAIN_W + NSA_KV:]
    small_ref[...] = (_dot(hb, wsh_ref[...]) + _dot(hl, wsh_ref[...]) + _dot(hb, wsl_ref[...])) + bs_ref[...]


def _inproj(x2, g, wm, wsh, wsl, bs, tm=512):
    n = x2.shape[0]
    full = lambda i: (0, 0)
    return pl.pallas_call(
        _inproj_kernel,
        grid=(n // tm,),
        in_specs=[pl.BlockSpec((tm, D_MODEL), lambda i: (i, 0)),
                  pl.BlockSpec((1, D_MODEL), full),
                  pl.BlockSpec((D_MODEL, PROJ_W), full),
                  pl.BlockSpec((D_MODEL, LANES), full),
                  pl.BlockSpec((D_MODEL, LANES), full),
                  pl.BlockSpec((1, LANES), full)],
        out_specs=[pl.BlockSpec((tm, MAIN_W), lambda i: (i, 0)),
                   pl.BlockSpec((tm, NSA_KV), lambda i: (i, 0)),
                   pl.BlockSpec((tm, NSA_KV), lambda i: (i, 0)),
                   pl.BlockSpec((tm, LANES), lambda i: (i, 0))],
        out_shape=[jax.ShapeDtypeStruct((n, MAIN_W), BF16),
                   jax.ShapeDtypeStruct((n, NSA_KV), BF16),
                   jax.ShapeDtypeStruct((n, NSA_KV), BF16),
                   jax.ShapeDtypeStruct((n, LANES), F32)],
        compiler_params=_cparams("parallel"),
        name="inproj",
    )(x2, g, wm, wsh, wsl, bs)


def _compress_kernel(r_ref, pe_ref, w1a_ref, w1b_ref, w2_ref, o_ref):
    r = r_ref[0].astype(F32)
    pe = pe_ref[...]
    ra = (r + pe[0:1, :]).astype(BF16)
    rb = (r + pe[1:2, :]).astype(BF16)
    a = _dot(ra, w1a_ref[...])
    b = _dot(rb, w1b_ref[...])
    n_slab = a.shape[0]
    hid = a + pltpu.roll(b, n_slab - 1, 0)
    hid = jax.nn.gelu(hid)
    o_ref[0] = _dot(hid.astype(BF16), w2_ref[...]).astype(BF16)


def _compress(r, pe2, w1a, w1b, w2bd):
    b, n_slab, w = r.shape
    full = lambda i: (0, 0)
    return pl.pallas_call(
        _compress_kernel,
        grid=(b,),
        in_specs=[pl.BlockSpec((1, n_slab, w), lambda i: (i, 0, 0)),
                  pl.BlockSpec((2, w), full),
                  pl.BlockSpec((w, 2 * CMP_HIDDEN), full),
                  pl.BlockSpec((w, 2 * CMP_HIDDEN), full),
                  pl.BlockSpec((2 * CMP_HIDDEN, LANES), full)],
        out_specs=pl.BlockSpec((1, n_slab, LANES), lambda i: (i, 0, 0)),
        out_shape=jax.ShapeDtypeStruct((b, n_slab, LANES), BF16),
        compiler_params=_cparams("parallel"),
        name="compress",
    )(r, pe2, w1a, w1b, w2bd)


def _cumgate_kernel(z_ref, o_ref, carry_ref, *, tc):
    @pl.when(pl.program_id(1) == 0)
    def _():
        carry_ref[...] = jnp.zeros_like(carry_ref)

    z = z_ref[0]
    logf = -(jnp.maximum(-z, 0.0) + jnp.log(1.0 + jnp.exp(-jnp.abs(z))))
    hi = logf.astype(BF16)
    r1 = logf - hi.astype(F32)
    mid = r1.astype(BF16)
    lo = (r1 - mid.astype(F32)).astype(BF16)
    tri = (lax.broadcasted_iota(jnp.int32, (tc, tc), 0) >= lax.broadcasted_iota(jnp.int32, (tc, tc), 1))
    tri = jnp.where(tri, 1.0, 0.0).astype(BF16)
    c = (_dot(tri, hi) + _dot(tri, mid)) + _dot(tri, lo) + carry_ref[0:1, :]
    carry_ref[...] = jnp.broadcast_to(c[tc - 1:tc, :], carry_ref.shape)
    v = c * (-LOG2E)
    p0 = v.astype(BF16)
    r1 = v - p0.astype(F32)
    p1 = r1.astype(BF16)
    p2 = (r1 - p1.astype(F32)).astype(BF16)
    lane = lax.broadcasted_iota(jnp.int32, v.shape, 1)
    pieces = jnp.where(lane < FGT_LANES[1], p0.astype(F32), jnp.where(lane < FGT_LANES[2], p1.astype(F32), p2.astype(F32)))
    keep = (lane >= FGT_LANES[0]) & (lane < FGT_LANES[2] + FOX_HEADS)
    o_ref[0] = jnp.where(keep, pieces, 0.0).astype(BF16)


def _cumgate(small3, tc=512):
    b, t, w = small3.shape
    return pl.pallas_call(
        functools.partial(_cumgate_kernel, tc=tc),
        grid=(b, t // tc),
        in_specs=[pl.BlockSpec((1, tc, w), lambda i, j: (i, j, 0))],
        out_specs=pl.BlockSpec((1, tc, w), lambda i, j: (i, j, 0)),
        out_shape=jax.ShapeDtypeStruct((b, t, w), BF16),
        scratch_shapes=[pltpu.VMEM((8, w), F32)],
        compiler_params=_cparams("parallel", "arbitrary"),
        name="cumgate",
    )(small3)


def _cmp_kernel(q_ref, kc_ref, vc_ref, sm_ref, ov_ref, o_ref, pen_ref, used_ref, *, tq, n_cmp, n_sel, top_n):
    t0 = pl.program_id(1) * tq
    nck = kc_ref.shape[1]
    row = t0 + lax.broadcasted_iota(jnp.int32, (tq, nck), 0)
    col = lax.broadcasted_iota(jnp.int32, (tq, nck), 1)
    dist = row - (col * CMP_STRIDE + (CMP_BLOCK - 1))
    mask = (dist >= 0) & (col < n_cmp)
    distf = dist.astype(F32)
    lane = lax.broadcasted_iota(jnp.int32, (tq, LANES), 1)
    low = lane < HEAD_DIM
    kc = kc_ref[0]
    vc = vc_ref[0]
    sig = _sigmoid(sm_ref[0])
    slopes = _alibi(NSA_HEADS)
    psum = [None, None]
    hg = NSA_HEADS // NSA_KV_GROUPS
    for j in range(hg):
        qblk = q_ref[0, :, j * LANES:(j + 1) * LANES]
        outs = []
        for g in range(NSA_KV_GROUPS):
            h = j + hg * g
            qm = _keep_lanes(low if g == 0 else ~low, qblk)
            s = _dot_nt(qm, kc) - (slopes[h] * LOG2E) * distf
            s = jnp.where(mask, s, NEG_INF)
            m = jnp.max(s, axis=-1, keepdims=True)
            p = jnp.where(mask, jnp.exp2(s - m), 0.0)
            l = jnp.sum(p, axis=-1, keepdims=True)
            p = p * jnp.where(l > 0.0, 1.0 / l, 0.0)
            psum[g] = p if psum[g] is None else psum[g] + p
            outs.append(_dot(p.astype(BF16), vc))
        gate = jnp.where(low, sig[:, j:j + 1], sig[:, j + hg:j + hg + 1])
        o_ref[0, :, j * LANES:(j + 1) * LANES] = (jnp.where(low, outs[0], outs[1]) * gate).astype(o_ref.dtype)

    sub8 = lax.broadcasted_iota(jnp.int32, (SUBLANES, tq), 0)
    cur_r = (t0 + lax.broadcasted_iota(jnp.int32, (tq, LANES), 0)) >> SLC_SHIFT
    pen_t = []
    for g in range(NSA_KV_GROUPS):
        ph, plo = _split2(psum[g])
        imp = _dot(ph, ov_ref[...]) + _dot(plo, ov_ref[...])
        valid = lane <= cur_r
        forced = (lane == 0) | (lane == cur_r) | (lane == cur_r - 1)
        imp = jnp.where(valid, jnp.where(forced, FORCE_SCORE, imp), NEG_INF)
        imp_t = imp.T[:n_sel, :]
        ngrp = n_sel // SUBLANES
        rows = [imp_t[i * SUBLANES:(i + 1) * SUBLANES, :] for i in range(ngrp)]
        cnt = [jnp.zeros((SUBLANES, tq), F32) for _ in range(ngrp)]
        for k in range(n_sel):
            rk = imp_t[k:k + 1, :]
            for i in range(ngrp):
                if i * SUBLANES > k:
                    one = jnp.where(rk >= rows[i], 1.0, 0.0)
                elif (i + 1) * SUBLANES <= k:
                    one = jnp.where(rk > rows[i], 1.0, 0.0)
                else:
                    one = jnp.where(sub8 > k - i * SUBLANES,
                                    jnp.where(rk >= rows[i], 1.0, 0.0), jnp.where(rk > rows[i], 1.0, 0.0))
                cnt[i] = cnt[i] + one
        pen_t.append(jnp.where(jnp.concatenate(cnt, axis=0) < float(top_n), 0.0, MASKED))
    for g in range(NSA_KV_GROUPS):
        full = jnp.concatenate([pen_t[g], jnp.zeros((LANES - n_sel, tq), F32)], axis=0)
        pen_ref[0, :, g * LANES:(g + 1) * LANES] = full.T.astype(BF16)
    picked = jnp.where(jnp.maximum(pen_t[0], pen_t[1]) == 0.0, 1.0, 0.0)
    bpt = TK // SLC_BLOCK
    lane1 = lax.broadcasted_iota(jnp.int32, (1, LANES), 1)
    used = jnp.zeros((1, LANES), F32)
    for kt in range(n_sel // bpt):
        hit = jnp.max(picked[kt * bpt:(kt + 1) * bpt, :], axis=(0, 1), keepdims=True)
        used = jnp.where(lane1 == kt, hit, used)
    used_ref[0, 0] = used


def _cmp_attn(main3, kcmp, vcmp, small3, ov, n_cmp, n_sel, top_n, tq=256):
    b, t, _ = main3.shape
    nck = kcmp.shape[1]
    return pl.pallas_call(
        functools.partial(_cmp_kernel, tq=tq, n_cmp=n_cmp, n_sel=n_sel, top_n=top_n),
        grid=(b, t // tq),
        in_specs=[pl.BlockSpec((1, tq, NSA_Q), lambda i, j: (i, j, 0)),
                  pl.BlockSpec((1, nck, LANES), lambda i, j: (i, 0, 0)),
                  pl.BlockSpec((1, nck, LANES), lambda i, j: (i, 0, 0)),
                  pl.BlockSpec((1, tq, LANES), lambda i, j: (i, j, 0)),
                  pl.BlockSpec((nck, LANES), lambda i, j: (0, 0))],
        out_specs=[pl.BlockSpec((1, tq, NSA_Q), lambda i, j: (i, j, 0)),
                   pl.BlockSpec((1, tq, NSA_KV_GROUPS * LANES), lambda i, j: (i, j, 0)),
                   pl.BlockSpec((1, 1, 1, LANES), lambda i, j: (i, j, 0, 0))],
        out_shape=[jax.ShapeDtypeStruct((b, t, NSA_Q), BF16),
                   jax.ShapeDtypeStruct((b, t, NSA_KV_GROUPS * LANES), BF16),
                   jax.ShapeDtypeStruct((b, t // tq, 1, LANES), F32)],
        compiler_params=_cparams("parallel", "parallel"),
        name="cmp_attn",
    )(main3, kcmp, vcmp, small3, ov)


def _bf16_terms(c, n=3):
    out, r = [], float(c)
    for _ in range(n):
        p = float(np.float32(r).astype(ml_dtypes.bfloat16))
        out.append(p)
        r -= p
    return out


def _alibi_q_aug(lane, a0, slope):
    aug = jnp.zeros(lane.shape, F32)
    for i, c in enumerate(_bf16_terms(slope * LOG2E)):
        aug = jnp.where(lane == a0 + 2 * i, float(SLC_BLOCK) * c, aug)
        aug = jnp.where(lane == a0 + 2 * i + 1, c, aug)
    return aug


def _alibi_k_aug(lane, rel, a0):
    hi = (rel >> SLC_SHIFT).astype(F32)
    lo = (rel & (SLC_BLOCK - 1)).astype(F32)
    inside = (lane >= a0) & (lane < a0 + 6)
    odd = ((lane - a0) & 1) == 1
    return jnp.where(inside, jnp.where(odd, lo, hi), 0.0)


ALIBI_LANE = {"slc": SLC_BLOCK, "win": 0, "diff": 0}
STACKS = {"slc": (1, 8), "win": (1, 8), "fox": (2, 2), "diff": (2, 4)}


def _v_half(mode, vh):
    nv = STACKS[mode][1]
    if mode in ("slc", "win"):
        return vh // (NSA_HEADS // NSA_KV_GROUPS)
    return (vh % nv) // (nv // 2)


def _flash_kernel(*refs, mode, tq, tk, lam_init, ntiles):
    sched_ref, refs = refs[0], refs[1:]
    if mode == "slc":
        use_ref, q_ref, k_ref, v_ref, pen_ref, sm_ref, add_ref, o_ref, qst_ref, m_ref, acc_ref = refs
    elif mode == "win":
        q_ref, k_ref, v_ref, sm_ref, add_ref, o_ref, qst_ref, m_ref, acc_ref = refs
    elif mode == "fox":
        q_ref, k_ref, v_ref, fa_ref, o_ref, qst_ref, m_ref, acc_ref = refs
    else:
        q_ref, k_ref, v_ref, lam_ref, sub_ref, o_ref, qst_ref, m_ref, acc_ref = refs
    nstack, nv = STACKS[mode]
    nsa = mode in ("slc", "win")
    hg = NSA_HEADS // NSA_KV_GROUPS
    step_id = pl.program_id(1)
    q0 = sched_ref[SCHED_Q, step_id] * tq
    k0 = sched_ref[SCHED_K, step_id] * tk

    @pl.when(sched_ref[SCHED_FIRST, step_id] == 1)
    def _():
        m_ref[...] = jnp.full(m_ref.shape, NEG_INF, F32)
        acc_ref[...] = jnp.zeros(acc_ref.shape, F32)
        lane = lax.broadcasted_iota(jnp.int32, (tq, LANES), 1)
        for vh in range(nstack * nv):
            if nsa:
                jb, g = vh % hg, vh // hg
                lo_lane, width = g * HEAD_DIM, HEAD_DIM
                aug = _alibi_q_aug(lane, ALIBI_LANE[mode], _alibi(NSA_HEADS)[vh])
                if mode == "slc":
                    aug = aug + pen_ref[0, :, g * LANES:(g + 1) * LANES].astype(F32)
            elif mode == "fox":
                jb, r = vh // nv, vh % nv
                lo_lane, width = r * HEAD_DIM, HEAD_DIM
                hit = (lane == FGT_LANES[0] + vh) | (lane == FGT_LANES[1] + vh) | (lane == FGT_LANES[2] + vh)
                aug = jnp.where(hit, 1.0, 0.0)
            else:
                jb, r = vh // nv, vh % nv
                lo_lane, width = r * DIFF_QK_DIM, DIFF_QK_DIM
                aug = _alibi_q_aug(lane, ALIBI_LANE[mode], _alibi(DIFF_HEADS)[vh // 2])
            sel = (lane >= lo_lane) & (lane < lo_lane + width)
            qst_ref[vh * tq:(vh + 1) * tq, 0:LANES] = _keep_lanes(sel, q_ref[0, :, jb * LANES:(jb + 1) * LANES])
            qst_ref[vh * tq:(vh + 1) * tq, LANES:2 * LANES] = aug.astype(BF16)

    def step(masked):
        lane_k = lax.broadcasted_iota(jnp.int32, (tk, LANES), 1)
        row_k = lax.broadcasted_iota(jnp.int32, (tk, LANES), 0)
        if mode == "fox":
            k_aug = fa_ref[0]
        else:
            ka = _alibi_k_aug(lane_k, (k0 - q0) + row_k, ALIBI_LANE[mode])
            if mode == "slc":
                ka = ka + jnp.where(((k0 + row_k) >> SLC_SHIFT) == lane_k, 1.0, 0.0)
            k_aug = ka.astype(BF16)
        if masked:
            dist = (q0 - k0) + lax.broadcasted_iota(jnp.int32, (tk, tq), 1) - lax.broadcasted_iota(jnp.int32, (tk, tq), 0)
            ok = dist >= 0
            if mode == "win":
                ok = ok & (dist < WINDOW)
            bias = jnp.where(ok, 0.0, MASKED)
        keys, vals_t = [], []
        for st in range(nstack):
            kblk = k_ref[0] if nsa else k_ref[0, :, st * LANES:(st + 1) * LANES]
            vblk = v_ref[0] if nsa else v_ref[0, :, st * LANES:(st + 1) * LANES]
            keys.append(jnp.concatenate([kblk, k_aug], axis=1))
            v_t = vblk.astype(F32).T.astype(BF16)
            ones = jnp.ones((ONES_ROWS, tk), BF16)
            vals_t.append([jnp.concatenate([v_t[hf * HEAD_DIM:(hf + 1) * HEAD_DIM, :], ones], axis=0)
                           for hf in range(LANES // HEAD_DIM)])

        nstrip = tk // KSTRIP
        nvh = nstack * nv

        def scores(vh):
            s = _dot_nt(keys[vh // nv], qst_ref[vh * tq:(vh + 1) * tq, :])
            if masked:
                s = s + bias
            return s, jnp.max(s, axis=0, keepdims=True)

        cur, cur_max = scores(0)
        for vh in range(nvh):
            cols = slice(vh * tq, (vh + 1) * tq)
            v_half = vals_t[vh // nv][_v_half(mode, vh)]
            m_old = m_ref[:, cols]
            m_new = jnp.maximum(m_old, cur_max)
            alpha = jnp.exp2(m_old - m_new)
            if vh + 1 < nvh:
                nxt, nxt_max = scores(vh + 1)
            pv = None
            for r in range(nstrip):
                p = jnp.exp2((cur[r * KSTRIP:(r + 1) * KSTRIP, :] - m_new).astype(BF16))
                part = _dot(v_half[:, r * KSTRIP:(r + 1) * KSTRIP], p)
                pv = part if pv is None else pv + part
            m_ref[:, cols] = m_new
            acc_ref[:, cols] = alpha * acc_ref[:, cols] + pv
            if vh + 1 < nvh:
                cur, cur_max = nxt, nxt_max

    is_masked = sched_ref[SCHED_MASKED, step_id] == 1
    if mode == "win":
        pl.when(is_masked)(lambda: step(True))
    elif mode == "slc":
        tile_id = (pl.program_id(0) * ntiles[0] + sched_ref[SCHED_Q, step_id]) * ntiles[1] + sched_ref[SCHED_K, step_id]
        needed = use_ref[tile_id] == 1
        pl.when(is_masked & needed)(lambda: step(True))
        pl.when(jnp.logical_not(is_masked) & needed)(lambda: step(False))
    else:
        pl.when(is_masked)(lambda: step(True))
        pl.when(jnp.logical_not(is_masked))(lambda: step(False))

    @pl.when(sched_ref[SCHED_LAST, step_id] == 1)
    def _():
        inv = 1.0 / acc_ref[HEAD_DIM:HEAD_DIM + 1, :]

        def out_t(vh):
            return acc_ref[0:HEAD_DIM, vh * tq:(vh + 1) * tq] * inv[:, vh * tq:(vh + 1) * tq]

        def pair(a, b):
            return jnp.concatenate([a, b], axis=0).T

        if nsa:
            sig_t = _sigmoid(sm_ref[0]).T
            br = 1 if mode == "slc" else 2
            for jb in range(hg):
                c0 = br * NSA_HEADS + jb
                o = pair(out_t(jb) * sig_t[c0:c0 + 1, :], out_t(jb + hg) * sig_t[c0 + hg:c0 + hg + 1, :])
                o_ref[0, :, jb * LANES:(jb + 1) * LANES] = (
                    add_ref[0, :, jb * LANES:(jb + 1) * LANES].astype(F32) + o).astype(o_ref.dtype)
        elif mode == "fox":
            for pb in range(nstack):
                o = pair(out_t(2 * pb), out_t(2 * pb + 1))
                o_ref[0, :, pb * LANES:(pb + 1) * LANES] = o.astype(o_ref.dtype)
        else:
            lam = lam_ref[...]
            lam_full = (jnp.exp(jnp.sum(lam[0:1, :] * lam[1:2, :], axis=-1, keepdims=True))
                        - jnp.exp(jnp.sum(lam[2:3, :] * lam[3:4, :], axis=-1, keepdims=True)) + lam_init)
            for pb in range(nstack):
                normed = []
                for hh in range(2):
                    d = out_t(pb * nv + 2 * hh) - lam_full * out_t(pb * nv + 2 * hh + 1)
                    ms = jnp.mean(d * d, axis=0, keepdims=True)
                    normed.append(d * lax.rsqrt(ms + RMS_EPS) * sub_ref[...] * (1.0 - lam_init))
                o_ref[0, :, pb * LANES:(pb + 1) * LANES] = pair(normed[0], normed[1]).astype(o_ref.dtype)


def _flash(main3, mode, extra, out_dtype, lam_init=0.0, tile_used=None, tq=TQ, tk=TK):
    b, t, _ = main3.shape
    tk = min(tk, t)
    nstack, nv = STACKS[mode]
    sched = _flash_schedule(t, tq, tk, mode == "win")
    prefetch = [jnp.asarray(sched)]
    if mode == "slc":
        prefetch.append(tile_used.reshape(-1))
    qtile = lambda w, c: pl.BlockSpec((1, tq, w), lambda bi, s, sch, *_: (bi, sch[SCHED_Q, s], c))
    ktile = lambda w, c: pl.BlockSpec((1, tk, w), lambda bi, s, sch, *_: (bi, sch[SCHED_K, s], c))
    if mode in ("slc", "win"):
        kcol, vcol = (COL_KS, COL_VS) if mode == "slc" else (COL_KW, COL_VW)
        in_specs = [qtile(NSA_Q, 0), ktile(LANES, kcol), ktile(LANES, vcol)]
        if mode == "slc":
            in_specs.append(qtile(NSA_KV_GROUPS * LANES, 0))
        in_specs += [qtile(LANES, 0), qtile(NSA_Q, 0)]
        out_w = NSA_Q
    else:
        qc, kc, vc = (COL_FQ, COL_FK, COL_FV) if mode == "fox" else (COL_DQ, COL_DK, COL_DV)
        in_specs = [qtile(2 * LANES, qc), ktile(2 * LANES, kc), ktile(2 * LANES, vc)]
        if mode == "fox":
            in_specs.append(ktile(LANES, 0))
        else:
            in_specs += [pl.BlockSpec((4, DIFF_QK_DIM), lambda bi, s, sch, *_: (0, 0)),
                         pl.BlockSpec((HEAD_DIM, 1), lambda bi, s, sch, *_: (0, 0))]
        out_w = 2 * LANES
    cols = nstack * nv * tq
    return pl.pallas_call(
        functools.partial(_flash_kernel, mode=mode, tq=tq, tk=tk, lam_init=lam_init, ntiles=(t // tq, t // tk)),
        grid_spec=pltpu.PrefetchScalarGridSpec(
            num_scalar_prefetch=len(prefetch),
            grid=(b, sched.shape[1]),
            in_specs=in_specs,
            out_specs=qtile(out_w, 0),
            scratch_shapes=[pltpu.VMEM((cols, 2 * LANES), BF16),
                            pltpu.VMEM((1, cols), F32),
                            pltpu.VMEM((HEAD_DIM + ONES_ROWS, cols), F32)]),
        out_shape=jax.ShapeDtypeStruct((b, t, out_w), out_dtype),
        compiler_params=_cparams("parallel", "arbitrary"),
        name="flash_" + mode,
    )(*prefetch, main3, main3, main3, *extra)


SCHED_Q, SCHED_K, SCHED_FIRST, SCHED_LAST, SCHED_MASKED = range(5)


def _flash_schedule(t, tq, tk, window):
    rows = []
    for i in range(t // tq):
        q_lo, q_hi = i * tq, i * tq + tq - 1
        k_first = max(q_lo - (WINDOW - 1), 0) // tk if window else 0
        tiles = list(range(k_first, q_hi // tk + 1))
        for n, kt in enumerate(tiles):
            fully_visible = kt * tk + tk - 1 <= q_lo and not window
            rows.append((i, kt, int(n == 0), int(n == len(tiles) - 1), int(not fully_visible)))
    return np.asarray(rows, np.int32).T


def _outproj_kernel(x_ref, oa_ref, ob_ref, oc_ref, wa_ref, wb_ref, wc_ref, o_ref):
    acc = _dot(oa_ref[...], wa_ref[...])
    acc = acc + _dot(ob_ref[...], wb_ref[...])
    acc = acc + _dot(oc_ref[...], wc_ref[...])
    o_ref[...] = x_ref[...] + acc


def _outproj(x2, oa, ob, oc, wa, wb, wc, tm=512):
    n = x2.shape[0]
    row = lambda i: (i, 0)
    full = lambda i: (0, 0)
    return pl.pallas_call(
        _outproj_kernel,
        grid=(n // tm,),
        in_specs=[pl.BlockSpec((tm, D_MODEL), row),
                  pl.BlockSpec((tm, NSA_Q), row),
                  pl.BlockSpec((tm, FOX_W), row),
                  pl.BlockSpec((tm, DIFF_W), row),
                  pl.BlockSpec((NSA_Q, D_MODEL), full),
                  pl.BlockSpec((FOX_W, D_MODEL), full),
                  pl.BlockSpec((DIFF_W, D_MODEL), full)],
        out_specs=pl.BlockSpec((tm, D_MODEL), row),
        out_shape=jax.ShapeDtypeStruct((n, D_MODEL), F32),
        compiler_params=_cparams("parallel"),
        name="outproj",
    )(x2, oa, ob, oc, wa, wb, wc)


ROUTE_OFF = N_GROUPS


MOE_CHUNK = 304
MOE_PREFIX_BLOCK = 256


def _moe_kernel(x_ref, g_ref, wrh_ref, wrl_ref, br_ref, wg_ref, wu_ref, wd_ref, gf_ref, o_ref,
                h_ref, gsel_ref, col_ref, row_ref, cnt_ref, *, final_norm):
    grp_id = pl.program_id(1)
    tm = x_ref.shape[0]
    ch = MOE_CHUNK

    @pl.when(grp_id == 0)
    def _():
        x = x_ref[...]
        ms = jnp.mean(x * x, axis=-1, keepdims=True)
        y = x * lax.rsqrt(ms + RMS_EPS) * g_ref[...]
        hb, hl = _split2(y)
        h_ref[...] = hb
        lg = (_dot(hb, wrh_ref[...]) + _dot(hl, wrh_ref[...]) + _dot(hb, wrl_ref[...])) + br_ref[...]
        lane = lax.broadcasted_iota(jnp.int32, lg.shape, 1).astype(F32)
        big = float(LANES)
        isg = lane < N_GROUPS
        gmax = jnp.max(jnp.where(isg, lg, -jnp.inf), axis=-1, keepdims=True)
        grp = jnp.min(jnp.where(isg & (lg == gmax), lane, big), axis=-1, keepdims=True)
        gprob = 1.0 / jnp.sum(jnp.where(isg, jnp.exp(lg - gmax), 0.0), axis=-1, keepdims=True)
        lo_lane = ROUTE_OFF + grp * EXPERTS_PER_GROUP
        ing = (lane >= lo_lane) & (lane < lo_lane + EXPERTS_PER_GROUP)
        v1 = jnp.max(jnp.where(ing, lg, -jnp.inf), axis=-1, keepdims=True)
        i1 = jnp.min(jnp.where(ing & (lg == v1), lane, big), axis=-1, keepdims=True)
        rest = ing & (lane != i1)
        v2 = jnp.max(jnp.where(rest, lg, -jnp.inf), axis=-1, keepdims=True)
        i2 = jnp.min(jnp.where(rest & (lg == v2), lane, big), axis=-1, keepdims=True)
        e2 = jnp.exp(v2 - v1)
        w1 = gprob / (1.0 + e2)
        w2 = gprob * e2 / (1.0 + e2)
        gate = jnp.where(lane == i1, w1, 0.0) + jnp.where(lane == i2, w2, 0.0)
        gsel = jnp.zeros_like(gate)
        for gg in range(N_GROUPS):
            shifted = pltpu.roll(gate, LANES - (ROUTE_OFF + gg * EXPERTS_PER_GROUP), 1)
            gsel = jnp.where(grp == float(gg), shifted, gsel)
        gsel = jnp.where(lane < EXPERTS_PER_GROUP, gsel, 0.0)
        ghi, glo = _split2(gsel)
        gsel_ref[:, 0:LANES] = ghi
        gsel_ref[:, LANES:2 * LANES] = glo
        onehot = jnp.where(lane == grp, 1.0, 0.0)
        pb = MOE_PREFIX_BLOCK
        tri = jnp.where(lax.broadcasted_iota(jnp.int32, (pb, pb), 0) > lax.broadcasted_iota(jnp.int32, (pb, pb), 1),
                        1.0, 0.0).astype(BF16)
        carry = jnp.zeros((1, LANES), F32)
        lane_b = lax.broadcasted_iota(jnp.int32, (pb, LANES), 1)
        for blk in range(tm // pb):
            oh_b = onehot[blk * pb:(blk + 1) * pb, :]
            prefix = _dot(tri, oh_b.astype(BF16)) + carry
            pos = jnp.sum(oh_b * prefix, axis=-1, keepdims=True)
            col_ref[blk * pb:(blk + 1) * pb, :] = jnp.where(lane_b == 1, pos, 0.0)
            carry = carry + jnp.sum(oh_b, axis=0, keepdims=True)
        for gg in range(N_GROUPS):
            cnt_ref[gg] = carry[0, gg].astype(jnp.int32)
        colinfo = col_ref[...] + jnp.where(lane == 0.0, grp, 0.0)
        col_ref[...] = colinfo
        row_ref[...] = colinfo.T[0:8, :]
        o_ref[...] = x

    grp_f = grp_id.astype(F32)
    key_col = jnp.where(col_ref[:, 0:1] == grp_f, col_ref[:, 1:2], -1.0)
    key_row = jnp.where(row_ref[0:1, :] == grp_f, row_ref[1:2, :], -1.0)

    def chunk(c, carry):
        base = (c * ch).astype(F32)
        sub_iota = lax.broadcasted_iota(jnp.int32, (ch, tm), 0).astype(F32)
        gather = jnp.where(key_row - base == sub_iota, 1.0, 0.0).astype(BF16)
        xg = _dot(gather, h_ref[...]).astype(BF16)
        gparts = _dot(gather, gsel_ref[...])
        gates = gparts[:, 0:LANES] + gparts[:, LANES:2 * LANES]
        y = None
        for e in range(EXPERTS_PER_GROUP):
            a = _dot(xg, wg_ref[0, e])
            u = _dot(xg, wu_ref[0, e])
            act = a * _sigmoid(a) * u * gates[:, e:e + 1]
            part = _dot(act.astype(BF16), wd_ref[0, e])
            y = part if y is None else y + part
        chp = -(-ch // LANES) * LANES
        lane_iota = lax.broadcasted_iota(jnp.int32, (tm, chp), 1).astype(F32)
        scatter = jnp.where(key_col - base == lane_iota, 1.0, 0.0).astype(BF16)
        yb = y.astype(BF16)
        if chp > ch:
            yb = jnp.concatenate([yb, jnp.zeros((chp - ch, D_MODEL), BF16)], axis=0)
        o_ref[...] = o_ref[...] + _dot(scatter, yb)
        return carry

    lax.fori_loop(0, (cnt_ref[grp_id] + ch - 1) // ch, chunk, 0)

    @pl.when(grp_id == N_GROUPS - 1)
    def _():
        if final_norm:
            y = o_ref[...]
            ms = jnp.mean(y * y, axis=-1, keepdims=True)
            o_ref[...] = y * lax.rsqrt(ms + RMS_EPS) * gf_ref[...]


def _moe(x2, g, wrh, wrl, br, wg, wu, wd, gf, layer, final_norm, tm=1024):
    n = x2.shape[0]
    tm = min(tm, n)
    row = lambda i, e: (i, 0)
    full = lambda i, e: (0, 0)
    epg = EXPERTS_PER_GROUP
    wspec_in = pl.BlockSpec((1, epg, D_MODEL, D_EXPERT), lambda i, e: (layer, e, 0, 0))
    wspec_out = pl.BlockSpec((1, epg, D_EXPERT, D_MODEL), lambda i, e: (layer, e, 0, 0))
    return pl.pallas_call(
        functools.partial(_moe_kernel, final_norm=final_norm),
        grid=(n // tm, N_GROUPS),
        in_specs=[pl.BlockSpec((tm, D_MODEL), row, pipeline_mode=pl.Buffered(1)),
                  pl.BlockSpec((1, D_MODEL), full),
                  pl.BlockSpec((D_MODEL, LANES), full),
                  pl.BlockSpec((D_MODEL, LANES), full),
                  pl.BlockSpec((1, LANES), full),
                  wspec_in, wspec_in, wspec_out,
                  pl.BlockSpec((1, D_MODEL), full)],
        out_specs=pl.BlockSpec((tm, D_MODEL), row),
        out_shape=jax.ShapeDtypeStruct((n, D_MODEL), F32),
        scratch_shapes=[pltpu.VMEM((tm, D_MODEL), BF16),
                        pltpu.VMEM((tm, 2 * LANES), BF16),
                        pltpu.VMEM((tm, LANES), F32),
                        pltpu.VMEM((8, tm), F32),
                        pltpu.SMEM((N_GROUPS,), jnp.int32)],
        compiler_params=_cparams("parallel", "arbitrary"),
        name="moe",
    )(x2, g, wrh, wrl, br, wg, wu, wd, gf)


def _head_perm():
    hg = NSA_HEADS // NSA_KV_GROUPS
    order = []
    for j in range(hg):
        order += [j, j + hg]
    return np.concatenate([np.arange(h * HEAD_DIM, (h + 1) * HEAD_DIM) for h in order])


def _split_hi_lo(w):
    hi = w.astype(BF16)
    return hi, (w - hi.astype(F32)).astype(BF16)


def _layer_params(l, w_in, b_gate, b_fgt, cmp_pe, cmp_w1, cmp_w2, w_out):
    offs = np.concatenate([[0], np.cumsum(IN_WIDTHS)])
    seg = [w_in[l][:, offs[i]:offs[i + 1]] for i in range(len(IN_WIDTHS))]
    (wq, wkc, wvc, wks, wvs, wkw, wvw, wgt, wfq, wfk, wfv, wff, wdq, wdk, wdv) = seg
    perm = _head_perm()
    gate_perm = np.array([h * 3 + br for br in range(3) for h in range(NSA_HEADS)])
    wm = jnp.concatenate([wq[:, perm] * (HEAD_DIM ** -0.5 * LOG2E), wks, wvs, wkw, wvw,
                          wfq * (HEAD_DIM ** -0.5 * LOG2E), wfk, wfv,
                          wdq * (DIFF_QK_DIM ** -0.5 * LOG2E), wdk, wdv, wkc, wvc], axis=1).astype(BF16)
    pad = LANES - NSA_GATE - 3 * FOX_HEADS
    ws = jnp.concatenate([wgt[:, gate_perm], wff, wff, wff, jnp.zeros((D_MODEL, pad), F32)], axis=1)
    wsh, wsl = _split_hi_lo(ws)
    bs = jnp.concatenate([b_gate[l][gate_perm], b_fgt[l], b_fgt[l], b_fgt[l], jnp.zeros((pad,), F32)])[None, :]

    eye = jnp.eye(NSA_KV_GROUPS, dtype=F32)
    half = CMP_BLOCK // 2
    cmp = []
    for pe, w1, w2 in zip(cmp_pe, cmp_w1, cmp_w2):
        w1r = w1[l].reshape(2, half, HEAD_DIM, CMP_HIDDEN)
        w1x = jnp.einsum('srdk,gh->srgdhk', w1r, eye).reshape(2, half * NSA_KV, NSA_KV_GROUPS * CMP_HIDDEN)
        w2bd = jnp.einsum('kd,gh->gkhd', w2[l], eye).reshape(NSA_KV_GROUPS * CMP_HIDDEN, NSA_KV)
        pe2 = jnp.broadcast_to(pe[l].reshape(2, half, 1, HEAD_DIM), (2, half, NSA_KV_GROUPS, HEAD_DIM))
        cmp.append((pe2.reshape(2, half * NSA_KV), w1x[0].astype(BF16), w1x[1].astype(BF16), w2bd.astype(BF16)))

    wo = w_out[l]
    wa = wo[:NSA_Q][perm].astype(BF16)
    wb = wo[NSA_Q:NSA_Q + FOX_W].astype(BF16)
    wc = wo[NSA_Q + FOX_W:].astype(BF16)
    return dict(wm=wm, wsh=wsh, wsl=wsl, bs=bs, cmp_k=cmp[0], cmp_v=cmp[1], wa=wa, wb=wb, wc=wc)


def _overlap_matrix(n_slab, n_cmp, n_sel):
    c_start = np.arange(n_cmp) * CMP_STRIDE
    s_start = np.arange(n_sel) * SLC_BLOCK
    ov = np.clip(np.minimum(c_start[:, None] + CMP_BLOCK, s_start[None, :] + SLC_BLOCK)
                 - np.maximum(c_start[:, None], s_start[None, :]), 0, None) / CMP_BLOCK
    full = np.zeros((n_slab, LANES), np.float32)
    full[:n_cmp, :n_sel] = ov
    return jnp.asarray(full, dtype=BF16)


def kernel(x, norm_attn, w_in, b_gate, b_fgt, cmp_k_pe, cmp_k_w1, cmp_k_w2, cmp_v_pe, cmp_v_w1, cmp_v_w2, diff_lambda, diff_subln, w_out, norm_ffn, w_grp, b_grp, w_exp, b_exp, w_e_gate, w_e_up, w_e_down, norm_final):
    b, t, d = x.shape
    depth = w_in.shape[0]
    n = b * t
    n_slab = t // CMP_STRIDE
    n_cmp = (t - CMP_BLOCK) // CMP_STRIDE + 1
    n_sel = t // SLC_BLOCK
    top_n = min(SLC_TOPK, n_sel)
    ov = _overlap_matrix(n_slab, n_cmp, n_sel)

    wg_all = w_e_gate.astype(BF16)
    wu_all = w_e_up.astype(BF16)
    wd_all = w_e_down.astype(BF16)
    gf = norm_final[None, :]

    x2 = x.reshape(n, d)
    for l in range(depth):
        p = _layer_params(l, w_in, b_gate, b_fgt, (cmp_k_pe, cmp_v_pe), (cmp_k_w1, cmp_v_w1),
                          (cmp_k_w2, cmp_v_w2), w_out)
        main, kc, vc, small = _inproj(x2, norm_attn[l][None, :], p["wm"], p["wsh"], p["wsl"], p["bs"])
        main3 = main.reshape(b, t, MAIN_W)
        small3 = small.reshape(b, t, LANES)
        kcmp = _compress(kc.reshape(b, n_slab, CMP_STRIDE * NSA_KV), *p["cmp_k"])
        vcmp = _compress(vc.reshape(b, n_slab, CMP_STRIDE * NSA_KV), *p["cmp_v"])
        o_nsa, pen, used = _cmp_attn(main3, kcmp, vcmp, small3, ov, n_cmp, n_sel, top_n)
        nkt = t // min(TK, t)
        tile_used = used.reshape(b, t // TQ, -1, LANES).max(axis=2)[:, :, :nkt].astype(jnp.int32)
        o_nsa = _flash(main3, "slc", (pen, small3, o_nsa), BF16, tile_used=tile_used)
        o_nsa = _flash(main3, "win", (small3, o_nsa), BF16)
        o_fox = _flash(main3, "fox", (_cumgate(small3),), BF16)
        o_diff = _flash(main3, "diff", (diff_lambda[l], diff_subln[l][:, None]), BF16, lam_init=_lambda_init(l))

        x2 = _outproj(x2, o_nsa.reshape(n, NSA_Q), o_fox.reshape(n, FOX_W), o_diff.reshape(n, DIFF_W),
                      p["wa"], p["wb"], p["wc"])

        wr = jnp.concatenate([w_grp[l], w_exp[l], jnp.zeros((d, LANES - N_GROUPS - N_EXPERTS), F32)], axis=1)
        wrh, wrl = _split_hi_lo(wr)
        br = jnp.concatenate([b_grp[l], b_exp[l], jnp.zeros((LANES - N_GROUPS - N_EXPERTS,), F32)])[None, :]
        x2 = _moe(x2, norm_ffn[l][None, :], wrh, wrl, br, wg_all, wu_all, wd_all, gf, l,
                  final_norm=(l == depth - 1))
    return x2.reshape(b, t, d)
```

```python
import functools
import math

import ml_dtypes
import numpy as np
import jax
import jax.numpy as jnp
from jax import lax
from jax.experimental import pallas as pl
from jax.experimental.pallas import tpu as pltpu

F32 = jnp.float32
BF16 = jnp.bfloat16

D_MODEL = 1024
HEAD_DIM = 64
NSA_HEADS = 8
NSA_KV_GROUPS = 2
CMP_BLOCK = 32
CMP_STRIDE = 16
CMP_HIDDEN = 256
SLC_BLOCK = 64
SLC_SHIFT = 6
SLC_TOPK = 16
WINDOW = 512
FOX_HEADS = 4
DIFF_HEADS = 4
DIFF_QK_DIM = HEAD_DIM // 2
N_GROUPS = 4
EXPERTS_PER_GROUP = 8
N_EXPERTS = N_GROUPS * EXPERTS_PER_GROUP
D_EXPERT = 256
RMS_EPS = 1e-6
FORCE_SCORE = 1e4
NEG_INF = -1e30
MASKED = -2e30

NSA_Q = NSA_HEADS * HEAD_DIM
NSA_KV = NSA_KV_GROUPS * HEAD_DIM
NSA_GATE = NSA_HEADS * 3
FOX_W = FOX_HEADS * HEAD_DIM
DIFF_W = DIFF_HEADS * HEAD_DIM
MIX_WIDTH = NSA_Q + FOX_W + DIFF_W
IN_WIDTHS = (NSA_Q, NSA_KV, NSA_KV, NSA_KV, NSA_KV, NSA_KV, NSA_KV, NSA_GATE,
             FOX_W, FOX_W, FOX_W, FOX_HEADS, DIFF_W, DIFF_W, DIFF_W)

LANES = 128
SUBLANES = 8
VMEM_LIMIT_BYTES = 56 * 1024 * 1024

MAIN_W = NSA_Q + 4 * NSA_KV + 3 * FOX_W + 3 * DIFF_W
PROJ_W = MAIN_W + 2 * NSA_KV
COL_KS, COL_VS, COL_KW, COL_VW = 4, 5, 6, 7
COL_FQ, COL_FK, COL_FV = 4, 5, 6
COL_DQ, COL_DK, COL_DV = 7, 8, 9
FGT_LANES = (NSA_GATE, NSA_GATE + FOX_HEADS, NSA_GATE + 2 * FOX_HEADS)
LOG2E = 1.4426950408889634

TQ = 512
TK = 512
KSTRIP = 512
ONES_ROWS = 16

def _alibi(n):
    return [float(2.0 ** (-8.0 * (i + 1) / n)) for i in range(n)]


def _lambda_init(layer):
    return 0.8 - 0.6 * math.exp(-0.3 * layer)


def _cparams(*sem):
    return pltpu.CompilerParams(dimension_semantics=sem, vmem_limit_bytes=VMEM_LIMIT_BYTES)


def _dot(a, b):
    return jnp.dot(a, b, preferred_element_type=F32)


def _dot_nt(a, b):
    return lax.dot_general(a, b, (((1,), (1,)), ((), ())), preferred_element_type=F32)


def _split2(x):
    hi = x.astype(BF16)
    lo = (x - hi.astype(F32)).astype(BF16)
    return hi, lo


def _sigmoid(z):
    return 1.0 / (1.0 + jnp.exp(-z))


def _keep_lanes(sel, blk):
    return jnp.where(sel, blk.astype(F32), 0.0).astype(BF16)


def _inproj_kernel(x_ref, g_ref, wm_ref, wsh_ref, wsl_ref, bs_ref, main_ref, kc_ref, vc_ref, small_ref):
    x = x_ref[...]
    ms = jnp.mean(x * x, axis=-1, keepdims=True)
    y = x * lax.rsqrt(ms + RMS_EPS) * g_ref[...]
    hb, hl = _split2(y)
    full = _dot(hb, wm_ref[...]).astype(BF16)
    main_ref[...] = full[:, :MAIN_W]
    kc_ref[...] = full[:, MAIN_W:MAIN_W + NSA_KV]
    vc_ref[...] = full[:, MAIN_W + NSA_KV:]
    small_ref[...] = (_dot(hb, wsh_ref[...]) + _dot(hl, wsh_ref[...]) + _dot(hb, wsl_ref[...])) + bs_ref[...]


def _inproj(x2, g, wm, wsh, wsl, bs, tm=512):
    n = x2.shape[0]
    full = lambda i: (0, 0)
    return pl.pallas_call(
        _inproj_kernel,
        grid=(n // tm,),
        in_specs=[pl.BlockSpec((tm, D_MODEL), lambda i: (i, 0)),
                  pl.BlockSpec((1, D_MODEL), full),
                  pl.BlockSpec((D_MODEL, PROJ_W), full),
                  pl.BlockSpec((D_MODEL, LANES), full),
                  pl.BlockSpec((D_MODEL, LANES), full),
                  pl.BlockSpec((1, LANES), full)],
        out_specs=[pl.BlockSpec((tm, MAIN_W), lambda i: (i, 0)),
                   pl.BlockSpec((tm, NSA_KV), lambda i: (i, 0)),
                   pl.BlockSpec((tm, NSA_KV), lambda i: (i, 0)),
                   pl.BlockSpec((tm, LANES), lambda i: (i, 0))],
        out_shape=[jax.ShapeDtypeStruct((n, MAIN_W), BF16),
                   jax.ShapeDtypeStruct((n, NSA_KV), BF16),
                   jax.ShapeDtypeStruct((n, NSA_KV), BF16),
                   jax.ShapeDtypeStruct((n, LANES), F32)],
        compiler_params=_cparams("parallel"),
        name="inproj",
    )(x2, g, wm, wsh, wsl, bs)


def _compress_kernel(r_ref, pe_ref, w1a_ref, w1b_ref, w2_ref, o_ref):
    r = r_ref[0].astype(F32)
    pe = pe_ref[...]
    ra = (r + pe[0:1, :]).astype(BF16)
    rb = (r + pe[1:2, :]).astype(BF16)
    a = _dot(ra, w1a_ref[...])
    b = _dot(rb, w1b_ref[...])
    n_slab = a.shape[0]
    hid = a + pltpu.roll(b, n_slab - 1, 0)
    hid = jax.nn.gelu(hid)
    o_ref[0] = _dot(hid.astype(BF16), w2_ref[...]).astype(BF16)


def _compress(r, pe2, w1a, w1b, w2bd):
    b, n_slab, w = r.shape
    full = lambda i: (0, 0)
    return pl.pallas_call(
        _compress_kernel,
        grid=(b,),
        in_specs=[pl.BlockSpec((1, n_slab, w), lambda i: (i, 0, 0)),
                  pl.BlockSpec((2, w), full),
                  pl.BlockSpec((w, 2 * CMP_HIDDEN), full),
                  pl.BlockSpec((w, 2 * CMP_HIDDEN), full),
                  pl.BlockSpec((2 * CMP_HIDDEN, LANES), full)],
        out_specs=pl.BlockSpec((1, n_slab, LANES), lambda i: (i, 0, 0)),
        out_shape=jax.ShapeDtypeStruct((b, n_slab, LANES), BF16),
        compiler_params=_cparams("parallel"),
        name="compress",
    )(r, pe2, w1a, w1b, w2bd)


def _cumgate_kernel(z_ref, o_ref, carry_ref, *, tc):
    @pl.when(pl.program_id(1) == 0)
    def _():
        carry_ref[...] = jnp.zeros_like(carry_ref)

    z = z_ref[0]
    logf = -(jnp.maximum(-z, 0.0) + jnp.log(1.0 + jnp.exp(-jnp.abs(z))))
    hi = logf.astype(BF16)
    r1 = logf - hi.astype(F32)
    mid = r1.astype(BF16)
    lo = (r1 - mid.astype(F32)).astype(BF16)
    tri = (lax.broadcasted_iota(jnp.int32, (tc, tc), 0) >= lax.broadcasted_iota(jnp.int32, (tc, tc), 1))
    tri = jnp.where(tri, 1.0, 0.0).astype(BF16)
    c = (_dot(tri, hi) + _dot(tri, mid)) + _dot(tri, lo) + carry_ref[0:1, :]
    carry_ref[...] = jnp.broadcast_to(c[tc - 1:tc, :], carry_ref.shape)
    v = c * (-LOG2E)
    p0 = v.astype(BF16)
    r1 = v - p0.astype(F32)
    p1 = r1.astype(BF16)
    p2 = (r1 - p1.astype(F32)).astype(BF16)
    lane = lax.broadcasted_iota(jnp.int32, v.shape, 1)
    pieces = jnp.where(lane < FGT_LANES[1], p0.astype(F32), jnp.where(lane < FGT_LANES[2], p1.astype(F32), p2.astype(F32)))
    keep = (lane >= FGT_LANES[0]) & (lane < FGT_LANES[2] + FOX_HEADS)
    o_ref[0] = jnp.where(keep, pieces, 0.0).astype(BF16)


def _cumgate(small3, tc=512):
    b, t, w = small3.shape
    return pl.pallas_call(
        functools.partial(_cumgate_kernel, tc=tc),
        grid=(b, t // tc),
        in_specs=[pl.BlockSpec((1, tc, w), lambda i, j: (i, j, 0))],
        out_specs=pl.BlockSpec((1, tc, w), lambda i, j: (i, j, 0)),
        out_shape=jax.ShapeDtypeStruct((b, t, w), BF16),
        scratch_shapes=[pltpu.VMEM((8, w), F32)],
        compiler_params=_cparams("parallel", "arbitrary"),
        name="cumgate",
    )(small3)


def _cmp_kernel(q_ref, kc_ref, vc_ref, sm_ref, ov_ref, o_ref, pen_ref, used_ref, *, tq, n_cmp, n_sel, top_n):
    t0 = pl.program_id(1) * tq
    nck = kc_ref.shape[1]
    row = t0 + lax.broadcasted_iota(jnp.int32, (tq, nck), 0)
    col = lax.broadcasted_iota(jnp.int32, (tq, nck), 1)
    dist = row - (col * CMP_STRIDE + (CMP_BLOCK - 1))
    mask = (dist >= 0) & (col < n_cmp)
    distf = dist.astype(F32)
    lane = lax.broadcasted_iota(jnp.int32, (tq, LANES), 1)
    low = lane < HEAD_DIM
    kc = kc_ref[0]
    vc = vc_ref[0]
    sig = _sigmoid(sm_ref[0])
    slopes = _alibi(NSA_HEADS)
    psum = [None, None]
    hg = NSA_HEADS // NSA_KV_GROUPS
    for j in range(hg):
        qblk = q_ref[0, :, j * LANES:(j + 1) * LANES]
        outs = []
        for g in range(NSA_KV_GROUPS):
            h = j + hg * g
            qm = _keep_lanes(low if g == 0 else ~low, qblk)
            s = _dot_nt(qm, kc) - (slopes[h] * LOG2E) * distf
            s = jnp.where(mask, s, NEG_INF)
            m = jnp.max(s, axis=-1, keepdims=True)
            p = jnp.where(mask, jnp.exp2(s - m), 0.0)
            l = jnp.sum(p, axis=-1, keepdims=True)
            p = p * jnp.where(l > 0.0, 1.0 / l, 0.0)
            psum[g] = p if psum[g] is None else psum[g] + p
            outs.append(_dot(p.astype(BF16), vc))
        gate = jnp.where(low, sig[:, j:j + 1], sig[:, j + hg:j + hg + 1])
        o_ref[0, :, j * LANES:(j + 1) * LANES] = (jnp.where(low, outs[0], outs[1]) * gate).astype(o_ref.dtype)

    sub8 = lax.broadcasted_iota(jnp.int32, (SUBLANES, tq), 0)
    cur_r = (t0 + lax.broadcasted_iota(jnp.int32, (tq, LANES), 0)) >> SLC_SHIFT
    pen_t = []
    for g in range(NSA_KV_GROUPS):
        ph, plo = _split2(psum[g])
        imp = _dot(ph, ov_ref[...]) + _dot(plo, ov_ref[...])
        valid = lane <= cur_r
        forced = (lane == 0) | (lane == cur_r) | (lane == cur_r - 1)
        imp = jnp.where(valid, jnp.where(forced, FORCE_SCORE, imp), NEG_INF)
        imp_t = imp.T[:n_sel, :]
        ngrp = n_sel // SUBLANES
        rows = [imp_t[i * SUBLANES:(i + 1) * SUBLANES, :] for i in range(ngrp)]
        cnt = [jnp.zeros((SUBLANES, tq), F32) for _ in range(ngrp)]
        for k in range(n_sel):
            rk = imp_t[k:k + 1, :]
            for i in range(ngrp):
                if i * SUBLANES > k:
                    one = jnp.where(rk >= rows[i], 1.0, 0.0)
                elif (i + 1) * SUBLANES <= k:
                    one = jnp.where(rk > rows[i], 1.0, 0.0)
                else:
                    one = jnp.where(sub8 > k - i * SUBLANES,
                                    jnp.where(rk >= rows[i], 1.0, 0.0), jnp.where(rk > rows[i], 1.0, 0.0))
                cnt[i] = cnt[i] + one
        pen_t.append(jnp.where(jnp.concatenate(cnt, axis=0) < float(top_n), 0.0, MASKED))
    for g in range(NSA_KV_GROUPS):
        full = jnp.concatenate([pen_t[g], jnp.zeros((LANES - n_sel, tq), F32)], axis=0)
        pen_ref[0, :, g * LANES:(g + 1) * LANES] = full.T.astype(BF16)
    picked = jnp.where(jnp.maximum(pen_t[0], pen_t[1]) == 0.0, 1.0, 0.0)
    bpt = TK // SLC_BLOCK
    lane1 = lax.broadcasted_iota(jnp.int32, (1, LANES), 1)
    used = jnp.zeros((1, LANES), F32)
    for kt in range(n_sel // bpt):
        hit = jnp.max(picked[kt * bpt:(kt + 1) * bpt, :], axis=(0, 1), keepdims=True)
        used = jnp.where(lane1 == kt, hit, used)
    used_ref[0, 0] = used


def _cmp_attn(main3, kcmp, vcmp, small3, ov, n_cmp, n_sel, top_n, tq=256):
    b, t, _ = main3.shape
    nck = kcmp.shape[1]
    return pl.pallas_call(
        functools.partial(_cmp_kernel, tq=tq, n_cmp=n_cmp, n_sel=n_sel, top_n=top_n),
        grid=(b, t // tq),
        in_specs=[pl.BlockSpec((1, tq, NSA_Q), lambda i, j: (i, j, 0)),
                  pl.BlockSpec((1, nck, LANES), lambda i, j: (i, 0, 0)),
                  pl.BlockSpec((1, nck, LANES), lambda i, j: (i, 0, 0)),
                  pl.BlockSpec((1, tq, LANES), lambda i, j: (i, j, 0)),
                  pl.BlockSpec((nck, LANES), lambda i, j: (0, 0))],
        out_specs=[pl.BlockSpec((1, tq, NSA_Q), lambda i, j: (i, j, 0)),
                   pl.BlockSpec((1, tq, NSA_KV_GROUPS * LANES), lambda i, j: (i, j, 0)),
                   pl.BlockSpec((1, 1, 1, LANES), lambda i, j: (i, j, 0, 0))],
        out_shape=[jax.ShapeDtypeStruct((b, t, NSA_Q), BF16),
                   jax.ShapeDtypeStruct((b, t, NSA_KV_GROUPS * LANES), BF16),
                   jax.ShapeDtypeStruct((b, t // tq, 1, LANES), F32)],
        compiler_params=_cparams("parallel", "parallel"),
        name="cmp_attn",
    )(main3, kcmp, vcmp, small3, ov)


def _bf16_terms(c, n=3):
    out, r = [], float(c)
    for _ in range(n):
        p = float(np.float32(r).astype(ml_dtypes.bfloat16))
        out.append(p)
        r -= p
    return out


def _alibi_q_aug(lane, a0, slope):
    aug = jnp.zeros(lane.shape, F32)
    for i, c in enumerate(_bf16_terms(slope * LOG2E)):
        aug = jnp.where(lane == a0 + 2 * i, float(SLC_BLOCK) * c, aug)
        aug = jnp.where(lane == a0 + 2 * i + 1, c, aug)
    return aug


def _alibi_k_aug(lane, rel, a0):
    hi = (rel >> SLC_SHIFT).astype(F32)
    lo = (rel & (SLC_BLOCK - 1)).astype(F32)
    inside = (lane >= a0) & (lane < a0 + 6)
    odd = ((lane - a0) & 1) == 1
    return jnp.where(inside, jnp.where(odd, lo, hi), 0.0)


ALIBI_LANE = {"slc": SLC_BLOCK, "win": 0, "diff": 0}
STACKS = {"slc": (1, 8), "win": (1, 8), "fox": (2, 2), "diff": (2, 4)}


def _v_half(mode, vh):
    nv = STACKS[mode][1]
    if mode in ("slc", "win"):
        return vh // (NSA_HEADS // NSA_KV_GROUPS)
    return (vh % nv) // (nv // 2)


def _flash_kernel(*refs, mode, tq, tk, lam_init, ntiles):
    sched_ref, refs = refs[0], refs[1:]
    if mode == "slc":
        use_ref, q_ref, k_ref, v_ref, pen_ref, sm_ref, add_ref, o_ref, qst_ref, m_ref, acc_ref = refs
    elif mode == "win":
        q_ref, k_ref, v_ref, sm_ref, add_ref, o_ref, qst_ref, m_ref, acc_ref = refs
    elif mode == "fox":
        q_ref, k_ref, v_ref, fa_ref, o_ref, qst_ref, m_ref, acc_ref = refs
    else:
        q_ref, k_ref, v_ref, lam_ref, sub_ref, o_ref, qst_ref, m_ref, acc_ref = refs
    nstack, nv = STACKS[mode]
    nsa = mode in ("slc", "win")
    hg = NSA_HEADS // NSA_KV_GROUPS
    step_id = pl.program_id(1)
    q0 = sched_ref[SCHED_Q, step_id] * tq
    k0 = sched_ref[SCHED_K, step_id] * tk

    @pl.when(sched_ref[SCHED_FIRST, step_id] == 1)
    def _():
        m_ref[...] = jnp.full(m_ref.shape, NEG_INF, F32)
        acc_ref[...] = jnp.zeros(acc_ref.shape, F32)
        lane = lax.broadcasted_iota(jnp.int32, (tq, LANES), 1)
        for vh in range(nstack * nv):
            if nsa:
                jb, g = vh % hg, vh // hg
                lo_lane, width = g * HEAD_DIM, HEAD_DIM
                aug = _alibi_q_aug(lane, ALIBI_LANE[mode], _alibi(NSA_HEADS)[vh])
                if mode == "slc":
                    aug = aug + pen_ref[0, :, g * LANES:(g + 1) * LANES].astype(F32)
            elif mode == "fox":
                jb, r = vh // nv, vh % nv
                lo_lane, width = r * HEAD_DIM, HEAD_DIM
                hit = (lane == FGT_LANES[0] + vh) | (lane == FGT_LANES[1] + vh) | (lane == FGT_LANES[2] + vh)
                aug = jnp.where(hit, 1.0, 0.0)
            else:
                jb, r = vh // nv, vh % nv
                lo_lane, width = r * DIFF_QK_DIM, DIFF_QK_DIM
                aug = _alibi_q_aug(lane, ALIBI_LANE[mode], _alibi(DIFF_HEADS)[vh // 2])
            sel = (lane >= lo_lane) & (lane < lo_lane + width)
            qst_ref[vh * tq:(vh + 1) * tq, 0:LANES] = _keep_lanes(sel, q_ref[0, :, jb * LANES:(jb + 1) * LANES])
            qst_ref[vh * tq:(vh + 1) * tq, LANES:2 * LANES] = aug.astype(BF16)

    def step(masked):
        lane_k = lax.broadcasted_iota(jnp.int32, (tk, LANES), 1)
        row_k = lax.broadcasted_iota(jnp.int32, (tk, LANES), 0)
        if mode == "fox":
            k_aug = fa_ref[0]
        else:
            ka = _alibi_k_aug(lane_k, (k0 - q0) + row_k, ALIBI_LANE[mode])
            if mode == "slc":
                ka = ka + jnp.where(((k0 + row_k) >> SLC_SHIFT) == lane_k, 1.0, 0.0)
            k_aug = ka.astype(BF16)
        if masked:
            dist = (q0 - k0) + lax.broadcasted_iota(jnp.int32, (tk, tq), 1) - lax.broadcasted_iota(jnp.int32, (tk, tq), 0)
            ok = dist >= 0
            if mode == "win":
                ok = ok & (dist < WINDOW)
            bias = jnp.where(ok, 0.0, MASKED)
        keys, vals_t = [], []
        for st in range(nstack):
            kblk = k_ref[0] if nsa else k_ref[0, :, st * LANES:(st + 1) * LANES]
            vblk = v_ref[0] if nsa else v_ref[0, :, st * LANES:(st + 1) * LANES]
            keys.append(jnp.concatenate([kblk, k_aug], axis=1))
            v_t = vblk.astype(F32).T.astype(BF16)
            ones = jnp.ones((ONES_ROWS, tk), BF16)
            vals_t.append([jnp.concatenate([v_t[hf * HEAD_DIM:(hf + 1) * HEAD_DIM, :], ones], axis=0)
                           for hf in range(LANES // HEAD_DIM)])

        nstrip = tk // KSTRIP
        nvh = nstack * nv

        def scores(vh):
            s = _dot_nt(keys[vh // nv], qst_ref[vh * tq:(vh + 1) * tq, :])
            if masked:
                s = s + bias
            return s, jnp.max(s, axis=0, keepdims=True)

        cur, cur_max = scores(0)
        for vh in range(nvh):
            cols = slice(vh * tq, (vh + 1) * tq)
            v_half = vals_t[vh // nv][_v_half(mode, vh)]
            m_old = m_ref[:, cols]
            m_new = jnp.maximum(m_old, cur_max)
            alpha = jnp.exp2(m_old - m_new)
            if vh + 1 < nvh:
                nxt, nxt_max = scores(vh + 1)
            pv = None
            for r in range(nstrip):
                p = jnp.exp2((cur[r * KSTRIP:(r + 1) * KSTRIP, :] - m_new).astype(BF16))
                part = _dot(v_half[:, r * KSTRIP:(r + 1) * KSTRIP], p)
                pv = part if pv is None else pv + part
            m_ref[:, cols] = m_new
            acc_ref[:, cols] = alpha * acc_ref[:, cols] + pv
            if vh + 1 < nvh:
                cur, cur_max = nxt, nxt_max

    is_masked = sched_ref[SCHED_MASKED, step_id] == 1
    if mode == "win":
        pl.when(is_masked)(lambda: step(True))
    elif mode == "slc":
        tile_id = (pl.program_id(0) * ntiles[0] + sched_ref[SCHED_Q, step_id]) * ntiles[1] + sched_ref[SCHED_K, step_id]
        needed = use_ref[tile_id] == 1
        pl.when(is_masked & needed)(lambda: step(True))
        pl.when(jnp.logical_not(is_masked) & needed)(lambda: step(False))
    else:
        pl.when(is_masked)(lambda: step(True))
        pl.when(jnp.logical_not(is_masked))(lambda: step(False))

    @pl.when(sched_ref[SCHED_LAST, step_id] == 1)
    def _():
        inv = 1.0 / acc_ref[HEAD_DIM:HEAD_DIM + 1, :]

        def out_t(vh):
            return acc_ref[0:HEAD_DIM, vh * tq:(vh + 1) * tq] * inv[:, vh * tq:(vh + 1) * tq]

        def pair(a, b):
            return jnp.concatenate([a, b], axis=0).T

        if nsa:
            sig_t = _sigmoid(sm_ref[0]).T
            br = 1 if mode == "slc" else 2
            for jb in range(hg):
                c0 = br * NSA_HEADS + jb
                o = pair(out_t(jb) * sig_t[c0:c0 + 1, :], out_t(jb + hg) * sig_t[c0 + hg:c0 + hg + 1, :])
                o_ref[0, :, jb * LANES:(jb + 1) * LANES] = (
                    add_ref[0, :, jb * LANES:(jb + 1) * LANES].astype(F32) + o).astype(o_ref.dtype)
        elif mode == "fox":
            for pb in range(nstack):
                o = pair(out_t(2 * pb), out_t(2 * pb + 1))
                o_ref[0, :, pb * LANES:(pb + 1) * LANES] = o.astype(o_ref.dtype)
        else:
            lam = lam_ref[...]
            lam_full = (jnp.exp(jnp.sum(lam[0:1, :] * lam[1:2, :], axis=-1, keepdims=True))
                        - jnp.exp(jnp.sum(lam[2:3, :] * lam[3:4, :], axis=-1, keepdims=True)) + lam_init)
            for pb in range(nstack):
                normed = []
                for hh in range(2):
                    d = out_t(pb * nv + 2 * hh) - lam_full * out_t(pb * nv + 2 * hh + 1)
                    ms = jnp.mean(d * d, axis=0, keepdims=True)
                    normed.append(d * lax.rsqrt(ms + RMS_EPS) * sub_ref[...] * (1.0 - lam_init))
                o_ref[0, :, pb * LANES:(pb + 1) * LANES] = pair(normed[0], normed[1]).astype(o_ref.dtype)


def _flash(main3, mode, extra, out_dtype, lam_init=0.0, tile_used=None, tq=TQ, tk=TK):
    b, t, _ = main3.shape
    tk = min(tk, t)
    nstack, nv = STACKS[mode]
    sched = _flash_schedule(t, tq, tk, mode == "win")
    prefetch = [jnp.asarray(sched)]
    if mode == "slc":
        prefetch.append(tile_used.reshape(-1))
    qtile = lambda w, c: pl.BlockSpec((1, tq, w), lambda bi, s, sch, *_: (bi, sch[SCHED_Q, s], c))
    ktile = lambda w, c: pl.BlockSpec((1, tk, w), lambda bi, s, sch, *_: (bi, sch[SCHED_K, s], c))
    if mode in ("slc", "win"):
        kcol, vcol = (COL_KS, COL_VS) if mode == "slc" else (COL_KW, COL_VW)
        in_specs = [qtile(NSA_Q, 0), ktile(LANES, kcol), ktile(LANES, vcol)]
        if mode == "slc":
            in_specs.append(qtile(NSA_KV_GROUPS * LANES, 0))
        in_specs += [qtile(LANES, 0), qtile(NSA_Q, 0)]
        out_w = NSA_Q
    else:
        qc, kc, vc = (COL_FQ, COL_FK, COL_FV) if mode == "fox" else (COL_DQ, COL_DK, COL_DV)
        in_specs = [qtile(2 * LANES, qc), ktile(2 * LANES, kc), ktile(2 * LANES, vc)]
        if mode == "fox":
            in_specs.append(ktile(LANES, 0))
        else:
            in_specs += [pl.BlockSpec((4, DIFF_QK_DIM), lambda bi, s, sch, *_: (0, 0)),
                         pl.BlockSpec((HEAD_DIM, 1), lambda bi, s, sch, *_: (0, 0))]
        out_w = 2 * LANES
    cols = nstack * nv * tq
    return pl.pallas_call(
        functools.partial(_flash_kernel, mode=mode, tq=tq, tk=tk, lam_init=lam_init, ntiles=(t // tq, t // tk)),
        grid_spec=pltpu.PrefetchScalarGridSpec(
            num_scalar_prefetch=len(prefetch),
            grid=(b, sched.shape[1]),
            in_specs=in_specs,
            out_specs=qtile(out_w, 0),
            scratch_shapes=[pltpu.VMEM((cols, 2 * LANES), BF16),
                            pltpu.VMEM((1, cols), F32),
                            pltpu.VMEM((HEAD_DIM + ONES_ROWS, cols), F32)]),
        out_shape=jax.ShapeDtypeStruct((b, t, out_w), out_dtype),
        compiler_params=_cparams("parallel", "arbitrary"),
        name="flash_" + mode,
    )(*prefetch, main3, main3, main3, *extra)


SCHED_Q, SCHED_K, SCHED_FIRST, SCHED_LAST, SCHED_MASKED = range(5)


def _flash_schedule(t, tq, tk, window):
    rows = []
    for i in range(t // tq):
        q_lo, q_hi = i * tq, i * tq + tq - 1
        k_first = max(q_lo - (WINDOW - 1), 0) // tk if window else 0
        tiles = list(range(k_first, q_hi // tk + 1))
        for n, kt in enumerate(tiles):
            fully_visible = kt * tk + tk - 1 <= q_lo and not window
            rows.append((i, kt, int(n == 0), int(n == len(tiles) - 1), int(not fully_visible)))
    return np.asarray(rows, np.int32).T


def _outproj_kernel(x_ref, oa_ref, ob_ref, oc_ref, wa_ref, wb_ref, wc_ref, o_ref):
    acc = _dot(oa_ref[...], wa_ref[...])
    acc = acc + _dot(ob_ref[...], wb_ref[...])
    acc = acc + _dot(oc_ref[...], wc_ref[...])
    o_ref[...] = x_ref[...] + acc


def _outproj(x2, oa, ob, oc, wa, wb, wc, tm=512):
    n = x2.shape[0]
    row = lambda i: (i, 0)
    full = lambda i: (0, 0)
    return pl.pallas_call(
        _outproj_kernel,
        grid=(n // tm,),
        in_specs=[pl.BlockSpec((tm, D_MODEL), row),
                  pl.BlockSpec((tm, NSA_Q), row),
                  pl.BlockSpec((tm, FOX_W), row),
                  pl.BlockSpec((tm, DIFF_W), row),
                  pl.BlockSpec((NSA_Q, D_MODEL), full),
                  pl.BlockSpec((FOX_W, D_MODEL), full),
                  pl.BlockSpec((DIFF_W, D_MODEL), full)],
        out_specs=pl.BlockSpec((tm, D_MODEL), row),
        out_shape=jax.ShapeDtypeStruct((n, D_MODEL), F32),
        compiler_params=_cparams("parallel"),
        name="outproj",
    )(x2, oa, ob, oc, wa, wb, wc)


ROUTE_OFF = SUBLANES
COL_GRP, COL_POS = EXPERTS_PER_GROUP, EXPERTS_PER_GROUP + 1


MOE_CHUNK = 304
MOE_PREFIX_BLOCK = 256


def _moe_kernel(x_ref, g_ref, wrh_ref, wrl_ref, br_ref, wg_ref, wu_ref, wd_ref, gf_ref, o_ref,
                h_ref, gsel_ref, col_ref, row_ref, cnt_ref, *, final_norm):
    grp_id = pl.program_id(1)
    tm = x_ref.shape[0]
    ch = MOE_CHUNK

    @pl.when(grp_id == 0)
    def _():
        x = x_ref[...]
        ms = jnp.mean(x * x, axis=-1, keepdims=True)
        y = x * lax.rsqrt(ms + RMS_EPS) * g_ref[...]
        hb, hl = _split2(y)
        h_ref[...] = hb
        lg = (_dot(hb, wrh_ref[...]) + _dot(hl, wrh_ref[...]) + _dot(hb, wrl_ref[...])) + br_ref[...]
        nrow = ROUTE_OFF + N_EXPERTS
        lt = lg.T[0:nrow, :]
        r = lax.broadcasted_iota(jnp.int32, (nrow, tm), 0).astype(F32)
        big = float(LANES)
        isg = r < N_GROUPS
        gmax = jnp.max(jnp.where(isg, lt, -jnp.inf), axis=0, keepdims=True)
        grp = jnp.min(jnp.where(isg & (lt == gmax), r, big), axis=0, keepdims=True)
        gprob = 1.0 / jnp.sum(jnp.where(isg, jnp.exp(lt - gmax), 0.0), axis=0, keepdims=True)
        lo_row = ROUTE_OFF + grp * EXPERTS_PER_GROUP
        ing = (r >= lo_row) & (r < lo_row + EXPERTS_PER_GROUP)
        v1 = jnp.max(jnp.where(ing, lt, -jnp.inf), axis=0, keepdims=True)
        i1 = jnp.min(jnp.where(ing & (lt == v1), r, big), axis=0, keepdims=True)
        rest = ing & (r != i1)
        v2 = jnp.max(jnp.where(rest, lt, -jnp.inf), axis=0, keepdims=True)
        i2 = jnp.min(jnp.where(rest & (lt == v2), r, big), axis=0, keepdims=True)
        e2 = jnp.exp(v2 - v1)
        w1 = gprob / (1.0 + e2)
        w2 = gprob * e2 / (1.0 + e2)
        gate_t = jnp.where(r == i1, w1, 0.0) + jnp.where(r == i2, w2, 0.0)
        epg = EXPERTS_PER_GROUP
        gsel_t = jnp.zeros((epg, tm), F32)
        for gg in range(N_GROUPS):
            gsel_t = jnp.where(grp == float(gg), gate_t[ROUTE_OFF + gg * epg:ROUTE_OFF + (gg + 1) * epg, :], gsel_t)
        r8 = lax.broadcasted_iota(jnp.int32, (SUBLANES, tm), 0).astype(F32)
        onehot_t = jnp.where(r8 == grp, 1.0, 0.0)
        pb = MOE_PREFIX_BLOCK
        tri = jnp.where(lax.broadcasted_iota(jnp.int32, (pb, pb), 0) < lax.broadcasted_iota(jnp.int32, (pb, pb), 1),
                        1.0, 0.0).astype(BF16)
        carry = jnp.zeros((SUBLANES, 1), F32)
        pos_parts = []
        for blk in range(tm // pb):
            oh_b = onehot_t[:, blk * pb:(blk + 1) * pb]
            prefix = _dot(oh_b.astype(BF16), tri) + carry
            pos_parts.append(jnp.sum(oh_b * prefix, axis=0, keepdims=True))
            carry = carry + jnp.sum(oh_b, axis=1, keepdims=True)
        pos = jnp.concatenate(pos_parts, axis=1)
        for gg in range(N_GROUPS):
            cnt_ref[gg] = carry[gg, 0].astype(jnp.int32)
        row_ref[...] = jnp.concatenate([grp, pos, jnp.zeros((SUBLANES - 2, tm), F32)], axis=0)
        stacked = jnp.concatenate([gsel_t, row_ref[...], jnp.zeros((LANES - 2 * SUBLANES, tm), F32)], axis=0)
        colinfo = stacked.T
        lane = lax.broadcasted_iota(jnp.int32, (tm, LANES), 1)
        ghi, glo = _split2(jnp.where(lane < epg, colinfo, 0.0))
        gsel_ref[:, 0:LANES] = ghi
        gsel_ref[:, LANES:2 * LANES] = glo
        col_ref[...] = colinfo
        o_ref[...] = x

    grp_f = grp_id.astype(F32)
    key_col = jnp.where(col_ref[:, COL_GRP:COL_GRP + 1] == grp_f, col_ref[:, COL_POS:COL_POS + 1], -1.0)
    key_row = jnp.where(row_ref[0:1, :] == grp_f, row_ref[1:2, :], -1.0)

    def chunk(c, carry):
        base = (c * ch).astype(F32)
        sub_iota = lax.broadcasted_iota(jnp.int32, (ch, tm), 0).astype(F32)
        gather = jnp.where(key_row - base == sub_iota, 1.0, 0.0).astype(BF16)
        xg = _dot(gather, h_ref[...]).astype(BF16)
        gparts = _dot(gather, gsel_ref[...])
        gates = gparts[:, 0:LANES] + gparts[:, LANES:2 * LANES]
        y = None
        for e in range(EXPERTS_PER_GROUP):
            a = _dot(xg, wg_ref[0, e])
            u = _dot(xg, wu_ref[0, e])
            act = a * _sigmoid(a) * u * gates[:, e:e + 1]
            part = _dot(act.astype(BF16), wd_ref[0, e])
            y = part if y is None else y + part
        chp = -(-ch // LANES) * LANES
        lane_iota = lax.broadcasted_iota(jnp.int32, (tm, chp), 1).astype(F32)
        scatter = jnp.where(key_col - base == lane_iota, 1.0, 0.0).astype(BF16)
        yb = y.astype(BF16)
        if chp > ch:
            yb = jnp.concatenate([yb, jnp.zeros((chp - ch, D_MODEL), BF16)], axis=0)
        o_ref[...] = o_ref[...] + _dot(scatter, yb)
        return carry

    lax.fori_loop(0, (cnt_ref[grp_id] + ch - 1) // ch, chunk, 0)

    @pl.when(grp_id == N_GROUPS - 1)
    def _():
        if final_norm:
            y = o_ref[...]
            ms = jnp.mean(y * y, axis=-1, keepdims=True)
            o_ref[...] = y * lax.rsqrt(ms + RMS_EPS) * gf_ref[...]


def _moe(x2, g, wrh, wrl, br, wg, wu, wd, gf, layer, final_norm, tm=1024):
    n = x2.shape[0]
    tm = min(tm, n)
    row = lambda i, e: (i, 0)
    full = lambda i, e: (0, 0)
    epg = EXPERTS_PER_GROUP
    wspec_in = pl.BlockSpec((1, epg, D_MODEL, D_EXPERT), lambda i, e: (layer, e, 0, 0))
    wspec_out = pl.BlockSpec((1, epg, D_EXPERT, D_MODEL), lambda i, e: (layer, e, 0, 0))
    return pl.pallas_call(
        functools.partial(_moe_kernel, final_norm=final_norm),
        grid=(n // tm, N_GROUPS),
        in_specs=[pl.BlockSpec((tm, D_MODEL), row, pipeline_mode=pl.Buffered(1)),
                  pl.BlockSpec((1, D_MODEL), full),
                  pl.BlockSpec((D_MODEL, LANES), full),
                  pl.BlockSpec((D_MODEL, LANES), full),
                  pl.BlockSpec((1, LANES), full),
                  wspec_in, wspec_in, wspec_out,
                  pl.BlockSpec((1, D_MODEL), full)],
        out_specs=pl.BlockSpec((tm, D_MODEL), row),
        out_shape=jax.ShapeDtypeStruct((n, D_MODEL), F32),
        scratch_shapes=[pltpu.VMEM((tm, D_MODEL), BF16),
                        pltpu.VMEM((tm, 2 * LANES), BF16),
                        pltpu.VMEM((tm, LANES), F32),
                        pltpu.VMEM((8, tm), F32),
                        pltpu.SMEM((N_GROUPS,), jnp.int32)],
        compiler_params=_cparams("parallel", "arbitrary"),
        name="moe",
    )(x2, g, wrh, wrl, br, wg, wu, wd, gf)


def _head_perm():
    hg = NSA_HEADS // NSA_KV_GROUPS
    order = []
    for j in range(hg):
        order += [j, j + hg]
    return np.concatenate([np.arange(h * HEAD_DIM, (h + 1) * HEAD_DIM) for h in order])


def _split_hi_lo(w):
    hi = w.astype(BF16)
    return hi, (w - hi.astype(F32)).astype(BF16)


def _layer_params(l, w_in, b_gate, b_fgt, cmp_pe, cmp_w1, cmp_w2, w_out):
    offs = np.concatenate([[0], np.cumsum(IN_WIDTHS)])
    seg = [w_in[l][:, offs[i]:offs[i + 1]] for i in range(len(IN_WIDTHS))]
    (wq, wkc, wvc, wks, wvs, wkw, wvw, wgt, wfq, wfk, wfv, wff, wdq, wdk, wdv) = seg
    perm = _head_perm()
    gate_perm = np.array([h * 3 + br for br in range(3) for h in range(NSA_HEADS)])
    wm = jnp.concatenate([wq[:, perm] * (HEAD_DIM ** -0.5 * LOG2E), wks, wvs, wkw, wvw,
                          wfq * (HEAD_DIM ** -0.5 * LOG2E), wfk, wfv,
                          wdq * (DIFF_QK_DIM ** -0.5 * LOG2E), wdk, wdv, wkc, wvc], axis=1).astype(BF16)
    pad = LANES - NSA_GATE - 3 * FOX_HEADS
    ws = jnp.concatenate([wgt[:, gate_perm], wff, wff, wff, jnp.zeros((D_MODEL, pad), F32)], axis=1)
    wsh, wsl = _split_hi_lo(ws)
    bs = jnp.concatenate([b_gate[l][gate_perm], b_fgt[l], b_fgt[l], b_fgt[l], jnp.zeros((pad,), F32)])[None, :]

    eye = jnp.eye(NSA_KV_GROUPS, dtype=F32)
    half = CMP_BLOCK // 2
    cmp = []
    for pe, w1, w2 in zip(cmp_pe, cmp_w1, cmp_w2):
        w1r = w1[l].reshape(2, half, HEAD_DIM, CMP_HIDDEN)
        w1x = jnp.einsum('srdk,gh->srgdhk', w1r, eye).reshape(2, half * NSA_KV, NSA_KV_GROUPS * CMP_HIDDEN)
        w2bd = jnp.einsum('kd,gh->gkhd', w2[l], eye).reshape(NSA_KV_GROUPS * CMP_HIDDEN, NSA_KV)
        pe2 = jnp.broadcast_to(pe[l].reshape(2, half, 1, HEAD_DIM), (2, half, NSA_KV_GROUPS, HEAD_DIM))
        cmp.append((pe2.reshape(2, half * NSA_KV), w1x[0].astype(BF16), w1x[1].astype(BF16), w2bd.astype(BF16)))

    wo = w_out[l]
    wa = wo[:NSA_Q][perm].astype(BF16)
    wb = wo[NSA_Q:NSA_Q + FOX_W].astype(BF16)
    wc = wo[NSA_Q + FOX_W:].astype(BF16)
    return dict(wm=wm, wsh=wsh, wsl=wsl, bs=bs, cmp_k=cmp[0], cmp_v=cmp[1], wa=wa, wb=wb, wc=wc)


def _overlap_matrix(n_slab, n_cmp, n_sel):
    c_start = np.arange(n_cmp) * CMP_STRIDE
    s_start = np.arange(n_sel) * SLC_BLOCK
    ov = np.clip(np.minimum(c_start[:, None] + CMP_BLOCK, s_start[None, :] + SLC_BLOCK)
                 - np.maximum(c_start[:, None], s_start[None, :]), 0, None) / CMP_BLOCK
    full = np.zeros((n_slab, LANES), np.float32)
    full[:n_cmp, :n_sel] = ov
    return jnp.asarray(full, dtype=BF16)


def kernel(x, norm_attn, w_in, b_gate, b_fgt, cmp_k_pe, cmp_k_w1, cmp_k_w2, cmp_v_pe, cmp_v_w1, cmp_v_w2, diff_lambda, diff_subln, w_out, norm_ffn, w_grp, b_grp, w_exp, b_exp, w_e_gate, w_e_up, w_e_down, norm_final):
    b, t, d = x.shape
    depth = w_in.shape[0]
    n = b * t
    n_slab = t // CMP_STRIDE
    n_cmp = (t - CMP_BLOCK) // CMP_STRIDE + 1
    n_sel = t // SLC_BLOCK
    top_n = min(SLC_TOPK, n_sel)
    ov = _overlap_matrix(n_slab, n_cmp, n_sel)

    wg_all = w_e_gate.astype(BF16)
    wu_all = w_e_up.astype(BF16)
    wd_all = w_e_down.astype(BF16)
    gf = norm_final[None, :]

    x2 = x.reshape(n, d)
    for l in range(depth):
        p = _layer_params(l, w_in, b_gate, b_fgt, (cmp_k_pe, cmp_v_pe), (cmp_k_w1, cmp_v_w1),
                          (cmp_k_w2, cmp_v_w2), w_out)
        main, kc, vc, small = _inproj(x2, norm_attn[l][None, :], p["wm"], p["wsh"], p["wsl"], p["bs"])
        main3 = main.reshape(b, t, MAIN_W)
        small3 = small.reshape(b, t, LANES)
        kcmp = _compress(kc.reshape(b, n_slab, CMP_STRIDE * NSA_KV), *p["cmp_k"])
        vcmp = _compress(vc.reshape(b, n_slab, CMP_STRIDE * NSA_KV), *p["cmp_v"])
        o_nsa, pen, used = _cmp_attn(main3, kcmp, vcmp, small3, ov, n_cmp, n_sel, top_n)
        nkt = t // min(TK, t)
        tile_used = used.reshape(b, t // TQ, -1, LANES).max(axis=2)[:, :, :nkt].astype(jnp.int32)
        o_nsa = _flash(main3, "slc", (pen, small3, o_nsa), BF16, tile_used=tile_used)
        o_nsa = _flash(main3, "win", (small3, o_nsa), BF16)
        o_fox = _flash(main3, "fox", (_cumgate(small3),), BF16)
        o_diff = _flash(main3, "diff", (diff_lambda[l], diff_subln[l][:, None]), BF16, lam_init=_lambda_init(l))

        x2 = _outproj(x2, o_nsa.reshape(n, NSA_Q), o_fox.reshape(n, FOX_W), o_diff.reshape(n, DIFF_W),
                      p["wa"], p["wb"], p["wc"])

        gpad, epad = ROUTE_OFF - N_GROUPS, LANES - ROUTE_OFF - N_EXPERTS
        wr = jnp.concatenate([w_grp[l], jnp.zeros((d, gpad), F32), w_exp[l], jnp.zeros((d, epad), F32)], axis=1)
        wrh, wrl = _split_hi_lo(wr)
        br = jnp.concatenate([b_grp[l], jnp.zeros((gpad,), F32), b_exp[l], jnp.zeros((epad,), F32)])[None, :]
        x2 = _moe(x2, norm_ffn[l][None, :], wrh, wrl, br, wg_all, wu_all, wd_all, gf, l,
                  final_norm=(l == depth - 1))
    return x2.reshape(b, t, d)
```

```python
import functools
import math

import ml_dtypes
import numpy as np
import jax
import jax.numpy as jnp
from jax import lax
from jax.experimental import pallas as pl
from jax.experimental.pallas import tpu as pltpu

F32 = jnp.float32
BF16 = jnp.bfloat16

D_MODEL = 1024
HEAD_DIM = 64
NSA_HEADS = 8
NSA_KV_GROUPS = 2
CMP_BLOCK = 32
CMP_STRIDE = 16
CMP_HIDDEN = 256
SLC_BLOCK = 64
SLC_SHIFT = 6
SLC_TOPK = 16
WINDOW = 512
FOX_HEADS = 4
DIFF_HEADS = 4
DIFF_QK_DIM = HEAD_DIM // 2
N_GROUPS = 4
EXPERTS_PER_GROUP = 8
N_EXPERTS = N_GROUPS * EXPERTS_PER_GROUP
D_EXPERT = 256
RMS_EPS = 1e-6
FORCE_SCORE = 1e4
NEG_INF = -1e30
MASKED = -2e30

NSA_Q = NSA_HEADS * HEAD_DIM
NSA_KV = NSA_KV_GROUPS * HEAD_DIM
NSA_GATE = NSA_HEADS * 3
FOX_W = FOX_HEADS * HEAD_DIM
DIFF_W = DIFF_HEADS * HEAD_DIM
MIX_WIDTH = NSA_Q + FOX_W + DIFF_W
IN_WIDTHS = (NSA_Q, NSA_KV, NSA_KV, NSA_KV, NSA_KV, NSA_KV, NSA_KV, NSA_GATE,
             FOX_W, FOX_W, FOX_W, FOX_HEADS, DIFF_W, DIFF_W, DIFF_W)

LANES = 128
SUBLANES = 8
VMEM_LIMIT_BYTES = 56 * 1024 * 1024

MAIN_W = NSA_Q + 4 * NSA_KV + 3 * FOX_W + 3 * DIFF_W
PROJ_W = MAIN_W + 2 * NSA_KV
COL_KS, COL_VS, COL_KW, COL_VW = 4, 5, 6, 7
COL_FQ, COL_FK, COL_FV = 4, 5, 6
COL_DQ, COL_DK, COL_DV = 7, 8, 9
FGT_LANES = (NSA_GATE, NSA_GATE + FOX_HEADS, NSA_GATE + 2 * FOX_HEADS)
LOG2E = 1.4426950408889634

TQ = 512
TK = 512
KSTRIP = 512
ONES_ROWS = 16

def _alibi(n):
    return [float(2.0 ** (-8.0 * (i + 1) / n)) for i in range(n)]


def _lambda_init(layer):
    return 0.8 - 0.6 * math.exp(-0.3 * layer)


def _cparams(*sem):
    return pltpu.CompilerParams(dimension_semantics=sem, vmem_limit_bytes=VMEM_LIMIT_BYTES)


def _dot(a, b):
    return jnp.dot(a, b, preferred_element_type=F32)


def _dot_nt(a, b):
    return lax.dot_general(a, b, (((1,), (1,)), ((), ())), preferred_element_type=F32)


def _split2(x):
    hi = x.astype(BF16)
    lo = (x - hi.astype(F32)).astype(BF16)
    return hi, lo


def _sigmoid(z):
    return 1.0 / (1.0 + jnp.exp(-z))


def _keep_lanes(sel, blk):
    return jnp.where(sel, blk.astype(F32), 0.0).astype(BF16)


def _inproj_kernel(x_ref, g_ref, wm_ref, wsh_ref, wsl_ref, bs_ref, main_ref, kc_ref, vc_ref, small_ref):
    x = x_ref[...]
    ms = jnp.mean(x * x, axis=-1, keepdims=True)
    y = x * lax.rsqrt(ms + RMS_EPS) * g_ref[...]
    hb, hl = _split2(y)
    full = _dot(hb, wm_ref[...]).astype(BF16)
    main_ref[...] = full[:, :MAIN_W]
    kc_ref[...] = full[:, MAIN_W:MAIN_W + NSA_KV]
    vc_ref[...] = full[:, MAIN_W + NSA_KV:]
    small_ref[...] = (_dot(hb, wsh_ref[...]) + _dot(hl, wsh_ref[...]) + _dot(hb, wsl_ref[...])) + bs_ref[...]


def _inproj(x2, g, wm, wsh, wsl, bs, tm=512):
    n = x2.shape[0]
    full = lambda i: (0, 0)
    return pl.pallas_call(
        _inproj_kernel,
        grid=(n // tm,),
        in_specs=[pl.BlockSpec((tm, D_MODEL), lambda i: (i, 0)),
                  pl.BlockSpec((1, D_MODEL), full),
                  pl.BlockSpec((D_MODEL, PROJ_W), full),
                  pl.BlockSpec((D_MODEL, LANES), full),
                  pl.BlockSpec((D_MODEL, LANES), full),
                  pl.BlockSpec((1, LANES), full)],
        out_specs=[pl.BlockSpec((tm, MAIN_W), lambda i: (i, 0)),
                   pl.BlockSpec((tm, NSA_KV), lambda i: (i, 0)),
                   pl.BlockSpec((tm, NSA_KV), lambda i: (i, 0)),
                   pl.BlockSpec((tm, LANES), lambda i: (i, 0))],
        out_shape=[jax.ShapeDtypeStruct((n, MAIN_W), BF16),
                   jax.ShapeDtypeStruct((n, NSA_KV), BF16),
                   jax.ShapeDtypeStruct((n, NSA_KV), BF16),
                   jax.ShapeDtypeStruct((n, LANES), F32)],
        compiler_params=_cparams("parallel"),
        name="inproj",
    )(x2, g, wm, wsh, wsl, bs)


def _compress_kernel(r_ref, pe_ref, w1a_ref, w1b_ref, w2_ref, o_ref):
    r = r_ref[0].astype(F32)
    pe = pe_ref[...]
    ra = (r + pe[0:1, :]).astype(BF16)
    rb = (r + pe[1:2, :]).astype(BF16)
    a = _dot(ra, w1a_ref[...])
    b = _dot(rb, w1b_ref[...])
    n_slab = a.shape[0]
    hid = a + pltpu.roll(b, n_slab - 1, 0)
    hid = jax.nn.gelu(hid)
    o_ref[0] = _dot(hid.astype(BF16), w2_ref[...]).astype(BF16)


def _compress(r, pe2, w1a, w1b, w2bd):
    b, n_slab, w = r.shape
    full = lambda i: (0, 0)
    return pl.pallas_call(
        _compress_kernel,
        grid=(b,),
        in_specs=[pl.BlockSpec((1, n_slab, w), lambda i: (i, 0, 0)),
                  pl.BlockSpec((2, w), full),
                  pl.BlockSpec((w, 2 * CMP_HIDDEN), full),
                  pl.BlockSpec((w, 2 * CMP_HIDDEN), full),
                  pl.BlockSpec((2 * CMP_HIDDEN, LANES), full)],
        out_specs=pl.BlockSpec((1, n_slab, LANES), lambda i: (i, 0, 0)),
        out_shape=jax.ShapeDtypeStruct((b, n_slab, LANES), BF16),
        compiler_params=_cparams("parallel"),
        name="compress",
    )(r, pe2, w1a, w1b, w2bd)


def _cumgate_kernel(z_ref, o_ref, carry_ref, *, tc):
    @pl.when(pl.program_id(1) == 0)
    def _():
        carry_ref[...] = jnp.zeros_like(carry_ref)

    z = z_ref[0]
    logf = -(jnp.maximum(-z, 0.0) + jnp.log(1.0 + jnp.exp(-jnp.abs(z))))
    hi = logf.astype(BF16)
    r1 = logf - hi.astype(F32)
    mid = r1.astype(BF16)
    lo = (r1 - mid.astype(F32)).astype(BF16)
    tri = (lax.broadcasted_iota(jnp.int32, (tc, tc), 0) >= lax.broadcasted_iota(jnp.int32, (tc, tc), 1))
    tri = jnp.where(tri, 1.0, 0.0).astype(BF16)
    c = (_dot(tri, hi) + _dot(tri, mid)) + _dot(tri, lo) + carry_ref[0:1, :]
    carry_ref[...] = jnp.broadcast_to(c[tc - 1:tc, :], carry_ref.shape)
    v = c * (-LOG2E)
    p0 = v.astype(BF16)
    r1 = v - p0.astype(F32)
    p1 = r1.astype(BF16)
    p2 = (r1 - p1.astype(F32)).astype(BF16)
    lane = lax.broadcasted_iota(jnp.int32, v.shape, 1)
    pieces = jnp.where(lane < FGT_LANES[1], p0.astype(F32), jnp.where(lane < FGT_LANES[2], p1.astype(F32), p2.astype(F32)))
    keep = (lane >= FGT_LANES[0]) & (lane < FGT_LANES[2] + FOX_HEADS)
    o_ref[0] = jnp.where(keep, pieces, 0.0).astype(BF16)


def _cumgate(small3, tc=512):
    b, t, w = small3.shape
    return pl.pallas_call(
        functools.partial(_cumgate_kernel, tc=tc),
        grid=(b, t // tc),
        in_specs=[pl.BlockSpec((1, tc, w), lambda i, j: (i, j, 0))],
        out_specs=pl.BlockSpec((1, tc, w), lambda i, j: (i, j, 0)),
        out_shape=jax.ShapeDtypeStruct((b, t, w), BF16),
        scratch_shapes=[pltpu.VMEM((8, w), F32)],
        compiler_params=_cparams("parallel", "arbitrary"),
        name="cumgate",
    )(small3)


def _cmp_kernel(q_ref, kc_ref, vc_ref, sm_ref, ovt_ref, o_ref, pen_ref, used_ref, imp_ref, cnt_ref,
                *, tq, n_cmp, n_sel, top_n):
    t0 = pl.program_id(1) * tq
    nck = kc_ref.shape[1]
    hg = NSA_HEADS // NSA_KV_GROUPS
    slopes = _alibi(NSA_HEADS)
    lane_k = lax.broadcasted_iota(jnp.int32, (nck, LANES), 1)
    end_k = lax.broadcasted_iota(jnp.int32, (nck, LANES), 0) * CMP_STRIDE + (CMP_BLOCK - 1)
    keys = jnp.concatenate([kc_ref[0], _alibi_k_aug(lane_k, end_k - t0, 0).astype(BF16)], axis=1)
    blk = lax.broadcasted_iota(jnp.int32, (nck, tq), 0)
    qpos = t0 + lax.broadcasted_iota(jnp.int32, (nck, tq), 1)
    visible = (qpos >= blk * CMP_STRIDE + (CMP_BLOCK - 1)) & (blk < n_cmp)
    bias = jnp.where(visible, 0.0, MASKED)
    v_t = vc_ref[0].astype(F32).T.astype(BF16)
    sig_t = _sigmoid(sm_ref[0]).T
    lane = lax.broadcasted_iota(jnp.int32, (tq, LANES), 1)
    psum = [None, None]
    outs = []
    for h in range(NSA_HEADS):
        jb, g = h % hg, h // hg
        sel = (lane >= g * HEAD_DIM) & (lane < (g + 1) * HEAD_DIM)
        qa = jnp.concatenate([_keep_lanes(sel, q_ref[0, :, jb * LANES:(jb + 1) * LANES]),
                              _alibi_q_aug(lane, 0, slopes[h]).astype(BF16)], axis=1)
        s = _dot_nt(keys, qa) + bias
        m = jnp.maximum(jnp.max(s, axis=0, keepdims=True), NEG_INF)
        p = jnp.exp2(s - m)
        l = jnp.sum(p, axis=0, keepdims=True)
        p = p * jnp.where(l > 0.0, 1.0 / l, 0.0)
        psum[g] = p if psum[g] is None else psum[g] + p
        outs.append(_dot(v_t[g * HEAD_DIM:(g + 1) * HEAD_DIM, :], p.astype(BF16)) * sig_t[h:h + 1, :])
    for jb in range(hg):
        o_ref[0, :, jb * LANES:(jb + 1) * LANES] = jnp.concatenate(
            [outs[jb], outs[jb + hg]], axis=0).T.astype(o_ref.dtype)

    sub8 = lax.broadcasted_iota(jnp.int32, (SUBLANES, tq), 0)
    row_s = lax.broadcasted_iota(jnp.int32, (n_sel, tq), 0)
    cur = (t0 + lax.broadcasted_iota(jnp.int32, (n_sel, tq), 1)) >> SLC_SHIFT
    for g in range(NSA_KV_GROUPS):
        ph, plo = _split2(psum[g])
        imp_t = (_dot(ovt_ref[...], ph) + _dot(ovt_ref[...], plo))[0:n_sel, :]
        forced = (row_s == 0) | (row_s == cur) | (row_s == cur - 1)
        imp_ref[g] = jnp.where(row_s <= cur, jnp.where(forced, FORCE_SCORE, imp_t), NEG_INF)
        cnt_ref[g] = jnp.zeros((n_sel, tq), F32)
    ngrp = n_sel // SUBLANES
    last_blk = (t0 + tq - 1) >> SLC_SHIFT
    for kg in range(ngrp):
        @pl.when(kg * SUBLANES <= last_blk)
        def _(kg=kg):
            for g in range(NSA_KV_GROUPS):
                rows = [imp_ref[g, i * SUBLANES:(i + 1) * SUBLANES, :] for i in range(ngrp)]
                part = [None] * ngrp
                for k in range(kg * SUBLANES, (kg + 1) * SUBLANES):
                    rk = rows[kg][k - kg * SUBLANES:k - kg * SUBLANES + 1, :]
                    for i in range(ngrp):
                        if i > kg:
                            one = jnp.where(rk >= rows[i], 1.0, 0.0)
                        elif i < kg:
                            one = jnp.where(rk > rows[i], 1.0, 0.0)
                        else:
                            one = jnp.where(sub8 > k - i * SUBLANES,
                                            jnp.where(rk >= rows[i], 1.0, 0.0), jnp.where(rk > rows[i], 1.0, 0.0))
                        part[i] = one if part[i] is None else part[i] + one
                for i in range(ngrp):
                    cnt_ref[g, i * SUBLANES:(i + 1) * SUBLANES, :] += part[i]
    pen_t = [jnp.where(cnt_ref[g] < float(top_n), 0.0, MASKED) for g in range(NSA_KV_GROUPS)]
    for g in range(NSA_KV_GROUPS):
        full = jnp.concatenate([pen_t[g], jnp.zeros((LANES - n_sel, tq), F32)], axis=0)
        pen_ref[0, :, g * LANES:(g + 1) * LANES] = full.T.astype(BF16)
    picked = jnp.where(jnp.maximum(pen_t[0], pen_t[1]) == 0.0, 1.0, 0.0)
    bpt = TK // SLC_BLOCK
    lane1 = lax.broadcasted_iota(jnp.int32, (1, LANES), 1)
    used = jnp.zeros((1, LANES), F32)
    for kt in range(n_sel // bpt):
        hit = jnp.max(picked[kt * bpt:(kt + 1) * bpt, :], axis=(0, 1), keepdims=True)
        used = jnp.where(lane1 == kt, hit, used)
    used_ref[0, 0] = used


def _cmp_attn(main3, kcmp, vcmp, small3, ov, n_cmp, n_sel, top_n, tq=TQ):
    b, t, _ = main3.shape
    nck = kcmp.shape[1]
    return pl.pallas_call(
        functools.partial(_cmp_kernel, tq=tq, n_cmp=n_cmp, n_sel=n_sel, top_n=top_n),
        grid=(b, t // tq),
        in_specs=[pl.BlockSpec((1, tq, NSA_Q), lambda i, j: (i, j, 0)),
                  pl.BlockSpec((1, nck, LANES), lambda i, j: (i, 0, 0)),
                  pl.BlockSpec((1, nck, LANES), lambda i, j: (i, 0, 0)),
                  pl.BlockSpec((1, tq, LANES), lambda i, j: (i, j, 0)),
                  pl.BlockSpec((LANES, nck), lambda i, j: (0, 0))],
        out_specs=[pl.BlockSpec((1, tq, NSA_Q), lambda i, j: (i, j, 0)),
                   pl.BlockSpec((1, tq, NSA_KV_GROUPS * LANES), lambda i, j: (i, j, 0)),
                   pl.BlockSpec((1, 1, 1, LANES), lambda i, j: (i, j, 0, 0))],
        out_shape=[jax.ShapeDtypeStruct((b, t, NSA_Q), BF16),
                   jax.ShapeDtypeStruct((b, t, NSA_KV_GROUPS * LANES), BF16),
                   jax.ShapeDtypeStruct((b, t // tq, 1, LANES), F32)],
        scratch_shapes=[pltpu.VMEM((NSA_KV_GROUPS, n_sel, tq), F32),
                        pltpu.VMEM((NSA_KV_GROUPS, n_sel, tq), F32)],
        compiler_params=_cparams("parallel", "parallel"),
        name="cmp_attn",
    )(main3, kcmp, vcmp, small3, ov)


def _bf16_terms(c, n=3):
    out, r = [], float(c)
    for _ in range(n):
        p = float(np.float32(r).astype(ml_dtypes.bfloat16))
        out.append(p)
        r -= p
    return out


def _alibi_q_aug(lane, a0, slope):
    aug = jnp.zeros(lane.shape, F32)
    for i, c in enumerate(_bf16_terms(slope * LOG2E)):
        aug = jnp.where(lane == a0 + 2 * i, float(SLC_BLOCK) * c, aug)
        aug = jnp.where(lane == a0 + 2 * i + 1, c, aug)
    return aug


def _alibi_k_aug(lane, rel, a0):
    hi = (rel >> SLC_SHIFT).astype(F32)
    lo = (rel & (SLC_BLOCK - 1)).astype(F32)
    inside = (lane >= a0) & (lane < a0 + 6)
    odd = ((lane - a0) & 1) == 1
    return jnp.where(inside, jnp.where(odd, lo, hi), 0.0)


ALIBI_LANE = {"slc": SLC_BLOCK, "win": 0, "diff": 0}
STACKS = {"slc": (1, 8), "win": (1, 8), "fox": (2, 2), "diff": (2, 4)}


def _v_half(mode, vh):
    nv = STACKS[mode][1]
    if mode in ("slc", "win"):
        return vh // (NSA_HEADS // NSA_KV_GROUPS)
    return (vh % nv) // (nv // 2)


def _flash_kernel(*refs, mode, tq, tk, lam_init, ntiles):
    sched_ref, refs = refs[0], refs[1:]
    if mode == "slc":
        use_ref, q_ref, k_ref, v_ref, pen_ref, sm_ref, add_ref, o_ref, qst_ref, m_ref, acc_ref = refs
    elif mode == "win":
        q_ref, k_ref, v_ref, sm_ref, add_ref, o_ref, qst_ref, m_ref, acc_ref = refs
    elif mode == "fox":
        q_ref, k_ref, v_ref, fa_ref, o_ref, qst_ref, m_ref, acc_ref = refs
    else:
        q_ref, k_ref, v_ref, lam_ref, sub_ref, o_ref, qst_ref, m_ref, acc_ref = refs
    nstack, nv = STACKS[mode]
    nsa = mode in ("slc", "win")
    hg = NSA_HEADS // NSA_KV_GROUPS
    step_id = pl.program_id(1)
    q0 = sched_ref[SCHED_Q, step_id] * tq
    k0 = sched_ref[SCHED_K, step_id] * tk

    @pl.when(sched_ref[SCHED_FIRST, step_id] == 1)
    def _():
        m_ref[...] = jnp.full(m_ref.shape, NEG_INF, F32)
        acc_ref[...] = jnp.zeros(acc_ref.shape, F32)
        lane = lax.broadcasted_iota(jnp.int32, (tq, LANES), 1)
        for vh in range(nstack * nv):
            if nsa:
                jb, g = vh % hg, vh // hg
                lo_lane, width = g * HEAD_DIM, HEAD_DIM
                aug = _alibi_q_aug(lane, ALIBI_LANE[mode], _alibi(NSA_HEADS)[vh])
                if mode == "slc":
                    aug = aug + pen_ref[0, :, g * LANES:(g + 1) * LANES].astype(F32)
            elif mode == "fox":
                jb, r = vh // nv, vh % nv
                lo_lane, width = r * HEAD_DIM, HEAD_DIM
                hit = (lane == FGT_LANES[0] + vh) | (lane == FGT_LANES[1] + vh) | (lane == FGT_LANES[2] + vh)
                aug = jnp.where(hit, 1.0, 0.0)
            else:
                jb, r = vh // nv, vh % nv
                lo_lane, width = r * DIFF_QK_DIM, DIFF_QK_DIM
                aug = _alibi_q_aug(lane, ALIBI_LANE[mode], _alibi(DIFF_HEADS)[vh // 2])
            sel = (lane >= lo_lane) & (lane < lo_lane + width)
            qst_ref[vh * tq:(vh + 1) * tq, 0:LANES] = _keep_lanes(sel, q_ref[0, :, jb * LANES:(jb + 1) * LANES])
            qst_ref[vh * tq:(vh + 1) * tq, LANES:2 * LANES] = aug.astype(BF16)

    def step(masked):
        lane_k = lax.broadcasted_iota(jnp.int32, (tk, LANES), 1)
        row_k = lax.broadcasted_iota(jnp.int32, (tk, LANES), 0)
        if mode == "fox":
            k_aug = fa_ref[0]
        else:
            ka = _alibi_k_aug(lane_k, (k0 - q0) + row_k, ALIBI_LANE[mode])
            if mode == "slc":
                ka = ka + jnp.where(((k0 + row_k) >> SLC_SHIFT) == lane_k, 1.0, 0.0)
            k_aug = ka.astype(BF16)
        if masked:
            dist = (q0 - k0) + lax.broadcasted_iota(jnp.int32, (tk, tq), 1) - lax.broadcasted_iota(jnp.int32, (tk, tq), 0)
            ok = dist >= 0
            if mode == "win":
                ok = ok & (dist < WINDOW)
            bias = jnp.where(ok, 0.0, MASKED)
        keys, vals_t = [], []
        for st in range(nstack):
            kblk = k_ref[0] if nsa else k_ref[0, :, st * LANES:(st + 1) * LANES]
            vblk = v_ref[0] if nsa else v_ref[0, :, st * LANES:(st + 1) * LANES]
            keys.append(jnp.concatenate([kblk, k_aug], axis=1))
            v_t = vblk.astype(F32).T.astype(BF16)
            ones = jnp.ones((ONES_ROWS, tk), BF16)
            vals_t.append([jnp.concatenate([v_t[hf * HEAD_DIM:(hf + 1) * HEAD_DIM, :], ones], axis=0)
                           for hf in range(LANES // HEAD_DIM)])

        nstrip = tk // KSTRIP
        nvh = nstack * nv

        def scores(vh):
            s = _dot_nt(keys[vh // nv], qst_ref[vh * tq:(vh + 1) * tq, :])
            if masked:
                s = s + bias
            return s, jnp.max(s, axis=0, keepdims=True)

        cur, cur_max = scores(0)
        for vh in range(nvh):
            cols = slice(vh * tq, (vh + 1) * tq)
            v_half = vals_t[vh // nv][_v_half(mode, vh)]
            m_old = m_ref[:, cols]
            m_new = jnp.maximum(m_old, cur_max)
            alpha = jnp.exp2(m_old - m_new)
            if vh + 1 < nvh:
                nxt, nxt_max = scores(vh + 1)
            pv = None
            for r in range(nstrip):
                p = jnp.exp2((cur[r * KSTRIP:(r + 1) * KSTRIP, :] - m_new).astype(BF16))
                part = _dot(v_half[:, r * KSTRIP:(r + 1) * KSTRIP], p)
                pv = part if pv is None else pv + part
            m_ref[:, cols] = m_new
            acc_ref[:, cols] = alpha * acc_ref[:, cols] + pv
            if vh + 1 < nvh:
                cur, cur_max = nxt, nxt_max

    is_masked = sched_ref[SCHED_MASKED, step_id] == 1
    if mode == "win":
        pl.when(is_masked)(lambda: step(True))
    elif mode == "slc":
        tile_id = (pl.program_id(0) * ntiles[0] + sched_ref[SCHED_Q, step_id]) * ntiles[1] + sched_ref[SCHED_K, step_id]
        needed = use_ref[tile_id] == 1
        pl.when(is_masked & needed)(lambda: step(True))
        pl.when(jnp.logical_not(is_masked) & needed)(lambda: step(False))
    else:
        pl.when(is_masked)(lambda: step(True))
        pl.when(jnp.logical_not(is_masked))(lambda: step(False))

    @pl.when(sched_ref[SCHED_LAST, step_id] == 1)
    def _():
        inv = 1.0 / acc_ref[HEAD_DIM:HEAD_DIM + 1, :]

        def out_t(vh):
            return acc_ref[0:HEAD_DIM, vh * tq:(vh + 1) * tq] * inv[:, vh * tq:(vh + 1) * tq]

        def pair(a, b):
            return jnp.concatenate([a, b], axis=0).T

        if nsa:
            sig_t = _sigmoid(sm_ref[0]).T
            br = 1 if mode == "slc" else 2
            for jb in range(hg):
                c0 = br * NSA_HEADS + jb
                o = pair(out_t(jb) * sig_t[c0:c0 + 1, :], out_t(jb + hg) * sig_t[c0 + hg:c0 + hg + 1, :])
                o_ref[0, :, jb * LANES:(jb + 1) * LANES] = (
                    add_ref[0, :, jb * LANES:(jb + 1) * LANES].astype(F32) + o).astype(o_ref.dtype)
        elif mode == "fox":
            for pb in range(nstack):
                o = pair(out_t(2 * pb), out_t(2 * pb + 1))
                o_ref[0, :, pb * LANES:(pb + 1) * LANES] = o.astype(o_ref.dtype)
        else:
            lam = lam_ref[...]
            lam_full = (jnp.exp(jnp.sum(lam[0:1, :] * lam[1:2, :], axis=-1, keepdims=True))
                        - jnp.exp(jnp.sum(lam[2:3, :] * lam[3:4, :], axis=-1, keepdims=True)) + lam_init)
            for pb in range(nstack):
                normed = []
                for hh in range(2):
                    d = out_t(pb * nv + 2 * hh) - lam_full * out_t(pb * nv + 2 * hh + 1)
                    ms = jnp.mean(d * d, axis=0, keepdims=True)
                    normed.append(d * lax.rsqrt(ms + RMS_EPS) * sub_ref[...] * (1.0 - lam_init))
                o_ref[0, :, pb * LANES:(pb + 1) * LANES] = pair(normed[0], normed[1]).astype(o_ref.dtype)


def _flash(main3, mode, extra, out_dtype, lam_init=0.0, tile_used=None, tq=TQ, tk=TK):
    b, t, _ = main3.shape
    tk = min(tk, t)
    nstack, nv = STACKS[mode]
    sched = _flash_schedule(t, tq, tk, mode == "win")
    prefetch = [jnp.asarray(sched)]
    if mode == "slc":
        prefetch.append(tile_used.reshape(-1))
    qtile = lambda w, c: pl.BlockSpec((1, tq, w), lambda bi, s, sch, *_: (bi, sch[SCHED_Q, s], c))
    ktile = lambda w, c: pl.BlockSpec((1, tk, w), lambda bi, s, sch, *_: (bi, sch[SCHED_K, s], c))
    if mode in ("slc", "win"):
        kcol, vcol = (COL_KS, COL_VS) if mode == "slc" else (COL_KW, COL_VW)
        in_specs = [qtile(NSA_Q, 0), ktile(LANES, kcol), ktile(LANES, vcol)]
        if mode == "slc":
            in_specs.append(qtile(NSA_KV_GROUPS * LANES, 0))
        in_specs += [qtile(LANES, 0), qtile(NSA_Q, 0)]
        out_w = NSA_Q
    else:
        qc, kc, vc = (COL_FQ, COL_FK, COL_FV) if mode == "fox" else (COL_DQ, COL_DK, COL_DV)
        in_specs = [qtile(2 * LANES, qc), ktile(2 * LANES, kc), ktile(2 * LANES, vc)]
        if mode == "fox":
            in_specs.append(ktile(LANES, 0))
        else:
            in_specs += [pl.BlockSpec((4, DIFF_QK_DIM), lambda bi, s, sch, *_: (0, 0)),
                         pl.BlockSpec((HEAD_DIM, 1), lambda bi, s, sch, *_: (0, 0))]
        out_w = 2 * LANES
    cols = nstack * nv * tq
    return pl.pallas_call(
        functools.partial(_flash_kernel, mode=mode, tq=tq, tk=tk, lam_init=lam_init, ntiles=(t // tq, t // tk)),
        grid_spec=pltpu.PrefetchScalarGridSpec(
            num_scalar_prefetch=len(prefetch),
            grid=(b, sched.shape[1]),
            in_specs=in_specs,
            out_specs=qtile(out_w, 0),
            scratch_shapes=[pltpu.VMEM((cols, 2 * LANES), BF16),
                            pltpu.VMEM((1, cols), F32),
                            pltpu.VMEM((HEAD_DIM + ONES_ROWS, cols), F32)]),
        out_shape=jax.ShapeDtypeStruct((b, t, out_w), out_dtype),
        compiler_params=_cparams("parallel", "arbitrary"),
        name="flash_" + mode,
    )(*prefetch, main3, main3, main3, *extra)


SCHED_Q, SCHED_K, SCHED_FIRST, SCHED_LAST, SCHED_MASKED = range(5)


def _flash_schedule(t, tq, tk, window):
    rows = []
    for i in range(t // tq):
        q_lo, q_hi = i * tq, i * tq + tq - 1
        k_first = max(q_lo - (WINDOW - 1), 0) // tk if window else 0
        tiles = list(range(k_first, q_hi // tk + 1))
        for n, kt in enumerate(tiles):
            fully_visible = kt * tk + tk - 1 <= q_lo and not window
            rows.append((i, kt, int(n == 0), int(n == len(tiles) - 1), int(not fully_visible)))
    return np.asarray(rows, np.int32).T


def _outproj_kernel(x_ref, oa_ref, ob_ref, oc_ref, wa_ref, wb_ref, wc_ref, o_ref):
    acc = _dot(oa_ref[...], wa_ref[...])
    acc = acc + _dot(ob_ref[...], wb_ref[...])
    acc = acc + _dot(oc_ref[...], wc_ref[...])
    o_ref[...] = x_ref[...] + acc


def _outproj(x2, oa, ob, oc, wa, wb, wc, tm=512):
    n = x2.shape[0]
    row = lambda i: (i, 0)
    full = lambda i: (0, 0)
    return pl.pallas_call(
        _outproj_kernel,
        grid=(n // tm,),
        in_specs=[pl.BlockSpec((tm, D_MODEL), row),
                  pl.BlockSpec((tm, NSA_Q), row),
                  pl.BlockSpec((tm, FOX_W), row),
                  pl.BlockSpec((tm, DIFF_W), row),
                  pl.BlockSpec((NSA_Q, D_MODEL), full),
                  pl.BlockSpec((FOX_W, D_MODEL), full),
                  pl.BlockSpec((DIFF_W, D_MODEL), full)],
        out_specs=pl.BlockSpec((tm, D_MODEL), row),
        out_shape=jax.ShapeDtypeStruct((n, D_MODEL), F32),
        compiler_params=_cparams("parallel"),
        name="outproj",
    )(x2, oa, ob, oc, wa, wb, wc)


ROUTE_OFF = SUBLANES
COL_GRP, COL_POS = EXPERTS_PER_GROUP, EXPERTS_PER_GROUP + 1


MOE_CHUNK = 304
MOE_PREFIX_BLOCK = 256


def _moe_kernel(x_ref, g_ref, wrh_ref, wrl_ref, br_ref, wg_ref, wu_ref, wd_ref, gf_ref, o_ref,
                h_ref, gsel_ref, col_ref, row_ref, cnt_ref, *, final_norm):
    grp_id = pl.program_id(1)
    tm = x_ref.shape[0]
    ch = MOE_CHUNK

    @pl.when(grp_id == 0)
    def _():
        x = x_ref[...]
        ms = jnp.mean(x * x, axis=-1, keepdims=True)
        y = x * lax.rsqrt(ms + RMS_EPS) * g_ref[...]
        hb, hl = _split2(y)
        h_ref[...] = hb
        lg = (_dot(hb, wrh_ref[...]) + _dot(hl, wrh_ref[...]) + _dot(hb, wrl_ref[...])) + br_ref[...]
        nrow = ROUTE_OFF + N_EXPERTS
        lt = lg.T[0:nrow, :]
        r = lax.broadcasted_iota(jnp.int32, (nrow, tm), 0).astype(F32)
        big = float(LANES)
        isg = r < N_GROUPS
        gmax = jnp.max(jnp.where(isg, lt, -jnp.inf), axis=0, keepdims=True)
        grp = jnp.min(jnp.where(isg & (lt == gmax), r, big), axis=0, keepdims=True)
        gprob = 1.0 / jnp.sum(jnp.where(isg, jnp.exp(lt - gmax), 0.0), axis=0, keepdims=True)
        lo_row = ROUTE_OFF + grp * EXPERTS_PER_GROUP
        ing = (r >= lo_row) & (r < lo_row + EXPERTS_PER_GROUP)
        v1 = jnp.max(jnp.where(ing, lt, -jnp.inf), axis=0, keepdims=True)
        i1 = jnp.min(jnp.where(ing & (lt == v1), r, big), axis=0, keepdims=True)
        rest = ing & (r != i1)
        v2 = jnp.max(jnp.where(rest, lt, -jnp.inf), axis=0, keepdims=True)
        i2 = jnp.min(jnp.where(rest & (lt == v2), r, big), axis=0, keepdims=True)
        e2 = jnp.exp(v2 - v1)
        w1 = gprob / (1.0 + e2)
        w2 = gprob * e2 / (1.0 + e2)
        gate_t = jnp.where(r == i1, w1, 0.0) + jnp.where(r == i2, w2, 0.0)
        epg = EXPERTS_PER_GROUP
        gsel_t = jnp.zeros((epg, tm), F32)
        for gg in range(N_GROUPS):
            gsel_t = jnp.where(grp == float(gg), gate_t[ROUTE_OFF + gg * epg:ROUTE_OFF + (gg + 1) * epg, :], gsel_t)
        r8 = lax.broadcasted_iota(jnp.int32, (SUBLANES, tm), 0).astype(F32)
        onehot_t = jnp.where(r8 == grp, 1.0, 0.0)
        pb = MOE_PREFIX_BLOCK
        tri = jnp.where(lax.broadcasted_iota(jnp.int32, (pb, pb), 0) < lax.broadcasted_iota(jnp.int32, (pb, pb), 1),
                        1.0, 0.0).astype(BF16)
        carry = jnp.zeros((SUBLANES, 1), F32)
        pos_parts = []
        for blk in range(tm // pb):
            oh_b = onehot_t[:, blk * pb:(blk + 1) * pb]
            prefix = _dot(oh_b.astype(BF16), tri) + carry
            pos_parts.append(jnp.sum(oh_b * prefix, axis=0, keepdims=True))
            carry = carry + jnp.sum(oh_b, axis=1, keepdims=True)
        pos = jnp.concatenate(pos_parts, axis=1)
        for gg in range(N_GROUPS):
            cnt_ref[gg] = carry[gg, 0].astype(jnp.int32)
        row_ref[...] = jnp.concatenate([grp, pos, jnp.zeros((SUBLANES - 2, tm), F32)], axis=0)
        stacked = jnp.concatenate([gsel_t, row_ref[...], jnp.zeros((LANES - 2 * SUBLANES, tm), F32)], axis=0)
        colinfo = stacked.T
        lane = lax.broadcasted_iota(jnp.int32, (tm, LANES), 1)
        ghi, glo = _split2(jnp.where(lane < epg, colinfo, 0.0))
        gsel_ref[:, 0:LANES] = ghi
        gsel_ref[:, LANES:2 * LANES] = glo
        col_ref[...] = colinfo
        o_ref[...] = x

    grp_f = grp_id.astype(F32)
    key_col = jnp.where(col_ref[:, COL_GRP:COL_GRP + 1] == grp_f, col_ref[:, COL_POS:COL_POS + 1], -1.0)
    key_row = jnp.where(row_ref[0:1, :] == grp_f, row_ref[1:2, :], -1.0)

    def chunk(c, carry):
        base = (c * ch).astype(F32)
        sub_iota = lax.broadcasted_iota(jnp.int32, (ch, tm), 0).astype(F32)
        gather = jnp.where(key_row - base == sub_iota, 1.0, 0.0).astype(BF16)
        xg = _dot(gather, h_ref[...]).astype(BF16)
        gparts = _dot(gather, gsel_ref[...])
        gates = gparts[:, 0:LANES] + gparts[:, LANES:2 * LANES]
        y = None
        for e in range(EXPERTS_PER_GROUP):
            a = _dot(xg, wg_ref[0, e])
            u = _dot(xg, wu_ref[0, e])
            act = a * _sigmoid(a) * u * gates[:, e:e + 1]
            part = _dot(act.astype(BF16), wd_ref[0, e])
            y = part if y is None else y + part
        chp = -(-ch // LANES) * LANES
        lane_iota = lax.broadcasted_iota(jnp.int32, (tm, chp), 1).astype(F32)
        scatter = jnp.where(key_col - base == lane_iota, 1.0, 0.0).astype(BF16)
        yb = y.astype(BF16)
        if chp > ch:
            yb = jnp.concatenate([yb, jnp.zeros((chp - ch, D_MODEL), BF16)], axis=0)
        o_ref[...] = o_ref[...] + _dot(scatter, yb)
        return carry

    lax.fori_loop(0, (cnt_ref[grp_id] + ch - 1) // ch, chunk, 0)

    @pl.when(grp_id == N_GROUPS - 1)
    def _():
        if final_norm:
            y = o_ref[...]
            ms = jnp.mean(y * y, axis=-1, keepdims=True)
            o_ref[...] = y * lax.rsqrt(ms + RMS_EPS) * gf_ref[...]


def _moe(x2, g, wrh, wrl, br, wg, wu, wd, gf, layer, final_norm, tm=1024):
    n = x2.shape[0]
    tm = min(tm, n)
    row = lambda i, e: (i, 0)
    full = lambda i, e: (0, 0)
    epg = EXPERTS_PER_GROUP
    wspec_in = pl.BlockSpec((1, epg, D_MODEL, D_EXPERT), lambda i, e: (layer, e, 0, 0))
    wspec_out = pl.BlockSpec((1, epg, D_EXPERT, D_MODEL), lambda i, e: (layer, e, 0, 0))
    return pl.pallas_call(
        functools.partial(_moe_kernel, final_norm=final_norm),
        grid=(n // tm, N_GROUPS),
        in_specs=[pl.BlockSpec((tm, D_MODEL), row, pipeline_mode=pl.Buffered(1)),
                  pl.BlockSpec((1, D_MODEL), full),
                  pl.BlockSpec((D_MODEL, LANES), full),
                  pl.BlockSpec((D_MODEL, LANES), full),
                  pl.BlockSpec((1, LANES), full),
                  wspec_in, wspec_in, wspec_out,
                  pl.BlockSpec((1, D_MODEL), full)],
        out_specs=pl.BlockSpec((tm, D_MODEL), row),
        out_shape=jax.ShapeDtypeStruct((n, D_MODEL), F32),
        scratch_shapes=[pltpu.VMEM((tm, D_MODEL), BF16),
                        pltpu.VMEM((tm, 2 * LANES), BF16),
                        pltpu.VMEM((tm, LANES), F32),
                        pltpu.VMEM((8, tm), F32),
                        pltpu.SMEM((N_GROUPS,), jnp.int32)],
        compiler_params=_cparams("parallel", "arbitrary"),
        name="moe",
    )(x2, g, wrh, wrl, br, wg, wu, wd, gf)


def _head_perm():
    hg = NSA_HEADS // NSA_KV_GROUPS
    order = []
    for j in range(hg):
        order += [j, j + hg]
    return np.concatenate([np.arange(h * HEAD_DIM, (h + 1) * HEAD_DIM) for h in order])


def _split_hi_lo(w):
    hi = w.astype(BF16)
    return hi, (w - hi.astype(F32)).astype(BF16)


def _layer_params(l, w_in, b_gate, b_fgt, cmp_pe, cmp_w1, cmp_w2, w_out):
    offs = np.concatenate([[0], np.cumsum(IN_WIDTHS)])
    seg = [w_in[l][:, offs[i]:offs[i + 1]] for i in range(len(IN_WIDTHS))]
    (wq, wkc, wvc, wks, wvs, wkw, wvw, wgt, wfq, wfk, wfv, wff, wdq, wdk, wdv) = seg
    perm = _head_perm()
    gate_perm = np.array([h * 3 + br for br in range(3) for h in range(NSA_HEADS)])
    wm = jnp.concatenate([wq[:, perm] * (HEAD_DIM ** -0.5 * LOG2E), wks, wvs, wkw, wvw,
                          wfq * (HEAD_DIM ** -0.5 * LOG2E), wfk, wfv,
                          wdq * (DIFF_QK_DIM ** -0.5 * LOG2E), wdk, wdv, wkc, wvc], axis=1).astype(BF16)
    pad = LANES - NSA_GATE - 3 * FOX_HEADS
    ws = jnp.concatenate([wgt[:, gate_perm], wff, wff, wff, jnp.zeros((D_MODEL, pad), F32)], axis=1)
    wsh, wsl = _split_hi_lo(ws)
    bs = jnp.concatenate([b_gate[l][gate_perm], b_fgt[l], b_fgt[l], b_fgt[l], jnp.zeros((pad,), F32)])[None, :]

    eye = jnp.eye(NSA_KV_GROUPS, dtype=F32)
    half = CMP_BLOCK // 2
    cmp = []
    for pe, w1, w2 in zip(cmp_pe, cmp_w1, cmp_w2):
        w1r = w1[l].reshape(2, half, HEAD_DIM, CMP_HIDDEN)
        w1x = jnp.einsum('srdk,gh->srgdhk', w1r, eye).reshape(2, half * NSA_KV, NSA_KV_GROUPS * CMP_HIDDEN)
        w2bd = jnp.einsum('kd,gh->gkhd', w2[l], eye).reshape(NSA_KV_GROUPS * CMP_HIDDEN, NSA_KV)
        pe2 = jnp.broadcast_to(pe[l].reshape(2, half, 1, HEAD_DIM), (2, half, NSA_KV_GROUPS, HEAD_DIM))
        cmp.append((pe2.reshape(2, half * NSA_KV), w1x[0].astype(BF16), w1x[1].astype(BF16), w2bd.astype(BF16)))

    wo = w_out[l]
    wa = wo[:NSA_Q][perm].astype(BF16)
    wb = wo[NSA_Q:NSA_Q + FOX_W].astype(BF16)
    wc = wo[NSA_Q + FOX_W:].astype(BF16)
    return dict(wm=wm, wsh=wsh, wsl=wsl, bs=bs, cmp_k=cmp[0], cmp_v=cmp[1], wa=wa, wb=wb, wc=wc)


def _overlap_matrix(n_slab, n_cmp, n_sel):
    c_start = np.arange(n_cmp) * CMP_STRIDE
    s_start = np.arange(n_sel) * SLC_BLOCK
    ov = np.clip(np.minimum(c_start[:, None] + CMP_BLOCK, s_start[None, :] + SLC_BLOCK)
                 - np.maximum(c_start[:, None], s_start[None, :]), 0, None) / CMP_BLOCK
    full = np.zeros((LANES, n_slab), np.float32)
    full[:n_sel, :n_cmp] = ov.T
    return jnp.asarray(full, dtype=BF16)


def kernel(x, norm_attn, w_in, b_gate, b_fgt, cmp_k_pe, cmp_k_w1, cmp_k_w2, cmp_v_pe, cmp_v_w1, cmp_v_w2, diff_lambda, diff_subln, w_out, norm_ffn, w_grp, b_grp, w_exp, b_exp, w_e_gate, w_e_up, w_e_down, norm_final):
    b, t, d = x.shape
    depth = w_in.shape[0]
    n = b * t
    n_slab = t // CMP_STRIDE
    n_cmp = (t - CMP_BLOCK) // CMP_STRIDE + 1
    n_sel = t // SLC_BLOCK
    top_n = min(SLC_TOPK, n_sel)
    ov = _overlap_matrix(n_slab, n_cmp, n_sel)

    wg_all = w_e_gate.astype(BF16)
    wu_all = w_e_up.astype(BF16)
    wd_all = w_e_down.astype(BF16)
    gf = norm_final[None, :]

    x2 = x.reshape(n, d)
    for l in range(depth):
        p = _layer_params(l, w_in, b_gate, b_fgt, (cmp_k_pe, cmp_v_pe), (cmp_k_w1, cmp_v_w1),
                          (cmp_k_w2, cmp_v_w2), w_out)
        main, kc, vc, small = _inproj(x2, norm_attn[l][None, :], p["wm"], p["wsh"], p["wsl"], p["bs"])
        main3 = main.reshape(b, t, MAIN_W)
        small3 = small.reshape(b, t, LANES)
        kcmp = _compress(kc.reshape(b, n_slab, CMP_STRIDE * NSA_KV), *p["cmp_k"])
        vcmp = _compress(vc.reshape(b, n_slab, CMP_STRIDE * NSA_KV), *p["cmp_v"])
        o_nsa, pen, used = _cmp_attn(main3, kcmp, vcmp, small3, ov, n_cmp, n_sel, top_n)
        nkt = t // min(TK, t)
        tile_used = used.reshape(b, t // TQ, -1, LANES).max(axis=2)[:, :, :nkt].astype(jnp.int32)
        o_nsa = _flash(main3, "slc", (pen, small3, o_nsa), BF16, tile_used=tile_used)
        o_nsa = _flash(main3, "win", (small3, o_nsa), BF16)
        o_fox = _flash(main3, "fox", (_cumgate(small3),), BF16)
        o_diff = _flash(main3, "diff", (diff_lambda[l], diff_subln[l][:, None]), BF16, lam_init=_lambda_init(l))

        x2 = _outproj(x2, o_nsa.reshape(n, NSA_Q), o_fox.reshape(n, FOX_W), o_diff.reshape(n, DIFF_W),
                      p["wa"], p["wb"], p["wc"])

        gpad, epad = ROUTE_OFF - N_GROUPS, LANES - ROUTE_OFF - N_EXPERTS
        wr = jnp.concatenate([w_grp[l], jnp.zeros((d, gpad), F32), w_exp[l], jnp.zeros((d, epad), F32)], axis=1)
        wrh, wrl = _split_hi_lo(wr)
        br = jnp.concatenate([b_grp[l], jnp.zeros((gpad,), F32), b_exp[l], jnp.zeros((epad,), F32)])[None, :]
        x2 = _moe(x2, norm_ffn[l][None, :], wrh, wrl, br, wg_all, wu_all, wd_all, gf, l,
                  final_norm=(l == depth - 1))
    return x2.reshape(b, t, d)
```

```python
import functools
import math

import ml_dtypes
import numpy as np
import jax
import jax.numpy as jnp
from jax import lax
from jax.experimental import pallas as pl
from jax.experimental.pallas import tpu as pltpu

F32 = jnp.float32
BF16 = jnp.bfloat16

D_MODEL = 1024
HEAD_DIM = 64
NSA_HEADS = 8
NSA_KV_GROUPS = 2
CMP_BLOCK = 32
CMP_STRIDE = 16
CMP_HIDDEN = 256
SLC_BLOCK = 64
SLC_SHIFT = 6
SLC_TOPK = 16
WINDOW = 512
FOX_HEADS = 4
DIFF_HEADS = 4
DIFF_QK_DIM = HEAD_DIM // 2
N_GROUPS = 4
EXPERTS_PER_GROUP = 8
N_EXPERTS = N_GROUPS * EXPERTS_PER_GROUP
D_EXPERT = 256
RMS_EPS = 1e-6
FORCE_SCORE = 1e4
NEG_INF = -1e30
MASKED = -2e30

NSA_Q = NSA_HEADS * HEAD_DIM
NSA_KV = NSA_KV_GROUPS * HEAD_DIM
NSA_GATE = NSA_HEADS * 3
FOX_W = FOX_HEADS * HEAD_DIM
DIFF_W = DIFF_HEADS * HEAD_DIM
MIX_WIDTH = NSA_Q + FOX_W + DIFF_W
IN_WIDTHS = (NSA_Q, NSA_KV, NSA_KV, NSA_KV, NSA_KV, NSA_KV, NSA_KV, NSA_GATE,
             FOX_W, FOX_W, FOX_W, FOX_HEADS, DIFF_W, DIFF_W, DIFF_W)

LANES = 128
SUBLANES = 8
VMEM_LIMIT_BYTES = 56 * 1024 * 1024

MAIN_W = 2 * NSA_KV + FOX_W + DIFF_W
PROJ_W = MAIN_W + 2 * NSA_KV
COL_KS, COL_KW = 0, 1
COL_FK, COL_DK = 1, 2
ROWS_T = NSA_Q + FOX_W + DIFF_W + 2 * NSA_KV + FOX_W + DIFF_W
ROW_QN, ROW_FQ, ROW_DQ = 0, 2, 3
ROW_VS, ROW_VW = 8, 9
ROW_FV, ROW_DV = 5, 6
FGT_LANES = (NSA_GATE, NSA_GATE + FOX_HEADS, NSA_GATE + 2 * FOX_HEADS)
LOG2E = 1.4426950408889634

TQ = 512
TK = 512
KSTRIP = 512
ONES_ROWS = 16

def _alibi(n):
    return [float(2.0 ** (-8.0 * (i + 1) / n)) for i in range(n)]


def _lambda_init(layer):
    return 0.8 - 0.6 * math.exp(-0.3 * layer)


def _cparams(*sem):
    return pltpu.CompilerParams(dimension_semantics=sem, vmem_limit_bytes=VMEM_LIMIT_BYTES)


def _dot(a, b):
    return jnp.dot(a, b, preferred_element_type=F32)


def _dot_nt(a, b):
    return lax.dot_general(a, b, (((1,), (1,)), ((), ())), preferred_element_type=F32)


def _split2(x):
    hi = x.astype(BF16)
    lo = (x - hi.astype(F32)).astype(BF16)
    return hi, lo


def _sigmoid(z):
    return 1.0 / (1.0 + jnp.exp(-z))


def _keep(sel, blk):
    return jnp.where(sel, blk.astype(F32), 0.0).astype(BF16)


def _inproj_kernel(x_ref, g_ref, wm_ref, wt_ref, wsh_ref, wsl_ref, bs_ref, main_ref, feat_ref, kc_ref, vc_ref,
                   small_ref):
    x = x_ref[...]
    ms = jnp.mean(x * x, axis=-1, keepdims=True)
    y = x * lax.rsqrt(ms + RMS_EPS) * g_ref[...]
    hb, hl = _split2(y)
    feat_ref[...] = _dot_nt(wt_ref[...], hb).astype(BF16)
    full = _dot(hb, wm_ref[...]).astype(BF16)
    main_ref[...] = full[:, :MAIN_W]
    kc_ref[...] = full[:, MAIN_W:MAIN_W + NSA_KV]
    vc_ref[...] = full[:, MAIN_W + NSA_KV:]
    small_ref[...] = (_dot(hb, wsh_ref[...]) + _dot(hl, wsh_ref[...]) + _dot(hb, wsl_ref[...])) + bs_ref[...]


def _inproj(x2, g, wm, wt, wsh, wsl, bs, tm=1024):
    n = x2.shape[0]
    full = lambda i: (0, 0)
    return pl.pallas_call(
        _inproj_kernel,
        grid=(n // tm,),
        in_specs=[pl.BlockSpec((tm, D_MODEL), lambda i: (i, 0)),
                  pl.BlockSpec((1, D_MODEL), full),
                  pl.BlockSpec((D_MODEL, PROJ_W), full),
                  pl.BlockSpec((ROWS_T, D_MODEL), full),
                  pl.BlockSpec((D_MODEL, LANES), full),
                  pl.BlockSpec((D_MODEL, LANES), full),
                  pl.BlockSpec((1, LANES), full)],
        out_specs=[pl.BlockSpec((tm, MAIN_W), lambda i: (i, 0)),
                   pl.BlockSpec((ROWS_T, tm), lambda i: (0, i)),
                   pl.BlockSpec((tm, NSA_KV), lambda i: (i, 0)),
                   pl.BlockSpec((tm, NSA_KV), lambda i: (i, 0)),
                   pl.BlockSpec((tm, LANES), lambda i: (i, 0))],
        out_shape=[jax.ShapeDtypeStruct((n, MAIN_W), BF16),
                   jax.ShapeDtypeStruct((ROWS_T, n), BF16),
                   jax.ShapeDtypeStruct((n, NSA_KV), BF16),
                   jax.ShapeDtypeStruct((n, NSA_KV), BF16),
                   jax.ShapeDtypeStruct((n, LANES), F32)],
        compiler_params=_cparams("parallel"),
        name="inproj",
    )(x2, g, wm, wt, wsh, wsl, bs)


def _compress_kernel(r_ref, pe_ref, w1a_ref, w1b_ref, w2_ref, o_ref):
    r = r_ref[0].astype(F32)
    pe = pe_ref[...]
    ra = (r + pe[0:1, :]).astype(BF16)
    rb = (r + pe[1:2, :]).astype(BF16)
    a = _dot(ra, w1a_ref[...])
    b = _dot(rb, w1b_ref[...])
    n_slab = a.shape[0]
    hid = a + pltpu.roll(b, n_slab - 1, 0)
    hid = jax.nn.gelu(hid)
    o_ref[0] = _dot(hid.astype(BF16), w2_ref[...]).astype(BF16)


def _compress(r, pe2, w1a, w1b, w2bd):
    b, n_slab, w = r.shape
    full = lambda i: (0, 0)
    return pl.pallas_call(
        _compress_kernel,
        grid=(b,),
        in_specs=[pl.BlockSpec((1, n_slab, w), lambda i: (i, 0, 0)),
                  pl.BlockSpec((2, w), full),
                  pl.BlockSpec((w, 2 * CMP_HIDDEN), full),
                  pl.BlockSpec((w, 2 * CMP_HIDDEN), full),
                  pl.BlockSpec((2 * CMP_HIDDEN, LANES), full)],
        out_specs=pl.BlockSpec((1, n_slab, LANES), lambda i: (i, 0, 0)),
        out_shape=jax.ShapeDtypeStruct((b, n_slab, LANES), BF16),
        compiler_params=_cparams("parallel"),
        name="compress",
    )(r, pe2, w1a, w1b, w2bd)


def _cumgate_kernel(z_ref, o_ref, carry_ref, *, tc):
    @pl.when(pl.program_id(1) == 0)
    def _():
        carry_ref[...] = jnp.zeros_like(carry_ref)

    z = z_ref[0]
    logf = -(jnp.maximum(-z, 0.0) + jnp.log(1.0 + jnp.exp(-jnp.abs(z))))
    hi = logf.astype(BF16)
    r1 = logf - hi.astype(F32)
    mid = r1.astype(BF16)
    lo = (r1 - mid.astype(F32)).astype(BF16)
    tri = (lax.broadcasted_iota(jnp.int32, (tc, tc), 0) >= lax.broadcasted_iota(jnp.int32, (tc, tc), 1))
    tri = jnp.where(tri, 1.0, 0.0).astype(BF16)
    c = (_dot(tri, hi) + _dot(tri, mid)) + _dot(tri, lo) + carry_ref[0:1, :]
    carry_ref[...] = jnp.broadcast_to(c[tc - 1:tc, :], carry_ref.shape)
    v = c * (-LOG2E)
    p0 = v.astype(BF16)
    r1 = v - p0.astype(F32)
    p1 = r1.astype(BF16)
    p2 = (r1 - p1.astype(F32)).astype(BF16)
    lane = lax.broadcasted_iota(jnp.int32, v.shape, 1)
    pieces = jnp.where(lane < FGT_LANES[1], p0.astype(F32), jnp.where(lane < FGT_LANES[2], p1.astype(F32), p2.astype(F32)))
    keep = (lane >= FGT_LANES[0]) & (lane < FGT_LANES[2] + FOX_HEADS)
    o_ref[0] = jnp.where(keep, pieces, 0.0).astype(BF16)


def _cumgate(small3, tc=512):
    b, t, w = small3.shape
    return pl.pallas_call(
        functools.partial(_cumgate_kernel, tc=tc),
        grid=(b, t // tc),
        in_specs=[pl.BlockSpec((1, tc, w), lambda i, j: (i, j, 0))],
        out_specs=pl.BlockSpec((1, tc, w), lambda i, j: (i, j, 0)),
        out_shape=jax.ShapeDtypeStruct((b, t, w), BF16),
        scratch_shapes=[pltpu.VMEM((8, w), F32)],
        compiler_params=_cparams("parallel", "arbitrary"),
        name="cumgate",
    )(small3)


def _cmp_kernel(q_ref, kc_ref, vc_ref, sm_ref, ovt_ref, o_ref, pen_ref, used_ref, imp_ref, cnt_ref,
                *, tq, n_cmp, n_sel, top_n):
    t0 = pl.program_id(1) * tq
    nck = kc_ref.shape[1]
    hg = NSA_HEADS // NSA_KV_GROUPS
    slopes = _alibi(NSA_HEADS)
    lane_k = lax.broadcasted_iota(jnp.int32, (nck, LANES), 1)
    end_k = lax.broadcasted_iota(jnp.int32, (nck, LANES), 0) * CMP_STRIDE + (CMP_BLOCK - 1)
    keys = jnp.concatenate([kc_ref[0], _alibi_k_aug(lane_k, end_k - t0, 0).astype(BF16)], axis=1)
    blk = lax.broadcasted_iota(jnp.int32, (nck, tq), 0)
    qpos = t0 + lax.broadcasted_iota(jnp.int32, (nck, tq), 1)
    visible = (qpos >= blk * CMP_STRIDE + (CMP_BLOCK - 1)) & (blk < n_cmp)
    bias = jnp.where(visible, 0.0, MASKED)
    v_t = vc_ref[0].astype(F32).T.astype(BF16)
    sig_t = _sigmoid(sm_ref[0]).T
    feat = lax.broadcasted_iota(jnp.int32, (LANES, tq), 0)
    psum = [None, None]
    outs = []
    for h in range(NSA_HEADS):
        jb, g = h % hg, h // hg
        sel = (feat >= g * HEAD_DIM) & (feat < (g + 1) * HEAD_DIM)
        qa = jnp.concatenate([_keep(sel, q_ref[jb * LANES:(jb + 1) * LANES, :]),
                              _alibi_q_aug(feat, 0, slopes[h]).astype(BF16)], axis=0)
        s = _dot(keys, qa) + bias
        m = jnp.maximum(jnp.max(s, axis=0, keepdims=True), NEG_INF)
        p = jnp.exp2(s - m)
        l = jnp.sum(p, axis=0, keepdims=True)
        p = p * jnp.where(l > 0.0, 1.0 / l, 0.0)
        psum[g] = p if psum[g] is None else psum[g] + p
        outs.append(_dot(v_t[g * HEAD_DIM:(g + 1) * HEAD_DIM, :], p.astype(BF16)) * sig_t[h:h + 1, :])
    for jb in range(hg):
        o_ref[0, :, jb * LANES:(jb + 1) * LANES] = jnp.concatenate(
            [outs[jb], outs[jb + hg]], axis=0).T.astype(o_ref.dtype)

    sub8 = lax.broadcasted_iota(jnp.int32, (SUBLANES, tq), 0)
    row_s = lax.broadcasted_iota(jnp.int32, (n_sel, tq), 0)
    cur = (t0 + lax.broadcasted_iota(jnp.int32, (n_sel, tq), 1)) >> SLC_SHIFT
    for g in range(NSA_KV_GROUPS):
        ph, plo = _split2(psum[g])
        imp_t = (_dot(ovt_ref[...], ph) + _dot(ovt_ref[...], plo))[0:n_sel, :]
        forced = (row_s == 0) | (row_s == cur) | (row_s == cur - 1)
        imp_ref[g] = jnp.where(row_s <= cur, jnp.where(forced, FORCE_SCORE, imp_t), NEG_INF)
        cnt_ref[g] = jnp.zeros((n_sel, tq), F32)
    ngrp = n_sel // SUBLANES
    last_blk = (t0 + tq - 1) >> SLC_SHIFT
    for kg in range(ngrp):
        @pl.when(kg * SUBLANES <= last_blk)
        def _(kg=kg):
            for g in range(NSA_KV_GROUPS):
                rows = [imp_ref[g, i * SUBLANES:(i + 1) * SUBLANES, :] for i in range(ngrp)]
                part = [None] * ngrp
                for k in range(kg * SUBLANES, (kg + 1) * SUBLANES):
                    rk = rows[kg][k - kg * SUBLANES:k - kg * SUBLANES + 1, :]
                    for i in range(ngrp):
                        if i > kg:
                            one = jnp.where(rk >= rows[i], 1.0, 0.0)
                        elif i < kg:
                            one = jnp.where(rk > rows[i], 1.0, 0.0)
                        else:
                            one = jnp.where(sub8 > k - i * SUBLANES,
                                            jnp.where(rk >= rows[i], 1.0, 0.0), jnp.where(rk > rows[i], 1.0, 0.0))
                        part[i] = one if part[i] is None else part[i] + one
                for i in range(ngrp):
                    cnt_ref[g, i * SUBLANES:(i + 1) * SUBLANES, :] += part[i]
    pen_t = [jnp.where(cnt_ref[g] < float(top_n), 0.0, MASKED) for g in range(NSA_KV_GROUPS)]
    for g in range(NSA_KV_GROUPS):
        pen_ref[0, g * HEAD_DIM:g * HEAD_DIM + n_sel, :] = pen_t[g].astype(BF16)
        if n_sel < HEAD_DIM:
            pen_ref[0, g * HEAD_DIM + n_sel:(g + 1) * HEAD_DIM, :] = jnp.zeros((HEAD_DIM - n_sel, tq), BF16)
    picked = jnp.where(jnp.maximum(pen_t[0], pen_t[1]) == 0.0, 1.0, 0.0)
    bpt = TK // SLC_BLOCK
    lane1 = lax.broadcasted_iota(jnp.int32, (1, LANES), 1)
    used = jnp.zeros((1, LANES), F32)
    for kt in range(n_sel // bpt):
        hit = jnp.max(picked[kt * bpt:(kt + 1) * bpt, :], axis=(0, 1), keepdims=True)
        used = jnp.where(lane1 == kt, hit, used)
    used_ref[0, 0] = used


def _cmp_attn(feat, kcmp, vcmp, small3, ov, n_cmp, n_sel, top_n, tq=TQ):
    b, t, _ = small3.shape
    nck = kcmp.shape[1]
    nq = t // tq
    assert n_sel <= HEAD_DIM
    return pl.pallas_call(
        functools.partial(_cmp_kernel, tq=tq, n_cmp=n_cmp, n_sel=n_sel, top_n=top_n),
        grid=(b, nq),
        in_specs=[pl.BlockSpec((NSA_Q, tq), lambda i, j: (ROW_QN, i * nq + j)),
                  pl.BlockSpec((1, nck, LANES), lambda i, j: (i, 0, 0)),
                  pl.BlockSpec((1, nck, LANES), lambda i, j: (i, 0, 0)),
                  pl.BlockSpec((1, tq, LANES), lambda i, j: (i, j, 0)),
                  pl.BlockSpec((LANES, nck), lambda i, j: (0, 0))],
        out_specs=[pl.BlockSpec((1, tq, NSA_Q), lambda i, j: (i, j, 0)),
                   pl.BlockSpec((1, NSA_KV_GROUPS * HEAD_DIM, tq), lambda i, j: (i, 0, j)),
                   pl.BlockSpec((1, 1, 1, LANES), lambda i, j: (i, j, 0, 0))],
        out_shape=[jax.ShapeDtypeStruct((b, t, NSA_Q), BF16),
                   jax.ShapeDtypeStruct((b, NSA_KV_GROUPS * HEAD_DIM, t), BF16),
                   jax.ShapeDtypeStruct((b, t // tq, 1, LANES), F32)],
        scratch_shapes=[pltpu.VMEM((NSA_KV_GROUPS, n_sel, tq), F32),
                        pltpu.VMEM((NSA_KV_GROUPS, n_sel, tq), F32)],
        compiler_params=_cparams("parallel", "parallel"),
        name="cmp_attn",
    )(feat, kcmp, vcmp, small3, ov)


def _bf16_terms(c, n=3):
    out, r = [], float(c)
    for _ in range(n):
        p = float(np.float32(r).astype(ml_dtypes.bfloat16))
        out.append(p)
        r -= p
    return out


def _alibi_q_aug(idx, a0, slope):
    aug = jnp.zeros(idx.shape, F32)
    for i, c in enumerate(_bf16_terms(slope * LOG2E)):
        aug = jnp.where(idx == a0 + 2 * i, float(SLC_BLOCK) * c, aug)
        aug = jnp.where(idx == a0 + 2 * i + 1, c, aug)
    return aug


def _alibi_k_aug(lane, rel, a0):
    hi = (rel >> SLC_SHIFT).astype(F32)
    lo = (rel & (SLC_BLOCK - 1)).astype(F32)
    inside = (lane >= a0) & (lane < a0 + 6)
    odd = ((lane - a0) & 1) == 1
    return jnp.where(inside, jnp.where(odd, lo, hi), 0.0)


ALIBI_LANE = {"slc": SLC_BLOCK, "win": 0, "diff": 0}
STACKS = {"slc": (1, 8), "win": (1, 8), "fox": (2, 2), "diff": (2, 4)}


def _v_half(mode, vh):
    nv = STACKS[mode][1]
    if mode in ("slc", "win"):
        return vh // (NSA_HEADS // NSA_KV_GROUPS)
    return (vh % nv) // (nv // 2)


def _flash_kernel(*refs, mode, tq, tk, lam_init, ntiles):
    sched_ref, refs = refs[0], refs[1:]
    if mode == "slc":
        use_ref, q_ref, k_ref, v_ref, pen_ref, sm_ref, add_ref, o_ref, qst_ref, m_ref, acc_ref = refs
    elif mode == "win":
        q_ref, k_ref, v_ref, sm_ref, add_ref, o_ref, qst_ref, m_ref, acc_ref = refs
    elif mode == "fox":
        q_ref, k_ref, v_ref, fa_ref, o_ref, qst_ref, m_ref, acc_ref = refs
    else:
        q_ref, k_ref, v_ref, lam_ref, sub_ref, o_ref, qst_ref, m_ref, acc_ref = refs
    nstack, nv = STACKS[mode]
    nsa = mode in ("slc", "win")
    hg = NSA_HEADS // NSA_KV_GROUPS
    step_id = pl.program_id(1)
    q0 = sched_ref[SCHED_Q, step_id] * tq
    k0 = sched_ref[SCHED_K, step_id] * tk

    @pl.when(sched_ref[SCHED_FIRST, step_id] == 1)
    def _():
        m_ref[...] = jnp.full(m_ref.shape, NEG_INF, F32)
        acc_ref[...] = jnp.zeros(acc_ref.shape, F32)
        feat = lax.broadcasted_iota(jnp.int32, (LANES, tq), 0)
        for vh in range(nstack * nv):
            if nsa:
                jb, g = vh % hg, vh // hg
                lo_row, width = g * HEAD_DIM, HEAD_DIM
                aug = _alibi_q_aug(feat, ALIBI_LANE[mode], _alibi(NSA_HEADS)[vh])
                if mode == "slc":
                    aug = aug + jnp.concatenate([pen_ref[0, g * HEAD_DIM:(g + 1) * HEAD_DIM, :].astype(F32),
                                                 jnp.zeros((LANES - HEAD_DIM, tq), F32)], axis=0)
            elif mode == "fox":
                jb, r = vh // nv, vh % nv
                lo_row, width = r * HEAD_DIM, HEAD_DIM
                hit = (feat == FGT_LANES[0] + vh) | (feat == FGT_LANES[1] + vh) | (feat == FGT_LANES[2] + vh)
                aug = jnp.where(hit, 1.0, 0.0)
            else:
                jb, r = vh // nv, vh % nv
                lo_row, width = r * DIFF_QK_DIM, DIFF_QK_DIM
                aug = _alibi_q_aug(feat, ALIBI_LANE[mode], _alibi(DIFF_HEADS)[vh // 2])
            sel = (feat >= lo_row) & (feat < lo_row + width)
            base = vh * 2 * LANES
            qst_ref[base:base + LANES, :] = _keep(sel, q_ref[jb * LANES:(jb + 1) * LANES, :])
            qst_ref[base + LANES:base + 2 * LANES, :] = aug.astype(BF16)

    def step(masked):
        lane_k = lax.broadcasted_iota(jnp.int32, (tk, LANES), 1)
        row_k = lax.broadcasted_iota(jnp.int32, (tk, LANES), 0)
        if mode == "fox":
            k_aug = fa_ref[0]
        else:
            ka = _alibi_k_aug(lane_k, (k0 - q0) + row_k, ALIBI_LANE[mode])
            if mode == "slc":
                ka = ka + jnp.where(((k0 + row_k) >> SLC_SHIFT) == lane_k, 1.0, 0.0)
            k_aug = ka.astype(BF16)
        if masked:
            dist = (q0 - k0) + lax.broadcasted_iota(jnp.int32, (tk, tq), 1) - lax.broadcasted_iota(jnp.int32, (tk, tq), 0)
            ok = dist >= 0
            if mode == "win":
                ok = ok & (dist < WINDOW)
            bias = jnp.where(ok, 0.0, MASKED)
        keys, vals_t = [], []
        ones = jnp.ones((ONES_ROWS, tk), BF16)
        for st in range(nstack):
            kblk = k_ref[0] if nsa else k_ref[0, :, st * LANES:(st + 1) * LANES]
            keys.append(jnp.concatenate([kblk, k_aug], axis=1))
            vals_t.append([jnp.concatenate([v_ref[st * LANES + hf * HEAD_DIM:st * LANES + (hf + 1) * HEAD_DIM, :],
                                            ones], axis=0) for hf in range(LANES // HEAD_DIM)])

        nstrip = tk // KSTRIP
        nvh = nstack * nv

        def scores(vh):
            s = _dot(keys[vh // nv], qst_ref[vh * 2 * LANES:(vh + 1) * 2 * LANES, :])
            if masked:
                s = s + bias
            return s, jnp.max(s, axis=0, keepdims=True)

        cur, cur_max = scores(0)
        for vh in range(nvh):
            cols = slice(vh * tq, (vh + 1) * tq)
            v_half = vals_t[vh // nv][_v_half(mode, vh)]
            m_old = m_ref[:, cols]
            m_new = jnp.maximum(m_old, cur_max)
            alpha = jnp.exp2(m_old - m_new)
            if vh + 1 < nvh:
                nxt, nxt_max = scores(vh + 1)
            pv = None
            for r in range(nstrip):
                p = jnp.exp2((cur[r * KSTRIP:(r + 1) * KSTRIP, :] - m_new).astype(BF16))
                part = _dot(v_half[:, r * KSTRIP:(r + 1) * KSTRIP], p)
                pv = part if pv is None else pv + part
            m_ref[:, cols] = m_new
            acc_ref[:, cols] = alpha * acc_ref[:, cols] + pv
            if vh + 1 < nvh:
                cur, cur_max = nxt, nxt_max

    is_masked = sched_ref[SCHED_MASKED, step_id] == 1
    if mode == "win":
        pl.when(is_masked)(lambda: step(True))
    elif mode == "slc":
        tile_id = (pl.program_id(0) * ntiles[0] + sched_ref[SCHED_Q, step_id]) * ntiles[1] + sched_ref[SCHED_K, step_id]
        needed = use_ref[tile_id] == 1
        pl.when(is_masked & needed)(lambda: step(True))
        pl.when(jnp.logical_not(is_masked) & needed)(lambda: step(False))
    else:
        pl.when(is_masked)(lambda: step(True))
        pl.when(jnp.logical_not(is_masked))(lambda: step(False))

    @pl.when(sched_ref[SCHED_LAST, step_id] == 1)
    def _():
        inv = 1.0 / acc_ref[HEAD_DIM:HEAD_DIM + 1, :]

        def out_t(vh):
            return acc_ref[0:HEAD_DIM, vh * tq:(vh + 1) * tq] * inv[:, vh * tq:(vh + 1) * tq]

        def pair(a, b):
            return jnp.concatenate([a, b], axis=0).T

        if nsa:
            sig_t = _sigmoid(sm_ref[0]).T
            br = 1 if mode == "slc" else 2
            for jb in range(hg):
                c0 = br * NSA_HEADS + jb
                o = pair(out_t(jb) * sig_t[c0:c0 + 1, :], out_t(jb + hg) * sig_t[c0 + hg:c0 + hg + 1, :])
                o_ref[0, :, jb * LANES:(jb + 1) * LANES] = (
                    add_ref[0, :, jb * LANES:(jb + 1) * LANES].astype(F32) + o).astype(o_ref.dtype)
        elif mode == "fox":
            for pb in range(nstack):
                o = pair(out_t(2 * pb), out_t(2 * pb + 1))
                o_ref[0, :, pb * LANES:(pb + 1) * LANES] = o.astype(o_ref.dtype)
        else:
            lam = lam_ref[...]
            lam_full = (jnp.exp(jnp.sum(lam[0:1, :] * lam[1:2, :], axis=-1, keepdims=True))
                        - jnp.exp(jnp.sum(lam[2:3, :] * lam[3:4, :], axis=-1, keepdims=True)) + lam_init)
            for pb in range(nstack):
                normed = []
                for hh in range(2):
                    d = out_t(pb * nv + 2 * hh) - lam_full * out_t(pb * nv + 2 * hh + 1)
                    ms = jnp.mean(d * d, axis=0, keepdims=True)
                    normed.append(d * lax.rsqrt(ms + RMS_EPS) * sub_ref[...] * (1.0 - lam_init))
                o_ref[0, :, pb * LANES:(pb + 1) * LANES] = pair(normed[0], normed[1]).astype(o_ref.dtype)


def _flash(main3, feat, mode, extra, lam_init=0.0, tile_used=None, tq=TQ, tk=TK):
    b, t, _ = main3.shape
    tk = min(tk, t)
    nq, nk = t // tq, t // tk
    nstack, nv = STACKS[mode]
    sched = _flash_schedule(t, tq, tk, mode == "win")
    prefetch = [jnp.asarray(sched)]
    if mode == "slc":
        prefetch.append(tile_used.reshape(-1))
    qtile = lambda w, c: pl.BlockSpec((1, tq, w), lambda bi, s, sch, *_: (bi, sch[SCHED_Q, s], c))
    ktile = lambda w, c: pl.BlockSpec((1, tk, w), lambda bi, s, sch, *_: (bi, sch[SCHED_K, s], c))
    qfeat = lambda rows, r: pl.BlockSpec((rows, tq), lambda bi, s, sch, *_: (r, bi * nq + sch[SCHED_Q, s]))
    vfeat = lambda rows, r: pl.BlockSpec((rows, tk), lambda bi, s, sch, *_: (r, bi * nk + sch[SCHED_K, s]))
    if mode in ("slc", "win"):
        kcol, vrow = (COL_KS, ROW_VS) if mode == "slc" else (COL_KW, ROW_VW)
        in_specs = [qfeat(NSA_Q, ROW_QN), ktile(LANES, kcol), vfeat(LANES, vrow)]
        if mode == "slc":
            in_specs.append(pl.BlockSpec((1, NSA_KV_GROUPS * HEAD_DIM, tq),
                                         lambda bi, s, sch, *_: (bi, 0, sch[SCHED_Q, s])))
        in_specs += [qtile(LANES, 0), qtile(NSA_Q, 0)]
        out_w = NSA_Q
    else:
        qr, kc, vr = (ROW_FQ, COL_FK, ROW_FV) if mode == "fox" else (ROW_DQ, COL_DK, ROW_DV)
        in_specs = [qfeat(2 * LANES, qr), ktile(2 * LANES, kc), vfeat(2 * LANES, vr)]
        if mode == "fox":
            in_specs.append(ktile(LANES, 0))
        else:
            in_specs += [pl.BlockSpec((4, DIFF_QK_DIM), lambda bi, s, sch, *_: (0, 0)),
                         pl.BlockSpec((HEAD_DIM, 1), lambda bi, s, sch, *_: (0, 0))]
        out_w = 2 * LANES
    cols = nstack * nv * tq
    return pl.pallas_call(
        functools.partial(_flash_kernel, mode=mode, tq=tq, tk=tk, lam_init=lam_init, ntiles=(t // tq, t // tk)),
        grid_spec=pltpu.PrefetchScalarGridSpec(
            num_scalar_prefetch=len(prefetch),
            grid=(b, sched.shape[1]),
            in_specs=in_specs,
            out_specs=qtile(out_w, 0),
            scratch_shapes=[pltpu.VMEM((nstack * nv * 2 * LANES, tq), BF16),
                            pltpu.VMEM((1, cols), F32),
                            pltpu.VMEM((HEAD_DIM + ONES_ROWS, cols), F32)]),
        out_shape=jax.ShapeDtypeStruct((b, t, out_w), BF16),
        compiler_params=_cparams("parallel", "arbitrary"),
        name="flash_" + mode,
    )(*prefetch, feat, main3, feat, *extra)


SCHED_Q, SCHED_K, SCHED_FIRST, SCHED_LAST, SCHED_MASKED = range(5)


def _flash_schedule(t, tq, tk, window):
    rows = []
    for i in range(t // tq):
        q_lo, q_hi = i * tq, i * tq + tq - 1
        k_first = max(q_lo - (WINDOW - 1), 0) // tk if window else 0
        tiles = list(range(k_first, q_hi // tk + 1))
        for n, kt in enumerate(tiles):
            fully_visible = kt * tk + tk - 1 <= q_lo and not window
            rows.append((i, kt, int(n == 0), int(n == len(tiles) - 1), int(not fully_visible)))
    return np.asarray(rows, np.int32).T


def _outproj_kernel(x_ref, oa_ref, ob_ref, oc_ref, wa_ref, wb_ref, wc_ref, o_ref):
    acc = _dot(oa_ref[...], wa_ref[...])
    acc = acc + _dot(ob_ref[...], wb_ref[...])
    acc = acc + _dot(oc_ref[...], wc_ref[...])
    o_ref[...] = x_ref[...] + acc


def _outproj(x2, oa, ob, oc, wa, wb, wc, tm=512):
    n = x2.shape[0]
    row = lambda i: (i, 0)
    full = lambda i: (0, 0)
    return pl.pallas_call(
        _outproj_kernel,
        grid=(n // tm,),
        in_specs=[pl.BlockSpec((tm, D_MODEL), row),
                  pl.BlockSpec((tm, NSA_Q), row),
                  pl.BlockSpec((tm, FOX_W), row),
                  pl.BlockSpec((tm, DIFF_W), row),
                  pl.BlockSpec((NSA_Q, D_MODEL), full),
                  pl.BlockSpec((FOX_W, D_MODEL), full),
                  pl.BlockSpec((DIFF_W, D_MODEL), full)],
        out_specs=pl.BlockSpec((tm, D_MODEL), row),
        out_shape=jax.ShapeDtypeStruct((n, D_MODEL), F32),
        compiler_params=_cparams("parallel"),
        name="outproj",
    )(x2, oa, ob, oc, wa, wb, wc)


ROUTE_OFF = SUBLANES
COL_GRP, COL_POS = EXPERTS_PER_GROUP, EXPERTS_PER_GROUP + 1


MOE_CHUNK = 304
MOE_PREFIX_BLOCK = 256


def _moe_kernel(x_ref, g_ref, wrh_ref, wrl_ref, br_ref, wg_ref, wu_ref, wd_ref, gf_ref, o_ref,
                h_ref, gsel_ref, col_ref, row_ref, cnt_ref, *, final_norm):
    grp_id = pl.program_id(1)
    tm = x_ref.shape[0]
    ch = MOE_CHUNK

    @pl.when(grp_id == 0)
    def _():
        x = x_ref[...]
        ms = jnp.mean(x * x, axis=-1, keepdims=True)
        y = x * lax.rsqrt(ms + RMS_EPS) * g_ref[...]
        hb, hl = _split2(y)
        h_ref[...] = hb
        lg = (_dot(hb, wrh_ref[...]) + _dot(hl, wrh_ref[...]) + _dot(hb, wrl_ref[...])) + br_ref[...]
        nrow = ROUTE_OFF + N_EXPERTS
        lt = lg.T[0:nrow, :]
        r = lax.broadcasted_iota(jnp.int32, (nrow, tm), 0).astype(F32)
        big = float(LANES)
        isg = r < N_GROUPS
        gmax = jnp.max(jnp.where(isg, lt, -jnp.inf), axis=0, keepdims=True)
        grp = jnp.min(jnp.where(isg & (lt == gmax), r, big), axis=0, keepdims=True)
        gprob = 1.0 / jnp.sum(jnp.where(isg, jnp.exp(lt - gmax), 0.0), axis=0, keepdims=True)
        lo_row = ROUTE_OFF + grp * EXPERTS_PER_GROUP
        ing = (r >= lo_row) & (r < lo_row + EXPERTS_PER_GROUP)
        v1 = jnp.max(jnp.where(ing, lt, -jnp.inf), axis=0, keepdims=True)
        i1 = jnp.min(jnp.where(ing & (lt == v1), r, big), axis=0, keepdims=True)
        rest = ing & (r != i1)
        v2 = jnp.max(jnp.where(rest, lt, -jnp.inf), axis=0, keepdims=True)
        i2 = jnp.min(jnp.where(rest & (lt == v2), r, big), axis=0, keepdims=True)
        e2 = jnp.exp(v2 - v1)
        w1 = gprob / (1.0 + e2)
        w2 = gprob * e2 / (1.0 + e2)
        gate_t = jnp.where(r == i1, w1, 0.0) + jnp.where(r == i2, w2, 0.0)
        epg = EXPERTS_PER_GROUP
        gsel_t = jnp.zeros((epg, tm), F32)
        for gg in range(N_GROUPS):
            gsel_t = jnp.where(grp == float(gg), gate_t[ROUTE_OFF + gg * epg:ROUTE_OFF + (gg + 1) * epg, :], gsel_t)
        r8 = lax.broadcasted_iota(jnp.int32, (SUBLANES, tm), 0).astype(F32)
        onehot_t = jnp.where(r8 == grp, 1.0, 0.0)
        pb = MOE_PREFIX_BLOCK
        tri = jnp.where(lax.broadcasted_iota(jnp.int32, (pb, pb), 0) < lax.broadcasted_iota(jnp.int32, (pb, pb), 1),
                        1.0, 0.0).astype(BF16)
        carry = jnp.zeros((SUBLANES, 1), F32)
        pos_parts = []
        for blk in range(tm // pb):
            oh_b = onehot_t[:, blk * pb:(blk + 1) * pb]
            prefix = _dot(oh_b.astype(BF16), tri) + carry
            pos_parts.append(jnp.sum(oh_b * prefix, axis=0, keepdims=True))
            carry = carry + jnp.sum(oh_b, axis=1, keepdims=True)
        pos = jnp.concatenate(pos_parts, axis=1)
        for gg in range(N_GROUPS):
            cnt_ref[gg] = carry[gg, 0].astype(jnp.int32)
        row_ref[...] = jnp.concatenate([grp, pos, jnp.zeros((SUBLANES - 2, tm), F32)], axis=0)
        stacked = jnp.concatenate([gsel_t, row_ref[...], jnp.zeros((LANES - 2 * SUBLANES, tm), F32)], axis=0)
        colinfo = stacked.T
        lane = lax.broadcasted_iota(jnp.int32, (tm, LANES), 1)
        ghi, glo = _split2(jnp.where(lane < epg, colinfo, 0.0))
        gsel_ref[:, 0:LANES] = ghi
        gsel_ref[:, LANES:2 * LANES] = glo
        col_ref[...] = colinfo
        o_ref[...] = x

    grp_f = grp_id.astype(F32)
    key_col = jnp.where(col_ref[:, COL_GRP:COL_GRP + 1] == grp_f, col_ref[:, COL_POS:COL_POS + 1], -1.0)
    key_row = jnp.where(row_ref[0:1, :] == grp_f, row_ref[1:2, :], -1.0)

    def chunk(c, carry):
        base = (c * ch).astype(F32)
        sub_iota = lax.broadcasted_iota(jnp.int32, (ch, tm), 0).astype(F32)
        gather = jnp.where(key_row - base == sub_iota, 1.0, 0.0).astype(BF16)
        xg = _dot(gather, h_ref[...]).astype(BF16)
        gparts = _dot(gather, gsel_ref[...])
        gates = gparts[:, 0:LANES] + gparts[:, LANES:2 * LANES]
        y = None
        for e in range(EXPERTS_PER_GROUP):
            a = _dot(xg, wg_ref[0, e])
            u = _dot(xg, wu_ref[0, e])
            act = a * _sigmoid(a) * u * gates[:, e:e + 1]
            part = _dot(act.astype(BF16), wd_ref[0, e])
            y = part if y is None else y + part
        chp = -(-ch // LANES) * LANES
        lane_iota = lax.broadcasted_iota(jnp.int32, (tm, chp), 1).astype(F32)
        scatter = jnp.where(key_col - base == lane_iota, 1.0, 0.0).astype(BF16)
        yb = y.astype(BF16)
        if chp > ch:
            yb = jnp.concatenate([yb, jnp.zeros((chp - ch, D_MODEL), BF16)], axis=0)
        o_ref[...] = o_ref[...] + _dot(scatter, yb)
        return carry

    lax.fori_loop(0, (cnt_ref[grp_id] + ch - 1) // ch, chunk, 0)

    @pl.when(grp_id == N_GROUPS - 1)
    def _():
        if final_norm:
            y = o_ref[...]
            ms = jnp.mean(y * y, axis=-1, keepdims=True)
            o_ref[...] = y * lax.rsqrt(ms + RMS_EPS) * gf_ref[...]


def _moe(x2, g, wrh, wrl, br, wg, wu, wd, gf, layer, final_norm, tm=1024):
    n = x2.shape[0]
    tm = min(tm, n)
    row = lambda i, e: (i, 0)
    full = lambda i, e: (0, 0)
    epg = EXPERTS_PER_GROUP
    wspec_in = pl.BlockSpec((1, epg, D_MODEL, D_EXPERT), lambda i, e: (layer, e, 0, 0))
    wspec_out = pl.BlockSpec((1, epg, D_EXPERT, D_MODEL), lambda i, e: (layer, e, 0, 0))
    return pl.pallas_call(
        functools.partial(_moe_kernel, final_norm=final_norm),
        grid=(n // tm, N_GROUPS),
        in_specs=[pl.BlockSpec((tm, D_MODEL), row, pipeline_mode=pl.Buffered(1)),
                  pl.BlockSpec((1, D_MODEL), full),
                  pl.BlockSpec((D_MODEL, LANES), full),
                  pl.BlockSpec((D_MODEL, LANES), full),
                  pl.BlockSpec((1, LANES), full),
                  wspec_in, wspec_in, wspec_out,
                  pl.BlockSpec((1, D_MODEL), full)],
        out_specs=pl.BlockSpec((tm, D_MODEL), row),
        out_shape=jax.ShapeDtypeStruct((n, D_MODEL), F32),
        scratch_shapes=[pltpu.VMEM((tm, D_MODEL), BF16),
                        pltpu.VMEM((tm, 2 * LANES), BF16),
                        pltpu.VMEM((tm, LANES), F32),
                        pltpu.VMEM((8, tm), F32),
                        pltpu.SMEM((N_GROUPS,), jnp.int32)],
        compiler_params=_cparams("parallel", "arbitrary"),
        name="moe",
    )(x2, g, wrh, wrl, br, wg, wu, wd, gf)


def _head_perm():
    hg = NSA_HEADS // NSA_KV_GROUPS
    order = []
    for j in range(hg):
        order += [j, j + hg]
    return np.concatenate([np.arange(h * HEAD_DIM, (h + 1) * HEAD_DIM) for h in order])


def _split_hi_lo(w):
    hi = w.astype(BF16)
    return hi, (w - hi.astype(F32)).astype(BF16)


def _layer_params(l, w_in, b_gate, b_fgt, cmp_pe, cmp_w1, cmp_w2, w_out):
    offs = np.concatenate([[0], np.cumsum(IN_WIDTHS)])
    seg = [w_in[l][:, offs[i]:offs[i + 1]] for i in range(len(IN_WIDTHS))]
    (wq, wkc, wvc, wks, wvs, wkw, wvw, wgt, wfq, wfk, wfv, wff, wdq, wdk, wdv) = seg
    perm = _head_perm()
    gate_perm = np.array([h * 3 + br for br in range(3) for h in range(NSA_HEADS)])
    wm = jnp.concatenate([wks, wkw, wfk, wdk, wkc, wvc], axis=1).astype(BF16)
    wt = jnp.concatenate([wq[:, perm] * (HEAD_DIM ** -0.5 * LOG2E), wfq * (HEAD_DIM ** -0.5 * LOG2E),
                          wdq * (DIFF_QK_DIM ** -0.5 * LOG2E), wvs, wvw, wfv, wdv], axis=1).T.astype(BF16)
    pad = LANES - NSA_GATE - 3 * FOX_HEADS
    ws = jnp.concatenate([wgt[:, gate_perm], wff, wff, wff, jnp.zeros((D_MODEL, pad), F32)], axis=1)
    wsh, wsl = _split_hi_lo(ws)
    bs = jnp.concatenate([b_gate[l][gate_perm], b_fgt[l], b_fgt[l], b_fgt[l], jnp.zeros((pad,), F32)])[None, :]

    eye = jnp.eye(NSA_KV_GROUPS, dtype=F32)
    half = CMP_BLOCK // 2
    cmp = []
    for pe, w1, w2 in zip(cmp_pe, cmp_w1, cmp_w2):
        w1r = w1[l].reshape(2, half, HEAD_DIM, CMP_HIDDEN)
        w1x = jnp.einsum('srdk,gh->srgdhk', w1r, eye).reshape(2, half * NSA_KV, NSA_KV_GROUPS * CMP_HIDDEN)
        w2bd = jnp.einsum('kd,gh->gkhd', w2[l], eye).reshape(NSA_KV_GROUPS * CMP_HIDDEN, NSA_KV)
        pe2 = jnp.broadcast_to(pe[l].reshape(2, half, 1, HEAD_DIM), (2, half, NSA_KV_GROUPS, HEAD_DIM))
        cmp.append((pe2.reshape(2, half * NSA_KV), w1x[0].astype(BF16), w1x[1].astype(BF16), w2bd.astype(BF16)))

    wo = w_out[l]
    wa = wo[:NSA_Q][perm].astype(BF16)
    wb = wo[NSA_Q:NSA_Q + FOX_W].astype(BF16)
    wc = wo[NSA_Q + FOX_W:].astype(BF16)
    return dict(wm=wm, wt=wt, wsh=wsh, wsl=wsl, bs=bs, cmp_k=cmp[0], cmp_v=cmp[1], wa=wa, wb=wb, wc=wc)


def _overlap_matrix(n_slab, n_cmp, n_sel):
    c_start = np.arange(n_cmp) * CMP_STRIDE
    s_start = np.arange(n_sel) * SLC_BLOCK
    ov = np.clip(np.minimum(c_start[:, None] + CMP_BLOCK, s_start[None, :] + SLC_BLOCK)
                 - np.maximum(c_start[:, None], s_start[None, :]), 0, None) / CMP_BLOCK
    full = np.zeros((LANES, n_slab), np.float32)
    full[:n_sel, :n_cmp] = ov.T
    return jnp.asarray(full, dtype=BF16)


def kernel(x, norm_attn, w_in, b_gate, b_fgt, cmp_k_pe, cmp_k_w1, cmp_k_w2, cmp_v_pe, cmp_v_w1, cmp_v_w2, diff_lambda, diff_subln, w_out, norm_ffn, w_grp, b_grp, w_exp, b_exp, w_e_gate, w_e_up, w_e_down, norm_final):
    b, t, d = x.shape
    depth = w_in.shape[0]
    n = b * t
    n_slab = t // CMP_STRIDE
    n_cmp = (t - CMP_BLOCK) // CMP_STRIDE + 1
    n_sel = t // SLC_BLOCK
    top_n = min(SLC_TOPK, n_sel)
    ov = _overlap_matrix(n_slab, n_cmp, n_sel)

    wg_all = w_e_gate.astype(BF16)
    wu_all = w_e_up.astype(BF16)
    wd_all = w_e_down.astype(BF16)
    gf = norm_final[None, :]

    x2 = x.reshape(n, d)
    for l in range(depth):
        p = _layer_params(l, w_in, b_gate, b_fgt, (cmp_k_pe, cmp_v_pe), (cmp_k_w1, cmp_v_w1),
                          (cmp_k_w2, cmp_v_w2), w_out)
        main, feat, kc, vc, small = _inproj(x2, norm_attn[l][None, :], p["wm"], p["wt"], p["wsh"], p["wsl"], p["bs"])
        main3 = main.reshape(b, t, MAIN_W)
        small3 = small.reshape(b, t, LANES)
        kcmp = _compress(kc.reshape(b, n_slab, CMP_STRIDE * NSA_KV), *p["cmp_k"])
        vcmp = _compress(vc.reshape(b, n_slab, CMP_STRIDE * NSA_KV), *p["cmp_v"])
        o_nsa, pen, used = _cmp_attn(feat, kcmp, vcmp, small3, ov, n_cmp, n_sel, top_n)
        nkt = t // min(TK, t)
        tile_used = used.reshape(b, t // TQ, -1, LANES).max(axis=2)[:, :, :nkt].astype(jnp.int32)
        o_nsa = _flash(main3, feat, "slc", (pen, small3, o_nsa), tile_used=tile_used)
        o_nsa = _flash(main3, feat, "win", (small3, o_nsa))
        o_fox = _flash(main3, feat, "fox", (_cumgate(small3),))
        o_diff = _flash(main3, feat, "diff", (diff_lambda[l], diff_subln[l][:, None]), lam_init=_lambda_init(l))

        x2 = _outproj(x2, o_nsa.reshape(n, NSA_Q), o_fox.reshape(n, FOX_W), o_diff.reshape(n, DIFF_W),
                      p["wa"], p["wb"], p["wc"])

        gpad, epad = ROUTE_OFF - N_GROUPS, LANES - ROUTE_OFF - N_EXPERTS
        wr = jnp.concatenate([w_grp[l], jnp.zeros((d, gpad), F32), w_exp[l], jnp.zeros((d, epad), F32)], axis=1)
        wrh, wrl = _split_hi_lo(wr)
        br = jnp.concatenate([b_grp[l], jnp.zeros((gpad,), F32), b_exp[l], jnp.zeros((epad,), F32)])[None, :]
        x2 = _moe(x2, norm_ffn[l][None, :], wrh, wrl, br, wg_all, wu_all, wd_all, gf, l,
                  final_norm=(l == depth - 1))
    return x2.reshape(b, t, d)
```

```python
import functools
import math

import ml_dtypes
import numpy as np
import jax
import jax.numpy as jnp
from jax import lax
from jax.experimental import pallas as pl
from jax.experimental.pallas import tpu as pltpu

F32 = jnp.float32
BF16 = jnp.bfloat16

D_MODEL = 1024
HEAD_DIM = 64
NSA_HEADS = 8
NSA_KV_GROUPS = 2
CMP_BLOCK = 32
CMP_STRIDE = 16
CMP_HIDDEN = 256
SLC_BLOCK = 64
SLC_SHIFT = 6
SLC_TOPK = 16
WINDOW = 512
FOX_HEADS = 4
DIFF_HEADS = 4
DIFF_QK_DIM = HEAD_DIM // 2
N_GROUPS = 4
EXPERTS_PER_GROUP = 8
N_EXPERTS = N_GROUPS * EXPERTS_PER_GROUP
D_EXPERT = 256
RMS_EPS = 1e-6
FORCE_SCORE = 1e4
NEG_INF = -1e30
MASKED = -2e30

NSA_Q = NSA_HEADS * HEAD_DIM
NSA_KV = NSA_KV_GROUPS * HEAD_DIM
NSA_GATE = NSA_HEADS * 3
FOX_W = FOX_HEADS * HEAD_DIM
DIFF_W = DIFF_HEADS * HEAD_DIM
MIX_WIDTH = NSA_Q + FOX_W + DIFF_W
IN_WIDTHS = (NSA_Q, NSA_KV, NSA_KV, NSA_KV, NSA_KV, NSA_KV, NSA_KV, NSA_GATE,
             FOX_W, FOX_W, FOX_W, FOX_HEADS, DIFF_W, DIFF_W, DIFF_W)

LANES = 128
SUBLANES = 8
VMEM_LIMIT_BYTES = 56 * 1024 * 1024

MAIN_W = 2 * NSA_KV + FOX_W + DIFF_W
PROJ_W = MAIN_W + 2 * NSA_KV
COL_KS, COL_KW = 0, 1
COL_FK, COL_DK = 1, 2
ROWS_T = NSA_Q + FOX_W + DIFF_W + 2 * NSA_KV + FOX_W + DIFF_W
ROW_QN, ROW_FQ, ROW_DQ = 0, 2, 3
ROW_VS, ROW_VW = 8, 9
ROW_FV, ROW_DV = 5, 6
FGT_LANES = (NSA_GATE, NSA_GATE + FOX_HEADS, NSA_GATE + 2 * FOX_HEADS)
LOG2E = 1.4426950408889634

TQ = 512
TK = 512
KSTRIP = 512
ONES_ROWS = 16

def _alibi(n):
    return [float(2.0 ** (-8.0 * (i + 1) / n)) for i in range(n)]


def _lambda_init(layer):
    return 0.8 - 0.6 * math.exp(-0.3 * layer)


def _cparams(*sem):
    return pltpu.CompilerParams(dimension_semantics=sem, vmem_limit_bytes=VMEM_LIMIT_BYTES)


def _dot(a, b):
    return jnp.dot(a, b, preferred_element_type=F32)


def _dot_nt(a, b):
    return lax.dot_general(a, b, (((1,), (1,)), ((), ())), preferred_element_type=F32)


def _split2(x):
    hi = x.astype(BF16)
    lo = (x - hi.astype(F32)).astype(BF16)
    return hi, lo


def _sigmoid(z):
    return 1.0 / (1.0 + jnp.exp(-z))


def _keep(sel, blk):
    return jnp.where(sel, blk.astype(F32), 0.0).astype(BF16)


def _inproj_kernel(x_ref, g_ref, wm_ref, wt_ref, wsh_ref, wsl_ref, bs_ref, main_ref, feat_ref, kc_ref, vc_ref,
                   small_ref):
    x = x_ref[...]
    ms = jnp.mean(x * x, axis=-1, keepdims=True)
    y = x * lax.rsqrt(ms + RMS_EPS) * g_ref[...]
    hb, hl = _split2(y)
    feat_ref[...] = _dot_nt(wt_ref[...], hb).astype(BF16)
    full = _dot(hb, wm_ref[...]).astype(BF16)
    main_ref[...] = full[:, :MAIN_W]
    kc_ref[...] = full[:, MAIN_W:MAIN_W + NSA_KV]
    vc_ref[...] = full[:, MAIN_W + NSA_KV:]
    small_ref[...] = (_dot(hb, wsh_ref[...]) + _dot(hl, wsh_ref[...]) + _dot(hb, wsl_ref[...])) + bs_ref[...]


def _inproj(x2, g, wm, wt, wsh, wsl, bs, tm=1024):
    n = x2.shape[0]
    full = lambda i: (0, 0)
    return pl.pallas_call(
        _inproj_kernel,
        grid=(n // tm,),
        in_specs=[pl.BlockSpec((tm, D_MODEL), lambda i: (i, 0)),
                  pl.BlockSpec((1, D_MODEL), full),
                  pl.BlockSpec((D_MODEL, PROJ_W), full),
                  pl.BlockSpec((ROWS_T, D_MODEL), full),
                  pl.BlockSpec((D_MODEL, LANES), full),
                  pl.BlockSpec((D_MODEL, LANES), full),
                  pl.BlockSpec((1, LANES), full)],
        out_specs=[pl.BlockSpec((tm, MAIN_W), lambda i: (i, 0)),
                   pl.BlockSpec((ROWS_T, tm), lambda i: (0, i)),
                   pl.BlockSpec((tm, NSA_KV), lambda i: (i, 0)),
                   pl.BlockSpec((tm, NSA_KV), lambda i: (i, 0)),
                   pl.BlockSpec((tm, LANES), lambda i: (i, 0))],
        out_shape=[jax.ShapeDtypeStruct((n, MAIN_W), BF16),
                   jax.ShapeDtypeStruct((ROWS_T, n), BF16),
                   jax.ShapeDtypeStruct((n, NSA_KV), BF16),
                   jax.ShapeDtypeStruct((n, NSA_KV), BF16),
                   jax.ShapeDtypeStruct((n, LANES), F32)],
        compiler_params=_cparams("parallel"),
        name="inproj",
    )(x2, g, wm, wt, wsh, wsl, bs)


def _compress_kernel(r_ref, pe_ref, w1a_ref, w1b_ref, w2_ref, o_ref):
    r = r_ref[0].astype(F32)
    pe = pe_ref[...]
    ra = (r + pe[0:1, :]).astype(BF16)
    rb = (r + pe[1:2, :]).astype(BF16)
    a = _dot(ra, w1a_ref[...])
    b = _dot(rb, w1b_ref[...])
    n_slab = a.shape[0]
    hid = a + pltpu.roll(b, n_slab - 1, 0)
    hid = jax.nn.gelu(hid)
    o_ref[0] = _dot(hid.astype(BF16), w2_ref[...]).astype(BF16)


def _compress(r, pe2, w1a, w1b, w2bd):
    b, n_slab, w = r.shape
    full = lambda i: (0, 0)
    return pl.pallas_call(
        _compress_kernel,
        grid=(b,),
        in_specs=[pl.BlockSpec((1, n_slab, w), lambda i: (i, 0, 0)),
                  pl.BlockSpec((2, w), full),
                  pl.BlockSpec((w, 2 * CMP_HIDDEN), full),
                  pl.BlockSpec((w, 2 * CMP_HIDDEN), full),
                  pl.BlockSpec((2 * CMP_HIDDEN, LANES), full)],
        out_specs=pl.BlockSpec((1, n_slab, LANES), lambda i: (i, 0, 0)),
        out_shape=jax.ShapeDtypeStruct((b, n_slab, LANES), BF16),
        compiler_params=_cparams("parallel"),
        name="compress",
    )(r, pe2, w1a, w1b, w2bd)


def _cumgate_kernel(z_ref, o_ref, carry_ref, *, tc):
    @pl.when(pl.program_id(1) == 0)
    def _():
        carry_ref[...] = jnp.zeros_like(carry_ref)

    z = z_ref[0]
    logf = -(jnp.maximum(-z, 0.0) + jnp.log(1.0 + jnp.exp(-jnp.abs(z))))
    hi = logf.astype(BF16)
    r1 = logf - hi.astype(F32)
    mid = r1.astype(BF16)
    lo = (r1 - mid.astype(F32)).astype(BF16)
    tri = (lax.broadcasted_iota(jnp.int32, (tc, tc), 0) >= lax.broadcasted_iota(jnp.int32, (tc, tc), 1))
    tri = jnp.where(tri, 1.0, 0.0).astype(BF16)
    c = (_dot(tri, hi) + _dot(tri, mid)) + _dot(tri, lo) + carry_ref[0:1, :]
    carry_ref[...] = jnp.broadcast_to(c[tc - 1:tc, :], carry_ref.shape)
    v = c * (-LOG2E)
    p0 = v.astype(BF16)
    r1 = v - p0.astype(F32)
    p1 = r1.astype(BF16)
    p2 = (r1 - p1.astype(F32)).astype(BF16)
    lane = lax.broadcasted_iota(jnp.int32, v.shape, 1)
    pieces = jnp.where(lane < FGT_LANES[1], p0.astype(F32), jnp.where(lane < FGT_LANES[2], p1.astype(F32), p2.astype(F32)))
    keep = (lane >= FGT_LANES[0]) & (lane < FGT_LANES[2] + FOX_HEADS)
    o_ref[0] = jnp.where(keep, pieces, 0.0).astype(BF16)


def _cumgate(small3, tc=1024):
    b, t, w = small3.shape
    return pl.pallas_call(
        functools.partial(_cumgate_kernel, tc=tc),
        grid=(b, t // tc),
        in_specs=[pl.BlockSpec((1, tc, w), lambda i, j: (i, j, 0))],
        out_specs=pl.BlockSpec((1, tc, w), lambda i, j: (i, j, 0)),
        out_shape=jax.ShapeDtypeStruct((b, t, w), BF16),
        scratch_shapes=[pltpu.VMEM((8, w), F32)],
        compiler_params=_cparams("parallel", "arbitrary"),
        name="cumgate",
    )(small3)


def _cmp_kernel(q_ref, kc_ref, vc_ref, sm_ref, ovt_ref, o_ref, pen_ref, used_ref, imp_ref, cnt_ref,
                *, tq, n_cmp, n_sel, top_n):
    t0 = pl.program_id(1) * tq
    nck = kc_ref.shape[1]
    hg = NSA_HEADS // NSA_KV_GROUPS
    slopes = _alibi(NSA_HEADS)
    lane_k = lax.broadcasted_iota(jnp.int32, (nck, LANES), 1)
    end_k = lax.broadcasted_iota(jnp.int32, (nck, LANES), 0) * CMP_STRIDE + (CMP_BLOCK - 1)
    keys = jnp.concatenate([kc_ref[0], _alibi_k_aug(lane_k, end_k - t0, 0).astype(BF16)], axis=1)
    blk = lax.broadcasted_iota(jnp.int32, (nck, tq), 0)
    qpos = t0 + lax.broadcasted_iota(jnp.int32, (nck, tq), 1)
    visible = (qpos >= blk * CMP_STRIDE + (CMP_BLOCK - 1)) & (blk < n_cmp)
    bias = jnp.where(visible, 0.0, MASKED)
    v_t = vc_ref[0].astype(F32).T.astype(BF16)
    sig_t = _sigmoid(sm_ref[0]).T
    feat = lax.broadcasted_iota(jnp.int32, (LANES, tq), 0)
    psum = [None, None]
    outs = []
    for h in range(NSA_HEADS):
        jb, g = h % hg, h // hg
        sel = (feat >= g * HEAD_DIM) & (feat < (g + 1) * HEAD_DIM)
        qa = jnp.concatenate([_keep(sel, q_ref[jb * LANES:(jb + 1) * LANES, :]),
                              _alibi_q_aug(feat, 0, slopes[h]).astype(BF16)], axis=0)
        s = _dot(keys, qa) + bias
        m = jnp.maximum(jnp.max(s, axis=0, keepdims=True), NEG_INF)
        p = jnp.exp2(s - m)
        l = jnp.sum(p, axis=0, keepdims=True)
        p = p * jnp.where(l > 0.0, 1.0 / l, 0.0)
        psum[g] = p if psum[g] is None else psum[g] + p
        outs.append(_dot(v_t[g * HEAD_DIM:(g + 1) * HEAD_DIM, :], p.astype(BF16)) * sig_t[h:h + 1, :])
    for jb in range(hg):
        o_ref[0, :, jb * LANES:(jb + 1) * LANES] = jnp.concatenate(
            [outs[jb], outs[jb + hg]], axis=0).T.astype(o_ref.dtype)

    sub8 = lax.broadcasted_iota(jnp.int32, (SUBLANES, tq), 0)
    row_s = lax.broadcasted_iota(jnp.int32, (n_sel, tq), 0)
    cur = (t0 + lax.broadcasted_iota(jnp.int32, (n_sel, tq), 1)) >> SLC_SHIFT
    for g in range(NSA_KV_GROUPS):
        ph, plo = _split2(psum[g])
        imp_t = (_dot(ovt_ref[...], ph) + _dot(ovt_ref[...], plo))[0:n_sel, :]
        forced = (row_s == 0) | (row_s == cur) | (row_s == cur - 1)
        imp_ref[g] = jnp.where(row_s <= cur, jnp.where(forced, FORCE_SCORE, imp_t), NEG_INF)
        cnt_ref[g] = jnp.zeros((n_sel, tq), F32)
    ngrp = n_sel // SUBLANES
    last_blk = (t0 + tq - 1) >> SLC_SHIFT
    for kg in range(ngrp):
        @pl.when(kg * SUBLANES <= last_blk)
        def _(kg=kg):
            for g in range(NSA_KV_GROUPS):
                rows = [imp_ref[g, i * SUBLANES:(i + 1) * SUBLANES, :] for i in range(ngrp)]
                part = [None] * ngrp
                for k in range(kg * SUBLANES, (kg + 1) * SUBLANES):
                    rk = rows[kg][k - kg * SUBLANES:k - kg * SUBLANES + 1, :]
                    for i in range(ngrp):
                        if i > kg:
                            one = jnp.where(rk >= rows[i], 1.0, 0.0)
                        elif i < kg:
                            one = jnp.where(rk > rows[i], 1.0, 0.0)
                        else:
                            one = jnp.where(sub8 > k - i * SUBLANES,
                                            jnp.where(rk >= rows[i], 1.0, 0.0), jnp.where(rk > rows[i], 1.0, 0.0))
                        part[i] = one if part[i] is None else part[i] + one
                for i in range(ngrp):
                    cnt_ref[g, i * SUBLANES:(i + 1) * SUBLANES, :] += part[i]
    pen_t = [jnp.where(cnt_ref[g] < float(top_n), 0.0, MASKED) for g in range(NSA_KV_GROUPS)]
    for g in range(NSA_KV_GROUPS):
        pen_ref[0, g * HEAD_DIM:g * HEAD_DIM + n_sel, :] = pen_t[g].astype(BF16)
        if n_sel < HEAD_DIM:
            pen_ref[0, g * HEAD_DIM + n_sel:(g + 1) * HEAD_DIM, :] = jnp.zeros((HEAD_DIM - n_sel, tq), BF16)
    picked = jnp.where(jnp.maximum(pen_t[0], pen_t[1]) == 0.0, 1.0, 0.0)
    bpt = TK // SLC_BLOCK
    lane1 = lax.broadcasted_iota(jnp.int32, (1, LANES), 1)
    used = jnp.zeros((1, LANES), F32)
    for kt in range(n_sel // bpt):
        hit = jnp.max(picked[kt * bpt:(kt + 1) * bpt, :], axis=(0, 1), keepdims=True)
        used = jnp.where(lane1 == kt, hit, used)
    used_ref[0, 0] = used


def _cmp_attn(feat, kcmp, vcmp, small3, ov, n_cmp, n_sel, top_n, tq=TQ):
    b, t, _ = small3.shape
    nck = kcmp.shape[1]
    nq = t // tq
    assert n_sel <= HEAD_DIM
    return pl.pallas_call(
        functools.partial(_cmp_kernel, tq=tq, n_cmp=n_cmp, n_sel=n_sel, top_n=top_n),
        grid=(b, nq),
        in_specs=[pl.BlockSpec((NSA_Q, tq), lambda i, j: (ROW_QN, i * nq + j)),
                  pl.BlockSpec((1, nck, LANES), lambda i, j: (i, 0, 0)),
                  pl.BlockSpec((1, nck, LANES), lambda i, j: (i, 0, 0)),
                  pl.BlockSpec((1, tq, LANES), lambda i, j: (i, j, 0)),
                  pl.BlockSpec((LANES, nck), lambda i, j: (0, 0))],
        out_specs=[pl.BlockSpec((1, tq, NSA_Q), lambda i, j: (i, j, 0)),
                   pl.BlockSpec((1, NSA_KV_GROUPS * HEAD_DIM, tq), lambda i, j: (i, 0, j)),
                   pl.BlockSpec((1, 1, 1, LANES), lambda i, j: (i, j, 0, 0))],
        out_shape=[jax.ShapeDtypeStruct((b, t, NSA_Q), BF16),
                   jax.ShapeDtypeStruct((b, NSA_KV_GROUPS * HEAD_DIM, t), BF16),
                   jax.ShapeDtypeStruct((b, t // tq, 1, LANES), F32)],
        scratch_shapes=[pltpu.VMEM((NSA_KV_GROUPS, n_sel, tq), F32),
                        pltpu.VMEM((NSA_KV_GROUPS, n_sel, tq), F32)],
        compiler_params=_cparams("parallel", "parallel"),
        name="cmp_attn",
    )(feat, kcmp, vcmp, small3, ov)


def _bf16_terms(c, n=3):
    out, r = [], float(c)
    for _ in range(n):
        p = float(np.float32(r).astype(ml_dtypes.bfloat16))
        out.append(p)
        r -= p
    return out


def _alibi_q_aug(idx, a0, slope):
    aug = jnp.zeros(idx.shape, F32)
    for i, c in enumerate(_bf16_terms(slope * LOG2E)):
        aug = jnp.where(idx == a0 + 2 * i, float(SLC_BLOCK) * c, aug)
        aug = jnp.where(idx == a0 + 2 * i + 1, c, aug)
    return aug


def _alibi_k_aug(lane, rel, a0):
    hi = (rel >> SLC_SHIFT).astype(F32)
    lo = (rel & (SLC_BLOCK - 1)).astype(F32)
    inside = (lane >= a0) & (lane < a0 + 6)
    odd = ((lane - a0) & 1) == 1
    return jnp.where(inside, jnp.where(odd, lo, hi), 0.0)


ALIBI_LANE = {"slc": SLC_BLOCK, "win": 0, "diff": 0}
STACKS = {"slc": (1, 8), "win": (1, 8), "fox": (2, 2), "diff": (2, 4)}


def _v_half(mode, vh):
    nv = STACKS[mode][1]
    if mode in ("slc", "win"):
        return vh // (NSA_HEADS // NSA_KV_GROUPS)
    return (vh % nv) // (nv // 2)


def _flash_kernel(*refs, mode, tq, tk, lam_init, ntiles):
    sched_ref, refs = refs[0], refs[1:]
    if mode == "slc":
        use_ref, q_ref, k_ref, v_ref, pen_ref, sm_ref, add_ref, o_ref, qst_ref, m_ref, acc_ref = refs
    elif mode == "win":
        q_ref, k_ref, v_ref, sm_ref, add_ref, o_ref, qst_ref, m_ref, acc_ref = refs
    elif mode == "fox":
        q_ref, k_ref, v_ref, fa_ref, o_ref, qst_ref, m_ref, acc_ref = refs
    else:
        q_ref, k_ref, v_ref, lam_ref, sub_ref, o_ref, qst_ref, m_ref, acc_ref = refs
    nstack, nv = STACKS[mode]
    nsa = mode in ("slc", "win")
    hg = NSA_HEADS // NSA_KV_GROUPS
    step_id = pl.program_id(1)
    q0 = sched_ref[SCHED_Q, step_id] * tq
    k0 = sched_ref[SCHED_K, step_id] * tk

    @pl.when(sched_ref[SCHED_FIRST, step_id] == 1)
    def _():
        m_ref[...] = jnp.full(m_ref.shape, NEG_INF, F32)
        acc_ref[...] = jnp.zeros(acc_ref.shape, F32)
        feat = lax.broadcasted_iota(jnp.int32, (LANES, tq), 0)
        for vh in range(nstack * nv):
            if nsa:
                jb, g = vh % hg, vh // hg
                lo_row, width = g * HEAD_DIM, HEAD_DIM
                aug = _alibi_q_aug(feat, ALIBI_LANE[mode], _alibi(NSA_HEADS)[vh])
                if mode == "slc":
                    aug = aug + jnp.concatenate([pen_ref[0, g * HEAD_DIM:(g + 1) * HEAD_DIM, :].astype(F32),
                                                 jnp.zeros((LANES - HEAD_DIM, tq), F32)], axis=0)
            elif mode == "fox":
                jb, r = vh // nv, vh % nv
                lo_row, width = r * HEAD_DIM, HEAD_DIM
                hit = (feat == FGT_LANES[0] + vh) | (feat == FGT_LANES[1] + vh) | (feat == FGT_LANES[2] + vh)
                aug = jnp.where(hit, 1.0, 0.0)
            else:
                jb, r = vh // nv, vh % nv
                lo_row, width = r * DIFF_QK_DIM, DIFF_QK_DIM
                aug = _alibi_q_aug(feat, ALIBI_LANE[mode], _alibi(DIFF_HEADS)[vh // 2])
            sel = (feat >= lo_row) & (feat < lo_row + width)
            base = vh * 2 * LANES
            qst_ref[base:base + LANES, :] = _keep(sel, q_ref[jb * LANES:(jb + 1) * LANES, :])
            qst_ref[base + LANES:base + 2 * LANES, :] = aug.astype(BF16)

    def step(masked):
        lane_k = lax.broadcasted_iota(jnp.int32, (tk, LANES), 1)
        row_k = lax.broadcasted_iota(jnp.int32, (tk, LANES), 0)
        if mode == "fox":
            k_aug = fa_ref[0]
        else:
            ka = _alibi_k_aug(lane_k, (k0 - q0) + row_k, ALIBI_LANE[mode])
            if mode == "slc":
                ka = ka + jnp.where(((k0 + row_k) >> SLC_SHIFT) == lane_k, 1.0, 0.0)
            k_aug = ka.astype(BF16)
        if masked:
            dist = (q0 - k0) + lax.broadcasted_iota(jnp.int32, (tk, tq), 1) - lax.broadcasted_iota(jnp.int32, (tk, tq), 0)
            ok = dist >= 0
            if mode == "win":
                ok = ok & (dist < WINDOW)
            bias = jnp.where(ok, 0.0, MASKED)
        keys, vals_t = [], []
        ones = jnp.ones((ONES_ROWS, tk), BF16)
        for st in range(nstack):
            kblk = k_ref[0] if nsa else k_ref[0, :, st * LANES:(st + 1) * LANES]
            keys.append(jnp.concatenate([kblk, k_aug], axis=1))
            vals_t.append([jnp.concatenate([v_ref[st * LANES + hf * HEAD_DIM:st * LANES + (hf + 1) * HEAD_DIM, :],
                                            ones], axis=0) for hf in range(LANES // HEAD_DIM)])

        nstrip = tk // KSTRIP
        nvh = nstack * nv

        def scores(vh):
            s = _dot(keys[vh // nv], qst_ref[vh * 2 * LANES:(vh + 1) * 2 * LANES, :])
            if masked:
                s = s + bias
            return s, jnp.max(s, axis=0, keepdims=True)

        cur, cur_max = scores(0)
        for vh in range(nvh):
            cols = slice(vh * tq, (vh + 1) * tq)
            v_half = vals_t[vh // nv][_v_half(mode, vh)]
            m_old = m_ref[:, cols]
            m_new = jnp.maximum(m_old, cur_max)
            alpha = jnp.exp2(m_old - m_new)
            if vh + 1 < nvh:
                nxt, nxt_max = scores(vh + 1)
            pv = None
            for r in range(nstrip):
                p = jnp.exp2((cur[r * KSTRIP:(r + 1) * KSTRIP, :] - m_new).astype(BF16))
                part = _dot(v_half[:, r * KSTRIP:(r + 1) * KSTRIP], p)
                pv = part if pv is None else pv + part
            m_ref[:, cols] = m_new
            acc_ref[:, cols] = alpha * acc_ref[:, cols] + pv
            if vh + 1 < nvh:
                cur, cur_max = nxt, nxt_max

    is_masked = sched_ref[SCHED_MASKED, step_id] == 1
    if mode == "win":
        pl.when(is_masked)(lambda: step(True))
    elif mode == "slc":
        tile_id = (pl.program_id(0) * ntiles[0] + sched_ref[SCHED_Q, step_id]) * ntiles[1] + sched_ref[SCHED_K, step_id]
        needed = use_ref[tile_id] == 1
        pl.when(is_masked & needed)(lambda: step(True))
        pl.when(jnp.logical_not(is_masked) & needed)(lambda: step(False))
    else:
        pl.when(is_masked)(lambda: step(True))
        pl.when(jnp.logical_not(is_masked))(lambda: step(False))

    @pl.when(sched_ref[SCHED_LAST, step_id] == 1)
    def _():
        inv = 1.0 / acc_ref[HEAD_DIM:HEAD_DIM + 1, :]

        def out_t(vh):
            return acc_ref[0:HEAD_DIM, vh * tq:(vh + 1) * tq] * inv[:, vh * tq:(vh + 1) * tq]

        def pair(a, b):
            return jnp.concatenate([a, b], axis=0).T

        if nsa:
            sig_t = _sigmoid(sm_ref[0]).T
            br = 1 if mode == "slc" else 2
            for jb in range(hg):
                c0 = br * NSA_HEADS + jb
                o = pair(out_t(jb) * sig_t[c0:c0 + 1, :], out_t(jb + hg) * sig_t[c0 + hg:c0 + hg + 1, :])
                o_ref[0, :, jb * LANES:(jb + 1) * LANES] = (
                    add_ref[0, :, jb * LANES:(jb + 1) * LANES].astype(F32) + o).astype(o_ref.dtype)
        elif mode == "fox":
            for pb in range(nstack):
                o = pair(out_t(2 * pb), out_t(2 * pb + 1))
                o_ref[0, :, pb * LANES:(pb + 1) * LANES] = o.astype(o_ref.dtype)
        else:
            lam = lam_ref[...]
            lam_full = (jnp.exp(jnp.sum(lam[0:1, :] * lam[1:2, :], axis=-1, keepdims=True))
                        - jnp.exp(jnp.sum(lam[2:3, :] * lam[3:4, :], axis=-1, keepdims=True)) + lam_init)
            for pb in range(nstack):
                normed = []
                for hh in range(2):
                    d = out_t(pb * nv + 2 * hh) - lam_full * out_t(pb * nv + 2 * hh + 1)
                    ms = jnp.mean(d * d, axis=0, keepdims=True)
                    normed.append(d * lax.rsqrt(ms + RMS_EPS) * sub_ref[...] * (1.0 - lam_init))
                o_ref[0, :, pb * LANES:(pb + 1) * LANES] = pair(normed[0], normed[1]).astype(o_ref.dtype)


def _flash(main3, feat, mode, extra, lam_init=0.0, tile_used=None, tq=TQ, tk=TK):
    b, t, _ = main3.shape
    tk = min(tk, t)
    nq, nk = t // tq, t // tk
    nstack, nv = STACKS[mode]
    sched = _flash_schedule(t, tq, tk, mode == "win")
    prefetch = [jnp.asarray(sched)]
    if mode == "slc":
        prefetch.append(tile_used.reshape(-1))
    qtile = lambda w, c: pl.BlockSpec((1, tq, w), lambda bi, s, sch, *_: (bi, sch[SCHED_Q, s], c))
    ktile = lambda w, c: pl.BlockSpec((1, tk, w), lambda bi, s, sch, *_: (bi, sch[SCHED_K, s], c))
    qfeat = lambda rows, r: pl.BlockSpec((rows, tq), lambda bi, s, sch, *_: (r, bi * nq + sch[SCHED_Q, s]))
    vfeat = lambda rows, r: pl.BlockSpec((rows, tk), lambda bi, s, sch, *_: (r, bi * nk + sch[SCHED_K, s]))
    if mode in ("slc", "win"):
        kcol, vrow = (COL_KS, ROW_VS) if mode == "slc" else (COL_KW, ROW_VW)
        in_specs = [qfeat(NSA_Q, ROW_QN), ktile(LANES, kcol), vfeat(LANES, vrow)]
        if mode == "slc":
            in_specs.append(pl.BlockSpec((1, NSA_KV_GROUPS * HEAD_DIM, tq),
                                         lambda bi, s, sch, *_: (bi, 0, sch[SCHED_Q, s])))
        in_specs += [qtile(LANES, 0), qtile(NSA_Q, 0)]
        out_w = NSA_Q
    else:
        qr, kc, vr = (ROW_FQ, COL_FK, ROW_FV) if mode == "fox" else (ROW_DQ, COL_DK, ROW_DV)
        in_specs = [qfeat(2 * LANES, qr), ktile(2 * LANES, kc), vfeat(2 * LANES, vr)]
        if mode == "fox":
            in_specs.append(ktile(LANES, 0))
        else:
            in_specs += [pl.BlockSpec((4, DIFF_QK_DIM), lambda bi, s, sch, *_: (0, 0)),
                         pl.BlockSpec((HEAD_DIM, 1), lambda bi, s, sch, *_: (0, 0))]
        out_w = 2 * LANES
    cols = nstack * nv * tq
    return pl.pallas_call(
        functools.partial(_flash_kernel, mode=mode, tq=tq, tk=tk, lam_init=lam_init, ntiles=(t // tq, t // tk)),
        grid_spec=pltpu.PrefetchScalarGridSpec(
            num_scalar_prefetch=len(prefetch),
            grid=(b, sched.shape[1]),
            in_specs=in_specs,
            out_specs=qtile(out_w, 0),
            scratch_shapes=[pltpu.VMEM((nstack * nv * 2 * LANES, tq), BF16),
                            pltpu.VMEM((1, cols), F32),
                            pltpu.VMEM((HEAD_DIM + ONES_ROWS, cols), F32)]),
        out_shape=jax.ShapeDtypeStruct((b, t, out_w), BF16),
        compiler_params=_cparams("parallel", "arbitrary"),
        name="flash_" + mode,
    )(*prefetch, feat, main3, feat, *extra)


SCHED_Q, SCHED_K, SCHED_FIRST, SCHED_LAST, SCHED_MASKED = range(5)


def _flash_schedule(t, tq, tk, window):
    rows = []
    for i in range(t // tq):
        q_lo, q_hi = i * tq, i * tq + tq - 1
        k_first = max(q_lo - (WINDOW - 1), 0) // tk if window else 0
        tiles = list(range(k_first, q_hi // tk + 1))
        for n, kt in enumerate(tiles):
            fully_visible = kt * tk + tk - 1 <= q_lo and not window
            rows.append((i, kt, int(n == 0), int(n == len(tiles) - 1), int(not fully_visible)))
    return np.asarray(rows, np.int32).T


def _outproj_kernel(x_ref, oa_ref, ob_ref, oc_ref, wa_ref, wb_ref, wc_ref, o_ref):
    acc = _dot(oa_ref[...], wa_ref[...])
    acc = acc + _dot(ob_ref[...], wb_ref[...])
    acc = acc + _dot(oc_ref[...], wc_ref[...])
    o_ref[...] = x_ref[...] + acc


def _outproj(x2, oa, ob, oc, wa, wb, wc, tm=1024):
    n = x2.shape[0]
    row = lambda i: (i, 0)
    full = lambda i: (0, 0)
    return pl.pallas_call(
        _outproj_kernel,
        grid=(n // tm,),
        in_specs=[pl.BlockSpec((tm, D_MODEL), row),
                  pl.BlockSpec((tm, NSA_Q), row),
                  pl.BlockSpec((tm, FOX_W), row),
                  pl.BlockSpec((tm, DIFF_W), row),
                  pl.BlockSpec((NSA_Q, D_MODEL), full),
                  pl.BlockSpec((FOX_W, D_MODEL), full),
                  pl.BlockSpec((DIFF_W, D_MODEL), full)],
        out_specs=pl.BlockSpec((tm, D_MODEL), row),
        out_shape=jax.ShapeDtypeStruct((n, D_MODEL), F32),
        compiler_params=_cparams("parallel"),
        name="outproj",
    )(x2, oa, ob, oc, wa, wb, wc)


ROUTE_OFF = SUBLANES
COL_GRP, COL_POS = EXPERTS_PER_GROUP, EXPERTS_PER_GROUP + 1


MOE_CHUNK = 304
MOE_PREFIX_BLOCK = 256


def _moe_kernel(x_ref, g_ref, wrh_ref, wrl_ref, br_ref, wg_ref, wu_ref, wd_ref, gf_ref, o_ref,
                h_ref, gsel_ref, col_ref, row_ref, cnt_ref, *, final_norm):
    grp_id = pl.program_id(1)
    tm = x_ref.shape[0]
    ch = MOE_CHUNK

    @pl.when(grp_id == 0)
    def _():
        x = x_ref[...]
        ms = jnp.mean(x * x, axis=-1, keepdims=True)
        y = x * lax.rsqrt(ms + RMS_EPS) * g_ref[...]
        hb, hl = _split2(y)
        h_ref[...] = hb
        lg = (_dot(hb, wrh_ref[...]) + _dot(hl, wrh_ref[...]) + _dot(hb, wrl_ref[...])) + br_ref[...]
        nrow = ROUTE_OFF + N_EXPERTS
        lt = lg.T[0:nrow, :]
        r = lax.broadcasted_iota(jnp.int32, (nrow, tm), 0).astype(F32)
        big = float(LANES)
        isg = r < N_GROUPS
        gmax = jnp.max(jnp.where(isg, lt, -jnp.inf), axis=0, keepdims=True)
        grp = jnp.min(jnp.where(isg & (lt == gmax), r, big), axis=0, keepdims=True)
        gprob = 1.0 / jnp.sum(jnp.where(isg, jnp.exp(lt - gmax), 0.0), axis=0, keepdims=True)
        lo_row = ROUTE_OFF + grp * EXPERTS_PER_GROUP
        ing = (r >= lo_row) & (r < lo_row + EXPERTS_PER_GROUP)
        v1 = jnp.max(jnp.where(ing, lt, -jnp.inf), axis=0, keepdims=True)
        i1 = jnp.min(jnp.where(ing & (lt == v1), r, big), axis=0, keepdims=True)
        rest = ing & (r != i1)
        v2 = jnp.max(jnp.where(rest, lt, -jnp.inf), axis=0, keepdims=True)
        i2 = jnp.min(jnp.where(rest & (lt == v2), r, big), axis=0, keepdims=True)
        e2 = jnp.exp(v2 - v1)
        w1 = gprob / (1.0 + e2)
        w2 = gprob * e2 / (1.0 + e2)
        gate_t = jnp.where(r == i1, w1, 0.0) + jnp.where(r == i2, w2, 0.0)
        epg = EXPERTS_PER_GROUP
        gsel_t = jnp.zeros((epg, tm), F32)
        for gg in range(N_GROUPS):
            gsel_t = jnp.where(grp == float(gg), gate_t[ROUTE_OFF + gg * epg:ROUTE_OFF + (gg + 1) * epg, :], gsel_t)
        r8 = lax.broadcasted_iota(jnp.int32, (SUBLANES, tm), 0).astype(F32)
        onehot_t = jnp.where(r8 == grp, 1.0, 0.0)
        pb = MOE_PREFIX_BLOCK
        tri = jnp.where(lax.broadcasted_iota(jnp.int32, (pb, pb), 0) < lax.broadcasted_iota(jnp.int32, (pb, pb), 1),
                        1.0, 0.0).astype(BF16)
        carry = jnp.zeros((SUBLANES, 1), F32)
        pos_parts = []
        for blk in range(tm // pb):
            oh_b = onehot_t[:, blk * pb:(blk + 1) * pb]
            prefix = _dot(oh_b.astype(BF16), tri) + carry
            pos_parts.append(jnp.sum(oh_b * prefix, axis=0, keepdims=True))
            carry = carry + jnp.sum(oh_b, axis=1, keepdims=True)
        pos = jnp.concatenate(pos_parts, axis=1)
        for gg in range(N_GROUPS):
            cnt_ref[gg] = carry[gg, 0].astype(jnp.int32)
        row_ref[...] = jnp.concatenate([grp, pos, jnp.zeros((SUBLANES - 2, tm), F32)], axis=0)
        stacked = jnp.concatenate([gsel_t, row_ref[...], jnp.zeros((LANES - 2 * SUBLANES, tm), F32)], axis=0)
        colinfo = stacked.T
        lane = lax.broadcasted_iota(jnp.int32, (tm, LANES), 1)
        ghi, glo = _split2(jnp.where(lane < epg, colinfo, 0.0))
        gsel_ref[:, 0:LANES] = ghi
        gsel_ref[:, LANES:2 * LANES] = glo
        col_ref[...] = colinfo
        o_ref[...] = x

    grp_f = grp_id.astype(F32)
    key_col = jnp.where(col_ref[:, COL_GRP:COL_GRP + 1] == grp_f, col_ref[:, COL_POS:COL_POS + 1], -1.0)
    key_row = jnp.where(row_ref[0:1, :] == grp_f, row_ref[1:2, :], -1.0)

    def chunk(c, carry):
        base = (c * ch).astype(F32)
        sub_iota = lax.broadcasted_iota(jnp.int32, (ch, tm), 0).astype(F32)
        gather = jnp.where(key_row - base == sub_iota, 1.0, 0.0).astype(BF16)
        xg = _dot(gather, h_ref[...]).astype(BF16)
        gparts = _dot(gather, gsel_ref[...])
        gates = gparts[:, 0:LANES] + gparts[:, LANES:2 * LANES]
        y = None
        for e in range(EXPERTS_PER_GROUP):
            a = _dot(xg, wg_ref[0, e])
            u = _dot(xg, wu_ref[0, e])
            act = a * _sigmoid(a) * u * gates[:, e:e + 1]
            part = _dot(act.astype(BF16), wd_ref[0, e])
            y = part if y is None else y + part
        chp = -(-ch // LANES) * LANES
        lane_iota = lax.broadcasted_iota(jnp.int32, (tm, chp), 1).astype(F32)
        scatter = jnp.where(key_col - base == lane_iota, 1.0, 0.0).astype(BF16)
        yb = y.astype(BF16)
        if chp > ch:
            yb = jnp.concatenate([yb, jnp.zeros((chp - ch, D_MODEL), BF16)], axis=0)
        o_ref[...] = o_ref[...] + _dot(scatter, yb)
        return carry

    lax.fori_loop(0, (cnt_ref[grp_id] + ch - 1) // ch, chunk, 0)

    @pl.when(grp_id == N_GROUPS - 1)
    def _():
        if final_norm:
            y = o_ref[...]
            ms = jnp.mean(y * y, axis=-1, keepdims=True)
            o_ref[...] = y * lax.rsqrt(ms + RMS_EPS) * gf_ref[...]


def _moe(x2, g, wrh, wrl, br, wg, wu, wd, gf, layer, final_norm, tm=1024):
    n = x2.shape[0]
    tm = min(tm, n)
    row = lambda i, e: (i, 0)
    full = lambda i, e: (0, 0)
    epg = EXPERTS_PER_GROUP
    wspec_in = pl.BlockSpec((1, epg, D_MODEL, D_EXPERT), lambda i, e: (layer, e, 0, 0))
    wspec_out = pl.BlockSpec((1, epg, D_EXPERT, D_MODEL), lambda i, e: (layer, e, 0, 0))
    return pl.pallas_call(
        functools.partial(_moe_kernel, final_norm=final_norm),
        grid=(n // tm, N_GROUPS),
        in_specs=[pl.BlockSpec((tm, D_MODEL), row),
                  pl.BlockSpec((1, D_MODEL), full),
                  pl.BlockSpec((D_MODEL, LANES), full),
                  pl.BlockSpec((D_MODEL, LANES), full),
                  pl.BlockSpec((1, LANES), full),
                  wspec_in, wspec_in, wspec_out,
                  pl.BlockSpec((1, D_MODEL), full)],
        out_specs=pl.BlockSpec((tm, D_MODEL), row),
        out_shape=jax.ShapeDtypeStruct((n, D_MODEL), F32),
        scratch_shapes=[pltpu.VMEM((tm, D_MODEL), BF16),
                        pltpu.VMEM((tm, 2 * LANES), BF16),
                        pltpu.VMEM((tm, LANES), F32),
                        pltpu.VMEM((8, tm), F32),
                        pltpu.SMEM((N_GROUPS,), jnp.int32)],
        compiler_params=_cparams("parallel", "arbitrary"),
        name="moe",
    )(x2, g, wrh, wrl, br, wg, wu, wd, gf)


def _head_perm():
    hg = NSA_HEADS // NSA_KV_GROUPS
    order = []
    for j in range(hg):
        order += [j, j + hg]
    return np.concatenate([np.arange(h * HEAD_DIM, (h + 1) * HEAD_DIM) for h in order])


def _split_hi_lo(w):
    hi = w.astype(BF16)
    return hi, (w - hi.astype(F32)).astype(BF16)


def _layer_params(l, w_in, b_gate, b_fgt, cmp_pe, cmp_w1, cmp_w2, w_out):
    offs = np.concatenate([[0], np.cumsum(IN_WIDTHS)])
    seg = [w_in[l][:, offs[i]:offs[i + 1]] for i in range(len(IN_WIDTHS))]
    (wq, wkc, wvc, wks, wvs, wkw, wvw, wgt, wfq, wfk, wfv, wff, wdq, wdk, wdv) = seg
    perm = _head_perm()
    gate_perm = np.array([h * 3 + br for br in range(3) for h in range(NSA_HEADS)])
    wm = jnp.concatenate([wks, wkw, wfk, wdk, wkc, wvc], axis=1).astype(BF16)
    wt = jnp.concatenate([wq[:, perm] * (HEAD_DIM ** -0.5 * LOG2E), wfq * (HEAD_DIM ** -0.5 * LOG2E),
                          wdq * (DIFF_QK_DIM ** -0.5 * LOG2E), wvs, wvw, wfv, wdv], axis=1).T.astype(BF16)
    pad = LANES - NSA_GATE - 3 * FOX_HEADS
    ws = jnp.concatenate([wgt[:, gate_perm], wff, wff, wff, jnp.zeros((D_MODEL, pad), F32)], axis=1)
    wsh, wsl = _split_hi_lo(ws)
    bs = jnp.concatenate([b_gate[l][gate_perm], b_fgt[l], b_fgt[l], b_fgt[l], jnp.zeros((pad,), F32)])[None, :]

    eye = jnp.eye(NSA_KV_GROUPS, dtype=F32)
    half = CMP_BLOCK // 2
    cmp = []
    for pe, w1, w2 in zip(cmp_pe, cmp_w1, cmp_w2):
        w1r = w1[l].reshape(2, half, HEAD_DIM, CMP_HIDDEN)
        w1x = jnp.einsum('srdk,gh->srgdhk', w1r, eye).reshape(2, half * NSA_KV, NSA_KV_GROUPS * CMP_HIDDEN)
        w2bd = jnp.einsum('kd,gh->gkhd', w2[l], eye).reshape(NSA_KV_GROUPS * CMP_HIDDEN, NSA_KV)
        pe2 = jnp.broadcast_to(pe[l].reshape(2, half, 1, HEAD_DIM), (2, half, NSA_KV_GROUPS, HEAD_DIM))
        cmp.append((pe2.reshape(2, half * NSA_KV), w1x[0].astype(BF16), w1x[1].astype(BF16), w2bd.astype(BF16)))

    wo = w_out[l]
    wa = wo[:NSA_Q][perm].astype(BF16)
    wb = wo[NSA_Q:NSA_Q + FOX_W].astype(BF16)
    wc = wo[NSA_Q + FOX_W:].astype(BF16)
    return dict(wm=wm, wt=wt, wsh=wsh, wsl=wsl, bs=bs, cmp_k=cmp[0], cmp_v=cmp[1], wa=wa, wb=wb, wc=wc)


def _overlap_matrix(n_slab, n_cmp, n_sel):
    c_start = np.arange(n_cmp) * CMP_STRIDE
    s_start = np.arange(n_sel) * SLC_BLOCK
    ov = np.clip(np.minimum(c_start[:, None] + CMP_BLOCK, s_start[None, :] + SLC_BLOCK)
                 - np.maximum(c_start[:, None], s_start[None, :]), 0, None) / CMP_BLOCK
    full = np.zeros((LANES, n_slab), np.float32)
    full[:n_sel, :n_cmp] = ov.T
    return jnp.asarray(full, dtype=BF16)


def kernel(x, norm_attn, w_in, b_gate, b_fgt, cmp_k_pe, cmp_k_w1, cmp_k_w2, cmp_v_pe, cmp_v_w1, cmp_v_w2, diff_lambda, diff_subln, w_out, norm_ffn, w_grp, b_grp, w_exp, b_exp, w_e_gate, w_e_up, w_e_down, norm_final):
    b, t, d = x.shape
    depth = w_in.shape[0]
    n = b * t
    n_slab = t // CMP_STRIDE
    n_cmp = (t - CMP_BLOCK) // CMP_STRIDE + 1
    n_sel = t // SLC_BLOCK
    top_n = min(SLC_TOPK, n_sel)
    ov = _overlap_matrix(n_slab, n_cmp, n_sel)

    wg_all = w_e_gate.astype(BF16)
    wu_all = w_e_up.astype(BF16)
    wd_all = w_e_down.astype(BF16)
    gf = norm_final[None, :]

    x2 = x.reshape(n, d)
    for l in range(depth):
        p = _layer_params(l, w_in, b_gate, b_fgt, (cmp_k_pe, cmp_v_pe), (cmp_k_w1, cmp_v_w1),
                          (cmp_k_w2, cmp_v_w2), w_out)
        main, feat, kc, vc, small = _inproj(x2, norm_attn[l][None, :], p["wm"], p["wt"], p["wsh"], p["wsl"], p["bs"])
        main3 = main.reshape(b, t, MAIN_W)
        small3 = small.reshape(b, t, LANES)
        kcmp = _compress(kc.reshape(b, n_slab, CMP_STRIDE * NSA_KV), *p["cmp_k"])
        vcmp = _compress(vc.reshape(b, n_slab, CMP_STRIDE * NSA_KV), *p["cmp_v"])
        o_nsa, pen, used = _cmp_attn(feat, kcmp, vcmp, small3, ov, n_cmp, n_sel, top_n)
        nkt = t // min(TK, t)
        tile_used = used.reshape(b, t // TQ, -1, LANES).max(axis=2)[:, :, :nkt].astype(jnp.int32)
        o_nsa = _flash(main3, feat, "slc", (pen, small3, o_nsa), tile_used=tile_used)
        o_nsa = _flash(main3, feat, "win", (small3, o_nsa))
        o_fox = _flash(main3, feat, "fox", (_cumgate(small3),))
        o_diff = _flash(main3, feat, "diff", (diff_lambda[l], diff_subln[l][:, None]), lam_init=_lambda_init(l))

        x2 = _outproj(x2, o_nsa.reshape(n, NSA_Q), o_fox.reshape(n, FOX_W), o_diff.reshape(n, DIFF_W),
                      p["wa"], p["wb"], p["wc"])

        gpad, epad = ROUTE_OFF - N_GROUPS, LANES - ROUTE_OFF - N_EXPERTS
        wr = jnp.concatenate([w_grp[l], jnp.zeros((d, gpad), F32), w_exp[l], jnp.zeros((d, epad), F32)], axis=1)
        wrh, wrl = _split_hi_lo(wr)
        br = jnp.concatenate([b_grp[l], jnp.zeros((gpad,), F32), b_exp[l], jnp.zeros((epad,), F32)])[None, :]
        x2 = _moe(x2, norm_ffn[l][None, :], wrh, wrl, br, wg_all, wu_all, wd_all, gf, l,
                  final_norm=(l == depth - 1))
    return x2.reshape(b, t, d)
```

```python
import functools
import math

import ml_dtypes
import numpy as np
import jax
import jax.numpy as jnp
from jax import lax
from jax.experimental import pallas as pl
from jax.experimental.pallas import tpu as pltpu

F32 = jnp.float32
BF16 = jnp.bfloat16

D_MODEL = 1024
HEAD_DIM = 64
NSA_HEADS = 8
NSA_KV_GROUPS = 2
CMP_BLOCK = 32
CMP_STRIDE = 16
CMP_HIDDEN = 256
SLC_BLOCK = 64
SLC_SHIFT = 6
SLC_TOPK = 16
WINDOW = 512
FOX_HEADS = 4
DIFF_HEADS = 4
DIFF_QK_DIM = HEAD_DIM // 2
N_GROUPS = 4
EXPERTS_PER_GROUP = 8
N_EXPERTS = N_GROUPS * EXPERTS_PER_GROUP
D_EXPERT = 256
RMS_EPS = 1e-6
FORCE_SCORE = 1e4
NEG_INF = -1e30
MASKED = -2e30

NSA_Q = NSA_HEADS * HEAD_DIM
NSA_KV = NSA_KV_GROUPS * HEAD_DIM
NSA_GATE = NSA_HEADS * 3
FOX_W = FOX_HEADS * HEAD_DIM
DIFF_W = DIFF_HEADS * HEAD_DIM
MIX_WIDTH = NSA_Q + FOX_W + DIFF_W
IN_WIDTHS = (NSA_Q, NSA_KV, NSA_KV, NSA_KV, NSA_KV, NSA_KV, NSA_KV, NSA_GATE,
             FOX_W, FOX_W, FOX_W, FOX_HEADS, DIFF_W, DIFF_W, DIFF_W)

LANES = 128
SUBLANES = 8
VMEM_LIMIT_BYTES = 56 * 1024 * 1024

MAIN_W = 2 * NSA_KV + FOX_W + DIFF_W
PROJ_W = MAIN_W + 2 * NSA_KV
COL_KS, COL_KW = 0, 1
COL_FK, COL_DK = 1, 2
ROWS_T = NSA_Q + FOX_W + DIFF_W + 2 * NSA_KV + FOX_W + DIFF_W
ROW_QN, ROW_FQ, ROW_DQ = 0, 2, 3
ROW_VS, ROW_VW = 8, 9
ROW_FV, ROW_DV = 5, 6
FGT_LANES = (NSA_GATE, NSA_GATE + FOX_HEADS, NSA_GATE + 2 * FOX_HEADS)
LOG2E = 1.4426950408889634

TQ = 512
TK = 512
KSTRIP = 512
ONES_ROWS = 16

def _alibi(n):
    return [float(2.0 ** (-8.0 * (i + 1) / n)) for i in range(n)]


def _lambda_init(layer):
    return 0.8 - 0.6 * math.exp(-0.3 * layer)


def _cparams(*sem):
    return pltpu.CompilerParams(dimension_semantics=sem, vmem_limit_bytes=VMEM_LIMIT_BYTES)


def _dot(a, b):
    return jnp.dot(a, b, preferred_element_type=F32)


def _dot_nt(a, b):
    return lax.dot_general(a, b, (((1,), (1,)), ((), ())), preferred_element_type=F32)


def _split2(x):
    hi = x.astype(BF16)
    lo = (x - hi.astype(F32)).astype(BF16)
    return hi, lo


def _sigmoid(z):
    return 1.0 / (1.0 + jnp.exp(-z))


def _keep(sel, blk):
    return jnp.where(sel, blk.astype(F32), 0.0).astype(BF16)


def _inproj_kernel(x_ref, g_ref, wm_ref, wt_ref, wsh_ref, wsl_ref, bs_ref, main_ref, feat_ref, kc_ref, vc_ref,
                   small_ref):
    x = x_ref[...]
    ms = jnp.mean(x * x, axis=-1, keepdims=True)
    y = x * lax.rsqrt(ms + RMS_EPS) * g_ref[...]
    hb, hl = _split2(y)
    feat_ref[...] = _dot_nt(wt_ref[...], hb).astype(BF16)
    full = _dot(hb, wm_ref[...]).astype(BF16)
    main_ref[...] = full[:, :MAIN_W]
    kc_ref[...] = full[:, MAIN_W:MAIN_W + NSA_KV]
    vc_ref[...] = full[:, MAIN_W + NSA_KV:]
    small_ref[...] = (_dot(hb, wsh_ref[...]) + _dot(hl, wsh_ref[...]) + _dot(hb, wsl_ref[...])) + bs_ref[...]


def _inproj(x2, g, wm, wt, wsh, wsl, bs, tm=1024):
    n = x2.shape[0]
    full = lambda i: (0, 0)
    return pl.pallas_call(
        _inproj_kernel,
        grid=(n // tm,),
        in_specs=[pl.BlockSpec((tm, D_MODEL), lambda i: (i, 0)),
                  pl.BlockSpec((1, D_MODEL), full),
                  pl.BlockSpec((D_MODEL, PROJ_W), full),
                  pl.BlockSpec((ROWS_T, D_MODEL), full),
                  pl.BlockSpec((D_MODEL, LANES), full),
                  pl.BlockSpec((D_MODEL, LANES), full),
                  pl.BlockSpec((1, LANES), full)],
        out_specs=[pl.BlockSpec((tm, MAIN_W), lambda i: (i, 0)),
                   pl.BlockSpec((ROWS_T, tm), lambda i: (0, i)),
                   pl.BlockSpec((tm, NSA_KV), lambda i: (i, 0)),
                   pl.BlockSpec((tm, NSA_KV), lambda i: (i, 0)),
                   pl.BlockSpec((tm, LANES), lambda i: (i, 0))],
        out_shape=[jax.ShapeDtypeStruct((n, MAIN_W), BF16),
                   jax.ShapeDtypeStruct((ROWS_T, n), BF16),
                   jax.ShapeDtypeStruct((n, NSA_KV), BF16),
                   jax.ShapeDtypeStruct((n, NSA_KV), BF16),
                   jax.ShapeDtypeStruct((n, LANES), F32)],
        compiler_params=_cparams("parallel"),
        name="inproj",
    )(x2, g, wm, wt, wsh, wsl, bs)


def _compress_kernel(r_ref, pe_ref, w1a_ref, w1b_ref, w2_ref, o_ref):
    r = r_ref[0].astype(F32)
    pe = pe_ref[...]
    ra = (r + pe[0:1, :]).astype(BF16)
    rb = (r + pe[1:2, :]).astype(BF16)
    a = _dot(ra, w1a_ref[...])
    b = _dot(rb, w1b_ref[...])
    n_slab = a.shape[0]
    hid = a + pltpu.roll(b, n_slab - 1, 0)
    hid = jax.nn.gelu(hid)
    o_ref[0] = _dot(hid.astype(BF16), w2_ref[...]).astype(BF16)


def _compress(r, pe2, w1a, w1b, w2bd):
    b, n_slab, w = r.shape
    full = lambda i: (0, 0)
    return pl.pallas_call(
        _compress_kernel,
        grid=(b,),
        in_specs=[pl.BlockSpec((1, n_slab, w), lambda i: (i, 0, 0)),
                  pl.BlockSpec((2, w), full),
                  pl.BlockSpec((w, 2 * CMP_HIDDEN), full),
                  pl.BlockSpec((w, 2 * CMP_HIDDEN), full),
                  pl.BlockSpec((2 * CMP_HIDDEN, LANES), full)],
        out_specs=pl.BlockSpec((1, n_slab, LANES), lambda i: (i, 0, 0)),
        out_shape=jax.ShapeDtypeStruct((b, n_slab, LANES), BF16),
        compiler_params=_cparams("parallel"),
        name="compress",
    )(r, pe2, w1a, w1b, w2bd)


def _cumgate_kernel(z_ref, o_ref, carry_ref, *, tc):
    @pl.when(pl.program_id(1) == 0)
    def _():
        carry_ref[...] = jnp.zeros_like(carry_ref)

    z = z_ref[0]
    logf = -(jnp.maximum(-z, 0.0) + jnp.log(1.0 + jnp.exp(-jnp.abs(z))))
    hi = logf.astype(BF16)
    r1 = logf - hi.astype(F32)
    mid = r1.astype(BF16)
    lo = (r1 - mid.astype(F32)).astype(BF16)
    tri = (lax.broadcasted_iota(jnp.int32, (tc, tc), 0) >= lax.broadcasted_iota(jnp.int32, (tc, tc), 1))
    tri = jnp.where(tri, 1.0, 0.0).astype(BF16)
    c = (_dot(tri, hi) + _dot(tri, mid)) + _dot(tri, lo) + carry_ref[0:1, :]
    carry_ref[...] = jnp.broadcast_to(c[tc - 1:tc, :], carry_ref.shape)
    v = c * (-LOG2E)
    p0 = v.astype(BF16)
    r1 = v - p0.astype(F32)
    p1 = r1.astype(BF16)
    p2 = (r1 - p1.astype(F32)).astype(BF16)
    lane = lax.broadcasted_iota(jnp.int32, v.shape, 1)
    pieces = jnp.where(lane < FGT_LANES[1], p0.astype(F32), jnp.where(lane < FGT_LANES[2], p1.astype(F32), p2.astype(F32)))
    keep = (lane >= FGT_LANES[0]) & (lane < FGT_LANES[2] + FOX_HEADS)
    o_ref[0] = jnp.where(keep, pieces, 0.0).astype(BF16)


def _cumgate(small3, tc=512):
    b, t, w = small3.shape
    return pl.pallas_call(
        functools.partial(_cumgate_kernel, tc=tc),
        grid=(b, t // tc),
        in_specs=[pl.BlockSpec((1, tc, w), lambda i, j: (i, j, 0))],
        out_specs=pl.BlockSpec((1, tc, w), lambda i, j: (i, j, 0)),
        out_shape=jax.ShapeDtypeStruct((b, t, w), BF16),
        scratch_shapes=[pltpu.VMEM((8, w), F32)],
        compiler_params=_cparams("parallel", "arbitrary"),
        name="cumgate",
    )(small3)


def _cmp_kernel(q_ref, kc_ref, vc_ref, sm_ref, ovt_ref, o_ref, pen_ref, used_ref, imp_ref, cnt_ref,
                *, tq, n_cmp, n_sel, top_n):
    t0 = pl.program_id(1) * tq
    nck = kc_ref.shape[1]
    hg = NSA_HEADS // NSA_KV_GROUPS
    slopes = _alibi(NSA_HEADS)
    lane_k = lax.broadcasted_iota(jnp.int32, (nck, LANES), 1)
    end_k = lax.broadcasted_iota(jnp.int32, (nck, LANES), 0) * CMP_STRIDE + (CMP_BLOCK - 1)
    keys = jnp.concatenate([kc_ref[0], _alibi_k_aug(lane_k, end_k - t0, 0).astype(BF16)], axis=1)
    blk = lax.broadcasted_iota(jnp.int32, (nck, tq), 0)
    qpos = t0 + lax.broadcasted_iota(jnp.int32, (nck, tq), 1)
    visible = (qpos >= blk * CMP_STRIDE + (CMP_BLOCK - 1)) & (blk < n_cmp)
    bias = jnp.where(visible, 0.0, MASKED)
    v_t = vc_ref[0].astype(F32).T.astype(BF16)
    sig_t = _sigmoid(sm_ref[0]).T
    feat = lax.broadcasted_iota(jnp.int32, (LANES, tq), 0)
    psum = [None, None]
    outs = []
    for h in range(NSA_HEADS):
        jb, g = h % hg, h // hg
        sel = (feat >= g * HEAD_DIM) & (feat < (g + 1) * HEAD_DIM)
        qa = jnp.concatenate([_keep(sel, q_ref[jb * LANES:(jb + 1) * LANES, :]),
                              _alibi_q_aug(feat, 0, slopes[h]).astype(BF16)], axis=0)
        s = _dot(keys, qa) + bias
        m = jnp.maximum(jnp.max(s, axis=0, keepdims=True), NEG_INF)
        p = jnp.exp2(s - m)
        l = jnp.sum(p, axis=0, keepdims=True)
        p = p * jnp.where(l > 0.0, 1.0 / l, 0.0)
        psum[g] = p if psum[g] is None else psum[g] + p
        outs.append(_dot(v_t[g * HEAD_DIM:(g + 1) * HEAD_DIM, :], p.astype(BF16)) * sig_t[h:h + 1, :])
    for jb in range(hg):
        o_ref[0, :, jb * LANES:(jb + 1) * LANES] = jnp.concatenate(
            [outs[jb], outs[jb + hg]], axis=0).T.astype(o_ref.dtype)

    sub8 = lax.broadcasted_iota(jnp.int32, (SUBLANES, tq), 0)
    row_s = lax.broadcasted_iota(jnp.int32, (n_sel, tq), 0)
    cur = (t0 + lax.broadcasted_iota(jnp.int32, (n_sel, tq), 1)) >> SLC_SHIFT
    for g in range(NSA_KV_GROUPS):
        ph, plo = _split2(psum[g])
        imp_t = (_dot(ovt_ref[...], ph) + _dot(ovt_ref[...], plo))[0:n_sel, :]
        forced = (row_s == 0) | (row_s == cur) | (row_s == cur - 1)
        imp_ref[g] = jnp.where(row_s <= cur, jnp.where(forced, FORCE_SCORE, imp_t), NEG_INF)
        cnt_ref[g] = jnp.zeros((n_sel, tq), F32)
    ngrp = n_sel // SUBLANES
    last_blk = (t0 + tq - 1) >> SLC_SHIFT
    for kg in range(ngrp):
        @pl.when(kg * SUBLANES <= last_blk)
        def _(kg=kg):
            for g in range(NSA_KV_GROUPS):
                rows = [imp_ref[g, i * SUBLANES:(i + 1) * SUBLANES, :] for i in range(ngrp)]
                part = [None] * ngrp
                for k in range(kg * SUBLANES, (kg + 1) * SUBLANES):
                    rk = rows[kg][k - kg * SUBLANES:k - kg * SUBLANES + 1, :]
                    for i in range(ngrp):
                        if i > kg:
                            one = jnp.where(rk >= rows[i], 1.0, 0.0)
                        elif i < kg:
                            one = jnp.where(rk > rows[i], 1.0, 0.0)
                        else:
                            one = jnp.where(sub8 > k - i * SUBLANES,
                                            jnp.where(rk >= rows[i], 1.0, 0.0), jnp.where(rk > rows[i], 1.0, 0.0))
                        part[i] = one if part[i] is None else part[i] + one
                for i in range(ngrp):
                    cnt_ref[g, i * SUBLANES:(i + 1) * SUBLANES, :] += part[i]
    pen_t = [jnp.where(cnt_ref[g] < float(top_n), 0.0, MASKED) for g in range(NSA_KV_GROUPS)]
    for g in range(NSA_KV_GROUPS):
        pen_ref[0, g * HEAD_DIM:g * HEAD_DIM + n_sel, :] = pen_t[g].astype(BF16)
        if n_sel < HEAD_DIM:
            pen_ref[0, g * HEAD_DIM + n_sel:(g + 1) * HEAD_DIM, :] = jnp.zeros((HEAD_DIM - n_sel, tq), BF16)
    picked = jnp.where(jnp.maximum(pen_t[0], pen_t[1]) == 0.0, 1.0, 0.0)
    bpt = TK // SLC_BLOCK
    lane1 = lax.broadcasted_iota(jnp.int32, (1, LANES), 1)
    used = jnp.zeros((1, LANES), F32)
    for kt in range(n_sel // bpt):
        hit = jnp.max(picked[kt * bpt:(kt + 1) * bpt, :], axis=(0, 1), keepdims=True)
        used = jnp.where(lane1 == kt, hit, used)
    used_ref[0, 0] = used


def _cmp_attn(feat, kcmp, vcmp, small3, ov, n_cmp, n_sel, top_n, tq=TQ):
    b, t, _ = small3.shape
    nck = kcmp.shape[1]
    nq = t // tq
    assert n_sel <= HEAD_DIM
    return pl.pallas_call(
        functools.partial(_cmp_kernel, tq=tq, n_cmp=n_cmp, n_sel=n_sel, top_n=top_n),
        grid=(b, nq),
        in_specs=[pl.BlockSpec((NSA_Q, tq), lambda i, j: (ROW_QN, i * nq + j)),
                  pl.BlockSpec((1, nck, LANES), lambda i, j: (i, 0, 0)),
                  pl.BlockSpec((1, nck, LANES), lambda i, j: (i, 0, 0)),
                  pl.BlockSpec((1, tq, LANES), lambda i, j: (i, j, 0)),
                  pl.BlockSpec((LANES, nck), lambda i, j: (0, 0))],
        out_specs=[pl.BlockSpec((1, tq, NSA_Q), lambda i, j: (i, j, 0)),
                   pl.BlockSpec((1, NSA_KV_GROUPS * HEAD_DIM, tq), lambda i, j: (i, 0, j)),
                   pl.BlockSpec((1, 1, 1, LANES), lambda i, j: (i, j, 0, 0))],
        out_shape=[jax.ShapeDtypeStruct((b, t, NSA_Q), BF16),
                   jax.ShapeDtypeStruct((b, NSA_KV_GROUPS * HEAD_DIM, t), BF16),
                   jax.ShapeDtypeStruct((b, t // tq, 1, LANES), F32)],
        scratch_shapes=[pltpu.VMEM((NSA_KV_GROUPS, n_sel, tq), F32),
                        pltpu.VMEM((NSA_KV_GROUPS, n_sel, tq), F32)],
        compiler_params=_cparams("parallel", "parallel"),
        name="cmp_attn",
    )(feat, kcmp, vcmp, small3, ov)


def _bf16_terms(c, n=3):
    out, r = [], float(c)
    for _ in range(n):
        p = float(np.float32(r).astype(ml_dtypes.bfloat16))
        out.append(p)
        r -= p
    return out


def _alibi_q_aug(idx, a0, slope):
    aug = jnp.zeros(idx.shape, F32)
    for i, c in enumerate(_bf16_terms(slope * LOG2E)):
        aug = jnp.where(idx == a0 + 2 * i, float(SLC_BLOCK) * c, aug)
        aug = jnp.where(idx == a0 + 2 * i + 1, c, aug)
    return aug


def _alibi_k_aug(lane, rel, a0):
    hi = (rel >> SLC_SHIFT).astype(F32)
    lo = (rel & (SLC_BLOCK - 1)).astype(F32)
    inside = (lane >= a0) & (lane < a0 + 6)
    odd = ((lane - a0) & 1) == 1
    return jnp.where(inside, jnp.where(odd, lo, hi), 0.0)


ALIBI_LANE = {"slc": SLC_BLOCK, "win": 0, "diff": 0}
STACKS = {"slc": (1, 8), "win": (1, 8), "fox": (2, 2), "diff": (2, 4)}


def _v_half(mode, vh):
    nv = STACKS[mode][1]
    if mode in ("slc", "win"):
        return vh // (NSA_HEADS // NSA_KV_GROUPS)
    return (vh % nv) // (nv // 2)


def _flash_kernel(*refs, mode, tq, tk, lam_init, ntiles):
    sched_ref, refs = refs[0], refs[1:]
    if mode == "slc":
        use_ref, q_ref, k_ref, v_ref, pen_ref, sm_ref, add_ref, o_ref, qst_ref, m_ref, acc_ref = refs
    elif mode == "win":
        q_ref, k_ref, v_ref, sm_ref, add_ref, o_ref, qst_ref, m_ref, acc_ref = refs
    elif mode == "fox":
        q_ref, k_ref, v_ref, fa_ref, o_ref, qst_ref, m_ref, acc_ref = refs
    else:
        q_ref, k_ref, v_ref, lam_ref, sub_ref, o_ref, qst_ref, m_ref, acc_ref = refs
    nstack, nv = STACKS[mode]
    nsa = mode in ("slc", "win")
    hg = NSA_HEADS // NSA_KV_GROUPS
    step_id = pl.program_id(1)
    q0 = sched_ref[SCHED_Q, step_id] * tq
    k0 = sched_ref[SCHED_K, step_id] * tk

    @pl.when(sched_ref[SCHED_FIRST, step_id] == 1)
    def _():
        m_ref[...] = jnp.full(m_ref.shape, NEG_INF, F32)
        acc_ref[...] = jnp.zeros(acc_ref.shape, F32)
        feat = lax.broadcasted_iota(jnp.int32, (LANES, tq), 0)
        for vh in range(nstack * nv):
            if nsa:
                jb, g = vh % hg, vh // hg
                lo_row, width = g * HEAD_DIM, HEAD_DIM
                aug = _alibi_q_aug(feat, ALIBI_LANE[mode], _alibi(NSA_HEADS)[vh])
                if mode == "slc":
                    aug = aug + jnp.concatenate([pen_ref[0, g * HEAD_DIM:(g + 1) * HEAD_DIM, :].astype(F32),
                                                 jnp.zeros((LANES - HEAD_DIM, tq), F32)], axis=0)
            elif mode == "fox":
                jb, r = vh // nv, vh % nv
                lo_row, width = r * HEAD_DIM, HEAD_DIM
                hit = (feat == FGT_LANES[0] + vh) | (feat == FGT_LANES[1] + vh) | (feat == FGT_LANES[2] + vh)
                aug = jnp.where(hit, 1.0, 0.0)
            else:
                jb, r = vh // nv, vh % nv
                lo_row, width = r * DIFF_QK_DIM, DIFF_QK_DIM
                aug = _alibi_q_aug(feat, ALIBI_LANE[mode], _alibi(DIFF_HEADS)[vh // 2])
            sel = (feat >= lo_row) & (feat < lo_row + width)
            base = vh * 2 * LANES
            qst_ref[base:base + LANES, :] = _keep(sel, q_ref[jb * LANES:(jb + 1) * LANES, :])
            qst_ref[base + LANES:base + 2 * LANES, :] = aug.astype(BF16)

    def step(masked):
        lane_k = lax.broadcasted_iota(jnp.int32, (tk, LANES), 1)
        row_k = lax.broadcasted_iota(jnp.int32, (tk, LANES), 0)
        if mode == "fox":
            k_aug = fa_ref[0]
        else:
            ka = _alibi_k_aug(lane_k, (k0 - q0) + row_k, ALIBI_LANE[mode])
            if mode == "slc":
                ka = ka + jnp.where(((k0 + row_k) >> SLC_SHIFT) == lane_k, 1.0, 0.0)
            k_aug = ka.astype(BF16)
        if masked:
            dist = (q0 - k0) + lax.broadcasted_iota(jnp.int32, (tk, tq), 1) - lax.broadcasted_iota(jnp.int32, (tk, tq), 0)
            ok = dist >= 0
            if mode == "win":
                ok = ok & (dist < WINDOW)
            bias = jnp.where(ok, 0.0, MASKED)
        keys, vals_t = [], []
        ones = jnp.ones((ONES_ROWS, tk), BF16)
        for st in range(nstack):
            kblk = k_ref[0] if nsa else k_ref[0, :, st * LANES:(st + 1) * LANES]
            keys.append(jnp.concatenate([kblk, k_aug], axis=1))
            vals_t.append([jnp.concatenate([v_ref[st * LANES + hf * HEAD_DIM:st * LANES + (hf + 1) * HEAD_DIM, :],
                                            ones], axis=0) for hf in range(LANES // HEAD_DIM)])

        nstrip = tk // KSTRIP
        nvh = nstack * nv

        def scores(vh):
            s = _dot(keys[vh // nv], qst_ref[vh * 2 * LANES:(vh + 1) * 2 * LANES, :])
            if masked:
                s = s + bias
            return s, jnp.max(s, axis=0, keepdims=True)

        cur, cur_max = scores(0)
        for vh in range(nvh):
            cols = slice(vh * tq, (vh + 1) * tq)
            v_half = vals_t[vh // nv][_v_half(mode, vh)]
            m_old = m_ref[:, cols]
            m_new = jnp.maximum(m_old, cur_max)
            alpha = jnp.exp2(m_old - m_new)
            if vh + 1 < nvh:
                nxt, nxt_max = scores(vh + 1)
            pv = None
            for r in range(nstrip):
                p = jnp.exp2((cur[r * KSTRIP:(r + 1) * KSTRIP, :] - m_new).astype(BF16))
                part = _dot(v_half[:, r * KSTRIP:(r + 1) * KSTRIP], p)
                pv = part if pv is None else pv + part
            m_ref[:, cols] = m_new
            acc_ref[:, cols] = alpha * acc_ref[:, cols] + pv
            if vh + 1 < nvh:
                cur, cur_max = nxt, nxt_max

    is_masked = sched_ref[SCHED_MASKED, step_id] == 1
    if mode == "win":
        pl.when(is_masked)(lambda: step(True))
    elif mode == "slc":
        tile_id = (pl.program_id(0) * ntiles[0] + sched_ref[SCHED_Q, step_id]) * ntiles[1] + sched_ref[SCHED_K, step_id]
        needed = use_ref[tile_id] == 1
        pl.when(is_masked & needed)(lambda: step(True))
        pl.when(jnp.logical_not(is_masked) & needed)(lambda: step(False))
    else:
        pl.when(is_masked)(lambda: step(True))
        pl.when(jnp.logical_not(is_masked))(lambda: step(False))

    @pl.when(sched_ref[SCHED_LAST, step_id] == 1)
    def _():
        inv = 1.0 / acc_ref[HEAD_DIM:HEAD_DIM + 1, :]

        def out_t(vh):
            return acc_ref[0:HEAD_DIM, vh * tq:(vh + 1) * tq] * inv[:, vh * tq:(vh + 1) * tq]

        def pair(a, b):
            return jnp.concatenate([a, b], axis=0).T

        if nsa:
            sig_t = _sigmoid(sm_ref[0]).T
            br = 1 if mode == "slc" else 2
            for jb in range(hg):
                c0 = br * NSA_HEADS + jb
                o = pair(out_t(jb) * sig_t[c0:c0 + 1, :], out_t(jb + hg) * sig_t[c0 + hg:c0 + hg + 1, :])
                o_ref[0, :, jb * LANES:(jb + 1) * LANES] = (
                    add_ref[0, :, jb * LANES:(jb + 1) * LANES].astype(F32) + o).astype(o_ref.dtype)
        elif mode == "fox":
            for pb in range(nstack):
                o = pair(out_t(2 * pb), out_t(2 * pb + 1))
                o_ref[0, :, pb * LANES:(pb + 1) * LANES] = o.astype(o_ref.dtype)
        else:
            lam = lam_ref[...]
            lam_full = (jnp.exp(jnp.sum(lam[0:1, :] * lam[1:2, :], axis=-1, keepdims=True))
                        - jnp.exp(jnp.sum(lam[2:3, :] * lam[3:4, :], axis=-1, keepdims=True)) + lam_init)
            for pb in range(nstack):
                normed = []
                for hh in range(2):
                    d = out_t(pb * nv + 2 * hh) - lam_full * out_t(pb * nv + 2 * hh + 1)
                    ms = jnp.mean(d * d, axis=0, keepdims=True)
                    normed.append(d * lax.rsqrt(ms + RMS_EPS) * sub_ref[...] * (1.0 - lam_init))
                o_ref[0, :, pb * LANES:(pb + 1) * LANES] = pair(normed[0], normed[1]).astype(o_ref.dtype)


def _flash(main3, feat, mode, extra, lam_init=0.0, tile_used=None, tq=TQ, tk=TK):
    b, t, _ = main3.shape
    tk = min(tk, t)
    nq, nk = t // tq, t // tk
    nstack, nv = STACKS[mode]
    sched = _flash_schedule(t, tq, tk, mode == "win")
    prefetch = [jnp.asarray(sched)]
    if mode == "slc":
        prefetch.append(tile_used.reshape(-1))
    qtile = lambda w, c: pl.BlockSpec((1, tq, w), lambda bi, s, sch, *_: (bi, sch[SCHED_Q, s], c))
    ktile = lambda w, c: pl.BlockSpec((1, tk, w), lambda bi, s, sch, *_: (bi, sch[SCHED_K, s], c))
    qfeat = lambda rows, r: pl.BlockSpec((rows, tq), lambda bi, s, sch, *_: (r, bi * nq + sch[SCHED_Q, s]))
    vfeat = lambda rows, r: pl.BlockSpec((rows, tk), lambda bi, s, sch, *_: (r, bi * nk + sch[SCHED_K, s]))
    if mode in ("slc", "win"):
        kcol, vrow = (COL_KS, ROW_VS) if mode == "slc" else (COL_KW, ROW_VW)
        in_specs = [qfeat(NSA_Q, ROW_QN), ktile(LANES, kcol), vfeat(LANES, vrow)]
        if mode == "slc":
            in_specs.append(pl.BlockSpec((1, NSA_KV_GROUPS * HEAD_DIM, tq),
                                         lambda bi, s, sch, *_: (bi, 0, sch[SCHED_Q, s])))
        in_specs += [qtile(LANES, 0), qtile(NSA_Q, 0)]
        out_w = NSA_Q
    else:
        qr, kc, vr = (ROW_FQ, COL_FK, ROW_FV) if mode == "fox" else (ROW_DQ, COL_DK, ROW_DV)
        in_specs = [qfeat(2 * LANES, qr), ktile(2 * LANES, kc), vfeat(2 * LANES, vr)]
        if mode == "fox":
            in_specs.append(ktile(LANES, 0))
        else:
            in_specs += [pl.BlockSpec((4, DIFF_QK_DIM), lambda bi, s, sch, *_: (0, 0)),
                         pl.BlockSpec((HEAD_DIM, 1), lambda bi, s, sch, *_: (0, 0))]
        out_w = 2 * LANES
    cols = nstack * nv * tq
    return pl.pallas_call(
        functools.partial(_flash_kernel, mode=mode, tq=tq, tk=tk, lam_init=lam_init, ntiles=(t // tq, t // tk)),
        grid_spec=pltpu.PrefetchScalarGridSpec(
            num_scalar_prefetch=len(prefetch),
            grid=(b, sched.shape[1]),
            in_specs=in_specs,
            out_specs=qtile(out_w, 0),
            scratch_shapes=[pltpu.VMEM((nstack * nv * 2 * LANES, tq), BF16),
                            pltpu.VMEM((1, cols), F32),
                            pltpu.VMEM((HEAD_DIM + ONES_ROWS, cols), F32)]),
        out_shape=jax.ShapeDtypeStruct((b, t, out_w), BF16),
        compiler_params=_cparams("parallel", "arbitrary"),
        name="flash_" + mode,
    )(*prefetch, feat, main3, feat, *extra)


SCHED_Q, SCHED_K, SCHED_FIRST, SCHED_LAST, SCHED_MASKED = range(5)


def _flash_schedule(t, tq, tk, window):
    rows = []
    for i in range(t // tq):
        q_lo, q_hi = i * tq, i * tq + tq - 1
        k_first = max(q_lo - (WINDOW - 1), 0) // tk if window else 0
        tiles = list(range(k_first, q_hi // tk + 1))
        for n, kt in enumerate(tiles):
            fully_visible = kt * tk + tk - 1 <= q_lo and not window
            rows.append((i, kt, int(n == 0), int(n == len(tiles) - 1), int(not fully_visible)))
    return np.asarray(rows, np.int32).T


def _outproj_kernel(x_ref, oa_ref, ob_ref, oc_ref, wa_ref, wb_ref, wc_ref, o_ref):
    acc = _dot(oa_ref[...], wa_ref[...])
    acc = acc + _dot(ob_ref[...], wb_ref[...])
    acc = acc + _dot(oc_ref[...], wc_ref[...])
    o_ref[...] = x_ref[...] + acc


def _outproj(x2, oa, ob, oc, wa, wb, wc, tm=1024):
    n = x2.shape[0]
    row = lambda i: (i, 0)
    full = lambda i: (0, 0)
    return pl.pallas_call(
        _outproj_kernel,
        grid=(n // tm,),
        in_specs=[pl.BlockSpec((tm, D_MODEL), row),
                  pl.BlockSpec((tm, NSA_Q), row),
                  pl.BlockSpec((tm, FOX_W), row),
                  pl.BlockSpec((tm, DIFF_W), row),
                  pl.BlockSpec((NSA_Q, D_MODEL), full),
                  pl.BlockSpec((FOX_W, D_MODEL), full),
                  pl.BlockSpec((DIFF_W, D_MODEL), full)],
        out_specs=pl.BlockSpec((tm, D_MODEL), row),
        out_shape=jax.ShapeDtypeStruct((n, D_MODEL), F32),
        compiler_params=_cparams("parallel"),
        name="outproj",
    )(x2, oa, ob, oc, wa, wb, wc)


ROUTE_OFF = SUBLANES
COL_GRP, COL_POS = EXPERTS_PER_GROUP, EXPERTS_PER_GROUP + 1


MOE_CHUNK = 304
MOE_PREFIX_BLOCK = 256


def _moe_kernel(x_ref, g_ref, wrh_ref, wrl_ref, br_ref, wg_ref, wu_ref, wd_ref, gf_ref, o_ref,
                h_ref, gsel_ref, col_ref, row_ref, cnt_ref, *, final_norm):
    grp_id = pl.program_id(1)
    tm = x_ref.shape[0]
    ch = MOE_CHUNK

    @pl.when(grp_id == 0)
    def _():
        x = x_ref[...]
        ms = jnp.mean(x * x, axis=-1, keepdims=True)
        y = x * lax.rsqrt(ms + RMS_EPS) * g_ref[...]
        hb, hl = _split2(y)
        h_ref[...] = hb
        lg = (_dot(hb, wrh_ref[...]) + _dot(hl, wrh_ref[...]) + _dot(hb, wrl_ref[...])) + br_ref[...]
        nrow = ROUTE_OFF + N_EXPERTS
        lt = lg.T[0:nrow, :]
        r = lax.broadcasted_iota(jnp.int32, (nrow, tm), 0).astype(F32)
        big = float(LANES)
        isg = r < N_GROUPS
        gmax = jnp.max(jnp.where(isg, lt, -jnp.inf), axis=0, keepdims=True)
        grp = jnp.min(jnp.where(isg & (lt == gmax), r, big), axis=0, keepdims=True)
        gprob = 1.0 / jnp.sum(jnp.where(isg, jnp.exp(lt - gmax), 0.0), axis=0, keepdims=True)
        lo_row = ROUTE_OFF + grp * EXPERTS_PER_GROUP
        ing = (r >= lo_row) & (r < lo_row + EXPERTS_PER_GROUP)
        v1 = jnp.max(jnp.where(ing, lt, -jnp.inf), axis=0, keepdims=True)
        i1 = jnp.min(jnp.where(ing & (lt == v1), r, big), axis=0, keepdims=True)
        rest = ing & (r != i1)
        v2 = jnp.max(jnp.where(rest, lt, -jnp.inf), axis=0, keepdims=True)
        i2 = jnp.min(jnp.where(rest & (lt == v2), r, big), axis=0, keepdims=True)
        e2 = jnp.exp(v2 - v1)
        w1 = gprob / (1.0 + e2)
        w2 = gprob * e2 / (1.0 + e2)
        gate_t = jnp.where(r == i1, w1, 0.0) + jnp.where(r == i2, w2, 0.0)
        epg = EXPERTS_PER_GROUP
        gsel_t = jnp.zeros((epg, tm), F32)
        for gg in range(N_GROUPS):
            gsel_t = jnp.where(grp == float(gg), gate_t[ROUTE_OFF + gg * epg:ROUTE_OFF + (gg + 1) * epg, :], gsel_t)
        r8 = lax.broadcasted_iota(jnp.int32, (SUBLANES, tm), 0).astype(F32)
        onehot_t = jnp.where(r8 == grp, 1.0, 0.0)
        pb = MOE_PREFIX_BLOCK
        tri = jnp.where(lax.broadcasted_iota(jnp.int32, (pb, pb), 0) < lax.broadcasted_iota(jnp.int32, (pb, pb), 1),
                        1.0, 0.0).astype(BF16)
        carry = jnp.zeros((SUBLANES, 1), F32)
        pos_parts = []
        for blk in range(tm // pb):
            oh_b = onehot_t[:, blk * pb:(blk + 1) * pb]
            prefix = _dot(oh_b.astype(BF16), tri) + carry
            pos_parts.append(jnp.sum(oh_b * prefix, axis=0, keepdims=True))
            carry = carry + jnp.sum(oh_b, axis=1, keepdims=True)
        pos = jnp.concatenate(pos_parts, axis=1)
        for gg in range(N_GROUPS):
            cnt_ref[gg] = carry[gg, 0].astype(jnp.int32)
        row_ref[...] = jnp.concatenate([grp, pos, jnp.zeros((SUBLANES - 2, tm), F32)], axis=0)
        stacked = jnp.concatenate([gsel_t, row_ref[...], jnp.zeros((LANES - 2 * SUBLANES, tm), F32)], axis=0)
        colinfo = stacked.T
        lane = lax.broadcasted_iota(jnp.int32, (tm, LANES), 1)
        ghi, glo = _split2(jnp.where(lane < epg, colinfo, 0.0))
        gsel_ref[:, 0:LANES] = ghi
        gsel_ref[:, LANES:2 * LANES] = glo
        col_ref[...] = colinfo
        o_ref[...] = x

    grp_f = grp_id.astype(F32)
    key_col = jnp.where(col_ref[:, COL_GRP:COL_GRP + 1] == grp_f, col_ref[:, COL_POS:COL_POS + 1], -1.0)
    key_row = jnp.where(row_ref[0:1, :] == grp_f, row_ref[1:2, :], -1.0)

    def chunk(c, carry):
        base = (c * ch).astype(F32)
        sub_iota = lax.broadcasted_iota(jnp.int32, (ch, tm), 0).astype(F32)
        gather = jnp.where(key_row - base == sub_iota, 1.0, 0.0).astype(BF16)
        xg = _dot(gather, h_ref[...]).astype(BF16)
        gparts = _dot(gather, gsel_ref[...])
        gates = gparts[:, 0:LANES] + gparts[:, LANES:2 * LANES]
        y = None
        for e in range(EXPERTS_PER_GROUP):
            a = _dot(xg, wg_ref[0, e])
            u = _dot(xg, wu_ref[0, e])
            act = a * _sigmoid(a) * u * gates[:, e:e + 1]
            part = _dot(act.astype(BF16), wd_ref[0, e])
            y = part if y is None else y + part
        chp = -(-ch // LANES) * LANES
        lane_iota = lax.broadcasted_iota(jnp.int32, (tm, chp), 1).astype(F32)
        scatter = jnp.where(key_col - base == lane_iota, 1.0, 0.0).astype(BF16)
        yb = y.astype(BF16)
        if chp > ch:
            yb = jnp.concatenate([yb, jnp.zeros((chp - ch, D_MODEL), BF16)], axis=0)
        o_ref[...] = o_ref[...] + _dot(scatter, yb)
        return carry

    lax.fori_loop(0, (cnt_ref[grp_id] + ch - 1) // ch, chunk, 0)

    @pl.when(grp_id == N_GROUPS - 1)
    def _():
        if final_norm:
            y = o_ref[...]
            ms = jnp.mean(y * y, axis=-1, keepdims=True)
            o_ref[...] = y * lax.rsqrt(ms + RMS_EPS) * gf_ref[...]


def _moe(x2, g, wrh, wrl, br, wg, wu, wd, gf, layer, final_norm, tm=1024):
    n = x2.shape[0]
    tm = min(tm, n)
    row = lambda i, e: (i, 0)
    full = lambda i, e: (0, 0)
    epg = EXPERTS_PER_GROUP
    wspec_in = pl.BlockSpec((1, epg, D_MODEL, D_EXPERT), lambda i, e: (layer, e, 0, 0))
    wspec_out = pl.BlockSpec((1, epg, D_EXPERT, D_MODEL), lambda i, e: (layer, e, 0, 0))
    return pl.pallas_call(
        functools.partial(_moe_kernel, final_norm=final_norm),
        grid=(n // tm, N_GROUPS),
        in_specs=[pl.BlockSpec((tm, D_MODEL), row),
                  pl.BlockSpec((1, D_MODEL), full),
                  pl.BlockSpec((D_MODEL, LANES), full),
                  pl.BlockSpec((D_MODEL, LANES), full),
                  pl.BlockSpec((1, LANES), full),
                  wspec_in, wspec_in, wspec_out,
                  pl.BlockSpec((1, D_MODEL), full)],
        out_specs=pl.BlockSpec((tm, D_MODEL), row),
        out_shape=jax.ShapeDtypeStruct((n, D_MODEL), F32),
        scratch_shapes=[pltpu.VMEM((tm, D_MODEL), BF16),
                        pltpu.VMEM((tm, 2 * LANES), BF16),
                        pltpu.VMEM((tm, LANES), F32),
                        pltpu.VMEM((8, tm), F32),
                        pltpu.SMEM((N_GROUPS,), jnp.int32)],
        compiler_params=_cparams("parallel", "arbitrary"),
        name="moe",
    )(x2, g, wrh, wrl, br, wg, wu, wd, gf)


def _head_perm():
    hg = NSA_HEADS // NSA_KV_GROUPS
    order = []
    for j in range(hg):
        order += [j, j + hg]
    return np.concatenate([np.arange(h * HEAD_DIM, (h + 1) * HEAD_DIM) for h in order])


def _split_hi_lo(w):
    hi = w.astype(BF16)
    return hi, (w - hi.astype(F32)).astype(BF16)


def _layer_params(l, w_in, b_gate, b_fgt, cmp_pe, cmp_w1, cmp_w2, w_out):
    offs = np.concatenate([[0], np.cumsum(IN_WIDTHS)])
    seg = [w_in[l][:, offs[i]:offs[i + 1]] for i in range(len(IN_WIDTHS))]
    (wq, wkc, wvc, wks, wvs, wkw, wvw, wgt, wfq, wfk, wfv, wff, wdq, wdk, wdv) = seg
    perm = _head_perm()
    gate_perm = np.array([h * 3 + br for br in range(3) for h in range(NSA_HEADS)])
    wm = jnp.concatenate([wks, wkw, wfk, wdk, wkc, wvc], axis=1).astype(BF16)
    wt = jnp.concatenate([wq[:, perm] * (HEAD_DIM ** -0.5 * LOG2E), wfq * (HEAD_DIM ** -0.5 * LOG2E),
                          wdq * (DIFF_QK_DIM ** -0.5 * LOG2E), wvs, wvw, wfv, wdv], axis=1).T.astype(BF16)
    pad = LANES - NSA_GATE - 3 * FOX_HEADS
    ws = jnp.concatenate([wgt[:, gate_perm], wff, wff, wff, jnp.zeros((D_MODEL, pad), F32)], axis=1)
    wsh, wsl = _split_hi_lo(ws)
    bs = jnp.concatenate([b_gate[l][gate_perm], b_fgt[l], b_fgt[l], b_fgt[l], jnp.zeros((pad,), F32)])[None, :]

    eye = jnp.eye(NSA_KV_GROUPS, dtype=F32)
    half = CMP_BLOCK // 2
    cmp = []
    for pe, w1, w2 in zip(cmp_pe, cmp_w1, cmp_w2):
        w1r = w1[l].reshape(2, half, HEAD_DIM, CMP_HIDDEN)
        w1x = jnp.einsum('srdk,gh->srgdhk', w1r, eye).reshape(2, half * NSA_KV, NSA_KV_GROUPS * CMP_HIDDEN)
        w2bd = jnp.einsum('kd,gh->gkhd', w2[l], eye).reshape(NSA_KV_GROUPS * CMP_HIDDEN, NSA_KV)
        pe2 = jnp.broadcast_to(pe[l].reshape(2, half, 1, HEAD_DIM), (2, half, NSA_KV_GROUPS, HEAD_DIM))
        cmp.append((pe2.reshape(2, half * NSA_KV), w1x[0].astype(BF16), w1x[1].astype(BF16), w2bd.astype(BF16)))

    wo = w_out[l]
    wa = wo[:NSA_Q][perm].astype(BF16)
    wb = wo[NSA_Q:NSA_Q + FOX_W].astype(BF16)
    wc = wo[NSA_Q + FOX_W:].astype(BF16)
    return dict(wm=wm, wt=wt, wsh=wsh, wsl=wsl, bs=bs, cmp_k=cmp[0], cmp_v=cmp[1], wa=wa, wb=wb, wc=wc)


def _overlap_matrix(n_slab, n_cmp, n_sel):
    c_start = np.arange(n_cmp) * CMP_STRIDE
    s_start = np.arange(n_sel) * SLC_BLOCK
    ov = np.clip(np.minimum(c_start[:, None] + CMP_BLOCK, s_start[None, :] + SLC_BLOCK)
                 - np.maximum(c_start[:, None], s_start[None, :]), 0, None) / CMP_BLOCK
    full = np.zeros((LANES, n_slab), np.float32)
    full[:n_sel, :n_cmp] = ov.T
    return jnp.asarray(full, dtype=BF16)


def kernel(x, norm_attn, w_in, b_gate, b_fgt, cmp_k_pe, cmp_k_w1, cmp_k_w2, cmp_v_pe, cmp_v_w1, cmp_v_w2, diff_lambda, diff_subln, w_out, norm_ffn, w_grp, b_grp, w_exp, b_exp, w_e_gate, w_e_up, w_e_down, norm_final):
    b, t, d = x.shape
    depth = w_in.shape[0]
    n = b * t
    n_slab = t // CMP_STRIDE
    n_cmp = (t - CMP_BLOCK) // CMP_STRIDE + 1
    n_sel = t // SLC_BLOCK
    top_n = min(SLC_TOPK, n_sel)
    ov = _overlap_matrix(n_slab, n_cmp, n_sel)

    wg_all = w_e_gate.astype(BF16)
    wu_all = w_e_up.astype(BF16)
    wd_all = w_e_down.astype(BF16)
    gf = norm_final[None, :]

    x2 = x.reshape(n, d)
    for l in range(depth):
        p = _layer_params(l, w_in, b_gate, b_fgt, (cmp_k_pe, cmp_v_pe), (cmp_k_w1, cmp_v_w1),
                          (cmp_k_w2, cmp_v_w2), w_out)
        main, feat, kc, vc, small = _inproj(x2, norm_attn[l][None, :], p["wm"], p["wt"], p["wsh"], p["wsl"], p["bs"])
        main3 = main.reshape(b, t, MAIN_W)
        small3 = small.reshape(b, t, LANES)
        kcmp = _compress(kc.reshape(b, n_slab, CMP_STRIDE * NSA_KV), *p["cmp_k"])
        vcmp = _compress(vc.reshape(b, n_slab, CMP_STRIDE * NSA_KV), *p["cmp_v"])
        o_nsa, pen, used = _cmp_attn(feat, kcmp, vcmp, small3, ov, n_cmp, n_sel, top_n)
        nkt = t // min(TK, t)
        tile_used = used.reshape(b, t // TQ, -1, LANES).max(axis=2)[:, :, :nkt].astype(jnp.int32)
        o_nsa = _flash(main3, feat, "slc", (pen, small3, o_nsa), tile_used=tile_used)
        o_nsa = _flash(main3, feat, "win", (small3, o_nsa))
        o_fox = _flash(main3, feat, "fox", (_cumgate(small3),))
        o_diff = _flash(main3, feat, "diff", (diff_lambda[l], diff_subln[l][:, None]), lam_init=_lambda_init(l))

        x2 = _outproj(x2, o_nsa.reshape(n, NSA_Q), o_fox.reshape(n, FOX_W), o_diff.reshape(n, DIFF_W),
                      p["wa"], p["wb"], p["wc"])

        gpad, epad = ROUTE_OFF - N_GROUPS, LANES - ROUTE_OFF - N_EXPERTS
        wr = jnp.concatenate([w_grp[l], jnp.zeros((d, gpad), F32), w_exp[l], jnp.zeros((d, epad), F32)], axis=1)
        wrh, wrl = _split_hi_lo(wr)
        br = jnp.concatenate([b_grp[l], jnp.zeros((gpad,), F32), b_exp[l], jnp.zeros((epad,), F32)])[None, :]
        x2 = _moe(x2, norm_ffn[l][None, :], wrh, wrl, br, wg_all, wu_all, wd_all, gf, l,
                  final_norm=(l == depth - 1))
    return x2.reshape(b, t, d)
```

```python
import functools
import math

import ml_dtypes
import numpy as np
import jax
import jax.numpy as jnp
from jax import lax
from jax.experimental import pallas as pl
from jax.experimental.pallas import tpu as pltpu

F32 = jnp.float32
BF16 = jnp.bfloat16

D_MODEL = 1024
HEAD_DIM = 64
NSA_HEADS = 8
NSA_KV_GROUPS = 2
CMP_BLOCK = 32
CMP_STRIDE = 16
CMP_HIDDEN = 256
SLC_BLOCK = 64
SLC_SHIFT = 6
SLC_TOPK = 16
WINDOW = 512
FOX_HEADS = 4
DIFF_HEADS = 4
DIFF_QK_DIM = HEAD_DIM // 2
N_GROUPS = 4
EXPERTS_PER_GROUP = 8
N_EXPERTS = N_GROUPS * EXPERTS_PER_GROUP
D_EXPERT = 256
RMS_EPS = 1e-6
FORCE_SCORE = 1e4
NEG_INF = -1e30
MASKED = -2e30

NSA_Q = NSA_HEADS * HEAD_DIM
NSA_KV = NSA_KV_GROUPS * HEAD_DIM
NSA_GATE = NSA_HEADS * 3
FOX_W = FOX_HEADS * HEAD_DIM
DIFF_W = DIFF_HEADS * HEAD_DIM
MIX_WIDTH = NSA_Q + FOX_W + DIFF_W
IN_WIDTHS = (NSA_Q, NSA_KV, NSA_KV, NSA_KV, NSA_KV, NSA_KV, NSA_KV, NSA_GATE,
             FOX_W, FOX_W, FOX_W, FOX_HEADS, DIFF_W, DIFF_W, DIFF_W)

LANES = 128
SUBLANES = 8
VMEM_LIMIT_BYTES = 56 * 1024 * 1024

MAIN_W = 2 * NSA_KV + FOX_W + DIFF_W
PROJ_W = MAIN_W + 2 * NSA_KV
COL_KS, COL_KW = 0, 1
COL_FK, COL_DK = 1, 2
ROWS_T = NSA_Q + FOX_W + DIFF_W + 2 * NSA_KV + FOX_W + DIFF_W
ROW_QN, ROW_FQ, ROW_DQ = 0, 2, 3
ROW_VS, ROW_VW = 8, 9
ROW_FV, ROW_DV = 5, 6
FGT_LANES = (NSA_GATE, NSA_GATE + FOX_HEADS, NSA_GATE + 2 * FOX_HEADS)
LOG2E = 1.4426950408889634

TQ = 512
TK = 512
KSTRIP = 512
ONES_ROWS = 16

def _alibi(n):
    return [float(2.0 ** (-8.0 * (i + 1) / n)) for i in range(n)]


def _lambda_init(layer):
    return 0.8 - 0.6 * math.exp(-0.3 * layer)


def _cparams(*sem):
    return pltpu.CompilerParams(dimension_semantics=sem, vmem_limit_bytes=VMEM_LIMIT_BYTES)


def _dot(a, b):
    return jnp.dot(a, b, preferred_element_type=F32)


def _dot_nt(a, b):
    return lax.dot_general(a, b, (((1,), (1,)), ((), ())), preferred_element_type=F32)


def _split2(x):
    hi = x.astype(BF16)
    lo = (x - hi.astype(F32)).astype(BF16)
    return hi, lo


def _sigmoid(z):
    return 1.0 / (1.0 + jnp.exp(-z))


def _keep(sel, blk):
    return jnp.where(sel, blk.astype(F32), 0.0).astype(BF16)


def _inproj_kernel(x_ref, g_ref, wm_ref, wt_ref, ws_ref, bs_ref, main_ref, feat_ref, kc_ref, vc_ref, small_ref):
    x = x_ref[...]
    ms = jnp.mean(x * x, axis=-1, keepdims=True)
    y = x * lax.rsqrt(ms + RMS_EPS) * g_ref[...]
    hb, hl = _split2(y)
    feat_ref[...] = _dot_nt(wt_ref[...], hb).astype(BF16)
    full = _dot(hb, wm_ref[...]).astype(BF16)
    main_ref[...] = full[:, :MAIN_W]
    kc_ref[...] = full[:, MAIN_W:MAIN_W + NSA_KV]
    vc_ref[...] = full[:, MAIN_W + NSA_KV:]
    small_ref[...] = _dot_hi_lo(hb, hl, ws_ref) + bs_ref[...]


def _inproj(x2, g, wm, wt, ws, bs, tm=1024):
    n = x2.shape[0]
    full = lambda i: (0, 0)
    return pl.pallas_call(
        _inproj_kernel,
        grid=(n // tm,),
        in_specs=[pl.BlockSpec((tm, D_MODEL), lambda i: (i, 0)),
                  pl.BlockSpec((1, D_MODEL), full),
                  pl.BlockSpec((D_MODEL, PROJ_W), full),
                  pl.BlockSpec((ROWS_T, D_MODEL), full),
                  pl.BlockSpec((D_MODEL, 2 * LANES), full),
                  pl.BlockSpec((1, LANES), full)],
        out_specs=[pl.BlockSpec((tm, MAIN_W), lambda i: (i, 0)),
                   pl.BlockSpec((ROWS_T, tm), lambda i: (0, i)),
                   pl.BlockSpec((tm, NSA_KV), lambda i: (i, 0)),
                   pl.BlockSpec((tm, NSA_KV), lambda i: (i, 0)),
                   pl.BlockSpec((tm, LANES), lambda i: (i, 0))],
        out_shape=[jax.ShapeDtypeStruct((n, MAIN_W), BF16),
                   jax.ShapeDtypeStruct((ROWS_T, n), BF16),
                   jax.ShapeDtypeStruct((n, NSA_KV), BF16),
                   jax.ShapeDtypeStruct((n, NSA_KV), BF16),
                   jax.ShapeDtypeStruct((n, LANES), F32)],
        compiler_params=_cparams("parallel"),
        name="inproj",
    )(x2, g, wm, wt, ws, bs)


def _compress_kernel(r_ref, pe_ref, w1a_ref, w1b_ref, w2_ref, o_ref):
    r = r_ref[0].astype(F32)
    pe = pe_ref[...]
    ra = (r + pe[0:1, :]).astype(BF16)
    rb = (r + pe[1:2, :]).astype(BF16)
    a = _dot(ra, w1a_ref[...])
    b = _dot(rb, w1b_ref[...])
    n_slab = a.shape[0]
    hid = a + pltpu.roll(b, n_slab - 1, 0)
    hid = jax.nn.gelu(hid)
    o_ref[0] = _dot(hid.astype(BF16), w2_ref[...]).astype(BF16)


def _compress(r, pe2, w1a, w1b, w2bd):
    b, n_slab, w = r.shape
    full = lambda i: (0, 0)
    return pl.pallas_call(
        _compress_kernel,
        grid=(b,),
        in_specs=[pl.BlockSpec((1, n_slab, w), lambda i: (i, 0, 0)),
                  pl.BlockSpec((2, w), full),
                  pl.BlockSpec((w, 2 * CMP_HIDDEN), full),
                  pl.BlockSpec((w, 2 * CMP_HIDDEN), full),
                  pl.BlockSpec((2 * CMP_HIDDEN, LANES), full)],
        out_specs=pl.BlockSpec((1, n_slab, LANES), lambda i: (i, 0, 0)),
        out_shape=jax.ShapeDtypeStruct((b, n_slab, LANES), BF16),
        compiler_params=_cparams("parallel"),
        name="compress",
    )(r, pe2, w1a, w1b, w2bd)


def _cumgate_kernel(z_ref, o_ref, carry_ref, *, tc):
    @pl.when(pl.program_id(1) == 0)
    def _():
        carry_ref[...] = jnp.zeros_like(carry_ref)

    z = z_ref[0]
    logf = -(jnp.maximum(-z, 0.0) + jnp.log(1.0 + jnp.exp(-jnp.abs(z))))
    hi = logf.astype(BF16)
    r1 = logf - hi.astype(F32)
    mid = r1.astype(BF16)
    lo = (r1 - mid.astype(F32)).astype(BF16)
    tri = (lax.broadcasted_iota(jnp.int32, (tc, tc), 0) >= lax.broadcasted_iota(jnp.int32, (tc, tc), 1))
    tri = jnp.where(tri, 1.0, 0.0).astype(BF16)
    c = (_dot(tri, hi) + _dot(tri, mid)) + _dot(tri, lo) + carry_ref[0:1, :]
    carry_ref[...] = jnp.broadcast_to(c[tc - 1:tc, :], carry_ref.shape)
    v = c * (-LOG2E)
    p0 = v.astype(BF16)
    r1 = v - p0.astype(F32)
    p1 = r1.astype(BF16)
    p2 = (r1 - p1.astype(F32)).astype(BF16)
    lane = lax.broadcasted_iota(jnp.int32, v.shape, 1)
    pieces = jnp.where(lane < FGT_LANES[1], p0.astype(F32), jnp.where(lane < FGT_LANES[2], p1.astype(F32), p2.astype(F32)))
    keep = (lane >= FGT_LANES[0]) & (lane < FGT_LANES[2] + FOX_HEADS)
    o_ref[0] = jnp.where(keep, pieces, 0.0).astype(BF16)


def _cumgate(small3, tc=512):
    b, t, w = small3.shape
    return pl.pallas_call(
        functools.partial(_cumgate_kernel, tc=tc),
        grid=(b, t // tc),
        in_specs=[pl.BlockSpec((1, tc, w), lambda i, j: (i, j, 0))],
        out_specs=pl.BlockSpec((1, tc, w), lambda i, j: (i, j, 0)),
        out_shape=jax.ShapeDtypeStruct((b, t, w), BF16),
        scratch_shapes=[pltpu.VMEM((8, w), F32)],
        compiler_params=_cparams("parallel", "arbitrary"),
        name="cumgate",
    )(small3)


def _cmp_kernel(q_ref, kc_ref, vc_ref, sm_ref, ovt_ref, o_ref, pen_ref, used_ref, imp_ref, cnt_ref,
                *, tq, n_cmp, n_sel, top_n):
    t0 = pl.program_id(1) * tq
    nck = kc_ref.shape[1]
    hg = NSA_HEADS // NSA_KV_GROUPS
    slopes = _alibi(NSA_HEADS)
    lane_k = lax.broadcasted_iota(jnp.int32, (nck, LANES), 1)
    end_k = lax.broadcasted_iota(jnp.int32, (nck, LANES), 0) * CMP_STRIDE + (CMP_BLOCK - 1)
    keys = jnp.concatenate([kc_ref[0], _alibi_k_aug(lane_k, end_k - t0, 0).astype(BF16)], axis=1)
    blk = lax.broadcasted_iota(jnp.int32, (nck, tq), 0)
    qpos = t0 + lax.broadcasted_iota(jnp.int32, (nck, tq), 1)
    visible = (qpos >= blk * CMP_STRIDE + (CMP_BLOCK - 1)) & (blk < n_cmp)
    bias = jnp.where(visible, 0.0, MASKED)
    v_t = vc_ref[0].astype(F32).T.astype(BF16)
    sig_t = _sigmoid(sm_ref[0]).T
    feat = lax.broadcasted_iota(jnp.int32, (LANES, tq), 0)
    psum = [None, None]
    outs = []
    for h in range(NSA_HEADS):
        jb, g = h % hg, h // hg
        sel = (feat >= g * HEAD_DIM) & (feat < (g + 1) * HEAD_DIM)
        qa = jnp.concatenate([_keep(sel, q_ref[jb * LANES:(jb + 1) * LANES, :]),
                              _alibi_q_aug(feat, 0, slopes[h]).astype(BF16)], axis=0)
        s = _dot(keys, qa) + bias
        m = jnp.maximum(jnp.max(s, axis=0, keepdims=True), NEG_INF)
        p = jnp.exp2(s - m)
        l = jnp.sum(p, axis=0, keepdims=True)
        p = p * jnp.where(l > 0.0, 1.0 / l, 0.0)
        psum[g] = p if psum[g] is None else psum[g] + p
        outs.append(_dot(v_t[g * HEAD_DIM:(g + 1) * HEAD_DIM, :], p.astype(BF16)) * sig_t[h:h + 1, :])
    for jb in range(hg):
        o_ref[0, :, jb * LANES:(jb + 1) * LANES] = jnp.concatenate(
            [outs[jb], outs[jb + hg]], axis=0).T.astype(o_ref.dtype)

    sub8 = lax.broadcasted_iota(jnp.int32, (SUBLANES, tq), 0)
    row_s = lax.broadcasted_iota(jnp.int32, (n_sel, tq), 0)
    cur = (t0 + lax.broadcasted_iota(jnp.int32, (n_sel, tq), 1)) >> SLC_SHIFT
    for g in range(NSA_KV_GROUPS):
        ph, plo = _split2(psum[g])
        imp_t = (_dot(ovt_ref[...], ph) + _dot(ovt_ref[...], plo))[0:n_sel, :]
        forced = (row_s == 0) | (row_s == cur) | (row_s == cur - 1)
        imp_ref[g] = jnp.where(row_s <= cur, jnp.where(forced, FORCE_SCORE, imp_t), NEG_INF)
        cnt_ref[g] = jnp.zeros((n_sel, tq), F32)
    ngrp = n_sel // SUBLANES
    last_blk = (t0 + tq - 1) >> SLC_SHIFT
    for kg in range(ngrp):
        @pl.when(kg * SUBLANES <= last_blk)
        def _(kg=kg):
            for g in range(NSA_KV_GROUPS):
                rows = [imp_ref[g, i * SUBLANES:(i + 1) * SUBLANES, :] for i in range(ngrp)]
                part = [None] * ngrp
                for k in range(kg * SUBLANES, (kg + 1) * SUBLANES):
                    rk = rows[kg][k - kg * SUBLANES:k - kg * SUBLANES + 1, :]
                    for i in range(ngrp):
                        if i > kg:
                            one = jnp.where(rk >= rows[i], 1.0, 0.0)
                        elif i < kg:
                            one = jnp.where(rk > rows[i], 1.0, 0.0)
                        else:
                            one = jnp.where(sub8 > k - i * SUBLANES,
                                            jnp.where(rk >= rows[i], 1.0, 0.0), jnp.where(rk > rows[i], 1.0, 0.0))
                        part[i] = one if part[i] is None else part[i] + one
                for i in range(ngrp):
                    cnt_ref[g, i * SUBLANES:(i + 1) * SUBLANES, :] += part[i]
    pen_t = [jnp.where(cnt_ref[g] < float(top_n), 0.0, MASKED) for g in range(NSA_KV_GROUPS)]
    for g in range(NSA_KV_GROUPS):
        pen_ref[0, g * HEAD_DIM:g * HEAD_DIM + n_sel, :] = pen_t[g].astype(BF16)
        if n_sel < HEAD_DIM:
            pen_ref[0, g * HEAD_DIM + n_sel:(g + 1) * HEAD_DIM, :] = jnp.zeros((HEAD_DIM - n_sel, tq), BF16)
    picked = jnp.where(jnp.maximum(pen_t[0], pen_t[1]) == 0.0, 1.0, 0.0)
    bpt = TK // SLC_BLOCK
    lane1 = lax.broadcasted_iota(jnp.int32, (1, LANES), 1)
    used = jnp.zeros((1, LANES), F32)
    for kt in range(n_sel // bpt):
        hit = jnp.max(picked[kt * bpt:(kt + 1) * bpt, :], axis=(0, 1), keepdims=True)
        used = jnp.where(lane1 == kt, hit, used)
    used_ref[0, 0] = used


def _cmp_attn(feat, kcmp, vcmp, small3, ov, n_cmp, n_sel, top_n, tq=TQ):
    b, t, _ = small3.shape
    nck = kcmp.shape[1]
    nq = t // tq
    assert n_sel <= HEAD_DIM
    return pl.pallas_call(
        functools.partial(_cmp_kernel, tq=tq, n_cmp=n_cmp, n_sel=n_sel, top_n=top_n),
        grid=(b, nq),
        in_specs=[pl.BlockSpec((NSA_Q, tq), lambda i, j: (ROW_QN, i * nq + j)),
                  pl.BlockSpec((1, nck, LANES), lambda i, j: (i, 0, 0)),
                  pl.BlockSpec((1, nck, LANES), lambda i, j: (i, 0, 0)),
                  pl.BlockSpec((1, tq, LANES), lambda i, j: (i, j, 0)),
                  pl.BlockSpec((LANES, nck), lambda i, j: (0, 0))],
        out_specs=[pl.BlockSpec((1, tq, NSA_Q), lambda i, j: (i, j, 0)),
                   pl.BlockSpec((1, NSA_KV_GROUPS * HEAD_DIM, tq), lambda i, j: (i, 0, j)),
                   pl.BlockSpec((1, 1, 1, LANES), lambda i, j: (i, j, 0, 0))],
        out_shape=[jax.ShapeDtypeStruct((b, t, NSA_Q), BF16),
                   jax.ShapeDtypeStruct((b, NSA_KV_GROUPS * HEAD_DIM, t), BF16),
                   jax.ShapeDtypeStruct((b, t // tq, 1, LANES), F32)],
        scratch_shapes=[pltpu.VMEM((NSA_KV_GROUPS, n_sel, tq), F32),
                        pltpu.VMEM((NSA_KV_GROUPS, n_sel, tq), F32)],
        compiler_params=_cparams("parallel", "parallel"),
        name="cmp_attn",
    )(feat, kcmp, vcmp, small3, ov)


def _bf16_terms(c, n=3):
    out, r = [], float(c)
    for _ in range(n):
        p = float(np.float32(r).astype(ml_dtypes.bfloat16))
        out.append(p)
        r -= p
    return out


def _alibi_q_aug(idx, a0, slope):
    aug = jnp.zeros(idx.shape, F32)
    for i, c in enumerate(_bf16_terms(slope * LOG2E)):
        aug = jnp.where(idx == a0 + 2 * i, float(SLC_BLOCK) * c, aug)
        aug = jnp.where(idx == a0 + 2 * i + 1, c, aug)
    return aug


def _alibi_k_aug(lane, rel, a0):
    hi = (rel >> SLC_SHIFT).astype(F32)
    lo = (rel & (SLC_BLOCK - 1)).astype(F32)
    inside = (lane >= a0) & (lane < a0 + 6)
    odd = ((lane - a0) & 1) == 1
    return jnp.where(inside, jnp.where(odd, lo, hi), 0.0)


ALIBI_LANE = {"slc": SLC_BLOCK, "win": 0, "diff": 0}
STACKS = {"slc": (1, 8), "win": (1, 8), "fox": (2, 2), "diff": (2, 4)}


def _v_half(mode, vh):
    nv = STACKS[mode][1]
    if mode in ("slc", "win"):
        return vh // (NSA_HEADS // NSA_KV_GROUPS)
    return (vh % nv) // (nv // 2)


def _flash_kernel(*refs, mode, tq, tk, lam_init, ntiles):
    sched_ref, refs = refs[0], refs[1:]
    if mode == "slc":
        use_ref, q_ref, k_ref, v_ref, pen_ref, sm_ref, add_ref, o_ref, qst_ref, m_ref, acc_ref = refs
    elif mode == "win":
        q_ref, k_ref, v_ref, sm_ref, add_ref, o_ref, qst_ref, m_ref, acc_ref = refs
    elif mode == "fox":
        q_ref, k_ref, v_ref, fa_ref, o_ref, qst_ref, m_ref, acc_ref = refs
    else:
        q_ref, k_ref, v_ref, lam_ref, sub_ref, o_ref, qst_ref, m_ref, acc_ref = refs
    nstack, nv = STACKS[mode]
    nsa = mode in ("slc", "win")
    hg = NSA_HEADS // NSA_KV_GROUPS
    step_id = pl.program_id(1)
    q0 = sched_ref[SCHED_Q, step_id] * tq
    k0 = sched_ref[SCHED_K, step_id] * tk

    @pl.when(sched_ref[SCHED_FIRST, step_id] == 1)
    def _():
        m_ref[...] = jnp.full(m_ref.shape, NEG_INF, F32)
        acc_ref[...] = jnp.zeros(acc_ref.shape, F32)
        feat = lax.broadcasted_iota(jnp.int32, (LANES, tq), 0)
        for vh in range(nstack * nv):
            if nsa:
                jb, g = vh % hg, vh // hg
                lo_row, width = g * HEAD_DIM, HEAD_DIM
                aug = _alibi_q_aug(feat, ALIBI_LANE[mode], _alibi(NSA_HEADS)[vh])
                if mode == "slc":
                    aug = aug + jnp.concatenate([pen_ref[0, g * HEAD_DIM:(g + 1) * HEAD_DIM, :].astype(F32),
                                                 jnp.zeros((LANES - HEAD_DIM, tq), F32)], axis=0)
            elif mode == "fox":
                jb, r = vh // nv, vh % nv
                lo_row, width = r * HEAD_DIM, HEAD_DIM
                hit = (feat == FGT_LANES[0] + vh) | (feat == FGT_LANES[1] + vh) | (feat == FGT_LANES[2] + vh)
                aug = jnp.where(hit, 1.0, 0.0)
            else:
                jb, r = vh // nv, vh % nv
                lo_row, width = r * DIFF_QK_DIM, DIFF_QK_DIM
                aug = _alibi_q_aug(feat, ALIBI_LANE[mode], _alibi(DIFF_HEADS)[vh // 2])
            sel = (feat >= lo_row) & (feat < lo_row + width)
            base = vh * 2 * LANES
            qst_ref[base:base + LANES, :] = _keep(sel, q_ref[jb * LANES:(jb + 1) * LANES, :])
            qst_ref[base + LANES:base + 2 * LANES, :] = aug.astype(BF16)

    def step(masked):
        lane_k = lax.broadcasted_iota(jnp.int32, (tk, LANES), 1)
        row_k = lax.broadcasted_iota(jnp.int32, (tk, LANES), 0)
        if mode == "fox":
            k_aug = fa_ref[0]
        else:
            ka = _alibi_k_aug(lane_k, (k0 - q0) + row_k, ALIBI_LANE[mode])
            if mode == "slc":
                ka = ka + jnp.where(((k0 + row_k) >> SLC_SHIFT) == lane_k, 1.0, 0.0)
            k_aug = ka.astype(BF16)
        if masked:
            dist = (q0 - k0) + lax.broadcasted_iota(jnp.int32, (tk, tq), 1) - lax.broadcasted_iota(jnp.int32, (tk, tq), 0)
            ok = dist >= 0
            if mode == "win":
                ok = ok & (dist < WINDOW)
            bias = jnp.where(ok, 0.0, MASKED)
        keys, vals_t = [], []
        ones = jnp.ones((ONES_ROWS, tk), BF16)
        for st in range(nstack):
            kblk = k_ref[0] if nsa else k_ref[0, :, st * LANES:(st + 1) * LANES]
            keys.append(jnp.concatenate([kblk, k_aug], axis=1))
            vals_t.append([jnp.concatenate([v_ref[st * LANES + hf * HEAD_DIM:st * LANES + (hf + 1) * HEAD_DIM, :],
                                            ones], axis=0) for hf in range(LANES // HEAD_DIM)])

        nstrip = tk // KSTRIP
        nvh = nstack * nv

        def scores(vh):
            s = _dot(keys[vh // nv], qst_ref[vh * 2 * LANES:(vh + 1) * 2 * LANES, :])
            if masked:
                s = s + bias
            return s, jnp.max(s, axis=0, keepdims=True)

        cur, cur_max = scores(0)
        for vh in range(nvh):
            cols = slice(vh * tq, (vh + 1) * tq)
            v_half = vals_t[vh // nv][_v_half(mode, vh)]
            m_old = m_ref[:, cols]
            m_new = jnp.maximum(m_old, cur_max)
            alpha = jnp.exp2(m_old - m_new)
            if vh + 1 < nvh:
                nxt, nxt_max = scores(vh + 1)
            pv = None
            for r in range(nstrip):
                p = jnp.exp2((cur[r * KSTRIP:(r + 1) * KSTRIP, :] - m_new).astype(BF16))
                part = _dot(v_half[:, r * KSTRIP:(r + 1) * KSTRIP], p)
                pv = part if pv is None else pv + part
            m_ref[:, cols] = m_new
            acc_ref[:, cols] = alpha * acc_ref[:, cols] + pv
            if vh + 1 < nvh:
                cur, cur_max = nxt, nxt_max

    is_masked = sched_ref[SCHED_MASKED, step_id] == 1
    if mode == "win":
        pl.when(is_masked)(lambda: step(True))
    elif mode == "slc":
        tile_id = (pl.program_id(0) * ntiles[0] + sched_ref[SCHED_Q, step_id]) * ntiles[1] + sched_ref[SCHED_K, step_id]
        needed = use_ref[tile_id] == 1
        pl.when(is_masked & needed)(lambda: step(True))
        pl.when(jnp.logical_not(is_masked) & needed)(lambda: step(False))
    else:
        pl.when(is_masked)(lambda: step(True))
        pl.when(jnp.logical_not(is_masked))(lambda: step(False))

    @pl.when(sched_ref[SCHED_LAST, step_id] == 1)
    def _():
        inv = 1.0 / acc_ref[HEAD_DIM:HEAD_DIM + 1, :]

        def out_t(vh):
            return acc_ref[0:HEAD_DIM, vh * tq:(vh + 1) * tq] * inv[:, vh * tq:(vh + 1) * tq]

        def pair(a, b):
            return jnp.concatenate([a, b], axis=0).T

        if nsa:
            sig_t = _sigmoid(sm_ref[0]).T
            br = 1 if mode == "slc" else 2
            for jb in range(hg):
                c0 = br * NSA_HEADS + jb
                o = pair(out_t(jb) * sig_t[c0:c0 + 1, :], out_t(jb + hg) * sig_t[c0 + hg:c0 + hg + 1, :])
                o_ref[0, :, jb * LANES:(jb + 1) * LANES] = (
                    add_ref[0, :, jb * LANES:(jb + 1) * LANES].astype(F32) + o).astype(o_ref.dtype)
        elif mode == "fox":
            for pb in range(nstack):
                o = pair(out_t(2 * pb), out_t(2 * pb + 1))
                o_ref[0, :, pb * LANES:(pb + 1) * LANES] = o.astype(o_ref.dtype)
        else:
            lam = lam_ref[...]
            lam_full = (jnp.exp(jnp.sum(lam[0:1, :] * lam[1:2, :], axis=-1, keepdims=True))
                        - jnp.exp(jnp.sum(lam[2:3, :] * lam[3:4, :], axis=-1, keepdims=True)) + lam_init)
            for pb in range(nstack):
                normed = []
                for hh in range(2):
                    d = out_t(pb * nv + 2 * hh) - lam_full * out_t(pb * nv + 2 * hh + 1)
                    ms = jnp.mean(d * d, axis=0, keepdims=True)
                    normed.append(d * lax.rsqrt(ms + RMS_EPS) * sub_ref[...] * (1.0 - lam_init))
                o_ref[0, :, pb * LANES:(pb + 1) * LANES] = pair(normed[0], normed[1]).astype(o_ref.dtype)


def _flash(main3, feat, mode, extra, lam_init=0.0, tile_used=None, tq=TQ, tk=TK):
    b, t, _ = main3.shape
    tk = min(tk, t)
    nq, nk = t // tq, t // tk
    nstack, nv = STACKS[mode]
    sched = _flash_schedule(t, tq, tk, mode == "win")
    prefetch = [jnp.asarray(sched)]
    if mode == "slc":
        prefetch.append(tile_used.reshape(-1))
    qtile = lambda w, c: pl.BlockSpec((1, tq, w), lambda bi, s, sch, *_: (bi, sch[SCHED_Q, s], c))
    ktile = lambda w, c: pl.BlockSpec((1, tk, w), lambda bi, s, sch, *_: (bi, sch[SCHED_K, s], c))
    qfeat = lambda rows, r: pl.BlockSpec((rows, tq), lambda bi, s, sch, *_: (r, bi * nq + sch[SCHED_Q, s]))
    vfeat = lambda rows, r: pl.BlockSpec((rows, tk), lambda bi, s, sch, *_: (r, bi * nk + sch[SCHED_K, s]))
    if mode in ("slc", "win"):
        kcol, vrow = (COL_KS, ROW_VS) if mode == "slc" else (COL_KW, ROW_VW)
        in_specs = [qfeat(NSA_Q, ROW_QN), ktile(LANES, kcol), vfeat(LANES, vrow)]
        if mode == "slc":
            in_specs.append(pl.BlockSpec((1, NSA_KV_GROUPS * HEAD_DIM, tq),
                                         lambda bi, s, sch, *_: (bi, 0, sch[SCHED_Q, s])))
        in_specs += [qtile(LANES, 0), qtile(NSA_Q, 0)]
        out_w = NSA_Q
    else:
        qr, kc, vr = (ROW_FQ, COL_FK, ROW_FV) if mode == "fox" else (ROW_DQ, COL_DK, ROW_DV)
        in_specs = [qfeat(2 * LANES, qr), ktile(2 * LANES, kc), vfeat(2 * LANES, vr)]
        if mode == "fox":
            in_specs.append(ktile(LANES, 0))
        else:
            in_specs += [pl.BlockSpec((4, DIFF_QK_DIM), lambda bi, s, sch, *_: (0, 0)),
                         pl.BlockSpec((HEAD_DIM, 1), lambda bi, s, sch, *_: (0, 0))]
        out_w = 2 * LANES
    cols = nstack * nv * tq
    return pl.pallas_call(
        functools.partial(_flash_kernel, mode=mode, tq=tq, tk=tk, lam_init=lam_init, ntiles=(t // tq, t // tk)),
        grid_spec=pltpu.PrefetchScalarGridSpec(
            num_scalar_prefetch=len(prefetch),
            grid=(b, sched.shape[1]),
            in_specs=in_specs,
            out_specs=qtile(out_w, 0),
            scratch_shapes=[pltpu.VMEM((nstack * nv * 2 * LANES, tq), BF16),
                            pltpu.VMEM((1, cols), F32),
                            pltpu.VMEM((HEAD_DIM + ONES_ROWS, cols), F32)]),
        out_shape=jax.ShapeDtypeStruct((b, t, out_w), BF16),
        compiler_params=_cparams("parallel", "arbitrary"),
        name="flash_" + mode,
    )(*prefetch, feat, main3, feat, *extra)


SCHED_Q, SCHED_K, SCHED_FIRST, SCHED_LAST, SCHED_MASKED = range(5)


def _flash_schedule(t, tq, tk, window):
    rows = []
    for i in range(t // tq):
        q_lo, q_hi = i * tq, i * tq + tq - 1
        k_first = max(q_lo - (WINDOW - 1), 0) // tk if window else 0
        tiles = list(range(k_first, q_hi // tk + 1))
        for n, kt in enumerate(tiles):
            fully_visible = kt * tk + tk - 1 <= q_lo and not window
            rows.append((i, kt, int(n == 0), int(n == len(tiles) - 1), int(not fully_visible)))
    return np.asarray(rows, np.int32).T


def _outproj_kernel(x_ref, oa_ref, ob_ref, oc_ref, wa_ref, wb_ref, wc_ref, o_ref):
    acc = _dot(oa_ref[...], wa_ref[...])
    acc = acc + _dot(ob_ref[...], wb_ref[...])
    acc = acc + _dot(oc_ref[...], wc_ref[...])
    o_ref[...] = x_ref[...] + acc


def _outproj(x2, oa, ob, oc, wa, wb, wc, tm=1024):
    n = x2.shape[0]
    row = lambda i: (i, 0)
    full = lambda i: (0, 0)
    return pl.pallas_call(
        _outproj_kernel,
        grid=(n // tm,),
        in_specs=[pl.BlockSpec((tm, D_MODEL), row),
                  pl.BlockSpec((tm, NSA_Q), row),
                  pl.BlockSpec((tm, FOX_W), row),
                  pl.BlockSpec((tm, DIFF_W), row),
                  pl.BlockSpec((NSA_Q, D_MODEL), full),
                  pl.BlockSpec((FOX_W, D_MODEL), full),
                  pl.BlockSpec((DIFF_W, D_MODEL), full)],
        out_specs=pl.BlockSpec((tm, D_MODEL), row),
        out_shape=jax.ShapeDtypeStruct((n, D_MODEL), F32),
        compiler_params=_cparams("parallel"),
        name="outproj",
    )(x2, oa, ob, oc, wa, wb, wc)


ROUTE_OFF = SUBLANES
COL_GRP, COL_POS = EXPERTS_PER_GROUP, EXPERTS_PER_GROUP + 1


MOE_CHUNK = 304
MOE_PREFIX_BLOCK = 256


def _moe_kernel(x_ref, g_ref, wr_ref, br_ref, wg_ref, wu_ref, wd_ref, gf_ref, o_ref,
                h_ref, gsel_ref, col_ref, row_ref, cnt_ref, *, final_norm):
    grp_id = pl.program_id(1)
    tm = x_ref.shape[0]
    ch = MOE_CHUNK

    @pl.when(grp_id == 0)
    def _():
        x = x_ref[...]
        ms = jnp.mean(x * x, axis=-1, keepdims=True)
        y = x * lax.rsqrt(ms + RMS_EPS) * g_ref[...]
        hb, hl = _split2(y)
        h_ref[...] = hb
        lg = _dot_hi_lo(hb, hl, wr_ref) + br_ref[...]
        nrow = ROUTE_OFF + N_EXPERTS
        lt = lg.T[0:nrow, :]
        r = lax.broadcasted_iota(jnp.int32, (nrow, tm), 0).astype(F32)
        big = float(LANES)
        isg = r < N_GROUPS
        gmax = jnp.max(jnp.where(isg, lt, -jnp.inf), axis=0, keepdims=True)
        grp = jnp.min(jnp.where(isg & (lt == gmax), r, big), axis=0, keepdims=True)
        gprob = 1.0 / jnp.sum(jnp.where(isg, jnp.exp(lt - gmax), 0.0), axis=0, keepdims=True)
        lo_row = ROUTE_OFF + grp * EXPERTS_PER_GROUP
        ing = (r >= lo_row) & (r < lo_row + EXPERTS_PER_GROUP)
        v1 = jnp.max(jnp.where(ing, lt, -jnp.inf), axis=0, keepdims=True)
        i1 = jnp.min(jnp.where(ing & (lt == v1), r, big), axis=0, keepdims=True)
        rest = ing & (r != i1)
        v2 = jnp.max(jnp.where(rest, lt, -jnp.inf), axis=0, keepdims=True)
        i2 = jnp.min(jnp.where(rest & (lt == v2), r, big), axis=0, keepdims=True)
        e2 = jnp.exp(v2 - v1)
        w1 = gprob / (1.0 + e2)
        w2 = gprob * e2 / (1.0 + e2)
        gate_t = jnp.where(r == i1, w1, 0.0) + jnp.where(r == i2, w2, 0.0)
        epg = EXPERTS_PER_GROUP
        gsel_t = jnp.zeros((epg, tm), F32)
        for gg in range(N_GROUPS):
            gsel_t = jnp.where(grp == float(gg), gate_t[ROUTE_OFF + gg * epg:ROUTE_OFF + (gg + 1) * epg, :], gsel_t)
        r8 = lax.broadcasted_iota(jnp.int32, (SUBLANES, tm), 0).astype(F32)
        onehot_t = jnp.where(r8 == grp, 1.0, 0.0)
        pb = MOE_PREFIX_BLOCK
        tri = jnp.where(lax.broadcasted_iota(jnp.int32, (pb, pb), 0) < lax.broadcasted_iota(jnp.int32, (pb, pb), 1),
                        1.0, 0.0).astype(BF16)
        carry = jnp.zeros((SUBLANES, 1), F32)
        pos_parts = []
        for blk in range(tm // pb):
            oh_b = onehot_t[:, blk * pb:(blk + 1) * pb]
            prefix = _dot(oh_b.astype(BF16), tri) + carry
            pos_parts.append(jnp.sum(oh_b * prefix, axis=0, keepdims=True))
            carry = carry + jnp.sum(oh_b, axis=1, keepdims=True)
        pos = jnp.concatenate(pos_parts, axis=1)
        for gg in range(N_GROUPS):
            cnt_ref[gg] = carry[gg, 0].astype(jnp.int32)
        row_ref[...] = jnp.concatenate([grp, pos, jnp.zeros((SUBLANES - 2, tm), F32)], axis=0)
        stacked = jnp.concatenate([gsel_t, row_ref[...], jnp.zeros((LANES - 2 * SUBLANES, tm), F32)], axis=0)
        colinfo = stacked.T
        lane = lax.broadcasted_iota(jnp.int32, (tm, LANES), 1)
        ghi, glo = _split2(jnp.where(lane < epg, colinfo, 0.0))
        gsel_ref[:, 0:LANES] = ghi
        gsel_ref[:, LANES:2 * LANES] = glo
        col_ref[...] = colinfo
        o_ref[...] = x

    grp_f = grp_id.astype(F32)
    key_col = jnp.where(col_ref[:, COL_GRP:COL_GRP + 1] == grp_f, col_ref[:, COL_POS:COL_POS + 1], -1.0)
    key_row = jnp.where(row_ref[0:1, :] == grp_f, row_ref[1:2, :], -1.0)

    def chunk(c, carry):
        base = (c * ch).astype(F32)
        sub_iota = lax.broadcasted_iota(jnp.int32, (ch, tm), 0).astype(F32)
        gather = jnp.where(key_row - base == sub_iota, 1.0, 0.0).astype(BF16)
        xg = _dot(gather, h_ref[...]).astype(BF16)
        gparts = _dot(gather, gsel_ref[...])
        gates = gparts[:, 0:LANES] + gparts[:, LANES:2 * LANES]
        y = None
        for e in range(EXPERTS_PER_GROUP):
            a = _dot(xg, wg_ref[0, e])
            u = _dot(xg, wu_ref[0, e])
            act = a * _sigmoid(a) * u * gates[:, e:e + 1]
            part = _dot(act.astype(BF16), wd_ref[0, e])
            y = part if y is None else y + part
        chp = -(-ch // LANES) * LANES
        lane_iota = lax.broadcasted_iota(jnp.int32, (tm, chp), 1).astype(F32)
        scatter = jnp.where(key_col - base == lane_iota, 1.0, 0.0).astype(BF16)
        yb = y.astype(BF16)
        if chp > ch:
            yb = jnp.concatenate([yb, jnp.zeros((chp - ch, D_MODEL), BF16)], axis=0)
        o_ref[...] = o_ref[...] + _dot(scatter, yb)
        return carry

    lax.fori_loop(0, (cnt_ref[grp_id] + ch - 1) // ch, chunk, 0)

    @pl.when(grp_id == N_GROUPS - 1)
    def _():
        if final_norm:
            y = o_ref[...]
            ms = jnp.mean(y * y, axis=-1, keepdims=True)
            o_ref[...] = y * lax.rsqrt(ms + RMS_EPS) * gf_ref[...]


def _moe(x2, g, wr, br, wg, wu, wd, gf, layer, final_norm, tm=1024):
    n = x2.shape[0]
    tm = min(tm, n)
    row = lambda i, e: (i, 0)
    full = lambda i, e: (0, 0)
    epg = EXPERTS_PER_GROUP
    wspec_in = pl.BlockSpec((1, epg, D_MODEL, D_EXPERT), lambda i, e: (layer, e, 0, 0))
    wspec_out = pl.BlockSpec((1, epg, D_EXPERT, D_MODEL), lambda i, e: (layer, e, 0, 0))
    return pl.pallas_call(
        functools.partial(_moe_kernel, final_norm=final_norm),
        grid=(n // tm, N_GROUPS),
        in_specs=[pl.BlockSpec((tm, D_MODEL), row),
                  pl.BlockSpec((1, D_MODEL), full),
                  pl.BlockSpec((D_MODEL, 2 * LANES), full),
                  pl.BlockSpec((1, LANES), full),
                  wspec_in, wspec_in, wspec_out,
                  pl.BlockSpec((1, D_MODEL), full)],
        out_specs=pl.BlockSpec((tm, D_MODEL), row),
        out_shape=jax.ShapeDtypeStruct((n, D_MODEL), F32),
        scratch_shapes=[pltpu.VMEM((tm, D_MODEL), BF16),
                        pltpu.VMEM((tm, 2 * LANES), BF16),
                        pltpu.VMEM((tm, LANES), F32),
                        pltpu.VMEM((8, tm), F32),
                        pltpu.SMEM((N_GROUPS,), jnp.int32)],
        compiler_params=_cparams("parallel", "arbitrary"),
        name="moe",
    )(x2, g, wr, br, wg, wu, wd, gf)


def _head_perm():
    hg = NSA_HEADS // NSA_KV_GROUPS
    order = []
    for j in range(hg):
        order += [j, j + hg]
    return np.concatenate([np.arange(h * HEAD_DIM, (h + 1) * HEAD_DIM) for h in order])


def _hi_lo_cols(w):
    hi = w.astype(BF16)
    return jnp.concatenate([hi, (w - hi.astype(F32)).astype(BF16)], axis=1)


def _dot_hi_lo(xh, xl, w_ref):
    r = _dot(xh, w_ref[...])
    return (r[:, 0:LANES] + r[:, LANES:2 * LANES]) + _dot(xl, w_ref[:, 0:LANES])


def _layer_params(l, w_in, b_gate, b_fgt, cmp_pe, cmp_w1, cmp_w2, w_out):
    offs = np.concatenate([[0], np.cumsum(IN_WIDTHS)])
    seg = [w_in[l][:, offs[i]:offs[i + 1]] for i in range(len(IN_WIDTHS))]
    (wq, wkc, wvc, wks, wvs, wkw, wvw, wgt, wfq, wfk, wfv, wff, wdq, wdk, wdv) = seg
    perm = _head_perm()
    gate_perm = np.array([h * 3 + br for br in range(3) for h in range(NSA_HEADS)])
    wm = jnp.concatenate([wks, wkw, wfk, wdk, wkc, wvc], axis=1).astype(BF16)
    wt = jnp.concatenate([wq[:, perm] * (HEAD_DIM ** -0.5 * LOG2E), wfq * (HEAD_DIM ** -0.5 * LOG2E),
                          wdq * (DIFF_QK_DIM ** -0.5 * LOG2E), wvs, wvw, wfv, wdv], axis=1).T.astype(BF16)
    pad = LANES - NSA_GATE - 3 * FOX_HEADS
    ws = jnp.concatenate([wgt[:, gate_perm], wff, wff, wff, jnp.zeros((D_MODEL, pad), F32)], axis=1)
    ws = _hi_lo_cols(ws)
    bs = jnp.concatenate([b_gate[l][gate_perm], b_fgt[l], b_fgt[l], b_fgt[l], jnp.zeros((pad,), F32)])[None, :]

    eye = jnp.eye(NSA_KV_GROUPS, dtype=F32)
    half = CMP_BLOCK // 2
    cmp = []
    for pe, w1, w2 in zip(cmp_pe, cmp_w1, cmp_w2):
        w1r = w1[l].reshape(2, half, HEAD_DIM, CMP_HIDDEN)
        w1x = jnp.einsum('srdk,gh->srgdhk', w1r, eye).reshape(2, half * NSA_KV, NSA_KV_GROUPS * CMP_HIDDEN)
        w2bd = jnp.einsum('kd,gh->gkhd', w2[l], eye).reshape(NSA_KV_GROUPS * CMP_HIDDEN, NSA_KV)
        pe2 = jnp.broadcast_to(pe[l].reshape(2, half, 1, HEAD_DIM), (2, half, NSA_KV_GROUPS, HEAD_DIM))
        cmp.append((pe2.reshape(2, half * NSA_KV), w1x[0].astype(BF16), w1x[1].astype(BF16), w2bd.astype(BF16)))

    wo = w_out[l]
    wa = wo[:NSA_Q][perm].astype(BF16)
    wb = wo[NSA_Q:NSA_Q + FOX_W].astype(BF16)
    wc = wo[NSA_Q + FOX_W:].astype(BF16)
    return dict(wm=wm, wt=wt, ws=ws, bs=bs, cmp_k=cmp[0], cmp_v=cmp[1], wa=wa, wb=wb, wc=wc)


def _overlap_matrix(n_slab, n_cmp, n_sel):
    c_start = np.arange(n_cmp) * CMP_STRIDE
    s_start = np.arange(n_sel) * SLC_BLOCK
    ov = np.clip(np.minimum(c_start[:, None] + CMP_BLOCK, s_start[None, :] + SLC_BLOCK)
                 - np.maximum(c_start[:, None], s_start[None, :]), 0, None) / CMP_BLOCK
    full = np.zeros((LANES, n_slab), np.float32)
    full[:n_sel, :n_cmp] = ov.T
    return jnp.asarray(full, dtype=BF16)


def kernel(x, norm_attn, w_in, b_gate, b_fgt, cmp_k_pe, cmp_k_w1, cmp_k_w2, cmp_v_pe, cmp_v_w1, cmp_v_w2, diff_lambda, diff_subln, w_out, norm_ffn, w_grp, b_grp, w_exp, b_exp, w_e_gate, w_e_up, w_e_down, norm_final):
    b, t, d = x.shape
    depth = w_in.shape[0]
    n = b * t
    n_slab = t // CMP_STRIDE
    n_cmp = (t - CMP_BLOCK) // CMP_STRIDE + 1
    n_sel = t // SLC_BLOCK
    top_n = min(SLC_TOPK, n_sel)
    ov = _overlap_matrix(n_slab, n_cmp, n_sel)

    wg_all = w_e_gate.astype(BF16)
    wu_all = w_e_up.astype(BF16)
    wd_all = w_e_down.astype(BF16)
    gf = norm_final[None, :]

    x2 = x.reshape(n, d)
    for l in range(depth):
        p = _layer_params(l, w_in, b_gate, b_fgt, (cmp_k_pe, cmp_v_pe), (cmp_k_w1, cmp_v_w1),
                          (cmp_k_w2, cmp_v_w2), w_out)
        main, feat, kc, vc, small = _inproj(x2, norm_attn[l][None, :], p["wm"], p["wt"], p["ws"], p["bs"])
        main3 = main.reshape(b, t, MAIN_W)
        small3 = small.reshape(b, t, LANES)
        kcmp = _compress(kc.reshape(b, n_slab, CMP_STRIDE * NSA_KV), *p["cmp_k"])
        vcmp = _compress(vc.reshape(b, n_slab, CMP_STRIDE * NSA_KV), *p["cmp_v"])
        o_nsa, pen, used = _cmp_attn(feat, kcmp, vcmp, small3, ov, n_cmp, n_sel, top_n)
        nkt = t // min(TK, t)
        tile_used = used.reshape(b, t // TQ, -1, LANES).max(axis=2)[:, :, :nkt].astype(jnp.int32)
        o_nsa = _flash(main3, feat, "slc", (pen, small3, o_nsa), tile_used=tile_used)
        o_nsa = _flash(main3, feat, "win", (small3, o_nsa))
        o_fox = _flash(main3, feat, "fox", (_cumgate(small3),))
        o_diff = _flash(main3, feat, "diff", (diff_lambda[l], diff_subln[l][:, None]), lam_init=_lambda_init(l))

        x2 = _outproj(x2, o_nsa.reshape(n, NSA_Q), o_fox.reshape(n, FOX_W), o_diff.reshape(n, DIFF_W),
                      p["wa"], p["wb"], p["wc"])

        gpad, epad = ROUTE_OFF - N_GROUPS, LANES - ROUTE_OFF - N_EXPERTS
        wr = jnp.concatenate([w_grp[l], jnp.zeros((d, gpad), F32), w_exp[l], jnp.zeros((d, epad), F32)], axis=1)
        br = jnp.concatenate([b_grp[l], jnp.zeros((gpad,), F32), b_exp[l], jnp.zeros((epad,), F32)])[None, :]
        x2 = _moe(x2, norm_ffn[l][None, :], _hi_lo_cols(wr), br, wg_all, wu_all, wd_all, gf, l,
                  final_norm=(l == depth - 1))
    return x2.reshape(b, t, d)
```
